```python
import jax, jax.numpy as jnp
from jax import lax
import numpy as np

D_MODEL = 1024
BATCH = 16
SEQ = 2048
DEPTH = 1

CTX_LEN = 256
GRID_W = 64
MIX_WIDTH = D_MODEL
NA_HEAD_DIM = 64
NA_WIDTH = MIX_WIDTH // 2
NA_HEADS = NA_WIDTH // NA_HEAD_DIM
NA_WIN_ROWS = 8
NA_WIN_COLS = 16
RET_HEADS = 4
RET_WIDTH = MIX_WIDTH - NA_WIDTH
RET_V_DIM = RET_WIDTH // RET_HEADS
RET_QK_DIM = RET_WIDTH // RET_HEADS
RET_QK_WIDTH = RET_HEADS * RET_QK_DIM
RET_CHUNK = 128
IN_WIDTH = 4 * NA_WIDTH + 2 * RET_QK_WIDTH + 2 * RET_WIDTH
ROPE_BASE = 10000.0
EPS = 1e-6

kernel_name = "hybrid_na_retention_dit_block"


def _rmsnorm(x, g):
    xf = x.astype(jnp.float32)
    xf = xf * lax.rsqrt(jnp.mean(xf * xf, axis=-1, keepdims=True) + EPS)
    return (xf * g.astype(jnp.float32)).astype(x.dtype)


def _heads(t, n_heads):
    b, l, _ = t.shape
    return t.reshape(b, l, n_heads, -1).transpose(0, 2, 1, 3)


def _merge(t):
    b, h, l, d = t.shape
    return t.transpose(0, 2, 1, 3).reshape(b, l, h * d)


def _split_proj(p):
    sizes = [NA_WIDTH] * 4 + [RET_QK_WIDTH, RET_QK_WIDTH, RET_WIDTH, RET_WIDTH]
    offs = [int(o) for o in np.cumsum(sizes)[:-1]]
    return jnp.split(p, offs, axis=-1)


def _axial_rotary(x):
    L, dk = x.shape[2], x.shape[-1]
    half = dk // 2
    nf = half // 2
    t = jnp.arange(L)
    row = (t // GRID_W).astype(jnp.float32)
    col = (t % GRID_W).astype(jnp.float32)
    inv = ROPE_BASE ** (-jnp.arange(nf, dtype=jnp.float32) / nf)
    ang = jnp.concatenate([row[:, None] * inv, col[:, None] * inv], axis=-1)
    cos, sin = jnp.cos(ang), jnp.sin(ang)
    xf = x.astype(jnp.float32)
    x1, x2 = xf[..., :half], xf[..., half:]
    out = jnp.concatenate([x1 * cos - x2 * sin, x1 * sin + x2 * cos], axis=-1)
    return out.astype(x.dtype)


def _neighbourhood_attention(q, k, v, k_ctx, v_ctx, rpb):
    B, H, L, dh = q.shape
    rows = L // GRID_W
    kr = min(NA_WIN_ROWS, rows)
    kc = NA_WIN_COLS
    r = jnp.arange(rows)
    r0 = jnp.clip(r - kr // 2, 0, rows - kr)
    row_idx = r0[:, None] + jnp.arange(kr)[None, :]
    cq = jnp.arange(GRID_W)
    c0 = jnp.clip(cq - kc // 2, 0, GRID_W - kc)
    ck = jnp.arange(GRID_W)
    col_in = (ck[None, :] >= c0[:, None]) & (ck[None, :] < c0[:, None] + kc)
    dr_i = row_idx - r[:, None] + NA_WIN_ROWS - 1
    dc_i = jnp.clip(ck[None, :] - cq[:, None] + kc - 1, 0, 2 * kc - 2)
    bias = rpb[:, dr_i[:, None, :, None], dc_i[None, :, None, :]]
    bias = bias.reshape(H, rows, GRID_W, kr * GRID_W).astype(jnp.float32)
    mask = jnp.broadcast_to(col_in[:, None, :], (GRID_W, kr, GRID_W)).reshape(GRID_W, kr * GRID_W)

    kg = k.reshape(B, H, rows, GRID_W, dh)[:, :, row_idx].reshape(B, H, rows, kr * GRID_W, dh)
    vg = v.reshape(B, H, rows, GRID_W, dh)[:, :, row_idx].reshape(B, H, rows, kr * GRID_W, dh)
    qr = q.reshape(B, H, rows, GRID_W, dh)
    scale = dh ** -0.5
    s_loc = jnp.einsum('bhrqd,bhrkd->bhrqk', qr, kg, preferred_element_type=jnp.float32) * scale + bias[None]
    s_loc = jnp.where(mask, s_loc, -jnp.inf)
    s_ctx = jnp.einsum('bhrqd,bhcd->bhrqc', qr, k_ctx, preferred_element_type=jnp.float32) * scale
    p = jax.nn.softmax(jnp.concatenate([s_loc, s_ctx], axis=-1), axis=-1)
    p_loc = p[..., :kr * GRID_W].astype(v.dtype)
    p_ctx = p[..., kr * GRID_W:].astype(v.dtype)
    o = jnp.einsum('bhrqk,bhrkd->bhrqd', p_loc, vg) + jnp.einsum('bhrqc,bhcd->bhrqd', p_ctx, v_ctx)
    return o.reshape(B, H, L, dh)


def _dense_attention(q, k, v):
    s = jnp.einsum('bhqd,bhkd->bhqk', q, k, preferred_element_type=jnp.float32) * (q.shape[-1] ** -0.5)
    p = jax.nn.softmax(s, axis=-1).astype(v.dtype)
    return jnp.einsum('bhqk,bhkd->bhqd', p, v)


def _retention_chunkwise(q, k, v, log_gamma, state0):
    B, H, L, dk = q.shape
    dv = v.shape[-1]
    C = RET_CHUNK
    n = L // C
    qc = q.astype(jnp.float32).reshape(B, H, n, C, dk)
    kc = k.astype(jnp.float32).reshape(B, H, n, C, dk)
    vc = v.astype(jnp.float32).reshape(B, H, n, C, dv)
    lg = log_gamma.astype(jnp.float32)
    i = jnp.arange(C, dtype=jnp.float32)
    dist = i[:, None] - i[None, :]
    decay = jnp.where(dist >= 0, jnp.exp(lg[:, None, None] * jnp.maximum(dist, 0.0)), 0.0)
    s = jnp.einsum('bhnid,bhnjd->bhnij', qc, kc) * decay[None, :, None]
    inner = jnp.einsum('bhnij,bhnje->bhnie', s, vc)
    k_decay = jnp.exp(lg[:, None] * (C - 1 - i))
    k_w = kc * k_decay[None, :, None, :, None]
    chunk_states = jnp.einsum('bhnjd,bhnje->nbhde', k_w, vc)
    chunk_decay = jnp.exp(lg * C)[None, :, None, None]

    def step(S, T):
        return chunk_decay * S + T, S

    S_final, S_prev = lax.scan(step, state0, chunk_states)
    q_decay = jnp.exp(lg[:, None] * (i + 1.0))
    q_w = qc * q_decay[None, :, None, :, None]
    cross = jnp.einsum('bhnid,nbhde->bhnie', q_w, S_prev)
    return (inner + cross).reshape(B, H, L, dv), S_final


def _retention_final_state(k, v, log_gamma):
    L = k.shape[2]
    w = jnp.exp(log_gamma.astype(jnp.float32)[:, None] * (L - 1 - jnp.arange(L, dtype=jnp.float32)))
    return jnp.einsum('bhld,bhle->bhde', k.astype(jnp.float32) * w[None, :, :, None], v.astype(jnp.float32))


def _head_rmsnorm(o, g):
    o = o * lax.rsqrt(jnp.mean(o * o, axis=-1, keepdims=True) + EPS)
    return _merge(o) * g.astype(jnp.float32)


def _mixer(h_lat, h_ctx, w_in, na_rpb, lg_f, lg_b, ret_norm_g, w_out, with_ctx_out):
    dt = h_lat.dtype
    na_q, na_k, na_v, na_g, r_q, r_k, r_v, r_g = _split_proj(h_lat @ w_in)
    cna_q, cna_k, cna_v, cna_g, cr_q, cr_k, cr_v, cr_g = _split_proj(h_ctx @ w_in)

    k_na_ctx = _heads(cna_k, NA_HEADS)
    v_na_ctx = _heads(cna_v, NA_HEADS)
    o_na = _neighbourhood_attention(_heads(na_q, NA_HEADS), _heads(na_k, NA_HEADS),
                                    _heads(na_v, NA_HEADS), k_na_ctx, v_na_ctx, na_rpb)
    y_na = _merge(o_na) * jax.nn.silu(na_g)

    kscale = RET_QK_DIM ** -0.5
    q = _axial_rotary(_heads(r_q, RET_HEADS))
    k = _axial_rotary(_heads(r_k, RET_HEADS)) * kscale
    v = _heads(r_v, RET_HEADS)
    k_ctx = _heads(cr_k, RET_HEADS) * kscale
    v_ctx = _heads(cr_v, RET_HEADS)
    S_f = _retention_final_state(k_ctx, v_ctx, lg_f)
    S_b = _retention_final_state(jnp.flip(k_ctx, 2), jnp.flip(v_ctx, 2), lg_b)
    o_f, _ = _retention_chunkwise(q, k, v, lg_f, S_f)
    o_b, _ = _retention_chunkwise(jnp.flip(q, 2), jnp.flip(k, 2), jnp.flip(v, 2), lg_b, S_b)
    o_ret = o_f + jnp.flip(o_b, 2)
    y_ret = _head_rmsnorm(o_ret, ret_norm_g).astype(dt) * jax.nn.silu(r_g)

    y_lat = jnp.concatenate([y_na, y_ret], axis=-1) @ w_out
    if not with_ctx_out:
        return y_lat, None

    o_cna = _dense_attention(_heads(cna_q, NA_HEADS), k_na_ctx, v_na_ctx)
    yc_na = _merge(o_cna) * jax.nn.silu(cna_g)
    q_ctx = _heads(cr_q, RET_HEADS)
    zeros = jnp.zeros(S_f.shape, jnp.float32)
    oc_f, _ = _retention_chunkwise(q_ctx, k_ctx, v_ctx, lg_f, zeros)
    oc_b, _ = _retention_chunkwise(jnp.flip(q_ctx, 2), jnp.flip(k_ctx, 2), jnp.flip(v_ctx, 2), lg_b, zeros)
    yc_ret = _head_rmsnorm(oc_f + jnp.flip(oc_b, 2), ret_norm_g).astype(dt) * jax.nn.silu(cr_g)
    y_ctx = jnp.concatenate([yc_na, yc_ret], axis=-1) @ w_out
    return y_lat, y_ctx


def setup_inputs(seed: int = 0) -> dict:
    key = jax.random.key(seed)
    ks = jax.random.split(key, 16)
    f32 = jnp.float32
    D = D_MODEL
    x = jax.random.normal(ks[0], (BATCH, SEQ, D), f32)
    c = jax.random.normal(ks[1], (BATCH, D), f32)
    ctx = jax.random.normal(ks[2], (BATCH, CTX_LEN, D), f32)
    c_ctx = jax.random.normal(ks[3], (D,), f32)
    norm_g = 1.0 + 0.02 * jax.random.normal(ks[4], (DEPTH, D), f32)
    w_ada = 0.5 * D ** -0.5 * jax.random.normal(ks[5], (DEPTH, D, 3 * D), f32)
    b_ada = 0.01 * jax.random.normal(ks[6], (DEPTH, 3 * D), f32)
    w_in = D ** -0.5 * jax.random.normal(ks[7], (DEPTH, D, IN_WIDTH), f32)
    na_rpb = 0.1 * jax.random.normal(ks[8], (DEPTH, NA_HEADS, 2 * NA_WIN_ROWS - 1, 2 * NA_WIN_COLS - 1), f32)
    gam = 1.0 - 2.0 ** (-5.0 - np.arange(RET_HEADS, dtype=np.float32))
    base = jnp.asarray(np.log(-np.log(gam)).astype(np.float32))
    ret_decay_fwd = base[None] + 0.05 * jax.random.normal(ks[9], (DEPTH, RET_HEADS), f32)
    ret_decay_bwd = base[None] + 0.05 * jax.random.normal(ks[10], (DEPTH, RET_HEADS), f32)
    ret_norm_g = 1.0 + 0.02 * jax.random.normal(ks[11], (DEPTH, RET_WIDTH), f32)
    w_out = MIX_WIDTH ** -0.5 * jax.random.normal(ks[12], (DEPTH, MIX_WIDTH, D), f32)
    final_norm_g = 1.0 + 0.02 * jax.random.normal(ks[13], (D,), f32)
    return {"x": x, "c": c, "ctx": ctx, "c_ctx": c_ctx, "norm_g": norm_g, "w_ada": w_ada,
            "b_ada": b_ada, "w_in": w_in, "na_rpb": na_rpb, "ret_decay_fwd": ret_decay_fwd,
            "ret_decay_bwd": ret_decay_bwd, "ret_norm_g": ret_norm_g, "w_out": w_out,
            "final_norm_g": final_norm_g}


def reference(x, c, ctx, c_ctx, norm_g, w_ada, b_ada, w_in, na_rpb, ret_decay_fwd,
              ret_decay_bwd, ret_norm_g, w_out, final_norm_g):
    D = D_MODEL
    for i in range(DEPTH):
        update_ctx = i < DEPTH - 1
        mod = jax.nn.silu(c) @ w_ada[i] + b_ada[i]
        mod_c = jax.nn.silu(c_ctx) @ w_ada[i] + b_ada[i]
        shift, scale, gate = mod[:, None, :D], mod[:, None, D:2 * D], mod[:, None, 2 * D:]
        shift_c, scale_c, gate_c = mod_c[:D], mod_c[D:2 * D], mod_c[2 * D:]
        h_lat = _rmsnorm(x, norm_g[i]) * (1.0 + scale) + shift
        h_ctx = _rmsnorm(ctx, norm_g[i]) * (1.0 + scale_c) + shift_c
        lg_f = -jnp.exp(ret_decay_fwd[i].astype(jnp.float32))
        lg_b = -jnp.exp(ret_decay_bwd[i].astype(jnp.float32))
        y_lat, y_ctx = _mixer(h_lat, h_ctx, w_in[i], na_rpb[i], lg_f, lg_b, ret_norm_g[i],
                              w_out[i], update_ctx)
        x = x + gate * y_lat
        if update_ctx:
            ctx = ctx + gate_c * y_ctx
    return _rmsnorm(x, final_norm_g)
```

```python
import functools

import jax
import jax.numpy as jnp
import numpy as np
from jax import lax
from jax.experimental import pallas as pl
from jax.experimental.pallas import tpu as pltpu

D_MODEL = 1024
GRID_W = 64
NA_HEAD_DIM = 64
NA_WIDTH = 512
NA_HEADS = 8
NA_WIN_ROWS = 8
NA_WIN_COLS = 16
RET_HEADS = 4
RET_WIDTH = 512
RET_DIM = 128
RET_CHUNK = 128
ROPE_BASE = 10000.0
EPS = 1e-6

LANES = 128
GROUP_COLS = 512
NA_Q_ROWS = 4
NA_K_ROWS = 12
MASK_VALUE = -1e30

VMEM_LIMIT = 56 * 1024 * 1024


def _silu(v):
    return v * (1.0 / (1.0 + jnp.exp(-v)))


def _mod_kernel(c_ref, w_ref, b_ref, o_ref):
    a = _silu(c_ref[...])
    o_ref[...] = jnp.dot(a, w_ref[...], preferred_element_type=jnp.float32,
                         precision=lax.Precision.HIGHEST) + b_ref[...]


def _adaln_mod(cc, w_ada, b_ada):
    m, d = cc.shape
    n = w_ada.shape[1]
    tn = 512
    return pl.pallas_call(
        _mod_kernel,
        out_shape=jax.ShapeDtypeStruct((m, n), jnp.float32),
        grid=(n // tn,),
        in_specs=[pl.BlockSpec((m, d), lambda j: (0, 0)),
                  pl.BlockSpec((d, tn), lambda j: (0, j)),
                  pl.BlockSpec((1, tn), lambda j: (0, j))],
        out_specs=pl.BlockSpec((m, tn), lambda j: (0, j)),
        compiler_params=pltpu.CompilerParams(
            dimension_semantics=("arbitrary",), vmem_limit_bytes=VMEM_LIMIT),
        name="adaln_mod",
    )(cc, w_ada, b_ada.reshape(1, n))


def _rotary(acc, cosf, sinf):
    outs = []
    for h in range(GROUP_COLS // LANES):
        xh = acc[:, h * LANES:(h + 1) * LANES]
        outs.append(xh * cosf + pltpu.roll(xh, LANES // 2, 1) * sinf)
    return jnp.concatenate(outs, axis=-1)


def _proj_kernel(x_ref, scale_ref, shift_ref, g_ref, w_ref, cos_ref, sin_ref, o_ref,
                 *, epilogues):
    x = x_ref[0]
    ms = jnp.mean(x * x, axis=-1, keepdims=True)
    h = x * lax.rsqrt(ms + EPS) * g_ref[...]
    h = h * (1.0 + scale_ref[0]) + shift_ref[0]
    hb = h.astype(jnp.bfloat16)
    for gi, kind in enumerate(epilogues):
        cols = slice(gi * GROUP_COLS, (gi + 1) * GROUP_COLS)
        acc = jnp.dot(hb, w_ref[:, cols], preferred_element_type=jnp.float32)
        if kind == "na_q":
            acc = acc * (NA_HEAD_DIM ** -0.5)
        elif kind == "silu":
            acc = _silu(acc)
        elif kind == "rot":
            acc = _rotary(acc, cos_ref[...], sin_ref[...])
        elif kind == "rot_kscale":
            acc = _rotary(acc, cos_ref[...], sin_ref[...]) * (RET_DIM ** -0.5)
        elif kind == "kscale":
            acc = acc * (RET_DIM ** -0.5)
        else:
            assert kind == "plain"
        o_ref[0, :, cols] = acc.astype(o_ref.dtype)


def _in_proj(x, scale, shift, norm_g, w, cosf, sinf, epilogues, tm):
    b, l, d = x.shape
    n = w.shape[1]
    assert n == GROUP_COLS * len(epilogues) and l % tm == 0
    per_batch = scale.shape[0] > 1
    mod_map = (lambda bi, i: (bi, 0, 0)) if per_batch else (lambda bi, i: (0, 0, 0))
    return pl.pallas_call(
        functools.partial(_proj_kernel, epilogues=tuple(epilogues)),
        out_shape=jax.ShapeDtypeStruct((b, l, n), jnp.bfloat16),
        grid=(b, l // tm),
        in_specs=[pl.BlockSpec((1, tm, d), lambda bi, i: (bi, i, 0)),
                  pl.BlockSpec((1, 1, d), mod_map),
                  pl.BlockSpec((1, 1, d), mod_map),
                  pl.BlockSpec((1, d), lambda bi, i: (0, 0)),
                  pl.BlockSpec((d, n), lambda bi, i: (0, 0)),
                  pl.BlockSpec((tm, LANES), lambda bi, i: (i, 0)),
                  pl.BlockSpec((tm, LANES), lambda bi, i: (i, 0))],
        out_specs=pl.BlockSpec((1, tm, n), lambda bi, i: (bi, i, 0)),
        compiler_params=pltpu.CompilerParams(
            dimension_semantics=("arbitrary", "arbitrary"), vmem_limit_bytes=VMEM_LIMIT),
        name="in_proj",
    )(x, scale, shift, norm_g.reshape(1, d), w, cosf, sinf)


def _na_geometry(rows):
    n_groups = rows // NA_Q_ROWS
    starts, variants = [], []
    for g in range(n_groups):
        starts.append(int(np.clip(NA_Q_ROWS * g - NA_WIN_ROWS // 2, 0, rows - NA_K_ROWS)))
        variants.append(0 if g == 0 else (2 if g == n_groups - 1 else 1))
    return starts, variants


def _na_bias_tables(rpb, rows):
    kr = min(NA_WIN_ROWS, rows)
    starts, variants = _na_geometry(rows)
    n_groups = len(starts)
    cq = np.arange(GRID_W)
    ck = np.arange(GRID_W)
    c0 = np.clip(cq - NA_WIN_COLS // 2, 0, GRID_W - NA_WIN_COLS)
    valid_c = (ck[None, :] >= c0[:, None]) & (ck[None, :] < c0[:, None] + NA_WIN_COLS)
    dc = np.clip(ck[None, :] - cq[:, None] + NA_WIN_COLS - 1, 0, 2 * NA_WIN_COLS - 2)
    tables = []
    for rep in (0, 1, n_groups - 1):
        rq = NA_Q_ROWS * rep + np.arange(NA_Q_ROWS)
        rk = starts[rep] + np.arange(NA_K_ROWS)
        r0 = np.clip(rq - kr // 2, 0, rows - kr)
        valid_r = (rk[None, :] >= r0[:, None]) & (rk[None, :] < r0[:, None] + kr)
        dr = np.clip(rk[None, :] - rq[:, None] + NA_WIN_ROWS - 1, 0, 2 * NA_WIN_ROWS - 2)
        vals = rpb[:, dr[:, None, :, None], dc[None, :, None, :]]
        valid = valid_r[:, None, :, None] & valid_c[None, :, None, :]
        vals = jnp.where(valid[None], vals.astype(jnp.float32), MASK_VALUE)
        tables.append(vals.reshape(NA_HEADS, NA_Q_ROWS * GRID_W, NA_K_ROWS * GRID_W))
    t = jnp.stack(tables, axis=1)
    t = t.reshape(NA_HEADS // 2, 2, 3, NA_Q_ROWS * GRID_W, NA_K_ROWS * GRID_W)
    return t.transpose(0, 2, 1, 3, 4)


def _na_kernel(q_ref, k_ref, v_ref, g_ref, kc_ref, vc_ref, bias_ref, o_ref, *, rows):
    starts, variants = _na_geometry(rows)
    tq = NA_Q_ROWS * GRID_W
    tk = NA_K_ROWS * GRID_W
    lane = lax.broadcasted_iota(jnp.int32, (1, LANES), 1)
    head0 = lane < NA_HEAD_DIM
    kc = kc_ref[0]
    vc = vc_ref[0]
    contract_last = (((1,), (1,)), ((), ()))
    for g, (ws, var) in enumerate(zip(starts, variants)):
        qg = q_ref[0, g * tq:(g + 1) * tq, :]
        kw = k_ref[0, ws * GRID_W:ws * GRID_W + tk, :]
        vw = v_ref[0, ws * GRID_W:ws * GRID_W + tk, :]
        outs = []
        for h in range(2):
            sel = head0 if h == 0 else jnp.logical_not(head0)
            qh = jnp.where(sel, qg, jnp.zeros_like(qg))
            s_loc = lax.dot_general(qh, kw, contract_last,
                                    preferred_element_type=jnp.float32) + bias_ref[0, var, h]
            s_ctx = lax.dot_general(qh, kc, contract_last, preferred_element_type=jnp.float32)
            m = jnp.maximum(jnp.max(s_loc, axis=-1, keepdims=True),
                            jnp.max(s_ctx, axis=-1, keepdims=True))
            p_loc = jnp.exp(s_loc - m)
            p_ctx = jnp.exp(s_ctx - m)
            denom = jnp.sum(p_loc, axis=-1, keepdims=True) + jnp.sum(p_ctx, axis=-1, keepdims=True)
            o = jnp.dot(p_loc.astype(jnp.bfloat16), vw, preferred_element_type=jnp.float32)
            o = o + jnp.dot(p_ctx.astype(jnp.bfloat16), vc, preferred_element_type=jnp.float32)
            outs.append(o / denom)
        o2 = jnp.where(head0, outs[0], outs[1])
        gate = g_ref[0, g * tq:(g + 1) * tq, :].astype(jnp.float32)
        o_ref[0, g * tq:(g + 1) * tq, :] = (o2 * gate).astype(o_ref.dtype)


def _na_attention(proj, proj_ctx, bias, rows):
    b, l, _ = proj.shape
    lc = proj_ctx.shape[1]
    n_pairs = NA_HEADS // 2
    blk = lambda off: pl.BlockSpec((1, l, LANES), lambda hp, bi, off=off: (bi, 0, off + hp))
    cblk = lambda off: pl.BlockSpec((1, lc, LANES), lambda hp, bi, off=off: (bi, 0, off + hp))
    return pl.pallas_call(
        functools.partial(_na_kernel, rows=rows),
        out_shape=jax.ShapeDtypeStruct((b, l, NA_WIDTH), jnp.bfloat16),
        grid=(n_pairs, b),
        in_specs=[blk(0), blk(n_pairs), blk(2 * n_pairs), blk(3 * n_pairs),
                  cblk(0), cblk(n_pairs),
                  pl.BlockSpec((1,) + bias.shape[1:], lambda hp, bi: (hp, 0, 0, 0, 0))],
        out_specs=pl.BlockSpec((1, l, LANES), lambda hp, bi: (bi, 0, hp)),
        compiler_params=pltpu.CompilerParams(
            dimension_semantics=("arbitrary", "arbitrary"), vmem_limit_bytes=VMEM_LIMIT),
        name="na_attn",
    )(proj, proj, proj, proj, proj_ctx, proj_ctx, bias)


def _ret_kernel(lg_ref, q_ref, k_ref, v_ref, g_ref, kc_ref, vc_ref, ng_ref, o_ref, acc_ref):
    h = pl.program_id(1)
    lg_f = lg_ref[0, h]
    lg_b = lg_ref[1, h]
    c_len = RET_CHUNK
    l = q_ref.shape[1]
    lc = kc_ref.shape[1]
    n_chunks = l // c_len
    contract_last = (((1,), (1,)), ((), ()))
    contract_first = (((0,), (0,)), ((), ()))

    ii = lax.broadcasted_iota(jnp.int32, (c_len, c_len), 0).astype(jnp.float32)
    jj = lax.broadcasted_iota(jnp.int32, (c_len, c_len), 1).astype(jnp.float32)
    dist = ii - jj
    decay = (jnp.where(dist >= 0, jnp.exp(lg_f * jnp.maximum(dist, 0.0)), 0.0)
             + jnp.where(dist <= 0, jnp.exp(lg_b * jnp.maximum(-dist, 0.0)), 0.0))
    ic = lax.broadcasted_iota(jnp.int32, (c_len, 1), 0).astype(jnp.float32)
    kdec_f = jnp.exp(lg_f * (c_len - 1 - ic))
    kdec_b = jnp.exp(lg_b * ic)
    qdec_f = jnp.exp(lg_f * (ic + 1.0))
    qdec_b = jnp.exp(lg_b * (c_len - ic))
    one = jnp.ones((1, 1), jnp.float32)
    cdec_f = jnp.exp(one * (lg_f * c_len))
    cdec_b = jnp.exp(one * (lg_b * c_len))

    jc = lax.broadcasted_iota(jnp.int32, (lc, 1), 0).astype(jnp.float32)
    kcf = kc_ref[0].astype(jnp.float32)
    vcx = vc_ref[0]
    s_f0 = lax.dot_general((kcf * jnp.exp(lg_f * (lc - 1 - jc))).astype(jnp.bfloat16), vcx,
                           contract_first, preferred_element_type=jnp.float32)
    s_b0 = lax.dot_general((kcf * jnp.exp(lg_b * jc)).astype(jnp.bfloat16), vcx,
                           contract_first, preferred_element_type=jnp.float32)

    def fwd(c, s_f):
        rows_c = pl.ds(pl.multiple_of(c * c_len, c_len), c_len)
        qc = q_ref[0, rows_c, :]
        kc = k_ref[0, rows_c, :]
        vc = v_ref[0, rows_c, :]
        s = lax.dot_general(qc, kc, contract_last, preferred_element_type=jnp.float32) * decay
        inner = jnp.dot(s.astype(jnp.bfloat16), vc, preferred_element_type=jnp.float32)
        cross = jnp.dot(qc, s_f.astype(jnp.bfloat16), preferred_element_type=jnp.float32) * qdec_f
        acc_ref[rows_c, :] = inner + cross
        t = lax.dot_general((kc.astype(jnp.float32) * kdec_f).astype(jnp.bfloat16), vc,
                            contract_first, preferred_element_type=jnp.float32)
        return cdec_f * s_f + t

    lax.fori_loop(0, n_chunks, fwd, s_f0)

    ng = ng_ref[...]

    def bwd(t_idx, s_b):
        c = n_chunks - 1 - t_idx
        rows_c = pl.ds(pl.multiple_of(c * c_len, c_len), c_len)
        qc = q_ref[0, rows_c, :]
        kc = k_ref[0, rows_c, :]
        vc = v_ref[0, rows_c, :]
        cross = jnp.dot(qc, s_b.astype(jnp.bfloat16), preferred_element_type=jnp.float32) * qdec_b
        o = acc_ref[rows_c, :] + cross
        o = o * lax.rsqrt(jnp.mean(o * o, axis=-1, keepdims=True) + EPS) * ng
        gate = g_ref[0, rows_c, :].astype(jnp.float32)
        o_ref[0, rows_c, :] = (o * gate).astype(o_ref.dtype)
        t = lax.dot_general((kc.astype(jnp.float32) * kdec_b).astype(jnp.bfloat16), vc,
                            contract_first, preferred_element_type=jnp.float32)
        return cdec_b * s_b + t

    lax.fori_loop(0, n_chunks, bwd, s_b0)


def _retention(lg, proj, proj_ctx, ret_norm_g):
    b, l, _ = proj.shape
    lc = proj_ctx.shape[1]
    base = 4 * NA_WIDTH // LANES
    cbase = 2 * NA_WIDTH // LANES
    blk = lambda off: pl.BlockSpec((1, l, LANES), lambda bi, h, lg_ref, off=off: (bi, 0, off + h))
    cblk = lambda off: pl.BlockSpec((1, lc, LANES), lambda bi, h, lg_ref, off=off: (bi, 0, off + h))
    grid_spec = pltpu.PrefetchScalarGridSpec(
        num_scalar_prefetch=1,
        grid=(b, RET_HEADS),
        in_specs=[blk(base), blk(base + RET_HEADS), blk(base + 2 * RET_HEADS), blk(base + 3 * RET_HEADS),
                  cblk(cbase), cblk(cbase + RET_HEADS),
                  pl.BlockSpec((1, LANES), lambda bi, h, lg_ref: (0, h))],
        out_specs=pl.BlockSpec((1, l, LANES), lambda bi, h, lg_ref: (bi, 0, h)),
        scratch_shapes=[pltpu.VMEM((l, LANES), jnp.float32)],
    )
    return pl.pallas_call(
        _ret_kernel,
        out_shape=jax.ShapeDtypeStruct((b, l, RET_WIDTH), jnp.bfloat16),
        grid_spec=grid_spec,
        compiler_params=pltpu.CompilerParams(
            dimension_semantics=("arbitrary", "arbitrary"), vmem_limit_bytes=VMEM_LIMIT),
        name="retention",
    )(lg, proj, proj, proj, proj, proj_ctx, proj_ctx, ret_norm_g.reshape(1, RET_WIDTH))


def _out_kernel(x_ref, gate_ref, yna_ref, yret_ref, wa_ref, wb_ref, fg_ref, o_ref):
    y = jnp.dot(yna_ref[0], wa_ref[...], preferred_element_type=jnp.float32)
    y = y + jnp.dot(yret_ref[0], wb_ref[...], preferred_element_type=jnp.float32)
    z = x_ref[0] + gate_ref[0] * y
    ms = jnp.mean(z * z, axis=-1, keepdims=True)
    o_ref[0] = z * lax.rsqrt(ms + EPS) * fg_ref[...]


def _out_proj(x, gate, y_na, y_ret, w_out, final_g, tm):
    b, l, d = x.shape
    wa = w_out[:NA_WIDTH]
    wb = w_out[NA_WIDTH:]
    return pl.pallas_call(
        _out_kernel,
        out_shape=jax.ShapeDtypeStruct((b, l, d), jnp.float32),
        grid=(b, l // tm),
        in_specs=[pl.BlockSpec((1, tm, d), lambda bi, i: (bi, i, 0)),
                  pl.BlockSpec((1, 1, d), lambda bi, i: (bi, 0, 0)),
                  pl.BlockSpec((1, tm, NA_WIDTH), lambda bi, i: (bi, i, 0)),
                  pl.BlockSpec((1, tm, RET_WIDTH), lambda bi, i: (bi, i, 0)),
                  pl.BlockSpec((NA_WIDTH, d), lambda bi, i: (0, 0)),
                  pl.BlockSpec((RET_WIDTH, d), lambda bi, i: (0, 0)),
                  pl.BlockSpec((1, d), lambda bi, i: (0, 0))],
        out_specs=pl.BlockSpec((1, tm, d), lambda bi, i: (bi, i, 0)),
        compiler_params=pltpu.CompilerParams(
            dimension_semantics=("arbitrary", "arbitrary"), vmem_limit_bytes=VMEM_LIMIT),
        name="out_proj",
    )(x, gate, y_na, y_ret, wa, wb, final_g.reshape(1, d))


def _rotary_tables(l):
    half = RET_DIM // 2
    nf = half // 2
    t = np.arange(l)
    row = (t // GRID_W).astype(np.float32)
    col = (t % GRID_W).astype(np.float32)
    inv = jnp.asarray(ROPE_BASE, jnp.float32) ** (-jnp.arange(nf, dtype=jnp.float32) / nf)
    ang = jnp.concatenate([jnp.asarray(row)[:, None] * inv, jnp.asarray(col)[:, None] * inv], axis=-1)
    cos, sin = jnp.cos(ang), jnp.sin(ang)
    return jnp.concatenate([cos, cos], axis=-1), jnp.concatenate([-sin, sin], axis=-1)


def kernel(x, c, ctx, c_ctx, norm_g, w_ada, b_ada, w_in, na_rpb, ret_decay_fwd, ret_decay_bwd,
           ret_norm_g, w_out, final_norm_g):
    depth = norm_g.shape[0]
    assert depth == 1, "context stream update between layers is not implemented"
    b, l, d = x.shape
    rows = l // GRID_W
    i = 0

    cc = jnp.concatenate([c, c_ctx[None, :]], axis=0)
    pad = (-cc.shape[0]) % 8
    cc = jnp.pad(cc, ((0, pad), (0, 0)))
    mod = _adaln_mod(cc, w_ada[i], b_ada[i])
    shift, scale, gate = (mod[:b, None, :d], mod[:b, None, d:2 * d], mod[:b, None, 2 * d:])
    shift_c, scale_c = mod[b:b + 1, None, :d], mod[b:b + 1, None, d:2 * d]

    w = w_in[i].astype(jnp.bfloat16)
    cosf, sinf = _rotary_tables(l)
    lat_epi = ("na_q", "plain", "plain", "silu", "rot", "rot_kscale", "plain", "silu")
    proj = _in_proj(x, scale, shift, norm_g[i], w, cosf, sinf, lat_epi, tm=512)

    gc = GROUP_COLS
    w_ctx = jnp.concatenate([w[:, 1 * gc:3 * gc], w[:, 5 * gc:7 * gc]], axis=1)
    lc = ctx.shape[1]
    ctx_epi = ("plain", "plain", "kscale", "plain")
    proj_ctx = _in_proj(ctx, scale_c, shift_c, norm_g[i], w_ctx, cosf[:lc], sinf[:lc], ctx_epi, tm=lc)

    bias = _na_bias_tables(na_rpb[i], rows)
    y_na = _na_attention(proj, proj_ctx, bias, rows)

    lg = jnp.stack([-jnp.exp(ret_decay_fwd[i].astype(jnp.float32)),
                    -jnp.exp(ret_decay_bwd[i].astype(jnp.float32))])
    y_ret = _retention(lg, proj, proj_ctx, ret_norm_g[i])

    return _out_proj(x, gate, y_na, y_ret, w_out[i].astype(jnp.bfloat16), final_norm_g, tm=512)
```

```python
import functools

import jax
import jax.numpy as jnp
import numpy as np
from jax import lax
from jax.experimental import pallas as pl
from jax.experimental.pallas import tpu as pltpu

D_MODEL = 1024
GRID_W = 64
NA_HEAD_DIM = 64
NA_WIDTH = 512
NA_HEADS = 8
NA_WIN_ROWS = 8
NA_WIN_COLS = 16
RET_HEADS = 4
RET_WIDTH = 512
RET_DIM = 128
RET_CHUNK = 128
ROPE_BASE = 10000.0
EPS = 1e-6

LANES = 128
GROUP_COLS = 512
NA_Q_ROWS = 4
NA_K_ROWS = 12
MASK_VALUE = -1e30

VMEM_LIMIT = 56 * 1024 * 1024


def _silu(v):
    return v * (1.0 / (1.0 + jnp.exp(-v)))


def _mod_kernel(c_ref, w_ref, b_ref, o_ref):
    a = _silu(c_ref[...])
    o_ref[...] = jnp.dot(a, w_ref[...], preferred_element_type=jnp.float32,
                         precision=lax.Precision.HIGHEST) + b_ref[...]


def _adaln_mod(cc, w_ada, b_ada):
    m, d = cc.shape
    n = w_ada.shape[1]
    tn = 512
    return pl.pallas_call(
        _mod_kernel,
        out_shape=jax.ShapeDtypeStruct((m, n), jnp.float32),
        grid=(n // tn,),
        in_specs=[pl.BlockSpec((m, d), lambda j: (0, 0)),
                  pl.BlockSpec((d, tn), lambda j: (0, j)),
                  pl.BlockSpec((1, tn), lambda j: (0, j))],
        out_specs=pl.BlockSpec((m, tn), lambda j: (0, j)),
        compiler_params=pltpu.CompilerParams(
            dimension_semantics=("arbitrary",), vmem_limit_bytes=VMEM_LIMIT),
        name="adaln_mod",
    )(cc, w_ada, b_ada.reshape(1, n))


def _rotary(acc, cosf, sinf):
    outs = []
    for h in range(GROUP_COLS // LANES):
        xh = acc[:, h * LANES:(h + 1) * LANES]
        outs.append(xh * cosf + pltpu.roll(xh, LANES // 2, 1) * sinf)
    return jnp.concatenate(outs, axis=-1)


def _proj_kernel(x_ref, scale_ref, shift_ref, g_ref, w_ref, cos_ref, sin_ref, o_ref,
                 *, epilogues):
    x = x_ref[0]
    ms = jnp.mean(x * x, axis=-1, keepdims=True)
    h = x * lax.rsqrt(ms + EPS) * g_ref[...]
    h = h * (1.0 + scale_ref[0]) + shift_ref[0]
    hb = h.astype(jnp.bfloat16)
    for gi, kind in enumerate(epilogues):
        cols = slice(gi * GROUP_COLS, (gi + 1) * GROUP_COLS)
        acc = jnp.dot(hb, w_ref[:, cols], preferred_element_type=jnp.float32)
        if kind == "na_q":
            acc = acc * (NA_HEAD_DIM ** -0.5)
        elif kind == "silu":
            acc = _silu(acc)
        elif kind == "rot":
            acc = _rotary(acc, cos_ref[...], sin_ref[...])
        elif kind == "rot_kscale":
            acc = _rotary(acc, cos_ref[...], sin_ref[...]) * (RET_DIM ** -0.5)
        elif kind == "kscale":
            acc = acc * (RET_DIM ** -0.5)
        else:
            assert kind == "plain"
        o_ref[0, :, cols] = acc.astype(o_ref.dtype)


def _in_proj(x, scale, shift, norm_g, w, cosf, sinf, epilogues, tm):
    b, l, d = x.shape
    n = w.shape[1]
    assert n == GROUP_COLS * len(epilogues) and l % tm == 0
    per_batch = scale.shape[0] > 1
    mod_map = (lambda bi, i: (bi, 0, 0)) if per_batch else (lambda bi, i: (0, 0, 0))
    return pl.pallas_call(
        functools.partial(_proj_kernel, epilogues=tuple(epilogues)),
        out_shape=jax.ShapeDtypeStruct((b, l, n), jnp.bfloat16),
        grid=(b, l // tm),
        in_specs=[pl.BlockSpec((1, tm, d), lambda bi, i: (bi, i, 0)),
                  pl.BlockSpec((1, 1, d), mod_map),
                  pl.BlockSpec((1, 1, d), mod_map),
                  pl.BlockSpec((1, d), lambda bi, i: (0, 0)),
                  pl.BlockSpec((d, n), lambda bi, i: (0, 0)),
                  pl.BlockSpec((tm, LANES), lambda bi, i: (i, 0)),
                  pl.BlockSpec((tm, LANES), lambda bi, i: (i, 0))],
        out_specs=pl.BlockSpec((1, tm, n), lambda bi, i: (bi, i, 0)),
        compiler_params=pltpu.CompilerParams(
            dimension_semantics=("arbitrary", "arbitrary"), vmem_limit_bytes=VMEM_LIMIT),
        name="in_proj",
    )(x, scale, shift, norm_g.reshape(1, d), w, cosf, sinf)


def _na_geometry(rows):
    n_groups = rows // NA_Q_ROWS
    starts, variants = [], []
    for g in range(n_groups):
        starts.append(int(np.clip(NA_Q_ROWS * g - NA_WIN_ROWS // 2, 0, rows - NA_K_ROWS)))
        variants.append(0 if g == 0 else (2 if g == n_groups - 1 else 1))
    return starts, variants


def _na_row_offsets(rows):
    kr = min(NA_WIN_ROWS, rows)
    starts, _ = _na_geometry(rows)
    n_groups = len(starts)
    table = []
    for rep in (0, 1, n_groups - 1):
        per_i = []
        for i in range(NA_Q_ROWS):
            rq = NA_Q_ROWS * rep + i
            r0 = int(np.clip(rq - kr // 2, 0, rows - kr))
            per_j = []
            for j in range(NA_K_ROWS):
                rk = starts[rep] + j
                per_j.append(rk - rq + NA_WIN_ROWS - 1 if r0 <= rk < r0 + kr else None)
            per_i.append(per_j)
        table.append(per_i)
    return table


def _na_build_bias(rpb_ref, hp, bias_ref, rows):
    n_dr = 2 * NA_WIN_ROWS - 1
    n_dc = 2 * NA_WIN_COLS - 1
    cq = lax.broadcasted_iota(jnp.int32, (GRID_W, LANES), 0)
    lane = lax.broadcasted_iota(jnp.int32, (GRID_W, LANES), 1)
    left = lane < GRID_W
    ck = jnp.where(left, lane, lane - GRID_W)
    c0 = jnp.clip(cq - NA_WIN_COLS // 2, 0, GRID_W - NA_WIN_COLS)
    valid_c = (ck >= c0) & (ck < c0 + NA_WIN_COLS)
    dc = jnp.clip(ck - cq + NA_WIN_COLS - 1, 0, n_dc - 1)

    offsets = _na_row_offsets(rows)
    users = {}
    for var in range(3):
        for i in range(NA_Q_ROWS):
            for jp in range(NA_K_ROWS // 2):
                key = (offsets[var][i][2 * jp], offsets[var][i][2 * jp + 1])
                users.setdefault(key, []).append((var, i, jp))

    for h in range(2):
        head = 2 * hp + h
        for (dr_l, dr_r), dests in users.items():
            if dr_l is None and dr_r is None:
                tile = jnp.full((GRID_W, LANES), MASK_VALUE, jnp.float32)
            else:
                def body(d, acc, dr_l=dr_l, dr_r=dr_r, head=head):
                    s_l = MASK_VALUE if dr_l is None else rpb_ref[(head * n_dr + dr_l) * n_dc + d]
                    s_r = MASK_VALUE if dr_r is None else rpb_ref[(head * n_dr + dr_r) * n_dc + d]
                    return jnp.where(dc == d, jnp.where(left, s_l, s_r), acc)
                acc = lax.fori_loop(0, n_dc, body, jnp.zeros((GRID_W, LANES), jnp.float32))
                valid = valid_c
                if dr_l is None:
                    valid = valid & jnp.logical_not(left)
                if dr_r is None:
                    valid = valid & left
                tile = jnp.where(valid, acc, MASK_VALUE)
            for var, i, jp in dests:
                bias_ref[var, h, i * GRID_W:(i + 1) * GRID_W, jp * LANES:(jp + 1) * LANES] = tile


def _na_kernel(rpb_ref, q_ref, k_ref, v_ref, g_ref, kc_ref, vc_ref, o_ref, bias_ref, *, rows):
    starts, variants = _na_geometry(rows)
    tq = NA_Q_ROWS * GRID_W
    tk = NA_K_ROWS * GRID_W

    @pl.when(pl.program_id(1) == 0)
    def _():
        _na_build_bias(rpb_ref, pl.program_id(0), bias_ref, rows)

    lane = lax.broadcasted_iota(jnp.int32, (1, LANES), 1)
    head0 = lane < NA_HEAD_DIM
    kc = kc_ref[0]
    vc = vc_ref[0]
    contract_last = (((1,), (1,)), ((), ()))
    for g, (ws, var) in enumerate(zip(starts, variants)):
        qg = q_ref[0, g * tq:(g + 1) * tq, :]
        kw = k_ref[0, ws * GRID_W:ws * GRID_W + tk, :]
        vw = v_ref[0, ws * GRID_W:ws * GRID_W + tk, :]
        outs = []
        for h in range(2):
            sel = head0 if h == 0 else jnp.logical_not(head0)
            qh = jnp.where(sel, qg, jnp.zeros_like(qg))
            s_loc = lax.dot_general(qh, kw, contract_last,
                                    preferred_element_type=jnp.float32) + bias_ref[var, h]
            s_ctx = lax.dot_general(qh, kc, contract_last, preferred_element_type=jnp.float32)
            m = jnp.maximum(jnp.max(s_loc, axis=-1, keepdims=True),
                            jnp.max(s_ctx, axis=-1, keepdims=True))
            p_loc = jnp.exp(s_loc - m)
            p_ctx = jnp.exp(s_ctx - m)
            denom = jnp.sum(p_loc, axis=-1, keepdims=True) + jnp.sum(p_ctx, axis=-1, keepdims=True)
            o = jnp.dot(p_loc.astype(jnp.bfloat16), vw, preferred_element_type=jnp.float32)
            o = o + jnp.dot(p_ctx.astype(jnp.bfloat16), vc, preferred_element_type=jnp.float32)
            outs.append(o / denom)
        o2 = jnp.where(head0, outs[0], outs[1])
        gate = g_ref[0, g * tq:(g + 1) * tq, :].astype(jnp.float32)
        o_ref[0, g * tq:(g + 1) * tq, :] = (o2 * gate).astype(o_ref.dtype)


def _na_attention(rpb, proj, proj_ctx, rows):
    b, l, _ = proj.shape
    lc = proj_ctx.shape[1]
    n_pairs = NA_HEADS // 2
    blk = lambda off: pl.BlockSpec((1, l, LANES), lambda hp, bi, rpb_ref, off=off: (bi, 0, off + hp))
    cblk = lambda off: pl.BlockSpec((1, lc, LANES), lambda hp, bi, rpb_ref, off=off: (bi, 0, off + hp))
    grid_spec = pltpu.PrefetchScalarGridSpec(
        num_scalar_prefetch=1,
        grid=(n_pairs, b),
        in_specs=[blk(0), blk(n_pairs), blk(2 * n_pairs), blk(3 * n_pairs), cblk(0), cblk(n_pairs)],
        out_specs=pl.BlockSpec((1, l, LANES), lambda hp, bi, rpb_ref: (bi, 0, hp)),
        scratch_shapes=[pltpu.VMEM((3, 2, NA_Q_ROWS * GRID_W, NA_K_ROWS * GRID_W), jnp.float32)],
    )
    return pl.pallas_call(
        functools.partial(_na_kernel, rows=rows),
        out_shape=jax.ShapeDtypeStruct((b, l, NA_WIDTH), jnp.bfloat16),
        grid_spec=grid_spec,
        compiler_params=pltpu.CompilerParams(
            dimension_semantics=("arbitrary", "arbitrary"), vmem_limit_bytes=VMEM_LIMIT),
        name="na_attn",
    )(rpb.astype(jnp.float32).reshape(-1), proj, proj, proj, proj, proj_ctx, proj_ctx)


def _ret_kernel(lg_ref, q_ref, k_ref, v_ref, g_ref, kc_ref, vc_ref, ng_ref, o_ref, acc_ref):
    h = pl.program_id(1)
    lg_f = lg_ref[0, h]
    lg_b = lg_ref[1, h]
    c_len = RET_CHUNK
    l = q_ref.shape[1]
    lc = kc_ref.shape[1]
    n_chunks = l // c_len
    contract_last = (((1,), (1,)), ((), ()))
    contract_first = (((0,), (0,)), ((), ()))

    ii = lax.broadcasted_iota(jnp.int32, (c_len, c_len), 0).astype(jnp.float32)
    jj = lax.broadcasted_iota(jnp.int32, (c_len, c_len), 1).astype(jnp.float32)
    dist = ii - jj
    decay = (jnp.where(dist >= 0, jnp.exp(lg_f * jnp.maximum(dist, 0.0)), 0.0)
             + jnp.where(dist <= 0, jnp.exp(lg_b * jnp.maximum(-dist, 0.0)), 0.0))
    ic = lax.broadcasted_iota(jnp.int32, (c_len, 1), 0).astype(jnp.float32)
    kdec_f = jnp.exp(lg_f * (c_len - 1 - ic))
    kdec_b = jnp.exp(lg_b * ic)
    qdec_f = jnp.exp(lg_f * (ic + 1.0))
    qdec_b = jnp.exp(lg_b * (c_len - ic))
    one = jnp.ones((1, 1), jnp.float32)
    cdec_f = jnp.exp(one * (lg_f * c_len))
    cdec_b = jnp.exp(one * (lg_b * c_len))

    jc = lax.broadcasted_iota(jnp.int32, (lc, 1), 0).astype(jnp.float32)
    kcf = kc_ref[0].astype(jnp.float32)
    vcx = vc_ref[0]
    s_f0 = lax.dot_general((kcf * jnp.exp(lg_f * (lc - 1 - jc))).astype(jnp.bfloat16), vcx,
                           contract_first, preferred_element_type=jnp.float32)
    s_b0 = lax.dot_general((kcf * jnp.exp(lg_b * jc)).astype(jnp.bfloat16), vcx,
                           contract_first, preferred_element_type=jnp.float32)

    def fwd(c, s_f):
        rows_c = pl.ds(pl.multiple_of(c * c_len, c_len), c_len)
        qc = q_ref[0, rows_c, :]
        kc = k_ref[0, rows_c, :]
        vc = v_ref[0, rows_c, :]
        s = lax.dot_general(qc, kc, contract_last, preferred_element_type=jnp.float32) * decay
        inner = jnp.dot(s.astype(jnp.bfloat16), vc, preferred_element_type=jnp.float32)
        cross = jnp.dot(qc, s_f.astype(jnp.bfloat16), preferred_element_type=jnp.float32) * qdec_f
        acc_ref[rows_c, :] = inner + cross
        t = lax.dot_general((kc.astype(jnp.float32) * kdec_f).astype(jnp.bfloat16), vc,
                            contract_first, preferred_element_type=jnp.float32)
        return cdec_f * s_f + t

    lax.fori_loop(0, n_chunks, fwd, s_f0)

    ng = ng_ref[...]

    def bwd(t_idx, s_b):
        c = n_chunks - 1 - t_idx
        rows_c = pl.ds(pl.multiple_of(c * c_len, c_len), c_len)
        qc = q_ref[0, rows_c, :]
        kc = k_ref[0, rows_c, :]
        vc = v_ref[0, rows_c, :]
        cross = jnp.dot(qc, s_b.astype(jnp.bfloat16), preferred_element_type=jnp.float32) * qdec_b
        o = acc_ref[rows_c, :] + cross
        o = o * lax.rsqrt(jnp.mean(o * o, axis=-1, keepdims=True) + EPS) * ng
        gate = g_ref[0, rows_c, :].astype(jnp.float32)
        o_ref[0, rows_c, :] = (o * gate).astype(o_ref.dtype)
        t = lax.dot_general((kc.astype(jnp.float32) * kdec_b).astype(jnp.bfloat16), vc,
                            contract_first, preferred_element_type=jnp.float32)
        return cdec_b * s_b + t

    lax.fori_loop(0, n_chunks, bwd, s_b0)


def _retention(lg, proj, proj_ctx, ret_norm_g):
    b, l, _ = proj.shape
    lc = proj_ctx.shape[1]
    base = 4 * NA_WIDTH // LANES
    cbase = 2 * NA_WIDTH // LANES
    blk = lambda off: pl.BlockSpec((1, l, LANES), lambda bi, h, lg_ref, off=off: (bi, 0, off + h))
    cblk = lambda off: pl.BlockSpec((1, lc, LANES), lambda bi, h, lg_ref, off=off: (bi, 0, off + h))
    grid_spec = pltpu.PrefetchScalarGridSpec(
        num_scalar_prefetch=1,
        grid=(b, RET_HEADS),
        in_specs=[blk(base), blk(base + RET_HEADS), blk(base + 2 * RET_HEADS), blk(base + 3 * RET_HEADS),
                  cblk(cbase), cblk(cbase + RET_HEADS),
                  pl.BlockSpec((1, LANES), lambda bi, h, lg_ref: (0, h))],
        out_specs=pl.BlockSpec((1, l, LANES), lambda bi, h, lg_ref: (bi, 0, h)),
        scratch_shapes=[pltpu.VMEM((l, LANES), jnp.float32)],
    )
    return pl.pallas_call(
        _ret_kernel,
        out_shape=jax.ShapeDtypeStruct((b, l, RET_WIDTH), jnp.bfloat16),
        grid_spec=grid_spec,
        compiler_params=pltpu.CompilerParams(
            dimension_semantics=("arbitrary", "arbitrary"), vmem_limit_bytes=VMEM_LIMIT),
        name="retention",
    )(lg, proj, proj, proj, proj, proj_ctx, proj_ctx, ret_norm_g.reshape(1, RET_WIDTH))


def _out_kernel(x_ref, gate_ref, yna_ref, yret_ref, wa_ref, wb_ref, fg_ref, o_ref):
    y = jnp.dot(yna_ref[0], wa_ref[...], preferred_element_type=jnp.float32)
    y = y + jnp.dot(yret_ref[0], wb_ref[...], preferred_element_type=jnp.float32)
    z = x_ref[0] + gate_ref[0] * y
    ms = jnp.mean(z * z, axis=-1, keepdims=True)
    o_ref[0] = z * lax.rsqrt(ms + EPS) * fg_ref[...]


def _out_proj(x, gate, y_na, y_ret, w_out, final_g, tm):
    b, l, d = x.shape
    wa = w_out[:NA_WIDTH]
    wb = w_out[NA_WIDTH:]
    return pl.pallas_call(
        _out_kernel,
        out_shape=jax.ShapeDtypeStruct((b, l, d), jnp.float32),
        grid=(b, l // tm),
        in_specs=[pl.BlockSpec((1, tm, d), lambda bi, i: (bi, i, 0)),
                  pl.BlockSpec((1, 1, d), lambda bi, i: (bi, 0, 0)),
                  pl.BlockSpec((1, tm, NA_WIDTH), lambda bi, i: (bi, i, 0)),
                  pl.BlockSpec((1, tm, RET_WIDTH), lambda bi, i: (bi, i, 0)),
                  pl.BlockSpec((NA_WIDTH, d), lambda bi, i: (0, 0)),
                  pl.BlockSpec((RET_WIDTH, d), lambda bi, i: (0, 0)),
                  pl.BlockSpec((1, d), lambda bi, i: (0, 0))],
        out_specs=pl.BlockSpec((1, tm, d), lambda bi, i: (bi, i, 0)),
        compiler_params=pltpu.CompilerParams(
            dimension_semantics=("arbitrary", "arbitrary"), vmem_limit_bytes=VMEM_LIMIT),
        name="out_proj",
    )(x, gate, y_na, y_ret, wa, wb, final_g.reshape(1, d))


def _rotary_tables(l):
    half = RET_DIM // 2
    nf = half // 2
    t = np.arange(l)
    row = (t // GRID_W).astype(np.float32)
    col = (t % GRID_W).astype(np.float32)
    inv = jnp.asarray(ROPE_BASE, jnp.float32) ** (-jnp.arange(nf, dtype=jnp.float32) / nf)
    ang = jnp.concatenate([jnp.asarray(row)[:, None] * inv, jnp.asarray(col)[:, None] * inv], axis=-1)
    cos, sin = jnp.cos(ang), jnp.sin(ang)
    return jnp.concatenate([cos, cos], axis=-1), jnp.concatenate([-sin, sin], axis=-1)


def kernel(x, c, ctx, c_ctx, norm_g, w_ada, b_ada, w_in, na_rpb, ret_decay_fwd, ret_decay_bwd,
           ret_norm_g, w_out, final_norm_g):
    depth = norm_g.shape[0]
    assert depth == 1, "context stream update between layers is not implemented"
    b, l, d = x.shape
    rows = l // GRID_W
    i = 0

    cc = jnp.concatenate([c, c_ctx[None, :]], axis=0)
    pad = (-cc.shape[0]) % 8
    cc = jnp.pad(cc, ((0, pad), (0, 0)))
    mod = _adaln_mod(cc, w_ada[i], b_ada[i])
    shift, scale, gate = (mod[:b, None, :d], mod[:b, None, d:2 * d], mod[:b, None, 2 * d:])
    shift_c, scale_c = mod[b:b + 1, None, :d], mod[b:b + 1, None, d:2 * d]

    w = w_in[i].astype(jnp.bfloat16)
    cosf, sinf = _rotary_tables(l)
    lat_epi = ("na_q", "plain", "plain", "silu", "rot", "rot_kscale", "plain", "silu")
    proj = _in_proj(x, scale, shift, norm_g[i], w, cosf, sinf, lat_epi, tm=512)

    gc = GROUP_COLS
    w_ctx = jnp.concatenate([w[:, 1 * gc:3 * gc], w[:, 5 * gc:7 * gc]], axis=1)
    lc = ctx.shape[1]
    ctx_epi = ("plain", "plain", "kscale", "plain")
    proj_ctx = _in_proj(ctx, scale_c, shift_c, norm_g[i], w_ctx, cosf[:lc], sinf[:lc], ctx_epi, tm=lc)

    y_na = _na_attention(na_rpb[i], proj, proj_ctx, rows)

    lg = jnp.stack([-jnp.exp(ret_decay_fwd[i].astype(jnp.float32)),
                    -jnp.exp(ret_decay_bwd[i].astype(jnp.float32))])
    y_ret = _retention(lg, proj, proj_ctx, ret_norm_g[i])

    return _out_proj(x, gate, y_na, y_ret, w_out[i].astype(jnp.bfloat16), final_norm_g, tm=512)
```

```python
import functools

import jax
import jax.numpy as jnp
import numpy as np
from jax import lax
from jax.experimental import pallas as pl
from jax.experimental.pallas import tpu as pltpu

D_MODEL = 1024
GRID_W = 64
NA_HEAD_DIM = 64
NA_WIDTH = 512
NA_HEADS = 8
NA_WIN_ROWS = 8
NA_WIN_COLS = 16
RET_HEADS = 4
RET_WIDTH = 512
RET_DIM = 128
RET_CHUNK = 128
ROPE_BASE = 10000.0
EPS = 1e-6

LANES = 128
GROUP_COLS = 512
NA_Q_ROWS = 4
NA_K_ROWS = 12
MASK_VALUE = -1e30

VMEM_LIMIT = 56 * 1024 * 1024


def _silu(v):
    return v * (1.0 / (1.0 + jnp.exp(-v)))


def _mod_kernel(c_ref, w_ref, b_ref, o_ref):
    a = _silu(c_ref[...])
    o_ref[...] = jnp.dot(a, w_ref[...], preferred_element_type=jnp.float32,
                         precision=lax.Precision.HIGHEST) + b_ref[...]


def _adaln_mod(cc, w_ada, b_ada):
    m, d = cc.shape
    n = w_ada.shape[1]
    tn = 512
    return pl.pallas_call(
        _mod_kernel,
        out_shape=jax.ShapeDtypeStruct((m, n), jnp.float32),
        grid=(n // tn,),
        in_specs=[pl.BlockSpec((m, d), lambda j: (0, 0)),
                  pl.BlockSpec((d, tn), lambda j: (0, j)),
                  pl.BlockSpec((1, tn), lambda j: (0, j))],
        out_specs=pl.BlockSpec((m, tn), lambda j: (0, j)),
        compiler_params=pltpu.CompilerParams(
            dimension_semantics=("arbitrary",), vmem_limit_bytes=VMEM_LIMIT),
        name="adaln_mod",
    )(cc, w_ada, b_ada.reshape(1, n))


def _rotary(acc, cosf, sinf):
    outs = []
    for h in range(GROUP_COLS // LANES):
        xh = acc[:, h * LANES:(h + 1) * LANES]
        outs.append(xh * cosf + pltpu.roll(xh, LANES // 2, 1) * sinf)
    return jnp.concatenate(outs, axis=-1)


def _proj_kernel(x_ref, scale_ref, shift_ref, g_ref, w_ref, cos_ref, sin_ref, o_ref,
                 *, epilogues):
    x = x_ref[0]
    ms = jnp.mean(x * x, axis=-1, keepdims=True)
    h = x * lax.rsqrt(ms + EPS) * g_ref[...]
    h = h * (1.0 + scale_ref[0]) + shift_ref[0]
    hb = h.astype(jnp.bfloat16)
    for gi, kind in enumerate(epilogues):
        cols = slice(gi * GROUP_COLS, (gi + 1) * GROUP_COLS)
        acc = jnp.dot(hb, w_ref[:, cols], preferred_element_type=jnp.float32)
        if kind == "na_q":
            acc = acc * (NA_HEAD_DIM ** -0.5)
        elif kind == "silu":
            acc = _silu(acc)
        elif kind == "rot":
            acc = _rotary(acc, cos_ref[...], sin_ref[...])
        elif kind == "rot_kscale":
            acc = _rotary(acc, cos_ref[...], sin_ref[...]) * (RET_DIM ** -0.5)
        elif kind == "kscale":
            acc = acc * (RET_DIM ** -0.5)
        else:
            assert kind == "plain"
        o_ref[0, :, cols] = acc.astype(o_ref.dtype)


def _in_proj(x, scale, shift, norm_g, w, cosf, sinf, epilogues, tm):
    b, l, d = x.shape
    n = w.shape[1]
    assert n == GROUP_COLS * len(epilogues) and l % tm == 0
    per_batch = scale.shape[0] > 1
    mod_map = (lambda bi, i: (bi, 0, 0)) if per_batch else (lambda bi, i: (0, 0, 0))
    return pl.pallas_call(
        functools.partial(_proj_kernel, epilogues=tuple(epilogues)),
        out_shape=jax.ShapeDtypeStruct((b, l, n), jnp.bfloat16),
        grid=(b, l // tm),
        in_specs=[pl.BlockSpec((1, tm, d), lambda bi, i: (bi, i, 0)),
                  pl.BlockSpec((1, 1, d), mod_map),
                  pl.BlockSpec((1, 1, d), mod_map),
                  pl.BlockSpec((1, d), lambda bi, i: (0, 0)),
                  pl.BlockSpec((d, n), lambda bi, i: (0, 0)),
                  pl.BlockSpec((tm, LANES), lambda bi, i: (i, 0)),
                  pl.BlockSpec((tm, LANES), lambda bi, i: (i, 0))],
        out_specs=pl.BlockSpec((1, tm, n), lambda bi, i: (bi, i, 0)),
        compiler_params=pltpu.CompilerParams(
            dimension_semantics=("arbitrary", "arbitrary"), vmem_limit_bytes=VMEM_LIMIT),
        name="in_proj",
    )(x, scale, shift, norm_g.reshape(1, d), w, cosf, sinf)


def _na_geometry(rows):
    n_groups = rows // NA_Q_ROWS
    starts, variants = [], []
    for g in range(n_groups):
        starts.append(int(np.clip(NA_Q_ROWS * g - NA_WIN_ROWS // 2, 0, rows - NA_K_ROWS)))
        variants.append(0 if g == 0 else (2 if g == n_groups - 1 else 1))
    return starts, variants


def _na_row_offsets(rows):
    kr = min(NA_WIN_ROWS, rows)
    starts, _ = _na_geometry(rows)
    n_groups = len(starts)
    table = []
    for rep in (0, 1, n_groups - 1):
        per_i = []
        for i in range(NA_Q_ROWS):
            rq = NA_Q_ROWS * rep + i
            r0 = int(np.clip(rq - kr // 2, 0, rows - kr))
            per_j = []
            for j in range(NA_K_ROWS):
                rk = starts[rep] + j
                per_j.append(rk - rq + NA_WIN_ROWS - 1 if r0 <= rk < r0 + kr else None)
            per_i.append(per_j)
        table.append(per_i)
    return table


def _na_build_bias(rpb_ref, hp, bias_ref, rows):
    n_dr = 2 * NA_WIN_ROWS - 1
    n_dc = 2 * NA_WIN_COLS - 1
    cq = lax.broadcasted_iota(jnp.int32, (GRID_W, LANES), 0)
    lane = lax.broadcasted_iota(jnp.int32, (GRID_W, LANES), 1)
    left = lane < GRID_W
    ck = jnp.where(left, lane, lane - GRID_W)
    c0 = jnp.clip(cq - NA_WIN_COLS // 2, 0, GRID_W - NA_WIN_COLS)
    valid_c = (ck >= c0) & (ck < c0 + NA_WIN_COLS)
    dc = jnp.clip(ck - cq + NA_WIN_COLS - 1, 0, n_dc - 1)

    offsets = _na_row_offsets(rows)
    users = {}
    for var in range(3):
        for i in range(NA_Q_ROWS):
            for jp in range(NA_K_ROWS // 2):
                key = (offsets[var][i][2 * jp], offsets[var][i][2 * jp + 1])
                users.setdefault(key, []).append((var, i, jp))

    for h in range(2):
        head = 2 * hp + h
        for (dr_l, dr_r), dests in users.items():
            if dr_l is None and dr_r is None:
                tile = jnp.full((GRID_W, LANES), MASK_VALUE, jnp.float32)
            else:
                def body(d, acc, dr_l=dr_l, dr_r=dr_r, head=head):
                    s_l = MASK_VALUE if dr_l is None else rpb_ref[(head * n_dr + dr_l) * n_dc + d]
                    s_r = MASK_VALUE if dr_r is None else rpb_ref[(head * n_dr + dr_r) * n_dc + d]
                    return jnp.where(dc == d, jnp.where(left, s_l, s_r), acc)
                acc = lax.fori_loop(0, n_dc, body, jnp.zeros((GRID_W, LANES), jnp.float32))
                valid = valid_c
                if dr_l is None:
                    valid = valid & jnp.logical_not(left)
                if dr_r is None:
                    valid = valid & left
                tile = jnp.where(valid, acc, MASK_VALUE)
            for var, i, jp in dests:
                bias_ref[var, h, i * GRID_W:(i + 1) * GRID_W, jp * LANES:(jp + 1) * LANES] = tile


def _na_kernel(rpb_ref, q_ref, k_ref, v_ref, g_ref, kc_ref, vc_ref, o_ref, bias_ref, *, rows):
    starts, variants = _na_geometry(rows)
    tq = NA_Q_ROWS * GRID_W
    tk = NA_K_ROWS * GRID_W

    @pl.when(pl.program_id(1) == 0)
    def _():
        _na_build_bias(rpb_ref, pl.program_id(0), bias_ref, rows)

    lane = lax.broadcasted_iota(jnp.int32, (1, LANES), 1)
    head0 = lane < NA_HEAD_DIM
    kc = kc_ref[0]
    vc = vc_ref[0]
    contract_last = (((1,), (1,)), ((), ()))
    for g, (ws, var) in enumerate(zip(starts, variants)):
        qg = q_ref[0, g * tq:(g + 1) * tq, :]
        kw = k_ref[0, ws * GRID_W:ws * GRID_W + tk, :]
        vw = v_ref[0, ws * GRID_W:ws * GRID_W + tk, :]
        outs = []
        for h in range(2):
            sel = head0 if h == 0 else jnp.logical_not(head0)
            qh = jnp.where(sel, qg, jnp.zeros_like(qg))
            s_loc = lax.dot_general(qh, kw, contract_last,
                                    preferred_element_type=jnp.float32) + bias_ref[var, h]
            s_ctx = lax.dot_general(qh, kc, contract_last, preferred_element_type=jnp.float32)
            m = jnp.maximum(jnp.max(s_loc, axis=-1, keepdims=True),
                            jnp.max(s_ctx, axis=-1, keepdims=True))
            p_loc = jnp.exp(s_loc - m)
            p_ctx = jnp.exp(s_ctx - m)
            denom = jnp.sum(p_loc, axis=-1, keepdims=True) + jnp.sum(p_ctx, axis=-1, keepdims=True)
            o = jnp.dot(p_loc.astype(jnp.bfloat16), vw, preferred_element_type=jnp.float32)
            o = o + jnp.dot(p_ctx.astype(jnp.bfloat16), vc, preferred_element_type=jnp.float32)
            outs.append(o / denom)
        o2 = jnp.where(head0, outs[0], outs[1])
        gate = g_ref[0, g * tq:(g + 1) * tq, :].astype(jnp.float32)
        o_ref[0, g * tq:(g + 1) * tq, :] = (o2 * gate).astype(o_ref.dtype)


def _na_attention(rpb, proj, proj_ctx, rows):
    b, l, _ = proj.shape
    lc = proj_ctx.shape[1]
    n_pairs = NA_HEADS // 2
    blk = lambda off: pl.BlockSpec((1, l, LANES), lambda hp, bi, rpb_ref, off=off: (bi, 0, off + hp))
    cblk = lambda off: pl.BlockSpec((1, lc, LANES), lambda hp, bi, rpb_ref, off=off: (bi, 0, off + hp))
    grid_spec = pltpu.PrefetchScalarGridSpec(
        num_scalar_prefetch=1,
        grid=(n_pairs, b),
        in_specs=[blk(0), blk(n_pairs), blk(2 * n_pairs), blk(3 * n_pairs), cblk(0), cblk(n_pairs)],
        out_specs=pl.BlockSpec((1, l, LANES), lambda hp, bi, rpb_ref: (bi, 0, hp)),
        scratch_shapes=[pltpu.VMEM((3, 2, NA_Q_ROWS * GRID_W, NA_K_ROWS * GRID_W), jnp.float32)],
    )
    return pl.pallas_call(
        functools.partial(_na_kernel, rows=rows),
        out_shape=jax.ShapeDtypeStruct((b, l, NA_WIDTH), jnp.bfloat16),
        grid_spec=grid_spec,
        compiler_params=pltpu.CompilerParams(
            dimension_semantics=("arbitrary", "arbitrary"), vmem_limit_bytes=VMEM_LIMIT),
        name="na_attn",
    )(rpb.astype(jnp.float32).reshape(-1), proj, proj, proj, proj, proj_ctx, proj_ctx)


def _ret_kernel(lg_ref, q_ref, k_ref, v_ref, g_ref, kc_ref, vc_ref, ng_ref, o_ref,
                tf_ref, tb_ref, sf_ref, sb_ref):
    h = pl.program_id(1)
    lg_f = lg_ref[0, h]
    lg_b = lg_ref[1, h]
    c_len = RET_CHUNK
    l = q_ref.shape[1]
    lc = kc_ref.shape[1]
    n_chunks = l // c_len
    contract_last = (((1,), (1,)), ((), ()))

    ii = lax.broadcasted_iota(jnp.int32, (c_len, c_len), 0).astype(jnp.float32)
    jj = lax.broadcasted_iota(jnp.int32, (c_len, c_len), 1).astype(jnp.float32)
    dist = ii - jj
    decay = (jnp.where(dist >= 0, jnp.exp(lg_f * jnp.maximum(dist, 0.0)), 0.0)
             + jnp.where(dist <= 0, jnp.exp(lg_b * jnp.maximum(-dist, 0.0)), 0.0))
    ic = lax.broadcasted_iota(jnp.int32, (c_len, 1), 0).astype(jnp.float32)
    jr = lax.broadcasted_iota(jnp.int32, (1, c_len), 1).astype(jnp.float32)
    kdec_f = jnp.exp(lg_f * (c_len - 1 - jr))
    kdec_b = jnp.exp(lg_b * jr)
    qdec_f = jnp.exp(lg_f * (ic + 1.0))
    qdec_b = jnp.exp(lg_b * (c_len - ic))
    one = jnp.ones((1, 1), jnp.float32)
    cdec_f = jnp.exp(one * (lg_f * c_len))
    cdec_b = jnp.exp(one * (lg_b * c_len))

    jcr = lax.broadcasted_iota(jnp.int32, (1, lc), 1).astype(jnp.float32)
    kct = kc_ref[0].astype(jnp.float32).T
    vcx = vc_ref[0]
    s_f = jnp.dot((kct * jnp.exp(lg_f * (lc - 1 - jcr))).astype(jnp.bfloat16), vcx,
                  preferred_element_type=jnp.float32)
    s_b = jnp.dot((kct * jnp.exp(lg_b * jcr)).astype(jnp.bfloat16), vcx,
                  preferred_element_type=jnp.float32)

    for c in range(n_chunks):
        rows_c = slice(c * c_len, (c + 1) * c_len)
        kt = k_ref[0, rows_c, :].astype(jnp.float32).T
        vc = v_ref[0, rows_c, :]
        tf_ref[c] = jnp.dot((kt * kdec_f).astype(jnp.bfloat16), vc, preferred_element_type=jnp.float32)
        tb_ref[c] = jnp.dot((kt * kdec_b).astype(jnp.bfloat16), vc, preferred_element_type=jnp.float32)

    for c in range(n_chunks):
        sf_ref[c] = s_f.astype(jnp.bfloat16)
        s_f = cdec_f * s_f + tf_ref[c]
    for c in reversed(range(n_chunks)):
        sb_ref[c] = s_b.astype(jnp.bfloat16)
        s_b = cdec_b * s_b + tb_ref[c]

    ng = ng_ref[...]
    for c in range(n_chunks):
        rows_c = slice(c * c_len, (c + 1) * c_len)
        qc = q_ref[0, rows_c, :]
        kc = k_ref[0, rows_c, :]
        vc = v_ref[0, rows_c, :]
        s = lax.dot_general(qc, kc, contract_last, preferred_element_type=jnp.float32) * decay
        o = jnp.dot(s.astype(jnp.bfloat16), vc, preferred_element_type=jnp.float32)
        o = o + jnp.dot(qc, sf_ref[c], preferred_element_type=jnp.float32) * qdec_f
        o = o + jnp.dot(qc, sb_ref[c], preferred_element_type=jnp.float32) * qdec_b
        o = o * lax.rsqrt(jnp.mean(o * o, axis=-1, keepdims=True) + EPS) * ng
        gate = g_ref[0, rows_c, :].astype(jnp.float32)
        o_ref[0, rows_c, :] = (o * gate).astype(o_ref.dtype)


def _retention(lg, proj, proj_ctx, ret_norm_g):
    b, l, _ = proj.shape
    lc = proj_ctx.shape[1]
    base = 4 * NA_WIDTH // LANES
    cbase = 2 * NA_WIDTH // LANES
    blk = lambda off: pl.BlockSpec((1, l, LANES), lambda bi, h, lg_ref, off=off: (bi, 0, off + h))
    cblk = lambda off: pl.BlockSpec((1, lc, LANES), lambda bi, h, lg_ref, off=off: (bi, 0, off + h))
    grid_spec = pltpu.PrefetchScalarGridSpec(
        num_scalar_prefetch=1,
        grid=(b, RET_HEADS),
        in_specs=[blk(base), blk(base + RET_HEADS), blk(base + 2 * RET_HEADS), blk(base + 3 * RET_HEADS),
                  cblk(cbase), cblk(cbase + RET_HEADS),
                  pl.BlockSpec((1, LANES), lambda bi, h, lg_ref: (0, h))],
        out_specs=pl.BlockSpec((1, l, LANES), lambda bi, h, lg_ref: (bi, 0, h)),
        scratch_shapes=[pltpu.VMEM((l // RET_CHUNK, RET_DIM, RET_DIM), jnp.float32),
                        pltpu.VMEM((l // RET_CHUNK, RET_DIM, RET_DIM), jnp.float32),
                        pltpu.VMEM((l // RET_CHUNK, RET_DIM, RET_DIM), jnp.bfloat16),
                        pltpu.VMEM((l // RET_CHUNK, RET_DIM, RET_DIM), jnp.bfloat16)],
    )
    return pl.pallas_call(
        _ret_kernel,
        out_shape=jax.ShapeDtypeStruct((b, l, RET_WIDTH), jnp.bfloat16),
        grid_spec=grid_spec,
        compiler_params=pltpu.CompilerParams(
            dimension_semantics=("arbitrary", "arbitrary"), vmem_limit_bytes=VMEM_LIMIT),
        name="retention",
    )(lg, proj, proj, proj, proj, proj_ctx, proj_ctx, ret_norm_g.reshape(1, RET_WIDTH))


def _out_kernel(x_ref, gate_ref, yna_ref, yret_ref, wa_ref, wb_ref, fg_ref, o_ref):
    y = jnp.dot(yna_ref[0], wa_ref[...], preferred_element_type=jnp.float32)
    y = y + jnp.dot(yret_ref[0], wb_ref[...], preferred_element_type=jnp.float32)
    z = x_ref[0] + gate_ref[0] * y
    ms = jnp.mean(z * z, axis=-1, keepdims=True)
    o_ref[0] = z * lax.rsqrt(ms + EPS) * fg_ref[...]


def _out_proj(x, gate, y_na, y_ret, w_out, final_g, tm):
    b, l, d = x.shape
    wa = w_out[:NA_WIDTH]
    wb = w_out[NA_WIDTH:]
    return pl.pallas_call(
        _out_kernel,
        out_shape=jax.ShapeDtypeStruct((b, l, d), jnp.float32),
        grid=(b, l // tm),
        in_specs=[pl.BlockSpec((1, tm, d), lambda bi, i: (bi, i, 0)),
                  pl.BlockSpec((1, 1, d), lambda bi, i: (bi, 0, 0)),
                  pl.BlockSpec((1, tm, NA_WIDTH), lambda bi, i: (bi, i, 0)),
                  pl.BlockSpec((1, tm, RET_WIDTH), lambda bi, i: (bi, i, 0)),
                  pl.BlockSpec((NA_WIDTH, d), lambda bi, i: (0, 0)),
                  pl.BlockSpec((RET_WIDTH, d), lambda bi, i: (0, 0)),
                  pl.BlockSpec((1, d), lambda bi, i: (0, 0))],
        out_specs=pl.BlockSpec((1, tm, d), lambda bi, i: (bi, i, 0)),
        compiler_params=pltpu.CompilerParams(
            dimension_semantics=("arbitrary", "arbitrary"), vmem_limit_bytes=VMEM_LIMIT),
        name="out_proj",
    )(x, gate, y_na, y_ret, wa, wb, final_g.reshape(1, d))


def _rotary_tables(l):
    half = RET_DIM // 2
    nf = half // 2
    t = np.arange(l)
    row = (t // GRID_W).astype(np.float32)
    col = (t % GRID_W).astype(np.float32)
    inv = jnp.asarray(ROPE_BASE, jnp.float32) ** (-jnp.arange(nf, dtype=jnp.float32) / nf)
    ang = jnp.concatenate([jnp.asarray(row)[:, None] * inv, jnp.asarray(col)[:, None] * inv], axis=-1)
    cos, sin = jnp.cos(ang), jnp.sin(ang)
    return jnp.concatenate([cos, cos], axis=-1), jnp.concatenate([-sin, sin], axis=-1)


def kernel(x, c, ctx, c_ctx, norm_g, w_ada, b_ada, w_in, na_rpb, ret_decay_fwd, ret_decay_bwd,
           ret_norm_g, w_out, final_norm_g):
    depth = norm_g.shape[0]
    assert depth == 1, "context stream update between layers is not implemented"
    b, l, d = x.shape
    rows = l // GRID_W
    i = 0

    cc = jnp.concatenate([c, c_ctx[None, :]], axis=0)
    pad = (-cc.shape[0]) % 8
    cc = jnp.pad(cc, ((0, pad), (0, 0)))
    mod = _adaln_mod(cc, w_ada[i], b_ada[i])
    shift, scale, gate = (mod[:b, None, :d], mod[:b, None, d:2 * d], mod[:b, None, 2 * d:])
    shift_c, scale_c = mod[b:b + 1, None, :d], mod[b:b + 1, None, d:2 * d]

    w = w_in[i].astype(jnp.bfloat16)
    cosf, sinf = _rotary_tables(l)
    lat_epi = ("na_q", "plain", "plain", "silu", "rot", "rot_kscale", "plain", "silu")
    proj = _in_proj(x, scale, shift, norm_g[i], w, cosf, sinf, lat_epi, tm=512)

    gc = GROUP_COLS
    w_ctx = jnp.concatenate([w[:, 1 * gc:3 * gc], w[:, 5 * gc:7 * gc]], axis=1)
    lc = ctx.shape[1]
    ctx_epi = ("plain", "plain", "kscale", "plain")
    proj_ctx = _in_proj(ctx, scale_c, shift_c, norm_g[i], w_ctx, cosf[:lc], sinf[:lc], ctx_epi, tm=lc)

    y_na = _na_attention(na_rpb[i], proj, proj_ctx, rows)

    lg = jnp.stack([-jnp.exp(ret_decay_fwd[i].astype(jnp.float32)),
                    -jnp.exp(ret_decay_bwd[i].astype(jnp.float32))])
    y_ret = _retention(lg, proj, proj_ctx, ret_norm_g[i])

    return _out_proj(x, gate, y_na, y_ret, w_out[i].astype(jnp.bfloat16), final_norm_g, tm=512)
```

```python
import functools
import math

import jax
import jax.numpy as jnp
import numpy as np
from jax import lax
from jax.experimental import pallas as pl
from jax.experimental.pallas import tpu as pltpu

D_MODEL = 1024
GRID_W = 64
NA_HEAD_DIM = 64
NA_WIDTH = 512
NA_HEADS = 8
NA_WIN_ROWS = 8
NA_WIN_COLS = 16
RET_HEADS = 4
RET_WIDTH = 512
RET_DIM = 128
RET_CHUNK = 128
ROPE_BASE = 10000.0
EPS = 1e-6

LANES = 128
GROUP_COLS = 512
NA_Q_ROWS = 4
NA_K_ROWS = 12
MASK_VALUE = -1e30
LOG2E = math.log2(math.e)

VMEM_LIMIT = 56 * 1024 * 1024


def _silu(v):
    return v * (1.0 / (1.0 + jnp.exp(-v)))


def _mod_kernel(c_ref, w_ref, b_ref, o_ref):
    a = _silu(c_ref[...])
    o_ref[...] = jnp.dot(a, w_ref[...], preferred_element_type=jnp.float32,
                         precision=lax.Precision.HIGHEST) + b_ref[...]


def _adaln_mod(cc, w_ada, b_ada):
    m, d = cc.shape
    n = w_ada.shape[1]
    tn = 512
    return pl.pallas_call(
        _mod_kernel,
        out_shape=jax.ShapeDtypeStruct((m, n), jnp.float32),
        grid=(n // tn,),
        in_specs=[pl.BlockSpec((m, d), lambda j: (0, 0)),
                  pl.BlockSpec((d, tn), lambda j: (0, j)),
                  pl.BlockSpec((1, tn), lambda j: (0, j))],
        out_specs=pl.BlockSpec((m, tn), lambda j: (0, j)),
        compiler_params=pltpu.CompilerParams(
            dimension_semantics=("arbitrary",), vmem_limit_bytes=VMEM_LIMIT),
        name="adaln_mod",
    )(cc, w_ada, b_ada.reshape(1, n))


def _rotary(acc, cosf, sinf):
    outs = []
    for h in range(GROUP_COLS // LANES):
        xh = acc[:, h * LANES:(h + 1) * LANES]
        outs.append(xh * cosf + pltpu.roll(xh, LANES // 2, 1) * sinf)
    return jnp.concatenate(outs, axis=-1)


def _proj_kernel(x_ref, scale_ref, shift_ref, g_ref, w_ref, wvt_ref, cos_ref, sin_ref, o_ref, vt_ref,
                 *, epilogues):
    x = x_ref[0]
    ms = jnp.mean(x * x, axis=-1, keepdims=True)
    h = x * lax.rsqrt(ms + EPS) * g_ref[...]
    h = h * (1.0 + scale_ref[0]) + shift_ref[0]
    hb = h.astype(jnp.bfloat16)
    vt = lax.dot_general(wvt_ref[...], hb, (((1,), (1,)), ((), ())), preferred_element_type=jnp.float32)
    vt_ref[0] = vt.astype(vt_ref.dtype)
    for gi, kind in enumerate(epilogues):
        cols = slice(gi * GROUP_COLS, (gi + 1) * GROUP_COLS)
        acc = jnp.dot(hb, w_ref[:, cols], preferred_element_type=jnp.float32)
        if kind == "na_q":
            acc = acc * (NA_HEAD_DIM ** -0.5 * LOG2E)
        elif kind == "silu":
            acc = _silu(acc)
        elif kind == "rot":
            acc = _rotary(acc, cos_ref[...], sin_ref[...])
        elif kind == "rot_kscale":
            acc = _rotary(acc, cos_ref[...], sin_ref[...]) * (RET_DIM ** -0.5)
        elif kind == "kscale":
            acc = acc * (RET_DIM ** -0.5)
        else:
            assert kind == "plain"
        o_ref[0, :, cols] = acc.astype(o_ref.dtype)


def _in_proj(x, scale, shift, norm_g, w, w_vt, cosf, sinf, epilogues, tm):
    b, l, d = x.shape
    n = w.shape[1]
    nv = w_vt.shape[0]
    assert n == GROUP_COLS * len(epilogues) and l % tm == 0
    per_batch = scale.shape[0] > 1
    mod_map = (lambda bi, i: (bi, 0, 0)) if per_batch else (lambda bi, i: (0, 0, 0))
    return pl.pallas_call(
        functools.partial(_proj_kernel, epilogues=tuple(epilogues)),
        out_shape=(jax.ShapeDtypeStruct((b, l, n), jnp.bfloat16),
                   jax.ShapeDtypeStruct((b, nv, l), jnp.bfloat16)),
        grid=(b, l // tm),
        in_specs=[pl.BlockSpec((1, tm, d), lambda bi, i: (bi, i, 0)),
                  pl.BlockSpec((1, 1, d), mod_map),
                  pl.BlockSpec((1, 1, d), mod_map),
                  pl.BlockSpec((1, d), lambda bi, i: (0, 0)),
                  pl.BlockSpec((d, n), lambda bi, i: (0, 0)),
                  pl.BlockSpec((nv, d), lambda bi, i: (0, 0)),
                  pl.BlockSpec((tm, LANES), lambda bi, i: (i, 0)),
                  pl.BlockSpec((tm, LANES), lambda bi, i: (i, 0))],
        out_specs=(pl.BlockSpec((1, tm, n), lambda bi, i: (bi, i, 0)),
                   pl.BlockSpec((1, nv, tm), lambda bi, i: (bi, 0, i))),
        compiler_params=pltpu.CompilerParams(
            dimension_semantics=("arbitrary", "arbitrary"), vmem_limit_bytes=VMEM_LIMIT),
        name="in_proj",
    )(x, scale, shift, norm_g.reshape(1, d), w, w_vt, cosf, sinf)


def _na_geometry(rows):
    n_groups = rows // NA_Q_ROWS
    starts, variants = [], []
    for g in range(n_groups):
        starts.append(int(np.clip(NA_Q_ROWS * g - NA_WIN_ROWS // 2, 0, rows - NA_K_ROWS)))
        variants.append(0 if g == 0 else (2 if g == n_groups - 1 else 1))
    return starts, variants


def _na_row_offsets(rows):
    kr = min(NA_WIN_ROWS, rows)
    starts, _ = _na_geometry(rows)
    n_groups = len(starts)
    table = []
    for rep in (0, 1, n_groups - 1):
        per_i = []
        for i in range(NA_Q_ROWS):
            rq = NA_Q_ROWS * rep + i
            r0 = int(np.clip(rq - kr // 2, 0, rows - kr))
            per_j = []
            for j in range(NA_K_ROWS):
                rk = starts[rep] + j
                per_j.append(rk - rq + NA_WIN_ROWS - 1 if r0 <= rk < r0 + kr else None)
            per_i.append(per_j)
        table.append(per_i)
    return table


def _na_build_bias(rpb_ref, hp, bias_ref, rows):
    n_dr = 2 * NA_WIN_ROWS - 1
    n_dc = 2 * NA_WIN_COLS - 1
    ck = lax.broadcasted_iota(jnp.int32, (GRID_W, LANES), 0)
    lane = lax.broadcasted_iota(jnp.int32, (GRID_W, LANES), 1)
    left = lane < GRID_W
    cq = jnp.where(left, lane, lane - GRID_W)
    c0 = jnp.clip(cq - NA_WIN_COLS // 2, 0, GRID_W - NA_WIN_COLS)
    valid_c = (ck >= c0) & (ck < c0 + NA_WIN_COLS)
    dc = jnp.clip(ck - cq + NA_WIN_COLS - 1, 0, n_dc - 1)

    offsets = _na_row_offsets(rows)
    users = {}
    for var in range(3):
        for j in range(NA_K_ROWS):
            for ip in range(NA_Q_ROWS // 2):
                key = (offsets[var][2 * ip][j], offsets[var][2 * ip + 1][j])
                users.setdefault(key, []).append((var, j, ip))

    for h in range(2):
        head = 2 * hp + h
        for (dr_l, dr_r), dests in users.items():
            if dr_l is None and dr_r is None:
                tile = jnp.full((GRID_W, LANES), MASK_VALUE, jnp.float32)
            else:
                def body(d, acc, dr_l=dr_l, dr_r=dr_r, head=head):
                    s_l = MASK_VALUE if dr_l is None else rpb_ref[(head * n_dr + dr_l) * n_dc + d]
                    s_r = MASK_VALUE if dr_r is None else rpb_ref[(head * n_dr + dr_r) * n_dc + d]
                    return jnp.where(dc == d, jnp.where(left, s_l, s_r), acc)
                acc = lax.fori_loop(0, n_dc, body, jnp.zeros((GRID_W, LANES), jnp.float32))
                valid = valid_c
                if dr_l is None:
                    valid = valid & jnp.logical_not(left)
                if dr_r is None:
                    valid = valid & left
                tile = jnp.where(valid, acc * LOG2E, MASK_VALUE)
            for var, j, ip in dests:
                bias_ref[var, h, j * GRID_W:(j + 1) * GRID_W, ip * LANES:(ip + 1) * LANES] = tile


def _na_kernel(rpb_ref, q_ref, k_ref, vt_ref, g_ref, kc_ref, vct_ref, o_ref, bias_ref, vth_ref, vcth_ref,
               *, rows):
    starts, variants = _na_geometry(rows)
    tq = NA_Q_ROWS * GRID_W
    tk = NA_K_ROWS * GRID_W
    dh = NA_HEAD_DIM

    @pl.when(pl.program_id(1) == 0)
    def _():
        _na_build_bias(rpb_ref, pl.program_id(0), bias_ref, rows)

    lane = lax.broadcasted_iota(jnp.int32, (1, LANES), 1)
    head0_lanes = lane < dh
    ones_l = jnp.ones((dh, vt_ref.shape[2]), vt_ref.dtype)
    ones_c = jnp.ones((dh, vct_ref.shape[2]), vct_ref.dtype)
    vth_ref[0, :dh, :] = vt_ref[0, :dh, :]
    vth_ref[0, dh:, :] = ones_l
    vth_ref[1, :dh, :] = ones_l
    vth_ref[1, dh:, :] = vt_ref[0, dh:, :]
    vcth_ref[0, :dh, :] = vct_ref[0, :dh, :]
    vcth_ref[0, dh:, :] = ones_c
    vcth_ref[1, :dh, :] = ones_c
    vcth_ref[1, dh:, :] = vct_ref[0, dh:, :]

    kc = kc_ref[0]
    contract_last = (((1,), (1,)), ((), ()))
    tasks = [(g, h) for g in range(len(starts)) for h in range(2)]

    def scores(g, h):
        ws, var = starts[g], variants[g]
        qg = q_ref[0, g * tq:(g + 1) * tq, :]
        kw = k_ref[0, ws * GRID_W:ws * GRID_W + tk, :]
        sel = head0_lanes if h == 0 else jnp.logical_not(head0_lanes)
        qh = jnp.where(sel, qg, jnp.zeros_like(qg))
        s_loc = lax.dot_general(kw, qh, contract_last,
                                preferred_element_type=jnp.float32) + bias_ref[var, h]
        s_ctx = lax.dot_general(kc, qh, contract_last, preferred_element_type=jnp.float32)
        return s_loc, s_ctx

    def attend(g, h, s_loc, s_ctx):
        ws = starts[g]
        m = jnp.maximum(jnp.max(s_loc, axis=0, keepdims=True), jnp.max(s_ctx, axis=0, keepdims=True))
        p_loc = jnp.exp2(s_loc - m).astype(jnp.bfloat16)
        p_ctx = jnp.exp2(s_ctx - m).astype(jnp.bfloat16)
        ot = jnp.dot(vth_ref[h, :, ws * GRID_W:ws * GRID_W + tk], p_loc, preferred_element_type=jnp.float32)
        ot = ot + jnp.dot(vcth_ref[h], p_ctx, preferred_element_type=jnp.float32)
        return ot[:dh] / ot[dh:] if h == 0 else ot[dh:] / ot[:dh]

    halves = []
    pending = scores(*tasks[0])
    for t, (g, h) in enumerate(tasks):
        current = pending
        if t + 1 < len(tasks):
            pending = scores(*tasks[t + 1])
        halves.append(attend(g, h, *current))
        if h == 1:
            o2 = jnp.concatenate(halves, axis=0).T
            halves = []
            gate = g_ref[0, g * tq:(g + 1) * tq, :].astype(jnp.float32)
            o_ref[0, g * tq:(g + 1) * tq, :] = (o2 * gate).astype(o_ref.dtype)


def _na_attention(rpb, proj, vt, proj_ctx, vct, rows, q_blk, k_blk, g_blk, kc_blk):
    b, l, _ = proj.shape
    lc = proj_ctx.shape[1]
    n_pairs = NA_HEADS // 2
    blk = lambda off: pl.BlockSpec((1, l, LANES), lambda hp, bi, rpb_ref, off=off: (bi, 0, off + hp))
    grid_spec = pltpu.PrefetchScalarGridSpec(
        num_scalar_prefetch=1,
        grid=(n_pairs, b),
        in_specs=[blk(q_blk), blk(k_blk),
                  pl.BlockSpec((1, LANES, l), lambda hp, bi, rpb_ref: (bi, hp, 0)),
                  blk(g_blk),
                  pl.BlockSpec((1, lc, LANES), lambda hp, bi, rpb_ref: (bi, 0, kc_blk + hp)),
                  pl.BlockSpec((1, LANES, lc), lambda hp, bi, rpb_ref: (bi, hp, 0))],
        out_specs=pl.BlockSpec((1, l, LANES), lambda hp, bi, rpb_ref: (bi, 0, hp)),
        scratch_shapes=[pltpu.VMEM((3, 2, NA_K_ROWS * GRID_W, NA_Q_ROWS * GRID_W), jnp.float32),
                        pltpu.VMEM((2, LANES, l), jnp.bfloat16),
                        pltpu.VMEM((2, LANES, lc), jnp.bfloat16)],
    )
    return pl.pallas_call(
        functools.partial(_na_kernel, rows=rows),
        out_shape=jax.ShapeDtypeStruct((b, l, NA_WIDTH), jnp.bfloat16),
        grid_spec=grid_spec,
        compiler_params=pltpu.CompilerParams(
            dimension_semantics=("arbitrary", "arbitrary"), vmem_limit_bytes=VMEM_LIMIT),
        name="na_attn",
    )(rpb.astype(jnp.float32).reshape(-1), proj, proj, vt, proj, proj_ctx, vct)


def _ret_kernel(lg_ref, q_ref, k_ref, v_ref, g_ref, kc_ref, vc_ref, ng_ref, o_ref,
                tf_ref, tb_ref, sf_ref, sb_ref):
    h = pl.program_id(1)
    lg_f = lg_ref[0, h]
    lg_b = lg_ref[1, h]
    c_len = RET_CHUNK
    l = q_ref.shape[1]
    lc = kc_ref.shape[1]
    n_chunks = l // c_len
    contract_last = (((1,), (1,)), ((), ()))

    ii = lax.broadcasted_iota(jnp.int32, (c_len, c_len), 0).astype(jnp.float32)
    jj = lax.broadcasted_iota(jnp.int32, (c_len, c_len), 1).astype(jnp.float32)
    dist = ii - jj
    decay = (jnp.where(dist >= 0, jnp.exp(lg_f * jnp.maximum(dist, 0.0)), 0.0)
             + jnp.where(dist <= 0, jnp.exp(lg_b * jnp.maximum(-dist, 0.0)), 0.0))
    ic = lax.broadcasted_iota(jnp.int32, (c_len, 1), 0).astype(jnp.float32)
    jr = lax.broadcasted_iota(jnp.int32, (1, c_len), 1).astype(jnp.float32)
    kdec_f = jnp.exp(lg_f * (c_len - 1 - jr))
    kdec_b = jnp.exp(lg_b * jr)
    qdec_f = jnp.exp(lg_f * (ic + 1.0))
    qdec_b = jnp.exp(lg_b * (c_len - ic))
    one = jnp.ones((1, 1), jnp.float32)
    cdec_f = jnp.exp(one * (lg_f * c_len))
    cdec_b = jnp.exp(one * (lg_b * c_len))

    jcr = lax.broadcasted_iota(jnp.int32, (1, lc), 1).astype(jnp.float32)
    kct = kc_ref[0].astype(jnp.float32).T
    vcx = vc_ref[0]
    s_f = jnp.dot((kct * jnp.exp(lg_f * (lc - 1 - jcr))).astype(jnp.bfloat16), vcx,
                  preferred_element_type=jnp.float32)
    s_b = jnp.dot((kct * jnp.exp(lg_b * jcr)).astype(jnp.bfloat16), vcx,
                  preferred_element_type=jnp.float32)

    for c in range(n_chunks):
        rows_c = slice(c * c_len, (c + 1) * c_len)
        kt = k_ref[0, rows_c, :].astype(jnp.float32).T
        vc = v_ref[0, rows_c, :]
        tf_ref[c] = jnp.dot((kt * kdec_f).astype(jnp.bfloat16), vc, preferred_element_type=jnp.float32)
        tb_ref[c] = jnp.dot((kt * kdec_b).astype(jnp.bfloat16), vc, preferred_element_type=jnp.float32)

    for c in range(n_chunks):
        sf_ref[c] = s_f.astype(jnp.bfloat16)
        s_f = cdec_f * s_f + tf_ref[c]
    for c in reversed(range(n_chunks)):
        sb_ref[c] = s_b.astype(jnp.bfloat16)
        s_b = cdec_b * s_b + tb_ref[c]

    ng = ng_ref[...]
    for c in range(n_chunks):
        rows_c = slice(c * c_len, (c + 1) * c_len)
        qc = q_ref[0, rows_c, :]
        kc = k_ref[0, rows_c, :]
        vc = v_ref[0, rows_c, :]
        s = lax.dot_general(qc, kc, contract_last, preferred_element_type=jnp.float32) * decay
        o = jnp.dot(s.astype(jnp.bfloat16), vc, preferred_element_type=jnp.float32)
        o = o + jnp.dot(qc, sf_ref[c], preferred_element_type=jnp.float32) * qdec_f
        o = o + jnp.dot(qc, sb_ref[c], preferred_element_type=jnp.float32) * qdec_b
        o = o * lax.rsqrt(jnp.mean(o * o, axis=-1, keepdims=True) + EPS) * ng
        gate = g_ref[0, rows_c, :].astype(jnp.float32)
        o_ref[0, rows_c, :] = (o * gate).astype(o_ref.dtype)


def _retention(lg, proj, proj_ctx, ret_norm_g, q_blk, k_blk, v_blk, g_blk, kc_blk, vc_blk):
    b, l, _ = proj.shape
    lc = proj_ctx.shape[1]
    blk = lambda off: pl.BlockSpec((1, l, LANES), lambda bi, h, lg_ref, off=off: (bi, 0, off + h))
    cblk = lambda off: pl.BlockSpec((1, lc, LANES), lambda bi, h, lg_ref, off=off: (bi, 0, off + h))
    n_chunks = l // RET_CHUNK
    grid_spec = pltpu.PrefetchScalarGridSpec(
        num_scalar_prefetch=1,
        grid=(b, RET_HEADS),
        in_specs=[blk(q_blk), blk(k_blk), blk(v_blk), blk(g_blk), cblk(kc_blk), cblk(vc_blk),
                  pl.BlockSpec((1, LANES), lambda bi, h, lg_ref: (0, h))],
        out_specs=pl.BlockSpec((1, l, LANES), lambda bi, h, lg_ref: (bi, 0, h)),
        scratch_shapes=[pltpu.VMEM((n_chunks, RET_DIM, RET_DIM), jnp.float32),
                        pltpu.VMEM((n_chunks, RET_DIM, RET_DIM), jnp.float32),
                        pltpu.VMEM((n_chunks, RET_DIM, RET_DIM), jnp.bfloat16),
                        pltpu.VMEM((n_chunks, RET_DIM, RET_DIM), jnp.bfloat16)],
    )
    return pl.pallas_call(
        _ret_kernel,
        out_shape=jax.ShapeDtypeStruct((b, l, RET_WIDTH), jnp.bfloat16),
        grid_spec=grid_spec,
        compiler_params=pltpu.CompilerParams(
            dimension_semantics=("arbitrary", "arbitrary"), vmem_limit_bytes=VMEM_LIMIT),
        name="retention",
    )(lg, proj, proj, proj, proj, proj_ctx, proj_ctx, ret_norm_g.reshape(1, RET_WIDTH))


def _out_kernel(x_ref, gate_ref, yna_ref, yret_ref, wa_ref, wb_ref, fg_ref, o_ref):
    y = jnp.dot(yna_ref[0], wa_ref[...], preferred_element_type=jnp.float32)
    y = y + jnp.dot(yret_ref[0], wb_ref[...], preferred_element_type=jnp.float32)
    z = x_ref[0] + gate_ref[0] * y
    ms = jnp.mean(z * z, axis=-1, keepdims=True)
    o_ref[0] = z * lax.rsqrt(ms + EPS) * fg_ref[...]


def _out_proj(x, gate, y_na, y_ret, w_out, final_g, tm):
    b, l, d = x.shape
    wa = w_out[:NA_WIDTH]
    wb = w_out[NA_WIDTH:]
    return pl.pallas_call(
        _out_kernel,
        out_shape=jax.ShapeDtypeStruct((b, l, d), jnp.float32),
        grid=(b, l // tm),
        in_specs=[pl.BlockSpec((1, tm, d), lambda bi, i: (bi, i, 0)),
                  pl.BlockSpec((1, 1, d), lambda bi, i: (bi, 0, 0)),
                  pl.BlockSpec((1, tm, NA_WIDTH), lambda bi, i: (bi, i, 0)),
                  pl.BlockSpec((1, tm, RET_WIDTH), lambda bi, i: (bi, i, 0)),
                  pl.BlockSpec((NA_WIDTH, d), lambda bi, i: (0, 0)),
                  pl.BlockSpec((RET_WIDTH, d), lambda bi, i: (0, 0)),
                  pl.BlockSpec((1, d), lambda bi, i: (0, 0))],
        out_specs=pl.BlockSpec((1, tm, d), lambda bi, i: (bi, i, 0)),
        compiler_params=pltpu.CompilerParams(
            dimension_semantics=("arbitrary", "arbitrary"), vmem_limit_bytes=VMEM_LIMIT),
        name="out_proj",
    )(x, gate, y_na, y_ret, wa, wb, final_g.reshape(1, d))


def _rotary_tables(l):
    half = RET_DIM // 2
    nf = half // 2
    t = np.arange(l)
    row = (t // GRID_W).astype(np.float32)
    col = (t % GRID_W).astype(np.float32)
    inv = jnp.asarray(ROPE_BASE, jnp.float32) ** (-jnp.arange(nf, dtype=jnp.float32) / nf)
    ang = jnp.concatenate([jnp.asarray(row)[:, None] * inv, jnp.asarray(col)[:, None] * inv], axis=-1)
    cos, sin = jnp.cos(ang), jnp.sin(ang)
    return jnp.concatenate([cos, cos], axis=-1), jnp.concatenate([-sin, sin], axis=-1)


def kernel(x, c, ctx, c_ctx, norm_g, w_ada, b_ada, w_in, na_rpb, ret_decay_fwd, ret_decay_bwd,
           ret_norm_g, w_out, final_norm_g):
    depth = norm_g.shape[0]
    assert depth == 1, "context stream update between layers is not implemented"
    b, l, d = x.shape
    rows = l // GRID_W
    i = 0

    cc = jnp.concatenate([c, c_ctx[None, :]], axis=0)
    pad = (-cc.shape[0]) % 8
    cc = jnp.pad(cc, ((0, pad), (0, 0)))
    mod = _adaln_mod(cc, w_ada[i], b_ada[i])
    shift, scale, gate = (mod[:b, None, :d], mod[:b, None, d:2 * d], mod[:b, None, 2 * d:])
    shift_c, scale_c = mod[b:b + 1, None, :d], mod[b:b + 1, None, d:2 * d]

    gc = GROUP_COLS
    w = w_in[i].astype(jnp.bfloat16)
    grp = lambda k: w[:, k * gc:(k + 1) * gc]
    w_vt = grp(2).T
    cosf, sinf = _rotary_tables(l)
    blocks = gc // LANES

    w_lat = jnp.concatenate([grp(0), grp(1), grp(3), grp(4), grp(5), grp(6), grp(7)], axis=1)
    lat_epi = ("na_q", "plain", "silu", "rot", "rot_kscale", "plain", "silu")
    proj, vt = _in_proj(x, scale, shift, norm_g[i], w_lat, w_vt, cosf, sinf, lat_epi, tm=512)

    w_ctx = jnp.concatenate([grp(1), grp(5), grp(6)], axis=1)
    lc = ctx.shape[1]
    ctx_epi = ("plain", "kscale", "plain")
    proj_ctx, vct = _in_proj(ctx, scale_c, shift_c, norm_g[i], w_ctx, w_vt, cosf[:lc], sinf[:lc], ctx_epi, tm=lc)

    y_na = _na_attention(na_rpb[i], proj, vt, proj_ctx, vct, rows,
                         q_blk=0, k_blk=blocks, g_blk=2 * blocks, kc_blk=0)

    lg = jnp.stack([-jnp.exp(ret_decay_fwd[i].astype(jnp.float32)),
                    -jnp.exp(ret_decay_bwd[i].astype(jnp.float32))])
    y_ret = _retention(lg, proj, proj_ctx, ret_norm_g[i],
                       q_blk=3 * blocks, k_blk=4 * blocks, v_blk=5 * blocks, g_blk=6 * blocks,
                       kc_blk=blocks, vc_blk=2 * blocks)

    return _out_proj(x, gate, y_na, y_ret, w_out[i].astype(jnp.bfloat16), final_norm_g, tm=512)
```

```python
import functools
import math

import jax
import jax.numpy as jnp
import numpy as np
from jax import lax
from jax.experimental import pallas as pl
from jax.experimental.pallas import tpu as pltpu

D_MODEL = 1024
GRID_W = 64
NA_HEAD_DIM = 64
NA_WIDTH = 512
NA_HEADS = 8
NA_WIN_ROWS = 8
NA_WIN_COLS = 16
RET_HEADS = 4
RET_WIDTH = 512
RET_DIM = 128
RET_CHUNK = 128
ROPE_BASE = 10000.0
EPS = 1e-6

LANES = 128
GROUP_COLS = 512
NA_Q_ROWS = 4
NA_K_ROWS = 12
MASK_VALUE = -1e30
LOG2E = math.log2(math.e)

VMEM_LIMIT = 56 * 1024 * 1024


def _silu(v):
    return v * (1.0 / (1.0 + jnp.exp(-v)))


def _mod_kernel(c_ref, w_ref, b_ref, o_ref):
    a = _silu(c_ref[...])
    o_ref[...] = jnp.dot(a, w_ref[...], preferred_element_type=jnp.float32,
                         precision=lax.Precision.HIGHEST) + b_ref[...]


def _adaln_mod(cc, w_ada, b_ada):
    m, d = cc.shape
    n = w_ada.shape[1]
    tn = 512
    return pl.pallas_call(
        _mod_kernel,
        out_shape=jax.ShapeDtypeStruct((m, n), jnp.float32),
        grid=(n // tn,),
        in_specs=[pl.BlockSpec((m, d), lambda j: (0, 0)),
                  pl.BlockSpec((d, tn), lambda j: (0, j)),
                  pl.BlockSpec((1, tn), lambda j: (0, j))],
        out_specs=pl.BlockSpec((m, tn), lambda j: (0, j)),
        compiler_params=pltpu.CompilerParams(
            dimension_semantics=("arbitrary",), vmem_limit_bytes=VMEM_LIMIT),
        name="adaln_mod",
    )(cc, w_ada, b_ada.reshape(1, n))


def _rotary(acc, cosf, sinf):
    outs = []
    for h in range(GROUP_COLS // LANES):
        xh = acc[:, h * LANES:(h + 1) * LANES]
        outs.append(xh * cosf + pltpu.roll(xh, LANES // 2, 1) * sinf)
    return jnp.concatenate(outs, axis=-1)


def _proj_kernel(x_ref, scale_ref, shift_ref, g_ref, w_ref, wvt_ref, cos_ref, sin_ref, o_ref, vt_ref,
                 *, epilogues):
    x = x_ref[0]
    ms = jnp.mean(x * x, axis=-1, keepdims=True)
    h = x * lax.rsqrt(ms + EPS) * g_ref[...]
    h = h * (1.0 + scale_ref[0]) + shift_ref[0]
    hb = h.astype(jnp.bfloat16)
    vt = lax.dot_general(wvt_ref[...], hb, (((1,), (1,)), ((), ())), preferred_element_type=jnp.float32)
    vt_ref[0] = vt.astype(vt_ref.dtype)
    for gi, kind in enumerate(epilogues):
        cols = slice(gi * GROUP_COLS, (gi + 1) * GROUP_COLS)
        acc = jnp.dot(hb, w_ref[:, cols], preferred_element_type=jnp.float32)
        if kind == "na_q":
            acc = acc * (NA_HEAD_DIM ** -0.5 * LOG2E)
        elif kind == "silu":
            acc = _silu(acc)
        elif kind == "rot":
            acc = _rotary(acc, cos_ref[...], sin_ref[...])
        elif kind == "rot_kscale":
            acc = _rotary(acc, cos_ref[...], sin_ref[...]) * (RET_DIM ** -0.5)
        elif kind == "kscale":
            acc = acc * (RET_DIM ** -0.5)
        else:
            assert kind == "plain"
        o_ref[0, :, cols] = acc.astype(o_ref.dtype)


def _in_proj(x, scale, shift, norm_g, w, w_vt, cosf, sinf, epilogues, tm):
    b, l, d = x.shape
    n = w.shape[1]
    nv = w_vt.shape[0]
    assert n == GROUP_COLS * len(epilogues) and l % tm == 0
    per_batch = scale.shape[0] > 1
    mod_map = (lambda bi, i: (bi, 0, 0)) if per_batch else (lambda bi, i: (0, 0, 0))
    return pl.pallas_call(
        functools.partial(_proj_kernel, epilogues=tuple(epilogues)),
        out_shape=(jax.ShapeDtypeStruct((b, l, n), jnp.bfloat16),
                   jax.ShapeDtypeStruct((b, nv, l), jnp.bfloat16)),
        grid=(b, l // tm),
        in_specs=[pl.BlockSpec((1, tm, d), lambda bi, i: (bi, i, 0)),
                  pl.BlockSpec((1, 1, d), mod_map),
                  pl.BlockSpec((1, 1, d), mod_map),
                  pl.BlockSpec((1, d), lambda bi, i: (0, 0)),
                  pl.BlockSpec((d, n), lambda bi, i: (0, 0)),
                  pl.BlockSpec((nv, d), lambda bi, i: (0, 0)),
                  pl.BlockSpec((tm, LANES), lambda bi, i: (i, 0)),
                  pl.BlockSpec((tm, LANES), lambda bi, i: (i, 0))],
        out_specs=(pl.BlockSpec((1, tm, n), lambda bi, i: (bi, i, 0)),
                   pl.BlockSpec((1, nv, tm), lambda bi, i: (bi, 0, i))),
        compiler_params=pltpu.CompilerParams(
            dimension_semantics=("arbitrary", "arbitrary"), vmem_limit_bytes=VMEM_LIMIT),
        name="in_proj",
    )(x, scale, shift, norm_g.reshape(1, d), w, w_vt, cosf, sinf)


def _na_geometry(rows):
    n_groups = rows // NA_Q_ROWS
    starts, variants = [], []
    for g in range(n_groups):
        starts.append(int(np.clip(NA_Q_ROWS * g - NA_WIN_ROWS // 2, 0, rows - NA_K_ROWS)))
        variants.append(0 if g == 0 else (2 if g == n_groups - 1 else 1))
    return starts, variants


def _na_row_offsets(rows):
    kr = min(NA_WIN_ROWS, rows)
    starts, _ = _na_geometry(rows)
    n_groups = len(starts)
    table = []
    for rep in (0, 1, n_groups - 1):
        per_i = []
        for i in range(NA_Q_ROWS):
            rq = NA_Q_ROWS * rep + i
            r0 = int(np.clip(rq - kr // 2, 0, rows - kr))
            per_j = []
            for j in range(NA_K_ROWS):
                rk = starts[rep] + j
                per_j.append(rk - rq + NA_WIN_ROWS - 1 if r0 <= rk < r0 + kr else None)
            per_i.append(per_j)
        table.append(per_i)
    return table


def _na_build_bias(rpb_ref, hp, bias_ref, rows):
    n_dr = 2 * NA_WIN_ROWS - 1
    n_dc = 2 * NA_WIN_COLS - 1
    ck = lax.broadcasted_iota(jnp.int32, (GRID_W, LANES), 0)
    lane = lax.broadcasted_iota(jnp.int32, (GRID_W, LANES), 1)
    left = lane < GRID_W
    cq = jnp.where(left, lane, lane - GRID_W)
    c0 = jnp.clip(cq - NA_WIN_COLS // 2, 0, GRID_W - NA_WIN_COLS)
    valid_c = (ck >= c0) & (ck < c0 + NA_WIN_COLS)
    dc = jnp.clip(ck - cq + NA_WIN_COLS - 1, 0, n_dc - 1)

    offsets = _na_row_offsets(rows)
    users = {}
    for var in range(3):
        for j in range(NA_K_ROWS):
            for ip in range(NA_Q_ROWS // 2):
                key = (offsets[var][2 * ip][j], offsets[var][2 * ip + 1][j])
                users.setdefault(key, []).append((var, j, ip))

    for h in range(2):
        head = 2 * hp + h
        for (dr_l, dr_r), dests in users.items():
            if dr_l is None and dr_r is None:
                tile = jnp.full((GRID_W, LANES), MASK_VALUE, jnp.float32)
            else:
                def body(d, acc, dr_l=dr_l, dr_r=dr_r, head=head):
                    s_l = MASK_VALUE if dr_l is None else rpb_ref[(head * n_dr + dr_l) * n_dc + d]
                    s_r = MASK_VALUE if dr_r is None else rpb_ref[(head * n_dr + dr_r) * n_dc + d]
                    return jnp.where(dc == d, jnp.where(left, s_l, s_r), acc)
                acc = lax.fori_loop(0, n_dc, body, jnp.zeros((GRID_W, LANES), jnp.float32))
                valid = valid_c
                if dr_l is None:
                    valid = valid & jnp.logical_not(left)
                if dr_r is None:
                    valid = valid & left
                tile = jnp.where(valid, acc * LOG2E, MASK_VALUE)
            for var, j, ip in dests:
                bias_ref[var, h, j * GRID_W:(j + 1) * GRID_W, ip * LANES:(ip + 1) * LANES] = tile


def _na_kernel(rpb_ref, q_ref, k_ref, vt_ref, g_ref, kc_ref, vct_ref, o_ref, bias_ref, vth_ref, vcth_ref,
               *, rows):
    starts, variants = _na_geometry(rows)
    tq = NA_Q_ROWS * GRID_W
    tk = NA_K_ROWS * GRID_W
    dh = NA_HEAD_DIM

    @pl.when(pl.program_id(1) == 0)
    def _():
        _na_build_bias(rpb_ref, pl.program_id(0), bias_ref, rows)

    lane = lax.broadcasted_iota(jnp.int32, (1, LANES), 1)
    head0_lanes = lane < dh
    ones_l = jnp.ones((dh, vt_ref.shape[2]), vt_ref.dtype)
    ones_c = jnp.ones((dh, vct_ref.shape[2]), vct_ref.dtype)
    vth_ref[0, :dh, :] = vt_ref[0, :dh, :]
    vth_ref[0, dh:, :] = ones_l
    vth_ref[1, :dh, :] = ones_l
    vth_ref[1, dh:, :] = vt_ref[0, dh:, :]
    vcth_ref[0, :dh, :] = vct_ref[0, :dh, :]
    vcth_ref[0, dh:, :] = ones_c
    vcth_ref[1, :dh, :] = ones_c
    vcth_ref[1, dh:, :] = vct_ref[0, dh:, :]

    kc = kc_ref[0]
    contract_last = (((1,), (1,)), ((), ()))
    tasks = [(g, h) for g in range(len(starts)) for h in range(2)]

    def scores(g, h):
        ws, var = starts[g], variants[g]
        qg = q_ref[0, g * tq:(g + 1) * tq, :]
        kw = k_ref[0, ws * GRID_W:ws * GRID_W + tk, :]
        sel = head0_lanes if h == 0 else jnp.logical_not(head0_lanes)
        qh = jnp.where(sel, qg, jnp.zeros_like(qg))
        s_loc = lax.dot_general(kw, qh, contract_last,
                                preferred_element_type=jnp.float32) + bias_ref[var, h]
        s_ctx = lax.dot_general(kc, qh, contract_last, preferred_element_type=jnp.float32)
        return s_loc, s_ctx

    def attend(g, h, s_loc, s_ctx):
        ws = starts[g]
        m = jnp.maximum(jnp.max(s_loc, axis=0, keepdims=True), jnp.max(s_ctx, axis=0, keepdims=True))
        p_loc = jnp.exp2(s_loc - m).astype(jnp.bfloat16)
        p_ctx = jnp.exp2(s_ctx - m).astype(jnp.bfloat16)
        ot = jnp.dot(vth_ref[h, :, ws * GRID_W:ws * GRID_W + tk], p_loc, preferred_element_type=jnp.float32)
        ot = ot + jnp.dot(vcth_ref[h], p_ctx, preferred_element_type=jnp.float32)
        return ot[:dh] / ot[dh:] if h == 0 else ot[dh:] / ot[:dh]

    halves = []
    pending = scores(*tasks[0])
    for t, (g, h) in enumerate(tasks):
        current = pending
        if t + 1 < len(tasks):
            pending = scores(*tasks[t + 1])
        halves.append(attend(g, h, *current))
        if h == 1:
            o2 = jnp.concatenate(halves, axis=0).T
            halves = []
            gate = g_ref[0, g * tq:(g + 1) * tq, :].astype(jnp.float32)
            o_ref[0, g * tq:(g + 1) * tq, :] = (o2 * gate).astype(o_ref.dtype)


def _na_attention(rpb, proj, vt, proj_ctx, vct, rows, q_blk, k_blk, g_blk, kc_blk):
    b, l, _ = proj.shape
    lc = proj_ctx.shape[1]
    n_pairs = NA_HEADS // 2
    blk = lambda off: pl.BlockSpec((1, l, LANES), lambda hp, bi, rpb_ref, off=off: (bi, 0, off + hp))
    grid_spec = pltpu.PrefetchScalarGridSpec(
        num_scalar_prefetch=1,
        grid=(n_pairs, b),
        in_specs=[blk(q_blk), blk(k_blk),
                  pl.BlockSpec((1, LANES, l), lambda hp, bi, rpb_ref: (bi, hp, 0)),
                  blk(g_blk),
                  pl.BlockSpec((1, lc, LANES), lambda hp, bi, rpb_ref: (bi, 0, kc_blk + hp)),
                  pl.BlockSpec((1, LANES, lc), lambda hp, bi, rpb_ref: (bi, hp, 0))],
        out_specs=pl.BlockSpec((1, l, LANES), lambda hp, bi, rpb_ref: (bi, 0, hp)),
        scratch_shapes=[pltpu.VMEM((3, 2, NA_K_ROWS * GRID_W, NA_Q_ROWS * GRID_W), jnp.float32),
                        pltpu.VMEM((2, LANES, l), jnp.bfloat16),
                        pltpu.VMEM((2, LANES, lc), jnp.bfloat16)],
    )
    return pl.pallas_call(
        functools.partial(_na_kernel, rows=rows),
        out_shape=jax.ShapeDtypeStruct((b, l, NA_WIDTH), jnp.bfloat16),
        grid_spec=grid_spec,
        compiler_params=pltpu.CompilerParams(
            dimension_semantics=("arbitrary", "arbitrary"), vmem_limit_bytes=VMEM_LIMIT),
        name="na_attn",
    )(rpb.astype(jnp.float32).reshape(-1), proj, proj, vt, proj, proj_ctx, vct)


def _ret_kernel(lg_ref, q_ref, k_ref, v_ref, g_ref, kc_ref, vc_ref, ng_ref, o_ref, t_ref, s_ref):
    h = pl.program_id(1)
    lg_f = lg_ref[0, h]
    lg_b = lg_ref[1, h]
    c_len = RET_CHUNK
    l = q_ref.shape[1]
    lc = kc_ref.shape[1]
    n_chunks = l // c_len
    contract_last = (((1,), (1,)), ((), ()))

    ii = lax.broadcasted_iota(jnp.int32, (c_len, c_len), 0).astype(jnp.float32)
    jj = lax.broadcasted_iota(jnp.int32, (c_len, c_len), 1).astype(jnp.float32)
    dist = ii - jj
    decay = (jnp.where(dist >= 0, jnp.exp(lg_f * jnp.maximum(dist, 0.0)), 0.0)
             + jnp.where(dist <= 0, jnp.exp(lg_b * jnp.maximum(-dist, 0.0)), 0.0))
    ic = lax.broadcasted_iota(jnp.int32, (c_len, 1), 0).astype(jnp.float32)
    kdec_f = jnp.exp(lg_f * (c_len - 1 - ic))
    kdec_b = jnp.exp(lg_b * ic)
    qdec_f = jnp.exp(lg_f * (ic + 1.0))
    qdec_b = jnp.exp(lg_b * (c_len - ic))
    one = jnp.ones((1, 1), jnp.float32)
    cdec_f = jnp.exp(one * (lg_f * c_len))
    cdec_b = jnp.exp(one * (lg_b * c_len))

    jcr = lax.broadcasted_iota(jnp.int32, (1, lc), 1).astype(jnp.float32)
    kct = kc_ref[0].astype(jnp.float32).T
    vcx = vc_ref[0]
    s_f = jnp.dot((kct * jnp.exp(lg_f * (lc - 1 - jcr))).astype(jnp.bfloat16), vcx,
                  preferred_element_type=jnp.float32)
    s_b = jnp.dot((kct * jnp.exp(lg_b * jcr)).astype(jnp.bfloat16), vcx,
                  preferred_element_type=jnp.float32)

    for c in range(n_chunks):
        rows_c = slice(c * c_len, (c + 1) * c_len)
        kt = k_ref[0, rows_c, :].astype(jnp.float32).T.astype(jnp.bfloat16)
        vf = v_ref[0, rows_c, :].astype(jnp.float32)
        vw = jnp.concatenate([(vf * kdec_f).astype(jnp.bfloat16), (vf * kdec_b).astype(jnp.bfloat16)], axis=1)
        t_ref[c] = jnp.dot(kt, vw, preferred_element_type=jnp.float32)

    for c in range(n_chunks):
        s_ref[c, :, :RET_DIM] = s_f.astype(jnp.bfloat16)
        s_f = cdec_f * s_f + t_ref[c, :, :RET_DIM]
    for c in reversed(range(n_chunks)):
        s_ref[c, :, RET_DIM:] = s_b.astype(jnp.bfloat16)
        s_b = cdec_b * s_b + t_ref[c, :, RET_DIM:]

    ng = ng_ref[...]

    def scores(c):
        rows_c = slice(c * c_len, (c + 1) * c_len)
        qc = q_ref[0, rows_c, :]
        s = lax.dot_general(qc, k_ref[0, rows_c, :], contract_last, preferred_element_type=jnp.float32) * decay
        cross = jnp.dot(qc, s_ref[c], preferred_element_type=jnp.float32)
        return s.astype(jnp.bfloat16), cross[:, :RET_DIM] * qdec_f + cross[:, RET_DIM:] * qdec_b

    def finish(c, s, cross):
        rows_c = slice(c * c_len, (c + 1) * c_len)
        o = jnp.dot(s, v_ref[0, rows_c, :], preferred_element_type=jnp.float32) + cross
        o = o * lax.rsqrt(jnp.mean(o * o, axis=-1, keepdims=True) + EPS) * ng
        gate = g_ref[0, rows_c, :].astype(jnp.float32)
        o_ref[0, rows_c, :] = (o * gate).astype(o_ref.dtype)

    pending = scores(0)
    for c in range(n_chunks):
        current = pending
        if c + 1 < n_chunks:
            pending = scores(c + 1)
        finish(c, *current)


def _retention(lg, proj, proj_ctx, ret_norm_g, q_blk, k_blk, v_blk, g_blk, kc_blk, vc_blk):
    b, l, _ = proj.shape
    lc = proj_ctx.shape[1]
    blk = lambda off: pl.BlockSpec((1, l, LANES), lambda bi, h, lg_ref, off=off: (bi, 0, off + h))
    cblk = lambda off: pl.BlockSpec((1, lc, LANES), lambda bi, h, lg_ref, off=off: (bi, 0, off + h))
    n_chunks = l // RET_CHUNK
    grid_spec = pltpu.PrefetchScalarGridSpec(
        num_scalar_prefetch=1,
        grid=(b, RET_HEADS),
        in_specs=[blk(q_blk), blk(k_blk), blk(v_blk), blk(g_blk), cblk(kc_blk), cblk(vc_blk),
                  pl.BlockSpec((1, LANES), lambda bi, h, lg_ref: (0, h))],
        out_specs=pl.BlockSpec((1, l, LANES), lambda bi, h, lg_ref: (bi, 0, h)),
        scratch_shapes=[pltpu.VMEM((n_chunks, RET_DIM, 2 * RET_DIM), jnp.float32),
                        pltpu.VMEM((n_chunks, RET_DIM, 2 * RET_DIM), jnp.bfloat16)],
    )
    return pl.pallas_call(
        _ret_kernel,
        out_shape=jax.ShapeDtypeStruct((b, l, RET_WIDTH), jnp.bfloat16),
        grid_spec=grid_spec,
        compiler_params=pltpu.CompilerParams(
            dimension_semantics=("arbitrary", "arbitrary"), vmem_limit_bytes=VMEM_LIMIT),
        name="retention",
    )(lg, proj, proj, proj, proj, proj_ctx, proj_ctx, ret_norm_g.reshape(1, RET_WIDTH))


def _out_kernel(x_ref, gate_ref, yna_ref, yret_ref, wa_ref, wb_ref, fg_ref, o_ref):
    y = jnp.dot(yna_ref[0], wa_ref[...], preferred_element_type=jnp.float32)
    y = y + jnp.dot(yret_ref[0], wb_ref[...], preferred_element_type=jnp.float32)
    z = x_ref[0] + gate_ref[0] * y
    ms = jnp.mean(z * z, axis=-1, keepdims=True)
    o_ref[0] = z * lax.rsqrt(ms + EPS) * fg_ref[...]


def _out_proj(x, gate, y_na, y_ret, w_out, final_g, tm):
    b, l, d = x.shape
    wa = w_out[:NA_WIDTH]
    wb = w_out[NA_WIDTH:]
    return pl.pallas_call(
        _out_kernel,
        out_shape=jax.ShapeDtypeStruct((b, l, d), jnp.float32),
        grid=(b, l // tm),
        in_specs=[pl.BlockSpec((1, tm, d), lambda bi, i: (bi, i, 0)),
                  pl.BlockSpec((1, 1, d), lambda bi, i: (bi, 0, 0)),
                  pl.BlockSpec((1, tm, NA_WIDTH), lambda bi, i: (bi, i, 0)),
                  pl.BlockSpec((1, tm, RET_WIDTH), lambda bi, i: (bi, i, 0)),
                  pl.BlockSpec((NA_WIDTH, d), lambda bi, i: (0, 0)),
                  pl.BlockSpec((RET_WIDTH, d), lambda bi, i: (0, 0)),
                  pl.BlockSpec((1, d), lambda bi, i: (0, 0))],
        out_specs=pl.BlockSpec((1, tm, d), lambda bi, i: (bi, i, 0)),
        compiler_params=pltpu.CompilerParams(
            dimension_semantics=("arbitrary", "arbitrary"), vmem_limit_bytes=VMEM_LIMIT),
        name="out_proj",
    )(x, gate, y_na, y_ret, wa, wb, final_g.reshape(1, d))


def _rotary_tables(l):
    half = RET_DIM // 2
    nf = half // 2
    t = np.arange(l)
    row = (t // GRID_W).astype(np.float32)
    col = (t % GRID_W).astype(np.float32)
    inv = jnp.asarray(ROPE_BASE, jnp.float32) ** (-jnp.arange(nf, dtype=jnp.float32) / nf)
    ang = jnp.concatenate([jnp.asarray(row)[:, None] * inv, jnp.asarray(col)[:, None] * inv], axis=-1)
    cos, sin = jnp.cos(ang), jnp.sin(ang)
    return jnp.concatenate([cos, cos], axis=-1), jnp.concatenate([-sin, sin], axis=-1)


def kernel(x, c, ctx, c_ctx, norm_g, w_ada, b_ada, w_in, na_rpb, ret_decay_fwd, ret_decay_bwd,
           ret_norm_g, w_out, final_norm_g):
    depth = norm_g.shape[0]
    assert depth == 1, "context stream update between layers is not implemented"
    b, l, d = x.shape
    rows = l // GRID_W
    i = 0

    cc = jnp.concatenate([c, c_ctx[None, :]], axis=0)
    pad = (-cc.shape[0]) % 8
    cc = jnp.pad(cc, ((0, pad), (0, 0)))
    mod = _adaln_mod(cc, w_ada[i], b_ada[i])
    shift, scale, gate = (mod[:b, None, :d], mod[:b, None, d:2 * d], mod[:b, None, 2 * d:])
    shift_c, scale_c = mod[b:b + 1, None, :d], mod[b:b + 1, None, d:2 * d]

    gc = GROUP_COLS
    w = w_in[i].astype(jnp.bfloat16)
    grp = lambda k: w[:, k * gc:(k + 1) * gc]
    w_vt = grp(2).T
    cosf, sinf = _rotary_tables(l)
    blocks = gc // LANES

    w_lat = jnp.concatenate([grp(0), grp(1), grp(3), grp(4), grp(5), grp(6), grp(7)], axis=1)
    lat_epi = ("na_q", "plain", "silu", "rot", "rot_kscale", "plain", "silu")
    proj, vt = _in_proj(x, scale, shift, norm_g[i], w_lat, w_vt, cosf, sinf, lat_epi, tm=512)

    w_ctx = jnp.concatenate([grp(1), grp(5), grp(6)], axis=1)
    lc = ctx.shape[1]
    ctx_epi = ("plain", "kscale", "plain")
    proj_ctx, vct = _in_proj(ctx, scale_c, shift_c, norm_g[i], w_ctx, w_vt, cosf[:lc], sinf[:lc], ctx_epi, tm=lc)

    y_na = _na_attention(na_rpb[i], proj, vt, proj_ctx, vct, rows,
                         q_blk=0, k_blk=blocks, g_blk=2 * blocks, kc_blk=0)

    lg = jnp.stack([-jnp.exp(ret_decay_fwd[i].astype(jnp.float32)),
                    -jnp.exp(ret_decay_bwd[i].astype(jnp.float32))])
    y_ret = _retention(lg, proj, proj_ctx, ret_norm_g[i],
                       q_blk=3 * blocks, k_blk=4 * blocks, v_blk=5 * blocks, g_blk=6 * blocks,
                       kc_blk=blocks, vc_blk=2 * blocks)

    return _out_proj(x, gate, y_na, y_ret, w_out[i].astype(jnp.bfloat16), final_norm_g, tm=512)
```

```python
import functools
import math

import jax
import jax.numpy as jnp
import numpy as np
from jax import lax
from jax.experimental import pallas as pl
from jax.experimental.pallas import tpu as pltpu

D_MODEL = 1024
GRID_W = 64
NA_HEAD_DIM = 64
NA_WIDTH = 512
NA_HEADS = 8
NA_WIN_ROWS = 8
NA_WIN_COLS = 16
RET_HEADS = 4
RET_WIDTH = 512
RET_DIM = 128
RET_CHUNK = 128
ROPE_BASE = 10000.0
EPS = 1e-6

LANES = 128
GROUP_COLS = 512
NA_Q_ROWS = 4
NA_K_ROWS = 12
MASK_VALUE = -1e30
LOG2E = math.log2(math.e)

VMEM_LIMIT = 56 * 1024 * 1024


def _silu(v):
    return v * (1.0 / (1.0 + jnp.exp(-v)))


def _mod_kernel(c_ref, w_ref, b_ref, o_ref):
    a = _silu(c_ref[...])
    o_ref[...] = jnp.dot(a, w_ref[...], preferred_element_type=jnp.float32,
                         precision=lax.Precision.HIGHEST) + b_ref[...]


def _adaln_mod(cc, w_ada, b_ada):
    m, d = cc.shape
    n = w_ada.shape[1]
    tn = 512
    return pl.pallas_call(
        _mod_kernel,
        out_shape=jax.ShapeDtypeStruct((m, n), jnp.float32),
        grid=(n // tn,),
        in_specs=[pl.BlockSpec((m, d), lambda j: (0, 0)),
                  pl.BlockSpec((d, tn), lambda j: (0, j)),
                  pl.BlockSpec((1, tn), lambda j: (0, j))],
        out_specs=pl.BlockSpec((m, tn), lambda j: (0, j)),
        compiler_params=pltpu.CompilerParams(
            dimension_semantics=("arbitrary",), vmem_limit_bytes=VMEM_LIMIT),
        name="adaln_mod",
    )(cc, w_ada, b_ada.reshape(1, n))


def _rotary(acc, cosf, sinf):
    outs = []
    for h in range(GROUP_COLS // LANES):
        xh = acc[:, h * LANES:(h + 1) * LANES]
        outs.append(xh * cosf + pltpu.roll(xh, LANES // 2, 1) * sinf)
    return jnp.concatenate(outs, axis=-1)


def _proj_kernel(x_ref, scale_ref, shift_ref, g_ref, w_ref, wvt_ref, cos_ref, sin_ref, o_ref, vt_ref,
                 *, groups, sub):
    tm = x_ref.shape[1]

    def normed(j):
        x = x_ref[0, j * sub:(j + 1) * sub, :]
        ms = jnp.mean(x * x, axis=-1, keepdims=True)
        h = x * lax.rsqrt(ms + EPS) * g_ref[...]
        h = h * (1.0 + scale_ref[0]) + shift_ref[0]
        return h.astype(jnp.bfloat16)

    def project(j, hb):
        rows = slice(j * sub, (j + 1) * sub)
        vt = lax.dot_general(wvt_ref[...], hb, (((1,), (1,)), ((), ())), preferred_element_type=jnp.float32)
        vt_ref[0, :, rows] = vt.astype(vt_ref.dtype)
        for gi, (src, kind) in enumerate(groups):
            acc = jnp.dot(hb, w_ref[:, src * GROUP_COLS:(src + 1) * GROUP_COLS],
                          preferred_element_type=jnp.float32)
            if kind == "na_q":
                acc = acc * (NA_HEAD_DIM ** -0.5 * LOG2E)
            elif kind == "silu":
                acc = _silu(acc)
            elif kind == "rot":
                acc = _rotary(acc, cos_ref[rows, :], sin_ref[rows, :])
            elif kind == "rot_kscale":
                acc = _rotary(acc, cos_ref[rows, :], sin_ref[rows, :]) * (RET_DIM ** -0.5)
            elif kind == "kscale":
                acc = acc * (RET_DIM ** -0.5)
            else:
                assert kind == "plain"
            o_ref[0, rows, gi * GROUP_COLS:(gi + 1) * GROUP_COLS] = acc.astype(o_ref.dtype)

    pending = normed(0)
    for j in range(tm // sub):
        current = pending
        if (j + 1) * sub < tm:
            pending = normed(j + 1)
        project(j, current)


def _in_proj(x, scale, shift, norm_g, w, w_vt, cosf, sinf, groups, tm, sub):
    b, l, d = x.shape
    n_in = w.shape[1]
    n = GROUP_COLS * len(groups)
    nv = w_vt.shape[0]
    assert l % tm == 0 and tm % sub == 0
    per_batch = scale.shape[0] > 1
    mod_map = (lambda bi, i: (bi, 0, 0)) if per_batch else (lambda bi, i: (0, 0, 0))
    return pl.pallas_call(
        functools.partial(_proj_kernel, groups=tuple(groups), sub=sub),
        out_shape=(jax.ShapeDtypeStruct((b, l, n), jnp.bfloat16),
                   jax.ShapeDtypeStruct((b, nv, l), jnp.bfloat16)),
        grid=(b, l // tm),
        in_specs=[pl.BlockSpec((1, tm, d), lambda bi, i: (bi, i, 0)),
                  pl.BlockSpec((1, 1, d), mod_map),
                  pl.BlockSpec((1, 1, d), mod_map),
                  pl.BlockSpec((1, d), lambda bi, i: (0, 0)),
                  pl.BlockSpec((d, n_in), lambda bi, i: (0, 0)),
                  pl.BlockSpec((nv, d), lambda bi, i: (0, 0)),
                  pl.BlockSpec((tm, LANES), lambda bi, i: (i, 0)),
                  pl.BlockSpec((tm, LANES), lambda bi, i: (i, 0))],
        out_specs=(pl.BlockSpec((1, tm, n), lambda bi, i: (bi, i, 0)),
                   pl.BlockSpec((1, nv, tm), lambda bi, i: (bi, 0, i))),
        compiler_params=pltpu.CompilerParams(
            dimension_semantics=("arbitrary", "arbitrary"), vmem_limit_bytes=VMEM_LIMIT),
        name="in_proj",
    )(x, scale, shift, norm_g.reshape(1, d), w, w_vt, cosf, sinf)


def _na_geometry(rows):
    n_groups = rows // NA_Q_ROWS
    starts, variants = [], []
    for g in range(n_groups):
        starts.append(int(np.clip(NA_Q_ROWS * g - NA_WIN_ROWS // 2, 0, rows - NA_K_ROWS)))
        variants.append(0 if g == 0 else (2 if g == n_groups - 1 else 1))
    return starts, variants


def _na_row_offsets(rows):
    kr = min(NA_WIN_ROWS, rows)
    starts, _ = _na_geometry(rows)
    n_groups = len(starts)
    table = []
    for rep in (0, 1, n_groups - 1):
        per_i = []
        for i in range(NA_Q_ROWS):
            rq = NA_Q_ROWS * rep + i
            r0 = int(np.clip(rq - kr // 2, 0, rows - kr))
            per_j = []
            for j in range(NA_K_ROWS):
                rk = starts[rep] + j
                per_j.append(rk - rq + NA_WIN_ROWS - 1 if r0 <= rk < r0 + kr else None)
            per_i.append(per_j)
        table.append(per_i)
    return table


def _na_build_bias(rpb_ref, hp, bias_ref, rows):
    n_dr = 2 * NA_WIN_ROWS - 1
    n_dc = 2 * NA_WIN_COLS - 1
    ck = lax.broadcasted_iota(jnp.int32, (GRID_W, LANES), 0)
    lane = lax.broadcasted_iota(jnp.int32, (GRID_W, LANES), 1)
    left = lane < GRID_W
    cq = jnp.where(left, lane, lane - GRID_W)
    c0 = jnp.clip(cq - NA_WIN_COLS // 2, 0, GRID_W - NA_WIN_COLS)
    valid_c = (ck >= c0) & (ck < c0 + NA_WIN_COLS)
    dc = jnp.clip(ck - cq + NA_WIN_COLS - 1, 0, n_dc - 1)

    offsets = _na_row_offsets(rows)
    users = {}
    for var in range(3):
        for j in range(NA_K_ROWS):
            for ip in range(NA_Q_ROWS // 2):
                key = (offsets[var][2 * ip][j], offsets[var][2 * ip + 1][j])
                users.setdefault(key, []).append((var, j, ip))

    for h in range(2):
        head = 2 * hp + h
        for (dr_l, dr_r), dests in users.items():
            if dr_l is None and dr_r is None:
                tile = jnp.full((GRID_W, LANES), MASK_VALUE, jnp.float32)
            else:
                def body(d, acc, dr_l=dr_l, dr_r=dr_r, head=head):
                    s_l = MASK_VALUE if dr_l is None else rpb_ref[(head * n_dr + dr_l) * n_dc + d]
                    s_r = MASK_VALUE if dr_r is None else rpb_ref[(head * n_dr + dr_r) * n_dc + d]
                    return jnp.where(dc == d, jnp.where(left, s_l, s_r), acc)
                acc = lax.fori_loop(0, n_dc, body, jnp.zeros((GRID_W, LANES), jnp.float32))
                valid = valid_c
                if dr_l is None:
                    valid = valid & jnp.logical_not(left)
                if dr_r is None:
                    valid = valid & left
                tile = jnp.where(valid, acc * LOG2E, MASK_VALUE)
            for var, j, ip in dests:
                bias_ref[var, h, j * GRID_W:(j + 1) * GRID_W, ip * LANES:(ip + 1) * LANES] = tile


def _na_kernel(rpb_ref, q_ref, k_ref, vt_ref, g_ref, kc_ref, vct_ref, o_ref, bias_ref, vth_ref, vcth_ref,
               *, rows):
    starts, variants = _na_geometry(rows)
    tq = NA_Q_ROWS * GRID_W
    tk = NA_K_ROWS * GRID_W
    dh = NA_HEAD_DIM

    @pl.when(pl.program_id(1) == 0)
    def _():
        _na_build_bias(rpb_ref, pl.program_id(0), bias_ref, rows)

    lane = lax.broadcasted_iota(jnp.int32, (1, LANES), 1)
    head0_lanes = lane < dh
    ones_l = jnp.ones((dh, vt_ref.shape[2]), vt_ref.dtype)
    ones_c = jnp.ones((dh, vct_ref.shape[2]), vct_ref.dtype)
    vth_ref[0, :dh, :] = vt_ref[0, :dh, :]
    vth_ref[0, dh:, :] = ones_l
    vth_ref[1, :dh, :] = ones_l
    vth_ref[1, dh:, :] = vt_ref[0, dh:, :]
    vcth_ref[0, :dh, :] = vct_ref[0, :dh, :]
    vcth_ref[0, dh:, :] = ones_c
    vcth_ref[1, :dh, :] = ones_c
    vcth_ref[1, dh:, :] = vct_ref[0, dh:, :]

    kc = kc_ref[0]
    contract_last = (((1,), (1,)), ((), ()))
    tasks = [(g, h) for g in range(len(starts)) for h in range(2)]

    def scores(g, h):
        ws, var = starts[g], variants[g]
        qg = q_ref[0, g * tq:(g + 1) * tq, :]
        kw = k_ref[0, ws * GRID_W:ws * GRID_W + tk, :]
        sel = head0_lanes if h == 0 else jnp.logical_not(head0_lanes)
        qh = jnp.where(sel, qg, jnp.zeros_like(qg))
        s_loc = lax.dot_general(kw, qh, contract_last,
                                preferred_element_type=jnp.float32) + bias_ref[var, h]
        s_ctx = lax.dot_general(kc, qh, contract_last, preferred_element_type=jnp.float32)
        return s_loc, s_ctx

    def attend(g, h, s_loc, s_ctx):
        ws = starts[g]
        m = jnp.maximum(jnp.max(s_loc, axis=0, keepdims=True), jnp.max(s_ctx, axis=0, keepdims=True))
        p_loc = jnp.exp2(s_loc - m).astype(jnp.bfloat16)
        p_ctx = jnp.exp2(s_ctx - m).astype(jnp.bfloat16)
        ot = jnp.dot(vth_ref[h, :, ws * GRID_W:ws * GRID_W + tk], p_loc, preferred_element_type=jnp.float32)
        ot = ot + jnp.dot(vcth_ref[h], p_ctx, preferred_element_type=jnp.float32)
        return ot[:dh] / ot[dh:] if h == 0 else ot[dh:] / ot[:dh]

    halves = []
    pending = scores(*tasks[0])
    for t, (g, h) in enumerate(tasks):
        current = pending
        if t + 1 < len(tasks):
            pending = scores(*tasks[t + 1])
        halves.append(attend(g, h, *current))
        if h == 1:
            o2 = jnp.concatenate(halves, axis=0).T
            halves = []
            gate = g_ref[0, g * tq:(g + 1) * tq, :].astype(jnp.float32)
            o_ref[0, g * tq:(g + 1) * tq, :] = (o2 * gate).astype(o_ref.dtype)


def _na_attention(rpb, proj, vt, proj_ctx, vct, rows, q_blk, k_blk, g_blk, kc_blk):
    b, l, _ = proj.shape
    lc = proj_ctx.shape[1]
    n_pairs = NA_HEADS // 2
    blk = lambda off: pl.BlockSpec((1, l, LANES), lambda hp, bi, rpb_ref, off=off: (bi, 0, off + hp))
    grid_spec = pltpu.PrefetchScalarGridSpec(
        num_scalar_prefetch=1,
        grid=(n_pairs, b),
        in_specs=[blk(q_blk), blk(k_blk),
                  pl.BlockSpec((1, LANES, l), lambda hp, bi, rpb_ref: (bi, hp, 0)),
                  blk(g_blk),
                  pl.BlockSpec((1, lc, LANES), lambda hp, bi, rpb_ref: (bi, 0, kc_blk + hp)),
                  pl.BlockSpec((1, LANES, lc), lambda hp, bi, rpb_ref: (bi, hp, 0))],
        out_specs=pl.BlockSpec((1, l, LANES), lambda hp, bi, rpb_ref: (bi, 0, hp)),
        scratch_shapes=[pltpu.VMEM((3, 2, NA_K_ROWS * GRID_W, NA_Q_ROWS * GRID_W), jnp.float32),
                        pltpu.VMEM((2, LANES, l), jnp.bfloat16),
                        pltpu.VMEM((2, LANES, lc), jnp.bfloat16)],
    )
    return pl.pallas_call(
        functools.partial(_na_kernel, rows=rows),
        out_shape=jax.ShapeDtypeStruct((b, l, NA_WIDTH), jnp.bfloat16),
        grid_spec=grid_spec,
        compiler_params=pltpu.CompilerParams(
            dimension_semantics=("arbitrary", "arbitrary"), vmem_limit_bytes=VMEM_LIMIT),
        name="na_attn",
    )(rpb.astype(jnp.float32).reshape(-1), proj, proj, vt, proj, proj_ctx, vct)


def _ret_kernel(lg_ref, q_ref, k_ref, v_ref, g_ref, kc_ref, vc_ref, ng_ref, o_ref, t_ref, s_ref):
    h = pl.program_id(1)
    lg_f = lg_ref[0, h]
    lg_b = lg_ref[1, h]
    c_len = RET_CHUNK
    l = q_ref.shape[1]
    lc = kc_ref.shape[1]
    n_chunks = l // c_len
    contract_last = (((1,), (1,)), ((), ()))

    ii = lax.broadcasted_iota(jnp.int32, (c_len, c_len), 0).astype(jnp.float32)
    jj = lax.broadcasted_iota(jnp.int32, (c_len, c_len), 1).astype(jnp.float32)
    dist = ii - jj
    decay = (jnp.where(dist >= 0, jnp.exp(lg_f * jnp.maximum(dist, 0.0)), 0.0)
             + jnp.where(dist <= 0, jnp.exp(lg_b * jnp.maximum(-dist, 0.0)), 0.0))
    ic = lax.broadcasted_iota(jnp.int32, (c_len, 1), 0).astype(jnp.float32)
    kdec_f = jnp.exp(lg_f * (c_len - 1 - ic))
    kdec_b = jnp.exp(lg_b * ic)
    qdec_f = jnp.exp(lg_f * (ic + 1.0))
    qdec_b = jnp.exp(lg_b * (c_len - ic))
    one = jnp.ones((1, 1), jnp.float32)
    cdec_f = jnp.exp(one * (lg_f * c_len))
    cdec_b = jnp.exp(one * (lg_b * c_len))

    jcr = lax.broadcasted_iota(jnp.int32, (1, lc), 1).astype(jnp.float32)
    kct = kc_ref[0].astype(jnp.float32).T
    vcx = vc_ref[0]
    s_f = jnp.dot((kct * jnp.exp(lg_f * (lc - 1 - jcr))).astype(jnp.bfloat16), vcx,
                  preferred_element_type=jnp.float32)
    s_b = jnp.dot((kct * jnp.exp(lg_b * jcr)).astype(jnp.bfloat16), vcx,
                  preferred_element_type=jnp.float32)

    for c in range(n_chunks):
        rows_c = slice(c * c_len, (c + 1) * c_len)
        kt = k_ref[0, rows_c, :].astype(jnp.float32).T.astype(jnp.bfloat16)
        vf = v_ref[0, rows_c, :].astype(jnp.float32)
        vw = jnp.concatenate([(vf * kdec_f).astype(jnp.bfloat16), (vf * kdec_b).astype(jnp.bfloat16)], axis=1)
        t_ref[c] = jnp.dot(kt, vw, preferred_element_type=jnp.float32)

    for c in range(n_chunks):
        s_ref[c, :, :RET_DIM] = s_f.astype(jnp.bfloat16)
        s_f = cdec_f * s_f + t_ref[c, :, :RET_DIM]
    for c in reversed(range(n_chunks)):
        s_ref[c, :, RET_DIM:] = s_b.astype(jnp.bfloat16)
        s_b = cdec_b * s_b + t_ref[c, :, RET_DIM:]

    ng = ng_ref[...]

    def scores(c):
        rows_c = slice(c * c_len, (c + 1) * c_len)
        qc = q_ref[0, rows_c, :]
        s = lax.dot_general(qc, k_ref[0, rows_c, :], contract_last, preferred_element_type=jnp.float32) * decay
        cross = jnp.dot(qc, s_ref[c], preferred_element_type=jnp.float32)
        return s.astype(jnp.bfloat16), cross[:, :RET_DIM] * qdec_f + cross[:, RET_DIM:] * qdec_b

    def finish(c, s, cross):
        rows_c = slice(c * c_len, (c + 1) * c_len)
        o = jnp.dot(s, v_ref[0, rows_c, :], preferred_element_type=jnp.float32) + cross
        o = o * lax.rsqrt(jnp.mean(o * o, axis=-1, keepdims=True) + EPS) * ng
        gate = g_ref[0, rows_c, :].astype(jnp.float32)
        o_ref[0, rows_c, :] = (o * gate).astype(o_ref.dtype)

    pending = scores(0)
    for c in range(n_chunks):
        current = pending
        if c + 1 < n_chunks:
            pending = scores(c + 1)
        finish(c, *current)


def _retention(lg, proj, proj_ctx, ret_norm_g, q_blk, k_blk, v_blk, g_blk, kc_blk, vc_blk):
    b, l, _ = proj.shape
    lc = proj_ctx.shape[1]
    blk = lambda off: pl.BlockSpec((1, l, LANES), lambda bi, h, lg_ref, off=off: (bi, 0, off + h))
    cblk = lambda off: pl.BlockSpec((1, lc, LANES), lambda bi, h, lg_ref, off=off: (bi, 0, off + h))
    n_chunks = l // RET_CHUNK
    grid_spec = pltpu.PrefetchScalarGridSpec(
        num_scalar_prefetch=1,
        grid=(b, RET_HEADS),
        in_specs=[blk(q_blk), blk(k_blk), blk(v_blk), blk(g_blk), cblk(kc_blk), cblk(vc_blk),
                  pl.BlockSpec((1, LANES), lambda bi, h, lg_ref: (0, h))],
        out_specs=pl.BlockSpec((1, l, LANES), lambda bi, h, lg_ref: (bi, 0, h)),
        scratch_shapes=[pltpu.VMEM((n_chunks, RET_DIM, 2 * RET_DIM), jnp.float32),
                        pltpu.VMEM((n_chunks, RET_DIM, 2 * RET_DIM), jnp.bfloat16)],
    )
    return pl.pallas_call(
        _ret_kernel,
        out_shape=jax.ShapeDtypeStruct((b, l, RET_WIDTH), jnp.bfloat16),
        grid_spec=grid_spec,
        compiler_params=pltpu.CompilerParams(
            dimension_semantics=("arbitrary", "arbitrary"), vmem_limit_bytes=VMEM_LIMIT),
        name="retention",
    )(lg, proj, proj, proj, proj, proj_ctx, proj_ctx, ret_norm_g.reshape(1, RET_WIDTH))


def _out_kernel(x_ref, gate_ref, yna_ref, yret_ref, w_ref, fg_ref, o_ref, *, sub):
    tm = x_ref.shape[1]

    def mix(j):
        rows = slice(j * sub, (j + 1) * sub)
        y = jnp.concatenate([yna_ref[0, rows, :], yret_ref[0, rows, :]], axis=1)
        return jnp.dot(y, w_ref[...], preferred_element_type=jnp.float32)

    def finish(j, y):
        rows = slice(j * sub, (j + 1) * sub)
        z = x_ref[0, rows, :] + gate_ref[0] * y
        ms = jnp.mean(z * z, axis=-1, keepdims=True)
        o_ref[0, rows, :] = z * lax.rsqrt(ms + EPS) * fg_ref[...]

    pending = mix(0)
    for j in range(tm // sub):
        current = pending
        if (j + 1) * sub < tm:
            pending = mix(j + 1)
        finish(j, current)


def _out_proj(x, gate, y_na, y_ret, w_out, final_g, tm, sub):
    b, l, d = x.shape
    assert l % tm == 0 and tm % sub == 0
    return pl.pallas_call(
        functools.partial(_out_kernel, sub=sub),
        out_shape=jax.ShapeDtypeStruct((b, l, d), jnp.float32),
        grid=(b, l // tm),
        in_specs=[pl.BlockSpec((1, tm, d), lambda bi, i: (bi, i, 0)),
                  pl.BlockSpec((1, 1, d), lambda bi, i: (bi, 0, 0)),
                  pl.BlockSpec((1, tm, NA_WIDTH), lambda bi, i: (bi, i, 0)),
                  pl.BlockSpec((1, tm, RET_WIDTH), lambda bi, i: (bi, i, 0)),
                  pl.BlockSpec((NA_WIDTH + RET_WIDTH, d), lambda bi, i: (0, 0)),
                  pl.BlockSpec((1, d), lambda bi, i: (0, 0))],
        out_specs=pl.BlockSpec((1, tm, d), lambda bi, i: (bi, i, 0)),
        compiler_params=pltpu.CompilerParams(
            dimension_semantics=("arbitrary", "arbitrary"), vmem_limit_bytes=VMEM_LIMIT),
        name="out_proj",
    )(x, gate, y_na, y_ret, w_out, final_g.reshape(1, d))


def _rotary_tables(l):
    half = RET_DIM // 2
    nf = half // 2
    t = np.arange(l)
    row = (t // GRID_W).astype(np.float64)
    col = (t % GRID_W).astype(np.float64)
    inv = ROPE_BASE ** (-np.arange(nf, dtype=np.float64) / nf)
    ang = np.concatenate([row[:, None] * inv, col[:, None] * inv], axis=-1)
    cos, sin = np.cos(ang), np.sin(ang)
    cosf = np.concatenate([cos, cos], axis=-1).astype(np.float32)
    sinf = np.concatenate([-sin, sin], axis=-1).astype(np.float32)
    return jnp.asarray(cosf), jnp.asarray(sinf)


def kernel(x, c, ctx, c_ctx, norm_g, w_ada, b_ada, w_in, na_rpb, ret_decay_fwd, ret_decay_bwd,
           ret_norm_g, w_out, final_norm_g):
    depth = norm_g.shape[0]
    assert depth == 1, "context stream update between layers is not implemented"
    b, l, d = x.shape
    rows = l // GRID_W
    i = 0

    cc = jnp.concatenate([c, c_ctx[None, :]], axis=0)
    pad = (-cc.shape[0]) % 8
    cc = jnp.pad(cc, ((0, pad), (0, 0)))
    mod = _adaln_mod(cc, w_ada[i], b_ada[i])
    shift, scale, gate = (mod[:b, None, :d], mod[:b, None, d:2 * d], mod[:b, None, 2 * d:])
    shift_c, scale_c = mod[b:b + 1, None, :d], mod[b:b + 1, None, d:2 * d]

    gc = GROUP_COLS
    w = w_in[i].astype(jnp.bfloat16)
    w_vt = w[:, 2 * gc:3 * gc].T
    cosf, sinf = _rotary_tables(l)
    blocks = gc // LANES

    lat_groups = ((0, "na_q"), (1, "plain"), (3, "silu"), (4, "rot"), (5, "rot_kscale"), (6, "plain"), (7, "silu"))
    proj, vt = _in_proj(x, scale, shift, norm_g[i], w, w_vt, cosf, sinf, lat_groups, tm=512, sub=256)

    lc = ctx.shape[1]
    ctx_groups = ((1, "plain"), (5, "kscale"), (6, "plain"))
    proj_ctx, vct = _in_proj(ctx, scale_c, shift_c, norm_g[i], w, w_vt, cosf[:lc], sinf[:lc], ctx_groups,
                             tm=lc, sub=lc)

    y_na = _na_attention(na_rpb[i], proj, vt, proj_ctx, vct, rows,
                         q_blk=0, k_blk=blocks, g_blk=2 * blocks, kc_blk=0)

    lg = jnp.stack([-jnp.exp(ret_decay_fwd[i].astype(jnp.float32)),
                    -jnp.exp(ret_decay_bwd[i].astype(jnp.float32))])
    y_ret = _retention(lg, proj, proj_ctx, ret_norm_g[i],
                       q_blk=3 * blocks, k_blk=4 * blocks, v_blk=5 * blocks, g_blk=6 * blocks,
                       kc_blk=blocks, vc_blk=2 * blocks)

    return _out_proj(x, gate, y_na, y_ret, w_out[i].astype(jnp.bfloat16), final_norm_g, tm=1024, sub=256)
```

```python
import functools
import math

import jax
import jax.numpy as jnp
import numpy as np
from jax import lax
from jax.experimental import pallas as pl
from jax.experimental.pallas import tpu as pltpu

D_MODEL = 1024
GRID_W = 64
NA_HEAD_DIM = 64
NA_WIDTH = 512
NA_HEADS = 8
NA_WIN_ROWS = 8
NA_WIN_COLS = 16
RET_HEADS = 4
RET_WIDTH = 512
RET_DIM = 128
RET_CHUNK = 128
ROPE_BASE = 10000.0
EPS = 1e-6

LANES = 128
MXU_TILE = 256
BF16_SUBLANES = 16
GROUP_COLS = 512
NA_Q_ROWS = 4
NA_K_ROWS = 12
MASK_VALUE = -1e30
LOG2E = math.log2(math.e)

VMEM_LIMIT = 56 * 1024 * 1024


def _silu(v):
    return v * (1.0 / (1.0 + jnp.exp(-v)))


def _mod_kernel(c_ref, w_ref, b_ref, o_ref):
    a = _silu(c_ref[...])
    o_ref[...] = jnp.dot(a, w_ref[...], preferred_element_type=jnp.float32,
                         precision=lax.Precision.HIGHEST) + b_ref[...]


def _adaln_mod(cc, w_ada, b_ada):
    m, d = cc.shape
    n = w_ada.shape[1]
    tn = 512
    return pl.pallas_call(
        _mod_kernel,
        out_shape=jax.ShapeDtypeStruct((m, n), jnp.float32),
        grid=(n // tn,),
        in_specs=[pl.BlockSpec((m, d), lambda j: (0, 0)),
                  pl.BlockSpec((d, tn), lambda j: (0, j)),
                  pl.BlockSpec((1, tn), lambda j: (0, j))],
        out_specs=pl.BlockSpec((m, tn), lambda j: (0, j)),
        compiler_params=pltpu.CompilerParams(
            dimension_semantics=("arbitrary",), vmem_limit_bytes=VMEM_LIMIT),
        name="adaln_mod",
    )(cc, w_ada, b_ada.reshape(1, n))


def _rotary(acc, cosf, sinf):
    outs = []
    for h in range(GROUP_COLS // LANES):
        xh = acc[:, h * LANES:(h + 1) * LANES]
        outs.append(xh * cosf + pltpu.roll(xh, LANES // 2, 1) * sinf)
    return jnp.concatenate(outs, axis=-1)


def _proj_kernel(x_ref, scale_ref, shift_ref, g_ref, w_ref, wvt_ref, cos_ref, sin_ref, o_ref, vt_ref,
                 *, groups, sub):
    tm = x_ref.shape[1]

    def normed(j):
        x = x_ref[0, j * sub:(j + 1) * sub, :]
        ms = jnp.mean(x * x, axis=-1, keepdims=True)
        h = x * lax.rsqrt(ms + EPS) * g_ref[...]
        h = h * (1.0 + scale_ref[0]) + shift_ref[0]
        return h.astype(jnp.bfloat16)

    def project(j, hb):
        rows = slice(j * sub, (j + 1) * sub)
        vt = lax.dot_general(wvt_ref[...], hb, (((1,), (1,)), ((), ())), preferred_element_type=jnp.float32)
        vt_ref[0, :, rows] = vt.astype(vt_ref.dtype)
        for gi, (src, kind) in enumerate(groups):
            acc = jnp.dot(hb, w_ref[:, src * GROUP_COLS:(src + 1) * GROUP_COLS],
                          preferred_element_type=jnp.float32)
            if kind == "na_q":
                acc = acc * (NA_HEAD_DIM ** -0.5 * LOG2E)
            elif kind == "silu":
                acc = _silu(acc)
            elif kind == "rot":
                acc = _rotary(acc, cos_ref[rows, :], sin_ref[rows, :])
            elif kind == "rot_kscale":
                acc = _rotary(acc, cos_ref[rows, :], sin_ref[rows, :]) * (RET_DIM ** -0.5)
            elif kind == "kscale":
                acc = acc * (RET_DIM ** -0.5)
            else:
                assert kind == "plain"
            o_ref[0, rows, gi * GROUP_COLS:(gi + 1) * GROUP_COLS] = acc.astype(o_ref.dtype)

    pending = normed(0)
    for j in range(tm // sub):
        current = pending
        if (j + 1) * sub < tm:
            pending = normed(j + 1)
        project(j, current)


def _in_proj(x, scale, shift, norm_g, w, w_vt, cosf, sinf, groups, tm, sub):
    b, l, d = x.shape
    n_in = w.shape[1]
    n = GROUP_COLS * len(groups)
    nv = w_vt.shape[0]
    assert l % tm == 0 and tm % sub == 0
    per_batch = scale.shape[0] > 1
    mod_map = (lambda bi, i: (bi, 0, 0)) if per_batch else (lambda bi, i: (0, 0, 0))
    return pl.pallas_call(
        functools.partial(_proj_kernel, groups=tuple(groups), sub=sub),
        out_shape=(jax.ShapeDtypeStruct((b, l, n), jnp.bfloat16),
                   jax.ShapeDtypeStruct((b, nv, l), jnp.bfloat16)),
        grid=(b, l // tm),
        in_specs=[pl.BlockSpec((1, tm, d), lambda bi, i: (bi, i, 0)),
                  pl.BlockSpec((1, 1, d), mod_map),
                  pl.BlockSpec((1, 1, d), mod_map),
                  pl.BlockSpec((1, d), lambda bi, i: (0, 0)),
                  pl.BlockSpec((d, n_in), lambda bi, i: (0, 0)),
                  pl.BlockSpec((nv, d), lambda bi, i: (0, 0)),
                  pl.BlockSpec((tm, LANES), lambda bi, i: (i, 0)),
                  pl.BlockSpec((tm, LANES), lambda bi, i: (i, 0))],
        out_specs=(pl.BlockSpec((1, tm, n), lambda bi, i: (bi, i, 0)),
                   pl.BlockSpec((1, nv, tm), lambda bi, i: (bi, 0, i))),
        compiler_params=pltpu.CompilerParams(
            dimension_semantics=("arbitrary", "arbitrary"), vmem_limit_bytes=VMEM_LIMIT),
        name="in_proj",
    )(x, scale, shift, norm_g.reshape(1, d), w, w_vt, cosf, sinf)


def _na_window_start(rq, rows):
    kr = min(NA_WIN_ROWS, rows)
    return int(np.clip(rq - kr // 2, 0, rows - kr))


def _na_geometry(rows):
    kr = min(NA_WIN_ROWS, rows)
    n_groups = rows // NA_Q_ROWS
    assert n_groups >= 3
    geo = []
    for g in range(n_groups):
        lo = _na_window_start(NA_Q_ROWS * g, rows)
        hi = _na_window_start(NA_Q_ROWS * g + NA_Q_ROWS - 1, rows) + kr
        geo.append((lo, hi - lo, 0 if g == 0 else (2 if g == n_groups - 1 else 1)))
    assert len({n for _, n, v in geo if v == 1}) == 1 and max(n for _, n, _ in geo) <= NA_K_ROWS
    return geo


def _na_row_offsets(rows):
    kr = min(NA_WIN_ROWS, rows)
    geo = _na_geometry(rows)
    table = []
    for rep in (0, 1, len(geo) - 1):
        start, n_rows, _ = geo[rep]
        per_i = []
        for i in range(NA_Q_ROWS):
            rq = NA_Q_ROWS * rep + i
            r0 = _na_window_start(rq, rows)
            per_i.append([start + j - rq + NA_WIN_ROWS - 1 if r0 <= start + j < r0 + kr else None
                          for j in range(n_rows)])
        table.append(per_i)
    return table


def _na_build_bias(rpb_ref, hp, bias_ref, rows):
    n_dr = 2 * NA_WIN_ROWS - 1
    n_dc = 2 * NA_WIN_COLS - 1
    ck = lax.broadcasted_iota(jnp.int32, (GRID_W, LANES), 0)
    lane = lax.broadcasted_iota(jnp.int32, (GRID_W, LANES), 1)
    left = lane < GRID_W
    cq = jnp.where(left, lane, lane - GRID_W)
    c0 = jnp.clip(cq - NA_WIN_COLS // 2, 0, GRID_W - NA_WIN_COLS)
    valid_c = (ck >= c0) & (ck < c0 + NA_WIN_COLS)
    dc = jnp.clip(ck - cq + NA_WIN_COLS - 1, 0, n_dc - 1)

    offsets = _na_row_offsets(rows)
    users = {}
    for var in range(3):
        for j in range(len(offsets[var][0])):
            for ip in range(NA_Q_ROWS // 2):
                key = (offsets[var][2 * ip][j], offsets[var][2 * ip + 1][j])
                users.setdefault(key, []).append((var, j, ip))

    for h in range(2):
        head = 2 * hp + h
        for (dr_l, dr_r), dests in users.items():
            if dr_l is None and dr_r is None:
                tile = jnp.full((GRID_W, LANES), MASK_VALUE, jnp.float32)
            else:
                def body(d, acc, dr_l=dr_l, dr_r=dr_r, head=head):
                    s_l = MASK_VALUE if dr_l is None else rpb_ref[(head * n_dr + dr_l) * n_dc + d]
                    s_r = MASK_VALUE if dr_r is None else rpb_ref[(head * n_dr + dr_r) * n_dc + d]
                    return jnp.where(dc == d, jnp.where(left, s_l, s_r), acc)
                acc = lax.fori_loop(0, n_dc, body, jnp.zeros((GRID_W, LANES), jnp.float32))
                valid = valid_c
                if dr_l is None:
                    valid = valid & jnp.logical_not(left)
                if dr_r is None:
                    valid = valid & left
                tile = jnp.where(valid, acc * LOG2E, MASK_VALUE)
            for var, j, ip in dests:
                bias_ref[var, h, j * GRID_W:(j + 1) * GRID_W, ip * LANES:(ip + 1) * LANES] = tile


def _na_kernel(rpb_ref, q_ref, k_ref, vt_ref, g_ref, kc_ref, vct_ref, o_ref, bias_ref, vth_ref, vcth_ref,
               *, rows):
    geo = _na_geometry(rows)
    tq = NA_Q_ROWS * GRID_W
    dh = NA_HEAD_DIM

    @pl.when(pl.program_id(1) == 0)
    def _():
        _na_build_bias(rpb_ref, pl.program_id(0), bias_ref, rows)

    lane = lax.broadcasted_iota(jnp.int32, (1, LANES), 1)
    head0_lanes = lane < dh
    n_ones = vth_ref.shape[1] - dh
    for h in range(2):
        vth_ref[h, :dh, :] = vt_ref[0, h * dh:(h + 1) * dh, :]
        vth_ref[h, dh:, :] = jnp.ones((n_ones, vt_ref.shape[2]), vt_ref.dtype)
        vcth_ref[h, :dh, :] = vct_ref[0, h * dh:(h + 1) * dh, :]
        vcth_ref[h, dh:, :] = jnp.ones((n_ones, vct_ref.shape[2]), vct_ref.dtype)

    kc = kc_ref[0]
    contract_last = (((1,), (1,)), ((), ()))
    tasks = [(g, h) for g in range(len(geo)) for h in range(2)]

    def scores(g, h):
        ws, n_rows, var = geo[g]
        nk = n_rows * GRID_W
        qg = q_ref[0, g * tq:(g + 1) * tq, :]
        kw = k_ref[0, ws * GRID_W:ws * GRID_W + nk, :]
        sel = head0_lanes if h == 0 else jnp.logical_not(head0_lanes)
        qh = jnp.where(sel, qg, jnp.zeros_like(qg))
        s_loc = lax.dot_general(kw, qh, contract_last,
                                preferred_element_type=jnp.float32) + bias_ref[var, h, :nk, :]
        s_ctx = lax.dot_general(kc, qh, contract_last, preferred_element_type=jnp.float32)
        return s_loc, s_ctx

    def attend(g, h, s_loc, s_ctx):
        ws, n_rows, _ = geo[g]
        m = jnp.maximum(jnp.max(s_loc, axis=0, keepdims=True), jnp.max(s_ctx, axis=0, keepdims=True))
        p_loc = jnp.exp2(s_loc - m).astype(jnp.bfloat16)
        p_ctx = jnp.exp2(s_ctx - m).astype(jnp.bfloat16)
        pad_rows = (-n_rows) % (MXU_TILE // GRID_W)
        if pad_rows:
            p_loc = jnp.concatenate([p_loc, jnp.zeros((pad_rows * GRID_W, tq), p_loc.dtype)], axis=0)
        nk = (n_rows + pad_rows) * GRID_W
        ot = jnp.dot(vth_ref[h, :, ws * GRID_W:ws * GRID_W + nk], p_loc, preferred_element_type=jnp.float32)
        ot = ot + jnp.dot(vcth_ref[h], p_ctx, preferred_element_type=jnp.float32)
        return ot[:dh] / ot[dh:dh + 1]

    halves = []
    pending = scores(*tasks[0])
    for t, (g, h) in enumerate(tasks):
        current = pending
        if t + 1 < len(tasks):
            pending = scores(*tasks[t + 1])
        halves.append(attend(g, h, *current))
        if h == 1:
            o2 = jnp.concatenate(halves, axis=0).T
            halves = []
            gate = g_ref[0, g * tq:(g + 1) * tq, :].astype(jnp.float32)
            o_ref[0, g * tq:(g + 1) * tq, :] = (o2 * gate).astype(o_ref.dtype)


def _na_attention(rpb, proj, vt, proj_ctx, vct, rows, q_blk, k_blk, g_blk, kc_blk):
    b, l, _ = proj.shape
    lc = proj_ctx.shape[1]
    n_pairs = NA_HEADS // 2
    blk = lambda off: pl.BlockSpec((1, l, LANES), lambda hp, bi, rpb_ref, off=off: (bi, 0, off + hp))
    grid_spec = pltpu.PrefetchScalarGridSpec(
        num_scalar_prefetch=1,
        grid=(n_pairs, b),
        in_specs=[blk(q_blk), blk(k_blk),
                  pl.BlockSpec((1, LANES, l), lambda hp, bi, rpb_ref: (bi, hp, 0)),
                  blk(g_blk),
                  pl.BlockSpec((1, lc, LANES), lambda hp, bi, rpb_ref: (bi, 0, kc_blk + hp)),
                  pl.BlockSpec((1, LANES, lc), lambda hp, bi, rpb_ref: (bi, hp, 0))],
        out_specs=pl.BlockSpec((1, l, LANES), lambda hp, bi, rpb_ref: (bi, 0, hp)),
        scratch_shapes=[pltpu.VMEM((3, 2, NA_K_ROWS * GRID_W, NA_Q_ROWS * GRID_W), jnp.float32),
                        pltpu.VMEM((2, NA_HEAD_DIM + BF16_SUBLANES, l), jnp.bfloat16),
                        pltpu.VMEM((2, NA_HEAD_DIM + BF16_SUBLANES, lc), jnp.bfloat16)],
    )
    return pl.pallas_call(
        functools.partial(_na_kernel, rows=rows),
        out_shape=jax.ShapeDtypeStruct((b, l, NA_WIDTH), jnp.bfloat16),
        grid_spec=grid_spec,
        compiler_params=pltpu.CompilerParams(
            dimension_semantics=("arbitrary", "arbitrary"), vmem_limit_bytes=VMEM_LIMIT),
        name="na_attn",
    )(rpb.astype(jnp.float32).reshape(-1), proj, proj, vt, proj, proj_ctx, vct)


def _ret_kernel(lg_ref, q_ref, k_ref, v_ref, g_ref, kc_ref, vc_ref, ng_ref, o_ref, t_ref, s_ref):
    h = pl.program_id(1)
    lg_f = lg_ref[0, h]
    lg_b = lg_ref[1, h]
    c_len = RET_CHUNK
    l = q_ref.shape[1]
    lc = kc_ref.shape[1]
    n_chunks = l // c_len
    contract_last = (((1,), (1,)), ((), ()))

    ii = lax.broadcasted_iota(jnp.int32, (c_len, c_len), 0).astype(jnp.float32)
    jj = lax.broadcasted_iota(jnp.int32, (c_len, c_len), 1).astype(jnp.float32)
    dist = ii - jj
    decay = (jnp.where(dist >= 0, jnp.exp(lg_f * jnp.maximum(dist, 0.0)), 0.0)
             + jnp.where(dist <= 0, jnp.exp(lg_b * jnp.maximum(-dist, 0.0)), 0.0))
    ic = lax.broadcasted_iota(jnp.int32, (c_len, 1), 0).astype(jnp.float32)
    kdec_f = jnp.exp(lg_f * (c_len - 1 - ic))
    kdec_b = jnp.exp(lg_b * ic)
    qdec_f = jnp.exp(lg_f * (ic + 1.0))
    qdec_b = jnp.exp(lg_b * (c_len - ic))
    one = jnp.ones((1, 1), jnp.float32)
    cdec_f = jnp.exp(one * (lg_f * c_len))
    cdec_b = jnp.exp(one * (lg_b * c_len))

    jcr = lax.broadcasted_iota(jnp.int32, (1, lc), 1).astype(jnp.float32)
    kct = kc_ref[0].astype(jnp.float32).T
    vcx = vc_ref[0]
    s_f = jnp.dot((kct * jnp.exp(lg_f * (lc - 1 - jcr))).astype(jnp.bfloat16), vcx,
                  preferred_element_type=jnp.float32)
    s_b = jnp.dot((kct * jnp.exp(lg_b * jcr)).astype(jnp.bfloat16), vcx,
                  preferred_element_type=jnp.float32)

    for c in range(n_chunks):
        rows_c = slice(c * c_len, (c + 1) * c_len)
        kt = k_ref[0, rows_c, :].astype(jnp.float32).T.astype(jnp.bfloat16)
        vf = v_ref[0, rows_c, :].astype(jnp.float32)
        vw = jnp.concatenate([(vf * kdec_f).astype(jnp.bfloat16), (vf * kdec_b).astype(jnp.bfloat16)], axis=1)
        t_ref[c] = jnp.dot(kt, vw, preferred_element_type=jnp.float32)

    for c in range(n_chunks):
        s_ref[c, :, :RET_DIM] = s_f.astype(jnp.bfloat16)
        s_f = cdec_f * s_f + t_ref[c, :, :RET_DIM]
    for c in reversed(range(n_chunks)):
        s_ref[c, :, RET_DIM:] = s_b.astype(jnp.bfloat16)
        s_b = cdec_b * s_b + t_ref[c, :, RET_DIM:]

    ng = ng_ref[...]

    def scores(c):
        rows_c = slice(c * c_len, (c + 1) * c_len)
        qc = q_ref[0, rows_c, :]
        s = lax.dot_general(qc, k_ref[0, rows_c, :], contract_last, preferred_element_type=jnp.float32) * decay
        cross = jnp.dot(qc, s_ref[c], preferred_element_type=jnp.float32)
        return s.astype(jnp.bfloat16), cross[:, :RET_DIM] * qdec_f + cross[:, RET_DIM:] * qdec_b

    def finish(c, s, cross):
        rows_c = slice(c * c_len, (c + 1) * c_len)
        o = jnp.dot(s, v_ref[0, rows_c, :], preferred_element_type=jnp.float32) + cross
        o = o * lax.rsqrt(jnp.mean(o * o, axis=-1, keepdims=True) + EPS) * ng
        gate = g_ref[0, rows_c, :].astype(jnp.float32)
        o_ref[0, rows_c, :] = (o * gate).astype(o_ref.dtype)

    pending = scores(0)
    for c in range(n_chunks):
        current = pending
        if c + 1 < n_chunks:
            pending = scores(c + 1)
        finish(c, *current)


def _retention(lg, proj, proj_ctx, ret_norm_g, q_blk, k_blk, v_blk, g_blk, kc_blk, vc_blk):
    b, l, _ = proj.shape
    lc = proj_ctx.shape[1]
    blk = lambda off: pl.BlockSpec((1, l, LANES), lambda bi, h, lg_ref, off=off: (bi, 0, off + h))
    cblk = lambda off: pl.BlockSpec((1, lc, LANES), lambda bi, h, lg_ref, off=off: (bi, 0, off + h))
    n_chunks = l // RET_CHUNK
    grid_spec = pltpu.PrefetchScalarGridSpec(
        num_scalar_prefetch=1,
        grid=(b, RET_HEADS),
        in_specs=[blk(q_blk), blk(k_blk), blk(v_blk), blk(g_blk), cblk(kc_blk), cblk(vc_blk),
                  pl.BlockSpec((1, LANES), lambda bi, h, lg_ref: (0, h))],
        out_specs=pl.BlockSpec((1, l, LANES), lambda bi, h, lg_ref: (bi, 0, h)),
        scratch_shapes=[pltpu.VMEM((n_chunks, RET_DIM, 2 * RET_DIM), jnp.float32),
                        pltpu.VMEM((n_chunks, RET_DIM, 2 * RET_DIM), jnp.bfloat16)],
    )
    return pl.pallas_call(
        _ret_kernel,
        out_shape=jax.ShapeDtypeStruct((b, l, RET_WIDTH), jnp.bfloat16),
        grid_spec=grid_spec,
        compiler_params=pltpu.CompilerParams(
            dimension_semantics=("arbitrary", "arbitrary"), vmem_limit_bytes=VMEM_LIMIT),
        name="retention",
    )(lg, proj, proj, proj, proj, proj_ctx, proj_ctx, ret_norm_g.reshape(1, RET_WIDTH))


def _out_kernel(x_ref, gate_ref, yna_ref, yret_ref, w_ref, fg_ref, o_ref, *, sub):
    tm = x_ref.shape[1]

    def mix(j):
        rows = slice(j * sub, (j + 1) * sub)
        y = jnp.concatenate([yna_ref[0, rows, :], yret_ref[0, rows, :]], axis=1)
        return jnp.dot(y, w_ref[...], preferred_element_type=jnp.float32)

    def finish(j, y):
        rows = slice(j * sub, (j + 1) * sub)
        z = x_ref[0, rows, :] + gate_ref[0] * y
        ms = jnp.mean(z * z, axis=-1, keepdims=True)
        o_ref[0, rows, :] = z * lax.rsqrt(ms + EPS) * fg_ref[...]

    pending = mix(0)
    for j in range(tm // sub):
        current = pending
        if (j + 1) * sub < tm:
            pending = mix(j + 1)
        finish(j, current)


def _out_proj(x, gate, y_na, y_ret, w_out, final_g, tm, sub):
    b, l, d = x.shape
    assert l % tm == 0 and tm % sub == 0
    return pl.pallas_call(
        functools.partial(_out_kernel, sub=sub),
        out_shape=jax.ShapeDtypeStruct((b, l, d), jnp.float32),
        grid=(b, l // tm),
        in_specs=[pl.BlockSpec((1, tm, d), lambda bi, i: (bi, i, 0)),
                  pl.BlockSpec((1, 1, d), lambda bi, i: (bi, 0, 0)),
                  pl.BlockSpec((1, tm, NA_WIDTH), lambda bi, i: (bi, i, 0)),
                  pl.BlockSpec((1, tm, RET_WIDTH), lambda bi, i: (bi, i, 0)),
                  pl.BlockSpec((NA_WIDTH + RET_WIDTH, d), lambda bi, i: (0, 0)),
                  pl.BlockSpec((1, d), lambda bi, i: (0, 0))],
        out_specs=pl.BlockSpec((1, tm, d), lambda bi, i: (bi, i, 0)),
        compiler_params=pltpu.CompilerParams(
            dimension_semantics=("arbitrary", "arbitrary"), vmem_limit_bytes=VMEM_LIMIT),
        name="out_proj",
    )(x, gate, y_na, y_ret, w_out, final_g.reshape(1, d))


def _rotary_tables(l):
    half = RET_DIM // 2
    nf = half // 2
    t = np.arange(l)
    row = (t // GRID_W).astype(np.float64)
    col = (t % GRID_W).astype(np.float64)
    inv = ROPE_BASE ** (-np.arange(nf, dtype=np.float64) / nf)
    ang = np.concatenate([row[:, None] * inv, col[:, None] * inv], axis=-1)
    cos, sin = np.cos(ang), np.sin(ang)
    cosf = np.concatenate([cos, cos], axis=-1).astype(np.float32)
    sinf = np.concatenate([-sin, sin], axis=-1).astype(np.float32)
    return jnp.asarray(cosf), jnp.asarray(sinf)


def kernel(x, c, ctx, c_ctx, norm_g, w_ada, b_ada, w_in, na_rpb, ret_decay_fwd, ret_decay_bwd,
           ret_norm_g, w_out, final_norm_g):
    depth = norm_g.shape[0]
    assert depth == 1, "context stream update between layers is not implemented"
    b, l, d = x.shape
    rows = l // GRID_W
    i = 0

    cc = jnp.concatenate([c, c_ctx[None, :]], axis=0)
    pad = (-cc.shape[0]) % 8
    cc = jnp.pad(cc, ((0, pad), (0, 0)))
    mod = _adaln_mod(cc, w_ada[i], b_ada[i])
    shift, scale, gate = (mod[:b, None, :d], mod[:b, None, d:2 * d], mod[:b, None, 2 * d:])
    shift_c, scale_c = mod[b:b + 1, None, :d], mod[b:b + 1, None, d:2 * d]

    gc = GROUP_COLS
    w = w_in[i].astype(jnp.bfloat16)
    w_vt = w_in[i][:, 2 * gc:3 * gc].T.astype(jnp.bfloat16)
    cosf, sinf = _rotary_tables(l)
    blocks = gc // LANES

    lat_groups = ((0, "na_q"), (1, "plain"), (3, "silu"), (4, "rot"), (5, "rot_kscale"), (6, "plain"), (7, "silu"))
    proj, vt = _in_proj(x, scale, shift, norm_g[i], w, w_vt, cosf, sinf, lat_groups, tm=512, sub=256)

    lc = ctx.shape[1]
    ctx_groups = ((1, "plain"), (5, "kscale"), (6, "plain"))
    proj_ctx, vct = _in_proj(ctx, scale_c, shift_c, norm_g[i], w, w_vt, cosf[:lc], sinf[:lc], ctx_groups,
                             tm=lc, sub=lc)

    y_na = _na_attention(na_rpb[i], proj, vt, proj_ctx, vct, rows,
                         q_blk=0, k_blk=blocks, g_blk=2 * blocks, kc_blk=0)

    lg = jnp.stack([-jnp.exp(ret_decay_fwd[i].astype(jnp.float32)),
                    -jnp.exp(ret_decay_bwd[i].astype(jnp.float32))])
    y_ret = _retention(lg, proj, proj_ctx, ret_norm_g[i],
                       q_blk=3 * blocks, k_blk=4 * blocks, v_blk=5 * blocks, g_blk=6 * blocks,
                       kc_blk=blocks, vc_blk=2 * blocks)

    return _out_proj(x, gate, y_na, y_ret, w_out[i].astype(jnp.bfloat16), final_norm_g, tm=1024, sub=256)
```

```python
import functools
import math

import jax
import jax.numpy as jnp
import numpy as np
from jax import lax
from jax.experimental import pallas as pl
from jax.experimental.pallas import tpu as pltpu

D_MODEL = 1024
GRID_W = 64
NA_HEAD_DIM = 64
NA_WIDTH = 512
NA_HEADS = 8
NA_WIN_ROWS = 8
NA_WIN_COLS = 16
RET_HEADS = 4
RET_WIDTH = 512
RET_DIM = 128
RET_CHUNK = 128
ROPE_BASE = 10000.0
EPS = 1e-6

LANES = 128
MXU_TILE = 256
BF16_SUBLANES = 16
GROUP_COLS = 512
NA_Q_ROWS = 4
NA_K_ROWS = 12
MASK_VALUE = -1e30
LOG2E = math.log2(math.e)

VMEM_LIMIT = 56 * 1024 * 1024


def _silu(v):
    return v * (1.0 / (1.0 + jnp.exp(-v)))


def _mod_kernel(c_ref, w_ref, b_ref, o_ref):
    a = _silu(c_ref[...])
    o_ref[...] = jnp.dot(a, w_ref[...], preferred_element_type=jnp.float32,
                         precision=lax.Precision.HIGHEST) + b_ref[...]


def _adaln_mod(cc, w_ada, b_ada):
    m, d = cc.shape
    n = w_ada.shape[1]
    tn = 512
    return pl.pallas_call(
        _mod_kernel,
        out_shape=jax.ShapeDtypeStruct((m, n), jnp.float32),
        grid=(n // tn,),
        in_specs=[pl.BlockSpec((m, d), lambda j: (0, 0)),
                  pl.BlockSpec((d, tn), lambda j: (0, j)),
                  pl.BlockSpec((1, tn), lambda j: (0, j))],
        out_specs=pl.BlockSpec((m, tn), lambda j: (0, j)),
        compiler_params=pltpu.CompilerParams(
            dimension_semantics=("arbitrary",), vmem_limit_bytes=VMEM_LIMIT),
        name="adaln_mod",
    )(cc, w_ada, b_ada.reshape(1, n))


def _rotary(acc, cosf, sinf):
    outs = []
    for h in range(GROUP_COLS // LANES):
        xh = acc[:, h * LANES:(h + 1) * LANES]
        outs.append(xh * cosf + pltpu.roll(xh, LANES // 2, 1) * sinf)
    return jnp.concatenate(outs, axis=-1)


def _proj_kernel(x_ref, scale_ref, shift_ref, g_ref, w_ref, wvt_ref, cos_ref, sin_ref, o_ref, vt_ref,
                 *, groups, sub):
    tm = x_ref.shape[1]

    def normed(j):
        x = x_ref[0, j * sub:(j + 1) * sub, :]
        ms = jnp.mean(x * x, axis=-1, keepdims=True)
        h = x * lax.rsqrt(ms + EPS) * g_ref[...]
        h = h * (1.0 + scale_ref[0]) + shift_ref[0]
        return h.astype(jnp.bfloat16)

    def project(j, hb):
        rows = slice(j * sub, (j + 1) * sub)
        vt = lax.dot_general(wvt_ref[...], hb, (((1,), (1,)), ((), ())), preferred_element_type=jnp.float32)
        vt_ref[0, :, rows] = vt.astype(vt_ref.dtype)
        for gi, (src, kind) in enumerate(groups):
            acc = jnp.dot(hb, w_ref[:, src * GROUP_COLS:(src + 1) * GROUP_COLS],
                          preferred_element_type=jnp.float32)
            if kind == "na_q":
                acc = acc * (NA_HEAD_DIM ** -0.5 * LOG2E)
            elif kind == "silu":
                acc = _silu(acc)
            elif kind == "rot":
                acc = _rotary(acc, cos_ref[rows, :], sin_ref[rows, :])
            elif kind == "rot_kscale":
                acc = _rotary(acc, cos_ref[rows, :], sin_ref[rows, :]) * (RET_DIM ** -0.5)
            elif kind == "kscale":
                acc = acc * (RET_DIM ** -0.5)
            else:
                assert kind == "plain"
            o_ref[0, rows, gi * GROUP_COLS:(gi + 1) * GROUP_COLS] = acc.astype(o_ref.dtype)

    pending = normed(0)
    for j in range(tm // sub):
        current = pending
        if (j + 1) * sub < tm:
            pending = normed(j + 1)
        project(j, current)


def _in_proj(x, scale, shift, norm_g, w, w_vt, cosf, sinf, groups, tm, sub):
    b, l, d = x.shape
    n_in = w.shape[1]
    n = GROUP_COLS * len(groups)
    nv = w_vt.shape[0]
    assert l % tm == 0 and tm % sub == 0
    per_batch = scale.shape[0] > 1
    mod_map = (lambda bi, i: (bi, 0, 0)) if per_batch else (lambda bi, i: (0, 0, 0))
    return pl.pallas_call(
        functools.partial(_proj_kernel, groups=tuple(groups), sub=sub),
        out_shape=(jax.ShapeDtypeStruct((b, l, n), jnp.bfloat16),
                   jax.ShapeDtypeStruct((b, nv, l), jnp.bfloat16)),
        grid=(b, l // tm),
        in_specs=[pl.BlockSpec((1, tm, d), lambda bi, i: (bi, i, 0)),
                  pl.BlockSpec((1, 1, d), mod_map),
                  pl.BlockSpec((1, 1, d), mod_map),
                  pl.BlockSpec((1, d), lambda bi, i: (0, 0)),
                  pl.BlockSpec((d, n_in), lambda bi, i: (0, 0)),
                  pl.BlockSpec((nv, d), lambda bi, i: (0, 0)),
                  pl.BlockSpec((tm, LANES), lambda bi, i: (i, 0)),
                  pl.BlockSpec((tm, LANES), lambda bi, i: (i, 0))],
        out_specs=(pl.BlockSpec((1, tm, n), lambda bi, i: (bi, i, 0)),
                   pl.BlockSpec((1, nv, tm), lambda bi, i: (bi, 0, i))),
        compiler_params=pltpu.CompilerParams(
            dimension_semantics=("arbitrary", "arbitrary"), vmem_limit_bytes=VMEM_LIMIT),
        name="in_proj",
    )(x, scale, shift, norm_g.reshape(1, d), w, w_vt, cosf, sinf)


def _na_window_start(rq, rows):
    kr = min(NA_WIN_ROWS, rows)
    return int(np.clip(rq - kr // 2, 0, rows - kr))


def _na_geometry(rows):
    kr = min(NA_WIN_ROWS, rows)
    n_groups = rows // NA_Q_ROWS
    assert n_groups >= 3
    geo = []
    for g in range(n_groups):
        lo = _na_window_start(NA_Q_ROWS * g, rows)
        hi = _na_window_start(NA_Q_ROWS * g + NA_Q_ROWS - 1, rows) + kr
        geo.append((lo, hi - lo, 0 if g == 0 else (2 if g == n_groups - 1 else 1)))
    assert len({n for _, n, v in geo if v == 1}) == 1 and max(n for _, n, _ in geo) <= NA_K_ROWS
    return geo


def _na_row_offsets(rows):
    kr = min(NA_WIN_ROWS, rows)
    geo = _na_geometry(rows)
    table = []
    for rep in (0, 1, len(geo) - 1):
        start, n_rows, _ = geo[rep]
        per_i = []
        for i in range(NA_Q_ROWS):
            rq = NA_Q_ROWS * rep + i
            r0 = _na_window_start(rq, rows)
            per_i.append([start + j - rq + NA_WIN_ROWS - 1 if r0 <= start + j < r0 + kr else None
                          for j in range(n_rows)])
        table.append(per_i)
    return table


def _na_build_bias(rpb_ref, hp, bias_ref, base_ref, rows):
    n_dr = 2 * NA_WIN_ROWS - 1
    n_dc = 2 * NA_WIN_COLS - 1
    ck = lax.broadcasted_iota(jnp.int32, (GRID_W, LANES), 0)
    lane = lax.broadcasted_iota(jnp.int32, (GRID_W, LANES), 1)
    left = lane < GRID_W
    cq = jnp.where(left, lane, lane - GRID_W)
    c0 = jnp.clip(cq - NA_WIN_COLS // 2, 0, GRID_W - NA_WIN_COLS)
    valid_c = (ck >= c0) & (ck < c0 + NA_WIN_COLS)
    dc = jnp.clip(ck - cq + NA_WIN_COLS - 1, 0, n_dc - 1)

    offsets = _na_row_offsets(rows)
    users = {}
    for var in range(3):
        for j in range(len(offsets[var][0])):
            for ip in range(NA_Q_ROWS // 2):
                key = (offsets[var][2 * ip][j], offsets[var][2 * ip + 1][j])
                users.setdefault(key, []).append((var, j, ip))

    used = sorted({dr for pair in users for dr in pair if dr is not None})
    masked = jnp.full((GRID_W, LANES), MASK_VALUE, jnp.float32)
    for h in range(2):
        head = 2 * hp + h
        for dr in used:
            def body(d, acc, dr=dr, head=head):
                return jnp.where(dc == d, rpb_ref[(head * n_dr + dr) * n_dc + d], acc)
            acc = lax.fori_loop(0, n_dc, body, jnp.zeros((GRID_W, LANES), jnp.float32), unroll=True)
            base_ref[dr] = jnp.where(valid_c, acc * LOG2E, MASK_VALUE)
        for (dr_l, dr_r), dests in users.items():
            t_l = masked if dr_l is None else base_ref[dr_l]
            t_r = masked if dr_r is None else base_ref[dr_r]
            tile = t_l if dr_l == dr_r else jnp.where(left, t_l, t_r)
            for var, j, ip in dests:
                bias_ref[var, h, j * GRID_W:(j + 1) * GRID_W, ip * LANES:(ip + 1) * LANES] = tile


def _na_kernel(rpb_ref, q_ref, k_ref, vt_ref, g_ref, kc_ref, vct_ref, o_ref,
               bias_ref, base_ref, vth_ref, vcth_ref, *, rows):
    geo = _na_geometry(rows)
    tq = NA_Q_ROWS * GRID_W
    dh = NA_HEAD_DIM

    @pl.when(pl.program_id(1) == 0)
    def _():
        _na_build_bias(rpb_ref, pl.program_id(0), bias_ref, base_ref, rows)

    lane = lax.broadcasted_iota(jnp.int32, (1, LANES), 1)
    head0_lanes = lane < dh
    n_ones = vth_ref.shape[1] - dh
    for h in range(2):
        vth_ref[h, :dh, :] = vt_ref[0, h * dh:(h + 1) * dh, :]
        vth_ref[h, dh:, :] = jnp.ones((n_ones, vt_ref.shape[2]), vt_ref.dtype)
        vcth_ref[h, :dh, :] = vct_ref[0, h * dh:(h + 1) * dh, :]
        vcth_ref[h, dh:, :] = jnp.ones((n_ones, vct_ref.shape[2]), vct_ref.dtype)

    kc = kc_ref[0]
    contract_last = (((1,), (1,)), ((), ()))
    tasks = [(g, h) for g in range(len(geo)) for h in range(2)]

    def scores(g, h):
        ws, n_rows, var = geo[g]
        nk = n_rows * GRID_W
        qg = q_ref[0, g * tq:(g + 1) * tq, :]
        kw = k_ref[0, ws * GRID_W:ws * GRID_W + nk, :]
        sel = head0_lanes if h == 0 else jnp.logical_not(head0_lanes)
        qh = jnp.where(sel, qg, jnp.zeros_like(qg))
        s_loc = lax.dot_general(kw, qh, contract_last,
                                preferred_element_type=jnp.float32) + bias_ref[var, h, :nk, :]
        s_ctx = lax.dot_general(kc, qh, contract_last, preferred_element_type=jnp.float32)
        return s_loc, s_ctx

    def attend(g, h, s_loc, s_ctx):
        ws, n_rows, _ = geo[g]
        m = jnp.maximum(jnp.max(s_loc, axis=0, keepdims=True), jnp.max(s_ctx, axis=0, keepdims=True))
        p_loc = jnp.exp2(s_loc - m).astype(jnp.bfloat16)
        p_ctx = jnp.exp2(s_ctx - m).astype(jnp.bfloat16)
        pad_rows = (-n_rows) % (MXU_TILE // GRID_W)
        if pad_rows:
            p_loc = jnp.concatenate([p_loc, jnp.zeros((pad_rows * GRID_W, tq), p_loc.dtype)], axis=0)
        nk = (n_rows + pad_rows) * GRID_W
        ot = jnp.dot(vth_ref[h, :, ws * GRID_W:ws * GRID_W + nk], p_loc, preferred_element_type=jnp.float32)
        ot = ot + jnp.dot(vcth_ref[h], p_ctx, preferred_element_type=jnp.float32)
        return ot[:dh] / ot[dh:dh + 1]

    halves = []
    pending = scores(*tasks[0])
    for t, (g, h) in enumerate(tasks):
        current = pending
        if t + 1 < len(tasks):
            pending = scores(*tasks[t + 1])
        halves.append(attend(g, h, *current))
        if h == 1:
            o2 = jnp.concatenate(halves, axis=0).T
            halves = []
            gate = g_ref[0, g * tq:(g + 1) * tq, :].astype(jnp.float32)
            o_ref[0, g * tq:(g + 1) * tq, :] = (o2 * gate).astype(o_ref.dtype)


def _na_attention(rpb, proj, vt, proj_ctx, vct, rows, q_blk, k_blk, g_blk, kc_blk):
    b, l, _ = proj.shape
    lc = proj_ctx.shape[1]
    n_pairs = NA_HEADS // 2
    blk = lambda off: pl.BlockSpec((1, l, LANES), lambda hp, bi, rpb_ref, off=off: (bi, 0, off + hp))
    grid_spec = pltpu.PrefetchScalarGridSpec(
        num_scalar_prefetch=1,
        grid=(n_pairs, b),
        in_specs=[blk(q_blk), blk(k_blk),
                  pl.BlockSpec((1, LANES, l), lambda hp, bi, rpb_ref: (bi, hp, 0)),
                  blk(g_blk),
                  pl.BlockSpec((1, lc, LANES), lambda hp, bi, rpb_ref: (bi, 0, kc_blk + hp)),
                  pl.BlockSpec((1, LANES, lc), lambda hp, bi, rpb_ref: (bi, hp, 0))],
        out_specs=pl.BlockSpec((1, l, LANES), lambda hp, bi, rpb_ref: (bi, 0, hp)),
        scratch_shapes=[pltpu.VMEM((3, 2, NA_K_ROWS * GRID_W, NA_Q_ROWS * GRID_W), jnp.float32),
                        pltpu.VMEM((2 * NA_WIN_ROWS - 1, GRID_W, LANES), jnp.float32),
                        pltpu.VMEM((2, NA_HEAD_DIM + BF16_SUBLANES, l), jnp.bfloat16),
                        pltpu.VMEM((2, NA_HEAD_DIM + BF16_SUBLANES, lc), jnp.bfloat16)],
    )
    return pl.pallas_call(
        functools.partial(_na_kernel, rows=rows),
        out_shape=jax.ShapeDtypeStruct((b, l, NA_WIDTH), jnp.bfloat16),
        grid_spec=grid_spec,
        compiler_params=pltpu.CompilerParams(
            dimension_semantics=("arbitrary", "arbitrary"), vmem_limit_bytes=VMEM_LIMIT),
        name="na_attn",
    )(rpb.astype(jnp.float32).reshape(-1), proj, proj, vt, proj, proj_ctx, vct)


def _ret_kernel(lg_ref, q_ref, k_ref, v_ref, g_ref, kc_ref, vc_ref, ng_ref, o_ref, t_ref, s_ref):
    h = pl.program_id(1)
    lg_f = lg_ref[0, h]
    lg_b = lg_ref[1, h]
    c_len = RET_CHUNK
    l = q_ref.shape[1]
    lc = kc_ref.shape[1]
    n_chunks = l // c_len
    contract_last = (((1,), (1,)), ((), ()))

    ii = lax.broadcasted_iota(jnp.int32, (c_len, c_len), 0).astype(jnp.float32)
    jj = lax.broadcasted_iota(jnp.int32, (c_len, c_len), 1).astype(jnp.float32)
    dist = ii - jj
    decay = (jnp.where(dist >= 0, jnp.exp(lg_f * jnp.maximum(dist, 0.0)), 0.0)
             + jnp.where(dist <= 0, jnp.exp(lg_b * jnp.maximum(-dist, 0.0)), 0.0))
    ic = lax.broadcasted_iota(jnp.int32, (c_len, 1), 0).astype(jnp.float32)
    kdec_f = jnp.exp(lg_f * (c_len - 1 - ic))
    kdec_b = jnp.exp(lg_b * ic)
    qdec_f = jnp.exp(lg_f * (ic + 1.0))
    qdec_b = jnp.exp(lg_b * (c_len - ic))
    one = jnp.ones((1, 1), jnp.float32)
    cdec_f = jnp.exp(one * (lg_f * c_len))
    cdec_b = jnp.exp(one * (lg_b * c_len))

    jcr = lax.broadcasted_iota(jnp.int32, (1, lc), 1).astype(jnp.float32)
    kct = kc_ref[0].astype(jnp.float32).T
    vcx = vc_ref[0]
    s_f = jnp.dot((kct * jnp.exp(lg_f * (lc - 1 - jcr))).astype(jnp.bfloat16), vcx,
                  preferred_element_type=jnp.float32)
    s_b = jnp.dot((kct * jnp.exp(lg_b * jcr)).astype(jnp.bfloat16), vcx,
                  preferred_element_type=jnp.float32)

    for c in range(n_chunks):
        rows_c = slice(c * c_len, (c + 1) * c_len)
        kt = k_ref[0, rows_c, :].astype(jnp.float32).T.astype(jnp.bfloat16)
        vf = v_ref[0, rows_c, :].astype(jnp.float32)
        vw = jnp.concatenate([(vf * kdec_f).astype(jnp.bfloat16), (vf * kdec_b).astype(jnp.bfloat16)], axis=1)
        t_ref[c] = jnp.dot(kt, vw, preferred_element_type=jnp.float32)

    for c in range(n_chunks):
        s_ref[c, :, :RET_DIM] = s_f.astype(jnp.bfloat16)
        s_f = cdec_f * s_f + t_ref[c, :, :RET_DIM]
    for c in reversed(range(n_chunks)):
        s_ref[c, :, RET_DIM:] = s_b.astype(jnp.bfloat16)
        s_b = cdec_b * s_b + t_ref[c, :, RET_DIM:]

    ng = ng_ref[...]

    def scores(c):
        rows_c = slice(c * c_len, (c + 1) * c_len)
        qc = q_ref[0, rows_c, :]
        s = lax.dot_general(qc, k_ref[0, rows_c, :], contract_last, preferred_element_type=jnp.float32) * decay
        cross = jnp.dot(qc, s_ref[c], preferred_element_type=jnp.float32)
        return s.astype(jnp.bfloat16), cross[:, :RET_DIM] * qdec_f + cross[:, RET_DIM:] * qdec_b

    def finish(c, s, cross):
        rows_c = slice(c * c_len, (c + 1) * c_len)
        o = jnp.dot(s, v_ref[0, rows_c, :], preferred_element_type=jnp.float32) + cross
        o = o * lax.rsqrt(jnp.mean(o * o, axis=-1, keepdims=True) + EPS) * ng
        gate = g_ref[0, rows_c, :].astype(jnp.float32)
        o_ref[0, rows_c, :] = (o * gate).astype(o_ref.dtype)

    pending = scores(0)
    for c in range(n_chunks):
        current = pending
        if c + 1 < n_chunks:
            pending = scores(c + 1)
        finish(c, *current)


def _retention(lg, proj, proj_ctx, ret_norm_g, q_blk, k_blk, v_blk, g_blk, kc_blk, vc_blk):
    b, l, _ = proj.shape
    lc = proj_ctx.shape[1]
    blk = lambda off: pl.BlockSpec((1, l, LANES), lambda bi, h, lg_ref, off=off: (bi, 0, off + h))
    cblk = lambda off: pl.BlockSpec((1, lc, LANES), lambda bi, h, lg_ref, off=off: (bi, 0, off + h))
    n_chunks = l // RET_CHUNK
    grid_spec = pltpu.PrefetchScalarGridSpec(
        num_scalar_prefetch=1,
        grid=(b, RET_HEADS),
        in_specs=[blk(q_blk), blk(k_blk), blk(v_blk), blk(g_blk), cblk(kc_blk), cblk(vc_blk),
                  pl.BlockSpec((1, LANES), lambda bi, h, lg_ref: (0, h))],
        out_specs=pl.BlockSpec((1, l, LANES), lambda bi, h, lg_ref: (bi, 0, h)),
        scratch_shapes=[pltpu.VMEM((n_chunks, RET_DIM, 2 * RET_DIM), jnp.float32),
                        pltpu.VMEM((n_chunks, RET_DIM, 2 * RET_DIM), jnp.bfloat16)],
    )
    return pl.pallas_call(
        _ret_kernel,
        out_shape=jax.ShapeDtypeStruct((b, l, RET_WIDTH), jnp.bfloat16),
        grid_spec=grid_spec,
        compiler_params=pltpu.CompilerParams(
            dimension_semantics=("arbitrary", "arbitrary"), vmem_limit_bytes=VMEM_LIMIT),
        name="retention",
    )(lg, proj, proj, proj, proj, proj_ctx, proj_ctx, ret_norm_g.reshape(1, RET_WIDTH))


def _out_kernel(x_ref, gate_ref, yna_ref, yret_ref, w_ref, fg_ref, o_ref, *, sub):
    tm = x_ref.shape[1]

    def mix(j):
        rows = slice(j * sub, (j + 1) * sub)
        y = jnp.concatenate([yna_ref[0, rows, :], yret_ref[0, rows, :]], axis=1)
        return jnp.dot(y, w_ref[...], preferred_element_type=jnp.float32)

    def finish(j, y):
        rows = slice(j * sub, (j + 1) * sub)
        z = x_ref[0, rows, :] + gate_ref[0] * y
        ms = jnp.mean(z * z, axis=-1, keepdims=True)
        o_ref[0, rows, :] = z * lax.rsqrt(ms + EPS) * fg_ref[...]

    pending = mix(0)
    for j in range(tm // sub):
        current = pending
        if (j + 1) * sub < tm:
            pending = mix(j + 1)
        finish(j, current)


def _out_proj(x, gate, y_na, y_ret, w_out, final_g, tm, sub):
    b, l, d = x.shape
    assert l % tm == 0 and tm % sub == 0
    return pl.pallas_call(
        functools.partial(_out_kernel, sub=sub),
        out_shape=jax.ShapeDtypeStruct((b, l, d), jnp.float32),
        grid=(b, l // tm),
        in_specs=[pl.BlockSpec((1, tm, d), lambda bi, i: (bi, i, 0)),
                  pl.BlockSpec((1, 1, d), lambda bi, i: (bi, 0, 0)),
                  pl.BlockSpec((1, tm, NA_WIDTH), lambda bi, i: (bi, i, 0)),
                  pl.BlockSpec((1, tm, RET_WIDTH), lambda bi, i: (bi, i, 0)),
                  pl.BlockSpec((NA_WIDTH + RET_WIDTH, d), lambda bi, i: (0, 0)),
                  pl.BlockSpec((1, d), lambda bi, i: (0, 0))],
        out_specs=pl.BlockSpec((1, tm, d), lambda bi, i: (bi, i, 0)),
        compiler_params=pltpu.CompilerParams(
            dimension_semantics=("arbitrary", "arbitrary"), vmem_limit_bytes=VMEM_LIMIT),
        name="out_proj",
    )(x, gate, y_na, y_ret, w_out, final_g.reshape(1, d))


def _rotary_tables(l):
    half = RET_DIM // 2
    nf = half // 2
    t = np.arange(l)
    row = (t // GRID_W).astype(np.float64)
    col = (t % GRID_W).astype(np.float64)
    inv = ROPE_BASE ** (-np.arange(nf, dtype=np.float64) / nf)
    ang = np.concatenate([row[:, None] * inv, col[:, None] * inv], axis=-1)
    cos, sin = np.cos(ang), np.sin(ang)
    cosf = np.concatenate([cos, cos], axis=-1).astype(np.float32)
    sinf = np.concatenate([-sin, sin], axis=-1).astype(np.float32)
    return jnp.asarray(cosf), jnp.asarray(sinf)


def kernel(x, c, ctx, c_ctx, norm_g, w_ada, b_ada, w_in, na_rpb, ret_decay_fwd, ret_decay_bwd,
           ret_norm_g, w_out, final_norm_g):
    depth = norm_g.shape[0]
    assert depth == 1, "context stream update between layers is not implemented"
    b, l, d = x.shape
    rows = l // GRID_W
    i = 0

    cc = jnp.concatenate([c, c_ctx[None, :]], axis=0)
    pad = (-cc.shape[0]) % 8
    cc = jnp.pad(cc, ((0, pad), (0, 0)))
    mod = _adaln_mod(cc, w_ada[i], b_ada[i])
    shift, scale, gate = (mod[:b, None, :d], mod[:b, None, d:2 * d], mod[:b, None, 2 * d:])
    shift_c, scale_c = mod[b:b + 1, None, :d], mod[b:b + 1, None, d:2 * d]

    gc = GROUP_COLS
    w = w_in[i].astype(jnp.bfloat16)
    w_vt = lax.optimization_barrier(w_in[i][:, 2 * gc:3 * gc]).T.astype(jnp.bfloat16)
    cosf, sinf = _rotary_tables(l)
    blocks = gc // LANES

    lat_groups = ((0, "na_q"), (1, "plain"), (3, "silu"), (4, "rot"), (5, "rot_kscale"), (6, "plain"), (7, "silu"))
    proj, vt = _in_proj(x, scale, shift, norm_g[i], w, w_vt, cosf, sinf, lat_groups, tm=1024, sub=256)

    lc = ctx.shape[1]
    ctx_groups = ((1, "plain"), (5, "kscale"), (6, "plain"))
    proj_ctx, vct = _in_proj(ctx, scale_c, shift_c, norm_g[i], w, w_vt, cosf[:lc], sinf[:lc], ctx_groups,
                             tm=lc, sub=lc)

    y_na = _na_attention(na_rpb[i], proj, vt, proj_ctx, vct, rows,
                         q_blk=0, k_blk=blocks, g_blk=2 * blocks, kc_blk=0)

    lg = jnp.stack([-jnp.exp(ret_decay_fwd[i].astype(jnp.float32)),
                    -jnp.exp(ret_decay_bwd[i].astype(jnp.float32))])
    y_ret = _retention(lg, proj, proj_ctx, ret_norm_g[i],
                       q_blk=3 * blocks, k_blk=4 * blocks, v_blk=5 * blocks, g_blk=6 * blocks,
                       kc_blk=blocks, vc_blk=2 * blocks)

    return _out_proj(x, gate, y_na, y_ret, w_out[i].astype(jnp.bfloat16), final_norm_g, tm=1024, sub=256)
```

```python
import functools
import math

import jax
import jax.numpy as jnp
import numpy as np
from jax import lax
from jax.experimental import pallas as pl
from jax.experimental.pallas import tpu as pltpu

D_MODEL = 1024
GRID_W = 64
NA_HEAD_DIM = 64
NA_WIDTH = 512
NA_HEADS = 8
NA_WIN_ROWS = 8
NA_WIN_COLS = 16
RET_HEADS = 4
RET_WIDTH = 512
RET_DIM = 128
RET_CHUNK = 128
ROPE_BASE = 10000.0
EPS = 1e-6

LANES = 128
MXU_TILE = 256
BF16_SUBLANES = 16
GROUP_COLS = 512
NA_Q_ROWS = 4
NA_K_ROWS = 12
MASK_VALUE = -1e30
LOG2E = math.log2(math.e)

VMEM_LIMIT = 56 * 1024 * 1024


def _silu(v):
    return v * (1.0 / (1.0 + jnp.exp(-v)))


def _mod_kernel(c_ref, w_ref, b_ref, o_ref):
    a = _silu(c_ref[...])
    o_ref[...] = jnp.dot(a, w_ref[...], preferred_element_type=jnp.float32,
                         precision=lax.Precision.HIGHEST) + b_ref[...]


def _adaln_mod(cc, w_ada, b_ada):
    m, d = cc.shape
    n = w_ada.shape[1]
    tn = 512
    return pl.pallas_call(
        _mod_kernel,
        out_shape=jax.ShapeDtypeStruct((m, n), jnp.float32),
        grid=(n // tn,),
        in_specs=[pl.BlockSpec((m, d), lambda j: (0, 0)),
                  pl.BlockSpec((d, tn), lambda j: (0, j)),
                  pl.BlockSpec((1, tn), lambda j: (0, j))],
        out_specs=pl.BlockSpec((m, tn), lambda j: (0, j)),
        compiler_params=pltpu.CompilerParams(
            dimension_semantics=("arbitrary",), vmem_limit_bytes=VMEM_LIMIT),
        name="adaln_mod",
    )(cc, w_ada, b_ada.reshape(1, n))


def _cast_kernel(w_ref, o_ref, ot_ref, *, v_group):
    w = w_ref[...]
    o_ref[...] = w.astype(o_ref.dtype)
    ot_ref[...] = w[:, v_group * GROUP_COLS:(v_group + 1) * GROUP_COLS].T.astype(ot_ref.dtype)


def _cast_w_in(w, v_group):
    d, n = w.shape
    rb = 256
    return pl.pallas_call(
        functools.partial(_cast_kernel, v_group=v_group),
        out_shape=(jax.ShapeDtypeStruct((d, n), jnp.bfloat16),
                   jax.ShapeDtypeStruct((GROUP_COLS, d), jnp.bfloat16)),
        grid=(d // rb,),
        in_specs=[pl.BlockSpec((rb, n), lambda r: (r, 0))],
        out_specs=(pl.BlockSpec((rb, n), lambda r: (r, 0)),
                   pl.BlockSpec((GROUP_COLS, rb), lambda r: (0, r))),
        compiler_params=pltpu.CompilerParams(
            dimension_semantics=("arbitrary",), vmem_limit_bytes=VMEM_LIMIT),
        name="cast_w_in",
    )(w)


def _rotary(acc, cosf, sinf):
    outs = []
    for h in range(GROUP_COLS // LANES):
        xh = acc[:, h * LANES:(h + 1) * LANES]
        outs.append(xh * cosf + pltpu.roll(xh, LANES // 2, 1) * sinf)
    return jnp.concatenate(outs, axis=-1)


def _proj_kernel(x_ref, scale_ref, shift_ref, g_ref, w_ref, wvt_ref, cos_ref, sin_ref, o_ref, vt_ref,
                 *, groups, sub):
    tm = x_ref.shape[1]

    def normed(j):
        x = x_ref[0, j * sub:(j + 1) * sub, :]
        ms = jnp.mean(x * x, axis=-1, keepdims=True)
        h = x * lax.rsqrt(ms + EPS) * g_ref[...]
        h = h * (1.0 + scale_ref[0]) + shift_ref[0]
        return h.astype(jnp.bfloat16)

    def project(j, hb):
        rows = slice(j * sub, (j + 1) * sub)
        vt = lax.dot_general(wvt_ref[...], hb, (((1,), (1,)), ((), ())), preferred_element_type=jnp.float32)
        vt_ref[0, :, rows] = vt.astype(vt_ref.dtype)
        for gi, (src, kind) in enumerate(groups):
            acc = jnp.dot(hb, w_ref[:, src * GROUP_COLS:(src + 1) * GROUP_COLS],
                          preferred_element_type=jnp.float32)
            if kind == "na_q":
                acc = acc * (NA_HEAD_DIM ** -0.5 * LOG2E)
            elif kind == "silu":
                acc = _silu(acc)
            elif kind == "rot":
                acc = _rotary(acc, cos_ref[rows, :], sin_ref[rows, :])
            elif kind == "rot_kscale":
                acc = _rotary(acc, cos_ref[rows, :], sin_ref[rows, :]) * (RET_DIM ** -0.5)
            elif kind == "kscale":
                acc = acc * (RET_DIM ** -0.5)
            else:
                assert kind == "plain"
            o_ref[0, rows, gi * GROUP_COLS:(gi + 1) * GROUP_COLS] = acc.astype(o_ref.dtype)

    pending = normed(0)
    for j in range(tm // sub):
        current = pending
        if (j + 1) * sub < tm:
            pending = normed(j + 1)
        project(j, current)


def _in_proj(x, scale, shift, norm_g, w, w_vt, cosf, sinf, groups, tm, sub):
    b, l, d = x.shape
    n_in = w.shape[1]
    n = GROUP_COLS * len(groups)
    nv = w_vt.shape[0]
    assert l % tm == 0 and tm % sub == 0
    per_batch = scale.shape[0] > 1
    mod_map = (lambda bi, i: (bi, 0, 0)) if per_batch else (lambda bi, i: (0, 0, 0))
    return pl.pallas_call(
        functools.partial(_proj_kernel, groups=tuple(groups), sub=sub),
        out_shape=(jax.ShapeDtypeStruct((b, l, n), jnp.bfloat16),
                   jax.ShapeDtypeStruct((b, nv, l), jnp.bfloat16)),
        grid=(b, l // tm),
        in_specs=[pl.BlockSpec((1, tm, d), lambda bi, i: (bi, i, 0)),
                  pl.BlockSpec((1, 1, d), mod_map),
                  pl.BlockSpec((1, 1, d), mod_map),
                  pl.BlockSpec((1, d), lambda bi, i: (0, 0)),
                  pl.BlockSpec((d, n_in), lambda bi, i: (0, 0)),
                  pl.BlockSpec((nv, d), lambda bi, i: (0, 0)),
                  pl.BlockSpec((tm, LANES), lambda bi, i: (i, 0)),
                  pl.BlockSpec((tm, LANES), lambda bi, i: (i, 0))],
        out_specs=(pl.BlockSpec((1, tm, n), lambda bi, i: (bi, i, 0)),
                   pl.BlockSpec((1, nv, tm), lambda bi, i: (bi, 0, i))),
        compiler_params=pltpu.CompilerParams(
            dimension_semantics=("arbitrary", "arbitrary"), vmem_limit_bytes=VMEM_LIMIT),
        name="in_proj",
    )(x, scale, shift, norm_g.reshape(1, d), w, w_vt, cosf, sinf)


def _na_window_start(rq, rows):
    kr = min(NA_WIN_ROWS, rows)
    return int(np.clip(rq - kr // 2, 0, rows - kr))


def _na_geometry(rows):
    kr = min(NA_WIN_ROWS, rows)
    n_groups = rows // NA_Q_ROWS
    assert n_groups >= 3
    geo = []
    for g in range(n_groups):
        lo = _na_window_start(NA_Q_ROWS * g, rows)
        hi = _na_window_start(NA_Q_ROWS * g + NA_Q_ROWS - 1, rows) + kr
        geo.append((lo, hi - lo, 0 if g == 0 else (2 if g == n_groups - 1 else 1)))
    assert len({n for _, n, v in geo if v == 1}) == 1 and max(n for _, n, _ in geo) <= NA_K_ROWS
    return geo


def _na_row_offsets(rows):
    kr = min(NA_WIN_ROWS, rows)
    geo = _na_geometry(rows)
    table = []
    for rep in (0, 1, len(geo) - 1):
        start, n_rows, _ = geo[rep]
        per_i = []
        for i in range(NA_Q_ROWS):
            rq = NA_Q_ROWS * rep + i
            r0 = _na_window_start(rq, rows)
            per_i.append([start + j - rq + NA_WIN_ROWS - 1 if r0 <= start + j < r0 + kr else None
                          for j in range(n_rows)])
        table.append(per_i)
    return table


def _na_build_bias(rpb_ref, hp, bias_ref, base_ref, rows):
    n_dr = 2 * NA_WIN_ROWS - 1
    n_dc = 2 * NA_WIN_COLS - 1
    ck = lax.broadcasted_iota(jnp.int32, (GRID_W, LANES), 0)
    lane = lax.broadcasted_iota(jnp.int32, (GRID_W, LANES), 1)
    left = lane < GRID_W
    cq = jnp.where(left, lane, lane - GRID_W)
    c0 = jnp.clip(cq - NA_WIN_COLS // 2, 0, GRID_W - NA_WIN_COLS)
    valid_c = (ck >= c0) & (ck < c0 + NA_WIN_COLS)
    dc = jnp.clip(ck - cq + NA_WIN_COLS - 1, 0, n_dc - 1)

    offsets = _na_row_offsets(rows)
    users = {}
    for var in range(3):
        for j in range(len(offsets[var][0])):
            for ip in range(NA_Q_ROWS // 2):
                key = (offsets[var][2 * ip][j], offsets[var][2 * ip + 1][j])
                users.setdefault(key, []).append((var, j, ip))

    used = sorted({dr for pair in users for dr in pair if dr is not None})
    masked = jnp.full((GRID_W, LANES), MASK_VALUE, jnp.float32)
    for h in range(2):
        head = 2 * hp + h
        for dr in used:
            def body(d, acc, dr=dr, head=head):
                return jnp.where(dc == d, rpb_ref[(head * n_dr + dr) * n_dc + d], acc)
            acc = lax.fori_loop(0, n_dc, body, jnp.zeros((GRID_W, LANES), jnp.float32), unroll=True)
            base_ref[dr] = jnp.where(valid_c, acc * LOG2E, MASK_VALUE)
        for (dr_l, dr_r), dests in users.items():
            t_l = masked if dr_l is None else base_ref[dr_l]
            t_r = masked if dr_r is None else base_ref[dr_r]
            tile = t_l if dr_l == dr_r else jnp.where(left, t_l, t_r)
            for var, j, ip in dests:
                bias_ref[var, h, j * GRID_W:(j + 1) * GRID_W, ip * LANES:(ip + 1) * LANES] = tile


def _na_program(q_ref, k_ref, vt_ref, g_ref, kc_ref, vct_ref, o_ref, bias_ref, vth_ref, vcth_ref, rows):
    geo = _na_geometry(rows)
    tq = NA_Q_ROWS * GRID_W
    dh = NA_HEAD_DIM

    lane = lax.broadcasted_iota(jnp.int32, (1, LANES), 1)
    head0_lanes = lane < dh
    n_ones = vth_ref.shape[1] - dh
    for h in range(2):
        vth_ref[h, :dh, :] = vt_ref[0, h * dh:(h + 1) * dh, :]
        vth_ref[h, dh:, :] = jnp.ones((n_ones, vt_ref.shape[2]), vt_ref.dtype)
        vcth_ref[h, :dh, :] = vct_ref[0, h * dh:(h + 1) * dh, :]
        vcth_ref[h, dh:, :] = jnp.ones((n_ones, vct_ref.shape[2]), vct_ref.dtype)

    kc = kc_ref[0]
    contract_last = (((1,), (1,)), ((), ()))
    tasks = [(g, h) for g in range(len(geo)) for h in range(2)]

    def scores(t):
        g, h = tasks[t]
        ws, n_rows, var = geo[g]
        nk = n_rows * GRID_W
        qg = q_ref[0, g * tq:(g + 1) * tq, :]
        kw = k_ref[0, ws * GRID_W:ws * GRID_W + nk, :]
        sel = head0_lanes if h == 0 else jnp.logical_not(head0_lanes)
        qh = jnp.where(sel, qg, jnp.zeros_like(qg))
        s_loc = lax.dot_general(kw, qh, contract_last,
                                preferred_element_type=jnp.float32) + bias_ref[var, h, :nk, :]
        s_ctx = lax.dot_general(kc, qh, contract_last, preferred_element_type=jnp.float32)
        return s_loc, s_ctx

    halves = []

    def attend(t, s):
        g, h = tasks[t]
        s_loc, s_ctx = s
        ws, n_rows, _ = geo[g]
        m = jnp.maximum(jnp.max(s_loc, axis=0, keepdims=True), jnp.max(s_ctx, axis=0, keepdims=True))
        p_loc = jnp.exp2(s_loc - m).astype(jnp.bfloat16)
        p_ctx = jnp.exp2(s_ctx - m).astype(jnp.bfloat16)
        pad_rows = (-n_rows) % (MXU_TILE // GRID_W)
        if pad_rows:
            p_loc = jnp.concatenate([p_loc, jnp.zeros((pad_rows * GRID_W, tq), p_loc.dtype)], axis=0)
        nk = (n_rows + pad_rows) * GRID_W
        ot = jnp.dot(vth_ref[h, :, ws * GRID_W:ws * GRID_W + nk], p_loc, preferred_element_type=jnp.float32)
        ot = ot + jnp.dot(vcth_ref[h], p_ctx, preferred_element_type=jnp.float32)
        halves.append(ot[:dh] / ot[dh:dh + 1])
        if h == 1:
            o2 = jnp.concatenate(halves, axis=0).T
            halves.clear()
            gate = g_ref[0, g * tq:(g + 1) * tq, :].astype(jnp.float32)
            o_ref[0, g * tq:(g + 1) * tq, :] = (o2 * gate).astype(o_ref.dtype)

    return len(tasks), scores, attend


def _ret_program(lg_f, lg_b, q_ref, k_ref, v_ref, g_ref, kc_ref, vc_ref, ng_ref, o_ref, t_ref, s_ref):
    c_len = RET_CHUNK
    l = q_ref.shape[1]
    lc = kc_ref.shape[1]
    n_chunks = l // c_len
    contract_last = (((1,), (1,)), ((), ()))

    ii = lax.broadcasted_iota(jnp.int32, (c_len, c_len), 0).astype(jnp.float32)
    jj = lax.broadcasted_iota(jnp.int32, (c_len, c_len), 1).astype(jnp.float32)
    dist = ii - jj
    decay = (jnp.where(dist >= 0, jnp.exp(lg_f * jnp.maximum(dist, 0.0)), 0.0)
             + jnp.where(dist <= 0, jnp.exp(lg_b * jnp.maximum(-dist, 0.0)), 0.0))
    ic = lax.broadcasted_iota(jnp.int32, (c_len, 1), 0).astype(jnp.float32)
    kdec_f = jnp.exp(lg_f * (c_len - 1 - ic))
    kdec_b = jnp.exp(lg_b * ic)
    qdec_f = jnp.exp(lg_f * (ic + 1.0))
    qdec_b = jnp.exp(lg_b * (c_len - ic))
    one = jnp.ones((1, 1), jnp.float32)
    cdec_f = jnp.exp(one * (lg_f * c_len))
    cdec_b = jnp.exp(one * (lg_b * c_len))

    jcr = lax.broadcasted_iota(jnp.int32, (1, lc), 1).astype(jnp.float32)
    kct = kc_ref[0].astype(jnp.float32).T
    vcx = vc_ref[0]
    s_f = jnp.dot((kct * jnp.exp(lg_f * (lc - 1 - jcr))).astype(jnp.bfloat16), vcx,
                  preferred_element_type=jnp.float32)
    s_b = jnp.dot((kct * jnp.exp(lg_b * jcr)).astype(jnp.bfloat16), vcx,
                  preferred_element_type=jnp.float32)

    for c in range(n_chunks):
        rows_c = slice(c * c_len, (c + 1) * c_len)
        kt = k_ref[0, rows_c, :].astype(jnp.float32).T.astype(jnp.bfloat16)
        vf = v_ref[0, rows_c, :].astype(jnp.float32)
        vw = jnp.concatenate([(vf * kdec_f).astype(jnp.bfloat16), (vf * kdec_b).astype(jnp.bfloat16)], axis=1)
        t_ref[c] = jnp.dot(kt, vw, preferred_element_type=jnp.float32)

    for c in range(n_chunks):
        s_ref[c, :, :RET_DIM] = s_f.astype(jnp.bfloat16)
        s_f = cdec_f * s_f + t_ref[c, :, :RET_DIM]
    for c in reversed(range(n_chunks)):
        s_ref[c, :, RET_DIM:] = s_b.astype(jnp.bfloat16)
        s_b = cdec_b * s_b + t_ref[c, :, RET_DIM:]

    ng = ng_ref[...]

    def scores(c):
        rows_c = slice(c * c_len, (c + 1) * c_len)
        qc = q_ref[0, rows_c, :]
        s = lax.dot_general(qc, k_ref[0, rows_c, :], contract_last, preferred_element_type=jnp.float32) * decay
        cross = jnp.dot(qc, s_ref[c], preferred_element_type=jnp.float32)
        return s.astype(jnp.bfloat16), cross[:, :RET_DIM] * qdec_f + cross[:, RET_DIM:] * qdec_b

    def finish(c, sc):
        s, cross = sc
        rows_c = slice(c * c_len, (c + 1) * c_len)
        o = jnp.dot(s, v_ref[0, rows_c, :], preferred_element_type=jnp.float32) + cross
        o = o * lax.rsqrt(jnp.mean(o * o, axis=-1, keepdims=True) + EPS) * ng
        gate = g_ref[0, rows_c, :].astype(jnp.float32)
        o_ref[0, rows_c, :] = (o * gate).astype(o_ref.dtype)

    return n_chunks, scores, finish


def _mixer_kernel(rpb_ref, lg_ref,
                  aq_ref, ak_ref, avt_ref, ag_ref, akc_ref, avct_ref,
                  rq_ref, rk_ref, rv_ref, rg_ref, rkc_ref, rvc_ref, ng_ref,
                  yna_ref, yret_ref,
                  bias_ref, base_ref, vth_ref, vcth_ref, t_ref, s_ref, *, rows):
    i = pl.program_id(0)

    @pl.when(pl.program_id(1) == 0)
    def _():
        _na_build_bias(rpb_ref, i, bias_ref, base_ref, rows)

    n_ret, ret_scores, ret_finish = _ret_program(lg_ref[0, i], lg_ref[1, i], rq_ref, rk_ref, rv_ref, rg_ref,
                                                 rkc_ref, rvc_ref, ng_ref, yret_ref, t_ref, s_ref)
    n_na, na_scores, na_attend = _na_program(aq_ref, ak_ref, avt_ref, ag_ref, akc_ref, avct_ref, yna_ref,
                                             bias_ref, vth_ref, vcth_ref, rows)

    pend_na, pend_ret = na_scores(0), ret_scores(0)
    for t in range(max(n_na, n_ret)):
        cur_na, cur_ret = pend_na, pend_ret
        if t + 1 < n_na:
            pend_na = na_scores(t + 1)
        if t + 1 < n_ret:
            pend_ret = ret_scores(t + 1)
        if t < n_na:
            na_attend(t, cur_na)
        if t < n_ret:
            ret_finish(t, cur_ret)


def _mixers(rpb, lg, proj, vt, proj_ctx, vct, ret_norm_g, rows, cols, ctx_cols):
    b, l, _ = proj.shape
    lc = proj_ctx.shape[1]
    assert NA_HEADS // 2 == RET_HEADS
    blk = lambda name: pl.BlockSpec((1, l, LANES), lambda i, bi, *_, off=cols[name]: (bi, 0, off + i))
    cblk = lambda name: pl.BlockSpec((1, lc, LANES), lambda i, bi, *_, off=ctx_cols[name]: (bi, 0, off + i))
    n_chunks = l // RET_CHUNK
    grid_spec = pltpu.PrefetchScalarGridSpec(
        num_scalar_prefetch=2,
        grid=(RET_HEADS, b),
        in_specs=[blk("na_q"), blk("na_k"),
                  pl.BlockSpec((1, LANES, l), lambda i, bi, *_: (bi, i, 0)),
                  blk("na_g"), cblk("na_k"),
                  pl.BlockSpec((1, LANES, lc), lambda i, bi, *_: (bi, i, 0)),
                  blk("r_q"), blk("r_k"), blk("r_v"), blk("r_g"), cblk("r_k"), cblk("r_v"),
                  pl.BlockSpec((1, LANES), lambda i, bi, *_: (0, i))],
        out_specs=(pl.BlockSpec((1, l, LANES), lambda i, bi, *_: (bi, 0, i)),
                   pl.BlockSpec((1, l, LANES), lambda i, bi, *_: (bi, 0, i))),
        scratch_shapes=[pltpu.VMEM((3, 2, NA_K_ROWS * GRID_W, NA_Q_ROWS * GRID_W), jnp.float32),
                        pltpu.VMEM((2 * NA_WIN_ROWS - 1, GRID_W, LANES), jnp.float32),
                        pltpu.VMEM((2, NA_HEAD_DIM + BF16_SUBLANES, l), jnp.bfloat16),
                        pltpu.VMEM((2, NA_HEAD_DIM + BF16_SUBLANES, lc), jnp.bfloat16),
                        pltpu.VMEM((n_chunks, RET_DIM, 2 * RET_DIM), jnp.float32),
                        pltpu.VMEM((n_chunks, RET_DIM, 2 * RET_DIM), jnp.bfloat16)],
    )
    return pl.pallas_call(
        functools.partial(_mixer_kernel, rows=rows),
        out_shape=(jax.ShapeDtypeStruct((b, l, NA_WIDTH), jnp.bfloat16),
                   jax.ShapeDtypeStruct((b, l, RET_WIDTH), jnp.bfloat16)),
        grid_spec=grid_spec,
        compiler_params=pltpu.CompilerParams(
            dimension_semantics=("arbitrary", "arbitrary"), vmem_limit_bytes=VMEM_LIMIT),
        name="mixers",
    )(rpb.astype(jnp.float32).reshape(-1), lg,
      proj, proj, vt, proj, proj_ctx, vct,
      proj, proj, proj, proj, proj_ctx, proj_ctx, ret_norm_g.reshape(1, RET_WIDTH))


def _out_kernel(x_ref, gate_ref, yna_ref, yret_ref, w_ref, fg_ref, o_ref, *, sub):
    tm = x_ref.shape[1]

    def mix(j):
        rows = slice(j * sub, (j + 1) * sub)
        y = jnp.concatenate([yna_ref[0, rows, :], yret_ref[0, rows, :]], axis=1)
        return jnp.dot(y, w_ref[...], preferred_element_type=jnp.float32)

    def finish(j, y):
        rows = slice(j * sub, (j + 1) * sub)
        z = x_ref[0, rows, :] + gate_ref[0] * y
        ms = jnp.mean(z * z, axis=-1, keepdims=True)
        o_ref[0, rows, :] = z * lax.rsqrt(ms + EPS) * fg_ref[...]

    pending = mix(0)
    for j in range(tm // sub):
        current = pending
        if (j + 1) * sub < tm:
            pending = mix(j + 1)
        finish(j, current)


def _out_proj(x, gate, y_na, y_ret, w_out, final_g, tm, sub):
    b, l, d = x.shape
    assert l % tm == 0 and tm % sub == 0
    return pl.pallas_call(
        functools.partial(_out_kernel, sub=sub),
        out_shape=jax.ShapeDtypeStruct((b, l, d), jnp.float32),
        grid=(b, l // tm),
        in_specs=[pl.BlockSpec((1, tm, d), lambda bi, i: (bi, i, 0)),
                  pl.BlockSpec((1, 1, d), lambda bi, i: (bi, 0, 0)),
                  pl.BlockSpec((1, tm, NA_WIDTH), lambda bi, i: (bi, i, 0)),
                  pl.BlockSpec((1, tm, RET_WIDTH), lambda bi, i: (bi, i, 0)),
                  pl.BlockSpec((NA_WIDTH + RET_WIDTH, d), lambda bi, i: (0, 0)),
                  pl.BlockSpec((1, d), lambda bi, i: (0, 0))],
        out_specs=pl.BlockSpec((1, tm, d), lambda bi, i: (bi, i, 0)),
        compiler_params=pltpu.CompilerParams(
            dimension_semantics=("arbitrary", "arbitrary"), vmem_limit_bytes=VMEM_LIMIT),
        name="out_proj",
    )(x, gate, y_na, y_ret, w_out, final_g.reshape(1, d))


def _rotary_tables(l):
    half = RET_DIM // 2
    nf = half // 2
    t = np.arange(l)
    row = (t // GRID_W).astype(np.float64)
    col = (t % GRID_W).astype(np.float64)
    inv = ROPE_BASE ** (-np.arange(nf, dtype=np.float64) / nf)
    ang = np.concatenate([row[:, None] * inv, col[:, None] * inv], axis=-1)
    cos, sin = np.cos(ang), np.sin(ang)
    cosf = np.concatenate([cos, cos], axis=-1).astype(np.float32)
    sinf = np.concatenate([-sin, sin], axis=-1).astype(np.float32)
    return jnp.asarray(cosf), jnp.asarray(sinf)


def kernel(x, c, ctx, c_ctx, norm_g, w_ada, b_ada, w_in, na_rpb, ret_decay_fwd, ret_decay_bwd,
           ret_norm_g, w_out, final_norm_g):
    depth = norm_g.shape[0]
    assert depth == 1, "context stream update between layers is not implemented"
    b, l, d = x.shape
    rows = l // GRID_W
    i = 0

    cc = jnp.concatenate([c, c_ctx[None, :]], axis=0)
    pad = (-cc.shape[0]) % 8
    cc = jnp.pad(cc, ((0, pad), (0, 0)))
    mod = _adaln_mod(cc, w_ada[i], b_ada[i])
    shift, scale, gate = (mod[:b, None, :d], mod[:b, None, d:2 * d], mod[:b, None, 2 * d:])
    shift_c, scale_c = mod[b:b + 1, None, :d], mod[b:b + 1, None, d:2 * d]

    w, w_vt = _cast_w_in(w_in[i], v_group=2)
    cosf, sinf = _rotary_tables(l)
    blocks = GROUP_COLS // LANES

    lat_groups = ((0, "na_q"), (1, "plain"), (3, "silu"), (4, "rot"), (5, "rot_kscale"), (6, "plain"), (7, "silu"))
    lat_names = ("na_q", "na_k", "na_g", "r_q", "r_k", "r_v", "r_g")
    proj, vt = _in_proj(x, scale, shift, norm_g[i], w, w_vt, cosf, sinf, lat_groups, tm=1024, sub=256)

    lc = ctx.shape[1]
    ctx_groups = ((1, "plain"), (5, "kscale"), (6, "plain"))
    ctx_names = ("na_k", "r_k", "r_v")
    proj_ctx, vct = _in_proj(ctx, scale_c, shift_c, norm_g[i], w, w_vt, cosf[:lc], sinf[:lc], ctx_groups,
                             tm=lc, sub=lc)

    lg = jnp.stack([-jnp.exp(ret_decay_fwd[i].astype(jnp.float32)),
                    -jnp.exp(ret_decay_bwd[i].astype(jnp.float32))])
    y_na, y_ret = _mixers(na_rpb[i], lg, proj, vt, proj_ctx, vct, ret_norm_g[i], rows,
                          cols={name: k * blocks for k, name in enumerate(lat_names)},
                          ctx_cols={name: k * blocks for k, name in enumerate(ctx_names)})

    return _out_proj(x, gate, y_na, y_ret, w_out[i].astype(jnp.bfloat16), final_norm_g, tm=1024, sub=256)
```

```python
import functools
import math

import jax
import jax.numpy as jnp
import numpy as np
from jax import lax
from jax.experimental import pallas as pl
from jax.experimental.pallas import tpu as pltpu

D_MODEL = 1024
GRID_W = 64
NA_HEAD_DIM = 64
NA_WIDTH = 512
NA_HEADS = 8
NA_WIN_ROWS = 8
NA_WIN_COLS = 16
RET_HEADS = 4
RET_WIDTH = 512
RET_DIM = 128
RET_CHUNK = 128
ROPE_BASE = 10000.0
EPS = 1e-6

LANES = 128
MXU_TILE = 256
BF16_SUBLANES = 16
GROUP_COLS = 512
NA_Q_ROWS = 4
NA_K_ROWS = 12
MASK_VALUE = -1e30
LOG2E = math.log2(math.e)

VMEM_LIMIT = 56 * 1024 * 1024


def _silu(v):
    return v * (1.0 / (1.0 + jnp.exp(-v)))


def _mod_kernel(c_ref, w_ref, b_ref, o_ref):
    a = _silu(c_ref[...])
    o_ref[...] = jnp.dot(a, w_ref[...], preferred_element_type=jnp.float32,
                         precision=lax.Precision.HIGHEST) + b_ref[...]


def _adaln_mod(cc, w_ada, b_ada):
    m, d = cc.shape
    n = w_ada.shape[1]
    tn = 512
    return pl.pallas_call(
        _mod_kernel,
        out_shape=jax.ShapeDtypeStruct((m, n), jnp.float32),
        grid=(n // tn,),
        in_specs=[pl.BlockSpec((m, d), lambda j: (0, 0)),
                  pl.BlockSpec((d, tn), lambda j: (0, j)),
                  pl.BlockSpec((1, tn), lambda j: (0, j))],
        out_specs=pl.BlockSpec((m, tn), lambda j: (0, j)),
        compiler_params=pltpu.CompilerParams(
            dimension_semantics=("arbitrary",), vmem_limit_bytes=VMEM_LIMIT),
        name="adaln_mod",
    )(cc, w_ada, b_ada.reshape(1, n))


def _cast_kernel(w_ref, o_ref, ot_ref, *, v_group):
    w = w_ref[...]
    o_ref[...] = w.astype(o_ref.dtype)
    ot_ref[...] = w[:, v_group * GROUP_COLS:(v_group + 1) * GROUP_COLS].T.astype(ot_ref.dtype)


def _cast_w_in(w, v_group):
    d, n = w.shape
    rb = 256
    return pl.pallas_call(
        functools.partial(_cast_kernel, v_group=v_group),
        out_shape=(jax.ShapeDtypeStruct((d, n), jnp.bfloat16),
                   jax.ShapeDtypeStruct((GROUP_COLS, d), jnp.bfloat16)),
        grid=(d // rb,),
        in_specs=[pl.BlockSpec((rb, n), lambda r: (r, 0))],
        out_specs=(pl.BlockSpec((rb, n), lambda r: (r, 0)),
                   pl.BlockSpec((GROUP_COLS, rb), lambda r: (0, r))),
        compiler_params=pltpu.CompilerParams(
            dimension_semantics=("arbitrary",), vmem_limit_bytes=VMEM_LIMIT),
        name="cast_w_in",
    )(w)


def _rotary(acc, cosf, sinf):
    outs = []
    for h in range(GROUP_COLS // LANES):
        xh = acc[:, h * LANES:(h + 1) * LANES]
        outs.append(xh * cosf + pltpu.roll(xh, LANES // 2, 1) * sinf)
    return jnp.concatenate(outs, axis=-1)


def _proj_kernel(x_ref, scale_ref, shift_ref, g_ref, w_ref, wvt_ref, cos_ref, sin_ref, o_ref, vt_ref,
                 *, groups, sub):
    tm = x_ref.shape[1]

    def normed(j):
        x = x_ref[0, j * sub:(j + 1) * sub, :]
        ms = jnp.mean(x * x, axis=-1, keepdims=True)
        h = x * lax.rsqrt(ms + EPS) * g_ref[...]
        h = h * (1.0 + scale_ref[0]) + shift_ref[0]
        return h.astype(jnp.bfloat16)

    def project(j, hb):
        rows = slice(j * sub, (j + 1) * sub)
        vt = lax.dot_general(wvt_ref[...], hb, (((1,), (1,)), ((), ())), preferred_element_type=jnp.float32)
        vt_ref[0, :, rows] = vt.astype(vt_ref.dtype)
        for gi, (src, kind) in enumerate(groups):
            acc = jnp.dot(hb, w_ref[:, src * GROUP_COLS:(src + 1) * GROUP_COLS],
                          preferred_element_type=jnp.float32)
            if kind == "na_q":
                acc = acc * (NA_HEAD_DIM ** -0.5 * LOG2E)
            elif kind == "silu":
                acc = _silu(acc)
            elif kind == "rot":
                acc = _rotary(acc, cos_ref[rows, :], sin_ref[rows, :])
            elif kind == "rot_kscale":
                acc = _rotary(acc, cos_ref[rows, :], sin_ref[rows, :]) * (RET_DIM ** -0.5)
            elif kind == "kscale":
                acc = acc * (RET_DIM ** -0.5)
            else:
                assert kind == "plain"
            per_group = GROUP_COLS // LANES
            for k in range(per_group):
                o_ref[0, gi * per_group + k, rows, :] = acc[:, k * LANES:(k + 1) * LANES].astype(o_ref.dtype)

    pending = normed(0)
    for j in range(tm // sub):
        current = pending
        if (j + 1) * sub < tm:
            pending = normed(j + 1)
        project(j, current)


def _in_proj(x, scale, shift, norm_g, w, w_vt, cosf, sinf, groups, tm, sub):
    b, l, d = x.shape
    n_in = w.shape[1]
    n = GROUP_COLS * len(groups)
    nv = w_vt.shape[0]
    assert l % tm == 0 and tm % sub == 0
    per_batch = scale.shape[0] > 1
    mod_map = (lambda bi, i: (bi, 0, 0)) if per_batch else (lambda bi, i: (0, 0, 0))
    return pl.pallas_call(
        functools.partial(_proj_kernel, groups=tuple(groups), sub=sub),
        out_shape=(jax.ShapeDtypeStruct((b, n // LANES, l, LANES), jnp.bfloat16),
                   jax.ShapeDtypeStruct((b, nv, l), jnp.bfloat16)),
        grid=(b, l // tm),
        in_specs=[pl.BlockSpec((1, tm, d), lambda bi, i: (bi, i, 0)),
                  pl.BlockSpec((1, 1, d), mod_map),
                  pl.BlockSpec((1, 1, d), mod_map),
                  pl.BlockSpec((1, d), lambda bi, i: (0, 0)),
                  pl.BlockSpec((d, n_in), lambda bi, i: (0, 0)),
                  pl.BlockSpec((nv, d), lambda bi, i: (0, 0)),
                  pl.BlockSpec((tm, LANES), lambda bi, i: (i, 0)),
                  pl.BlockSpec((tm, LANES), lambda bi, i: (i, 0))],
        out_specs=(pl.BlockSpec((1, n // LANES, tm, LANES), lambda bi, i: (bi, 0, i, 0)),
                   pl.BlockSpec((1, nv, tm), lambda bi, i: (bi, 0, i))),
        compiler_params=pltpu.CompilerParams(
            dimension_semantics=("arbitrary", "arbitrary"), vmem_limit_bytes=VMEM_LIMIT),
        name="in_proj",
    )(x, scale, shift, norm_g.reshape(1, d), w, w_vt, cosf, sinf)


def _na_window_start(rq, rows):
    kr = min(NA_WIN_ROWS, rows)
    return int(np.clip(rq - kr // 2, 0, rows - kr))


def _na_geometry(rows):
    kr = min(NA_WIN_ROWS, rows)
    n_groups = rows // NA_Q_ROWS
    assert n_groups >= 3
    geo = []
    for g in range(n_groups):
        lo = _na_window_start(NA_Q_ROWS * g, rows)
        hi = _na_window_start(NA_Q_ROWS * g + NA_Q_ROWS - 1, rows) + kr
        geo.append((lo, hi - lo, 0 if g == 0 else (2 if g == n_groups - 1 else 1)))
    assert len({n for _, n, v in geo if v == 1}) == 1 and max(n for _, n, _ in geo) <= NA_K_ROWS
    return geo


def _na_row_offsets(rows):
    kr = min(NA_WIN_ROWS, rows)
    geo = _na_geometry(rows)
    table = []
    for rep in (0, 1, len(geo) - 1):
        start, n_rows, _ = geo[rep]
        per_i = []
        for i in range(NA_Q_ROWS):
            rq = NA_Q_ROWS * rep + i
            r0 = _na_window_start(rq, rows)
            per_i.append([start + j - rq + NA_WIN_ROWS - 1 if r0 <= start + j < r0 + kr else None
                          for j in range(n_rows)])
        table.append(per_i)
    return table


def _na_build_bias(rpb_ref, hp, bias_ref, base_ref, rows):
    n_dr = 2 * NA_WIN_ROWS - 1
    n_dc = 2 * NA_WIN_COLS - 1
    ck = lax.broadcasted_iota(jnp.int32, (GRID_W, LANES), 0)
    lane = lax.broadcasted_iota(jnp.int32, (GRID_W, LANES), 1)
    left = lane < GRID_W
    cq = jnp.where(left, lane, lane - GRID_W)
    c0 = jnp.clip(cq - NA_WIN_COLS // 2, 0, GRID_W - NA_WIN_COLS)
    valid_c = (ck >= c0) & (ck < c0 + NA_WIN_COLS)
    dc = jnp.clip(ck - cq + NA_WIN_COLS - 1, 0, n_dc - 1)

    offsets = _na_row_offsets(rows)
    users = {}
    for var in range(3):
        for j in range(len(offsets[var][0])):
            for ip in range(NA_Q_ROWS // 2):
                key = (offsets[var][2 * ip][j], offsets[var][2 * ip + 1][j])
                users.setdefault(key, []).append((var, j, ip))

    used = sorted({dr for pair in users for dr in pair if dr is not None})
    masked = jnp.full((GRID_W, LANES), MASK_VALUE, jnp.float32)
    for h in range(2):
        head = 2 * hp + h
        for dr in used:
            def body(d, acc, dr=dr, head=head):
                return jnp.where(dc == d, rpb_ref[(head * n_dr + dr) * n_dc + d], acc)
            acc = lax.fori_loop(0, n_dc, body, jnp.zeros((GRID_W, LANES), jnp.float32), unroll=True)
            base_ref[dr] = jnp.where(valid_c, acc * LOG2E, MASK_VALUE)
        for (dr_l, dr_r), dests in users.items():
            t_l = masked if dr_l is None else base_ref[dr_l]
            t_r = masked if dr_r is None else base_ref[dr_r]
            tile = t_l if dr_l == dr_r else jnp.where(left, t_l, t_r)
            for var, j, ip in dests:
                bias_ref[var, h, j * GRID_W:(j + 1) * GRID_W, ip * LANES:(ip + 1) * LANES] = tile


def _na_program(q_ref, k_ref, vt_ref, g_ref, kc_ref, vct_ref, o_ref, bias_ref, vth_ref, vcth_ref, rows):
    geo = _na_geometry(rows)
    tq = NA_Q_ROWS * GRID_W
    dh = NA_HEAD_DIM

    lane = lax.broadcasted_iota(jnp.int32, (1, LANES), 1)
    head0_lanes = lane < dh
    n_ones = vth_ref.shape[1] - dh
    for h in range(2):
        vth_ref[h, :dh, :] = vt_ref[0, h * dh:(h + 1) * dh, :]
        vth_ref[h, dh:, :] = jnp.ones((n_ones, vt_ref.shape[2]), vt_ref.dtype)
        vcth_ref[h, :dh, :] = vct_ref[0, h * dh:(h + 1) * dh, :]
        vcth_ref[h, dh:, :] = jnp.ones((n_ones, vct_ref.shape[2]), vct_ref.dtype)

    kc = kc_ref[0]
    contract_last = (((1,), (1,)), ((), ()))
    tasks = [(g, h) for g in range(len(geo)) for h in range(2)]

    def scores(t):
        g, h = tasks[t]
        ws, n_rows, var = geo[g]
        nk = n_rows * GRID_W
        qg = q_ref[0, g * tq:(g + 1) * tq, :]
        kw = k_ref[0, ws * GRID_W:ws * GRID_W + nk, :]
        sel = head0_lanes if h == 0 else jnp.logical_not(head0_lanes)
        qh = jnp.where(sel, qg, jnp.zeros_like(qg))
        s_loc = lax.dot_general(kw, qh, contract_last,
                                preferred_element_type=jnp.float32) + bias_ref[var, h, :nk, :]
        s_ctx = lax.dot_general(kc, qh, contract_last, preferred_element_type=jnp.float32)
        return s_loc, s_ctx

    halves = []

    def attend(t, s):
        g, h = tasks[t]
        s_loc, s_ctx = s
        ws, n_rows, _ = geo[g]
        m = jnp.maximum(jnp.max(s_loc, axis=0, keepdims=True), jnp.max(s_ctx, axis=0, keepdims=True))
        p_loc = jnp.exp2(s_loc - m).astype(jnp.bfloat16)
        p_ctx = jnp.exp2(s_ctx - m).astype(jnp.bfloat16)
        pad_rows = (-n_rows) % (MXU_TILE // GRID_W)
        if pad_rows:
            p_loc = jnp.concatenate([p_loc, jnp.zeros((pad_rows * GRID_W, tq), p_loc.dtype)], axis=0)
        nk = (n_rows + pad_rows) * GRID_W
        ot = jnp.dot(vth_ref[h, :, ws * GRID_W:ws * GRID_W + nk], p_loc, preferred_element_type=jnp.float32)
        ot = ot + jnp.dot(vcth_ref[h], p_ctx, preferred_element_type=jnp.float32)
        halves.append(ot[:dh] / ot[dh:dh + 1])
        if h == 1:
            o2 = jnp.concatenate(halves, axis=0).T
            halves.clear()
            gate = g_ref[0, g * tq:(g + 1) * tq, :].astype(jnp.float32)
            o_ref[0, g * tq:(g + 1) * tq, :] = (o2 * gate).astype(o_ref.dtype)

    return len(tasks), scores, attend


def _ret_program(lg_f, lg_b, q_ref, k_ref, v_ref, g_ref, kc_ref, vc_ref, ng_ref, o_ref, t_ref, s_ref):
    c_len = RET_CHUNK
    l = q_ref.shape[1]
    lc = kc_ref.shape[1]
    n_chunks = l // c_len
    contract_last = (((1,), (1,)), ((), ()))

    ii = lax.broadcasted_iota(jnp.int32, (c_len, c_len), 0).astype(jnp.float32)
    jj = lax.broadcasted_iota(jnp.int32, (c_len, c_len), 1).astype(jnp.float32)
    dist = ii - jj
    decay = (jnp.where(dist >= 0, jnp.exp(lg_f * jnp.maximum(dist, 0.0)), 0.0)
             + jnp.where(dist <= 0, jnp.exp(lg_b * jnp.maximum(-dist, 0.0)), 0.0))
    ic = lax.broadcasted_iota(jnp.int32, (c_len, 1), 0).astype(jnp.float32)
    kdec_f = jnp.exp(lg_f * (c_len - 1 - ic))
    kdec_b = jnp.exp(lg_b * ic)
    qdec_f = jnp.exp(lg_f * (ic + 1.0))
    qdec_b = jnp.exp(lg_b * (c_len - ic))
    one = jnp.ones((1, 1), jnp.float32)
    cdec_f = jnp.exp(one * (lg_f * c_len))
    cdec_b = jnp.exp(one * (lg_b * c_len))

    jcr = lax.broadcasted_iota(jnp.int32, (1, lc), 1).astype(jnp.float32)
    kct = kc_ref[0].astype(jnp.float32).T
    vcx = vc_ref[0]
    s_f = jnp.dot((kct * jnp.exp(lg_f * (lc - 1 - jcr))).astype(jnp.bfloat16), vcx,
                  preferred_element_type=jnp.float32)
    s_b = jnp.dot((kct * jnp.exp(lg_b * jcr)).astype(jnp.bfloat16), vcx,
                  preferred_element_type=jnp.float32)

    for c in range(n_chunks):
        rows_c = slice(c * c_len, (c + 1) * c_len)
        kt = k_ref[0, rows_c, :].astype(jnp.float32).T.astype(jnp.bfloat16)
        vf = v_ref[0, rows_c, :].astype(jnp.float32)
        vw = jnp.concatenate([(vf * kdec_f).astype(jnp.bfloat16), (vf * kdec_b).astype(jnp.bfloat16)], axis=1)
        t_ref[c] = jnp.dot(kt, vw, preferred_element_type=jnp.float32)

    for c in range(n_chunks):
        s_ref[c, :, :RET_DIM] = s_f.astype(jnp.bfloat16)
        s_f = cdec_f * s_f + t_ref[c, :, :RET_DIM]
    for c in reversed(range(n_chunks)):
        s_ref[c, :, RET_DIM:] = s_b.astype(jnp.bfloat16)
        s_b = cdec_b * s_b + t_ref[c, :, RET_DIM:]

    ng = ng_ref[...]

    def scores(c):
        rows_c = slice(c * c_len, (c + 1) * c_len)
        qc = q_ref[0, rows_c, :]
        s = lax.dot_general(qc, k_ref[0, rows_c, :], contract_last, preferred_element_type=jnp.float32) * decay
        cross = jnp.dot(qc, s_ref[c], preferred_element_type=jnp.float32)
        return s.astype(jnp.bfloat16), cross[:, :RET_DIM] * qdec_f + cross[:, RET_DIM:] * qdec_b

    def finish(c, sc):
        s, cross = sc
        rows_c = slice(c * c_len, (c + 1) * c_len)
        o = jnp.dot(s, v_ref[0, rows_c, :], preferred_element_type=jnp.float32) + cross
        o = o * lax.rsqrt(jnp.mean(o * o, axis=-1, keepdims=True) + EPS) * ng
        gate = g_ref[0, rows_c, :].astype(jnp.float32)
        o_ref[0, rows_c, :] = (o * gate).astype(o_ref.dtype)

    return n_chunks, scores, finish


def _mixer_kernel(rpb_ref, lg_ref,
                  aq_ref, ak_ref, avt_ref, ag_ref, akc_ref, avct_ref,
                  rq_ref, rk_ref, rv_ref, rg_ref, rkc_ref, rvc_ref, ng_ref,
                  yna_ref, yret_ref,
                  bias_ref, base_ref, vth_ref, vcth_ref, t_ref, s_ref, *, rows):
    i = pl.program_id(0)

    @pl.when(pl.program_id(1) == 0)
    def _():
        _na_build_bias(rpb_ref, i, bias_ref, base_ref, rows)

    slab = lambda ref: ref.at[0]
    n_ret, ret_scores, ret_finish = _ret_program(lg_ref[0, i], lg_ref[1, i], slab(rq_ref), slab(rk_ref),
                                                 slab(rv_ref), slab(rg_ref), slab(rkc_ref), slab(rvc_ref),
                                                 ng_ref, slab(yret_ref), t_ref, s_ref)
    n_na, na_scores, na_attend = _na_program(slab(aq_ref), slab(ak_ref), avt_ref, slab(ag_ref), slab(akc_ref),
                                             avct_ref, slab(yna_ref), bias_ref, vth_ref, vcth_ref, rows)

    pend_na, pend_ret = na_scores(0), ret_scores(0)
    for t in range(max(n_na, n_ret)):
        cur_na, cur_ret = pend_na, pend_ret
        if t + 1 < n_na:
            pend_na = na_scores(t + 1)
        if t + 1 < n_ret:
            pend_ret = ret_scores(t + 1)
        if t < n_na:
            na_attend(t, cur_na)
        if t < n_ret:
            ret_finish(t, cur_ret)


def _mixers(rpb, lg, proj, vt, proj_ctx, vct, ret_norm_g, rows, cols, ctx_cols):
    b, _, l, _ = proj.shape
    lc = proj_ctx.shape[2]
    assert NA_HEADS // 2 == RET_HEADS
    blk = lambda name: pl.BlockSpec((1, 1, l, LANES), lambda i, bi, *_, off=cols[name]: (bi, off + i, 0, 0))
    cblk = lambda name: pl.BlockSpec((1, 1, lc, LANES), lambda i, bi, *_, off=ctx_cols[name]: (bi, off + i, 0, 0))
    n_chunks = l // RET_CHUNK
    grid_spec = pltpu.PrefetchScalarGridSpec(
        num_scalar_prefetch=2,
        grid=(RET_HEADS, b),
        in_specs=[blk("na_q"), blk("na_k"),
                  pl.BlockSpec((1, LANES, l), lambda i, bi, *_: (bi, i, 0)),
                  blk("na_g"), cblk("na_k"),
                  pl.BlockSpec((1, LANES, lc), lambda i, bi, *_: (bi, i, 0)),
                  blk("r_q"), blk("r_k"), blk("r_v"), blk("r_g"), cblk("r_k"), cblk("r_v"),
                  pl.BlockSpec((1, LANES), lambda i, bi, *_: (0, i))],
        out_specs=(pl.BlockSpec((1, 1, l, LANES), lambda i, bi, *_: (bi, i, 0, 0)),
                   pl.BlockSpec((1, 1, l, LANES), lambda i, bi, *_: (bi, i, 0, 0))),
        scratch_shapes=[pltpu.VMEM((3, 2, NA_K_ROWS * GRID_W, NA_Q_ROWS * GRID_W), jnp.float32),
                        pltpu.VMEM((2 * NA_WIN_ROWS - 1, GRID_W, LANES), jnp.float32),
                        pltpu.VMEM((2, NA_HEAD_DIM + BF16_SUBLANES, l), jnp.bfloat16),
                        pltpu.VMEM((2, NA_HEAD_DIM + BF16_SUBLANES, lc), jnp.bfloat16),
                        pltpu.VMEM((n_chunks, RET_DIM, 2 * RET_DIM), jnp.float32),
                        pltpu.VMEM((n_chunks, RET_DIM, 2 * RET_DIM), jnp.bfloat16)],
    )
    return pl.pallas_call(
        functools.partial(_mixer_kernel, rows=rows),
        out_shape=(jax.ShapeDtypeStruct((b, NA_WIDTH // LANES, l, LANES), jnp.bfloat16),
                   jax.ShapeDtypeStruct((b, RET_WIDTH // LANES, l, LANES), jnp.bfloat16)),
        grid_spec=grid_spec,
        compiler_params=pltpu.CompilerParams(
            dimension_semantics=("arbitrary", "arbitrary"), vmem_limit_bytes=VMEM_LIMIT),
        name="mixers",
    )(rpb.astype(jnp.float32).reshape(-1), lg,
      proj, proj, vt, proj, proj_ctx, vct,
      proj, proj, proj, proj, proj_ctx, proj_ctx, ret_norm_g.reshape(1, RET_WIDTH))


def _out_kernel(x_ref, gate_ref, yna_ref, yret_ref, w_ref, fg_ref, o_ref, *, sub):
    tm = x_ref.shape[1]

    def mix(j):
        rows = slice(j * sub, (j + 1) * sub)
        y = jnp.concatenate([ref[0, k, rows, :] for ref in (yna_ref, yret_ref) for k in range(ref.shape[1])],
                            axis=1)
        return jnp.dot(y, w_ref[...], preferred_element_type=jnp.float32)

    def finish(j, y):
        rows = slice(j * sub, (j + 1) * sub)
        z = x_ref[0, rows, :] + gate_ref[0] * y
        ms = jnp.mean(z * z, axis=-1, keepdims=True)
        o_ref[0, rows, :] = z * lax.rsqrt(ms + EPS) * fg_ref[...]

    pending = mix(0)
    for j in range(tm // sub):
        current = pending
        if (j + 1) * sub < tm:
            pending = mix(j + 1)
        finish(j, current)


def _out_proj(x, gate, y_na, y_ret, w_out, final_g, tm, sub):
    b, l, d = x.shape
    assert l % tm == 0 and tm % sub == 0
    return pl.pallas_call(
        functools.partial(_out_kernel, sub=sub),
        out_shape=jax.ShapeDtypeStruct((b, l, d), jnp.float32),
        grid=(b, l // tm),
        in_specs=[pl.BlockSpec((1, tm, d), lambda bi, i: (bi, i, 0)),
                  pl.BlockSpec((1, 1, d), lambda bi, i: (bi, 0, 0)),
                  pl.BlockSpec((1, NA_WIDTH // LANES, tm, LANES), lambda bi, i: (bi, 0, i, 0)),
                  pl.BlockSpec((1, RET_WIDTH // LANES, tm, LANES), lambda bi, i: (bi, 0, i, 0)),
                  pl.BlockSpec((NA_WIDTH + RET_WIDTH, d), lambda bi, i: (0, 0)),
                  pl.BlockSpec((1, d), lambda bi, i: (0, 0))],
        out_specs=pl.BlockSpec((1, tm, d), lambda bi, i: (bi, i, 0)),
        compiler_params=pltpu.CompilerParams(
            dimension_semantics=("arbitrary", "arbitrary"), vmem_limit_bytes=VMEM_LIMIT),
        name="out_proj",
    )(x, gate, y_na, y_ret, w_out, final_g.reshape(1, d))


def _rotary_tables(l):
    half = RET_DIM // 2
    nf = half // 2
    t = np.arange(l)
    row = (t // GRID_W).astype(np.float64)
    col = (t % GRID_W).astype(np.float64)
    inv = ROPE_BASE ** (-np.arange(nf, dtype=np.float64) / nf)
    ang = np.concatenate([row[:, None] * inv, col[:, None] * inv], axis=-1)
    cos, sin = np.cos(ang), np.sin(ang)
    cosf = np.concatenate([cos, cos], axis=-1).astype(np.float32)
    sinf = np.concatenate([-sin, sin], axis=-1).astype(np.float32)
    return jnp.asarray(cosf), jnp.asarray(sinf)


def kernel(x, c, ctx, c_ctx, norm_g, w_ada, b_ada, w_in, na_rpb, ret_decay_fwd, ret_decay_bwd,
           ret_norm_g, w_out, final_norm_g):
    depth = norm_g.shape[0]
    assert depth == 1, "context stream update between layers is not implemented"
    b, l, d = x.shape
    rows = l // GRID_W
    i = 0

    cc = jnp.concatenate([c, c_ctx[None, :]], axis=0)
    pad = (-cc.shape[0]) % 8
    cc = jnp.pad(cc, ((0, pad), (0, 0)))
    mod = _adaln_mod(cc, w_ada[i], b_ada[i])
    shift, scale, gate = (mod[:b, None, :d], mod[:b, None, d:2 * d], mod[:b, None, 2 * d:])
    shift_c, scale_c = mod[b:b + 1, None, :d], mod[b:b + 1, None, d:2 * d]

    w, w_vt = _cast_w_in(w_in[i], v_group=2)
    cosf, sinf = _rotary_tables(l)
    blocks = GROUP_COLS // LANES

    lat_groups = ((0, "na_q"), (1, "plain"), (3, "silu"), (4, "rot"), (5, "rot_kscale"), (6, "plain"), (7, "silu"))
    lat_names = ("na_q", "na_k", "na_g", "r_q", "r_k", "r_v", "r_g")
    proj, vt = _in_proj(x, scale, shift, norm_g[i], w, w_vt, cosf, sinf, lat_groups, tm=1024, sub=256)

    lc = ctx.shape[1]
    ctx_groups = ((1, "plain"), (5, "kscale"), (6, "plain"))
    ctx_names = ("na_k", "r_k", "r_v")
    proj_ctx, vct = _in_proj(ctx, scale_c, shift_c, norm_g[i], w, w_vt, cosf[:lc], sinf[:lc], ctx_groups,
                             tm=lc, sub=lc)

    lg = jnp.stack([-jnp.exp(ret_decay_fwd[i].astype(jnp.float32)),
                    -jnp.exp(ret_decay_bwd[i].astype(jnp.float32))])
    y_na, y_ret = _mixers(na_rpb[i], lg, proj, vt, proj_ctx, vct, ret_norm_g[i], rows,
                          cols={name: k * blocks for k, name in enumerate(lat_names)},
                          ctx_cols={name: k * blocks for k, name in enumerate(ctx_names)})

    return _out_proj(x, gate, y_na, y_ret, w_out[i].astype(jnp.bfloat16), final_norm_g, tm=1024, sub=256)
```

```python
import functools
import math

import jax
import jax.numpy as jnp
import numpy as np
from jax import lax
from jax.experimental import pallas as pl
from jax.experimental.pallas import tpu as pltpu

D_MODEL = 1024
GRID_W = 64
NA_HEAD_DIM = 64
NA_WIDTH = 512
NA_HEADS = 8
NA_WIN_ROWS = 8
NA_WIN_COLS = 16
RET_HEADS = 4
RET_WIDTH = 512
RET_DIM = 128
RET_CHUNK = 128
ROPE_BASE = 10000.0
EPS = 1e-6

LANES = 128
MXU_TILE = 256
BF16_SUBLANES = 16
GROUP_COLS = 512
NA_Q_ROWS = 4
NA_K_ROWS = 12
NA_AHEAD = 1
RET_AHEAD = 1
MIX_BATCH = 2
MASK_VALUE = -1e30
LOG2E = math.log2(math.e)

VMEM_LIMIT = 56 * 1024 * 1024


def _silu(v):
    return v * (1.0 / (1.0 + jnp.exp(-v)))


def _mod_kernel(c_ref, w_ref, b_ref, o_ref):
    a = _silu(c_ref[...])
    o_ref[...] = jnp.dot(a, w_ref[...], preferred_element_type=jnp.float32,
                         precision=lax.Precision.HIGHEST) + b_ref[...]


def _adaln_mod(cc, w_ada, b_ada):
    m, d = cc.shape
    n = w_ada.shape[1]
    tn = 512
    return pl.pallas_call(
        _mod_kernel,
        out_shape=jax.ShapeDtypeStruct((m, n), jnp.float32),
        grid=(n // tn,),
        in_specs=[pl.BlockSpec((m, d), lambda j: (0, 0)),
                  pl.BlockSpec((d, tn), lambda j: (0, j)),
                  pl.BlockSpec((1, tn), lambda j: (0, j))],
        out_specs=pl.BlockSpec((m, tn), lambda j: (0, j)),
        compiler_params=pltpu.CompilerParams(
            dimension_semantics=("arbitrary",), vmem_limit_bytes=VMEM_LIMIT),
        name="adaln_mod",
    )(cc, w_ada, b_ada.reshape(1, n))


def _cast_kernel(w_ref, o_ref, ot_ref, *, v_group):
    w = w_ref[...]
    o_ref[...] = w.astype(o_ref.dtype)
    ot_ref[...] = w[:, v_group * GROUP_COLS:(v_group + 1) * GROUP_COLS].T.astype(ot_ref.dtype)


def _cast_w_in(w, v_group):
    d, n = w.shape
    rb = 256
    return pl.pallas_call(
        functools.partial(_cast_kernel, v_group=v_group),
        out_shape=(jax.ShapeDtypeStruct((d, n), jnp.bfloat16),
                   jax.ShapeDtypeStruct((GROUP_COLS, d), jnp.bfloat16)),
        grid=(d // rb,),
        in_specs=[pl.BlockSpec((rb, n), lambda r: (r, 0))],
        out_specs=(pl.BlockSpec((rb, n), lambda r: (r, 0)),
                   pl.BlockSpec((GROUP_COLS, rb), lambda r: (0, r))),
        compiler_params=pltpu.CompilerParams(
            dimension_semantics=("arbitrary",), vmem_limit_bytes=VMEM_LIMIT),
        name="cast_w_in",
    )(w)


def _rotary(acc, cosf, sinf):
    outs = []
    for h in range(GROUP_COLS // LANES):
        xh = acc[:, h * LANES:(h + 1) * LANES]
        outs.append(xh * cosf + pltpu.roll(xh, LANES // 2, 1) * sinf)
    return jnp.concatenate(outs, axis=-1)


def _proj_kernel(x_ref, scale_ref, shift_ref, g_ref, w_ref, wvt_ref, cos_ref, sin_ref, o_ref, vt_ref,
                 *, groups, sub):
    tm = x_ref.shape[1]

    def normed(j):
        x = x_ref[0, j * sub:(j + 1) * sub, :]
        ms = jnp.mean(x * x, axis=-1, keepdims=True)
        h = x * lax.rsqrt(ms + EPS) * g_ref[...]
        h = h * (1.0 + scale_ref[0]) + shift_ref[0]
        return h.astype(jnp.bfloat16)

    def project(j, hb):
        rows = slice(j * sub, (j + 1) * sub)
        vt = lax.dot_general(wvt_ref[...], hb, (((1,), (1,)), ((), ())), preferred_element_type=jnp.float32)
        vt_ref[0, :, rows] = vt.astype(vt_ref.dtype)
        for gi, (src, kind) in enumerate(groups):
            acc = jnp.dot(hb, w_ref[:, src * GROUP_COLS:(src + 1) * GROUP_COLS],
                          preferred_element_type=jnp.float32)
            if kind == "na_q":
                acc = acc * (NA_HEAD_DIM ** -0.5 * LOG2E)
            elif kind == "silu":
                acc = _silu(acc)
            elif kind == "rot":
                acc = _rotary(acc, cos_ref[rows, :], sin_ref[rows, :])
            elif kind == "rot_kscale":
                acc = _rotary(acc, cos_ref[rows, :], sin_ref[rows, :]) * (RET_DIM ** -0.5)
            elif kind == "kscale":
                acc = acc * (RET_DIM ** -0.5)
            else:
                assert kind == "plain"
            per_group = GROUP_COLS // LANES
            for k in range(per_group):
                o_ref[0, gi * per_group + k, rows, :] = acc[:, k * LANES:(k + 1) * LANES].astype(o_ref.dtype)

    pending = normed(0)
    for j in range(tm // sub):
        current = pending
        if (j + 1) * sub < tm:
            pending = normed(j + 1)
        project(j, current)


def _in_proj(x, scale, shift, norm_g, w, w_vt, cosf, sinf, groups, tm, sub):
    b, l, d = x.shape
    n_in = w.shape[1]
    n = GROUP_COLS * len(groups)
    nv = w_vt.shape[0]
    assert l % tm == 0 and tm % sub == 0
    per_batch = scale.shape[0] > 1
    mod_map = (lambda bi, i: (bi, 0, 0)) if per_batch else (lambda bi, i: (0, 0, 0))
    return pl.pallas_call(
        functools.partial(_proj_kernel, groups=tuple(groups), sub=sub),
        out_shape=(jax.ShapeDtypeStruct((b, n // LANES, l, LANES), jnp.bfloat16),
                   jax.ShapeDtypeStruct((b, nv, l), jnp.bfloat16)),
        grid=(b, l // tm),
        in_specs=[pl.BlockSpec((1, tm, d), lambda bi, i: (bi, i, 0)),
                  pl.BlockSpec((1, 1, d), mod_map),
                  pl.BlockSpec((1, 1, d), mod_map),
                  pl.BlockSpec((1, d), lambda bi, i: (0, 0)),
                  pl.BlockSpec((d, n_in), lambda bi, i: (0, 0)),
                  pl.BlockSpec((nv, d), lambda bi, i: (0, 0)),
                  pl.BlockSpec((tm, LANES), lambda bi, i: (i, 0)),
                  pl.BlockSpec((tm, LANES), lambda bi, i: (i, 0))],
        out_specs=(pl.BlockSpec((1, n // LANES, tm, LANES), lambda bi, i: (bi, 0, i, 0)),
                   pl.BlockSpec((1, nv, tm), lambda bi, i: (bi, 0, i))),
        compiler_params=pltpu.CompilerParams(
            dimension_semantics=("arbitrary", "arbitrary"), vmem_limit_bytes=VMEM_LIMIT),
        name="in_proj",
    )(x, scale, shift, norm_g.reshape(1, d), w, w_vt, cosf, sinf)


def _na_window_start(rq, rows):
    kr = min(NA_WIN_ROWS, rows)
    return int(np.clip(rq - kr // 2, 0, rows - kr))


def _na_geometry(rows):
    kr = min(NA_WIN_ROWS, rows)
    n_groups = rows // NA_Q_ROWS
    assert n_groups >= 3
    geo = []
    for g in range(n_groups):
        lo = _na_window_start(NA_Q_ROWS * g, rows)
        hi = _na_window_start(NA_Q_ROWS * g + NA_Q_ROWS - 1, rows) + kr
        geo.append((lo, hi - lo, 0 if g == 0 else (2 if g == n_groups - 1 else 1)))
    assert len({n for _, n, v in geo if v == 1}) == 1 and max(n for _, n, _ in geo) <= NA_K_ROWS
    return geo


def _na_row_offsets(rows):
    kr = min(NA_WIN_ROWS, rows)
    geo = _na_geometry(rows)
    table = []
    for rep in (0, 1, len(geo) - 1):
        start, n_rows, _ = geo[rep]
        per_i = []
        for i in range(NA_Q_ROWS):
            rq = NA_Q_ROWS * rep + i
            r0 = _na_window_start(rq, rows)
            per_i.append([start + j - rq + NA_WIN_ROWS - 1 if r0 <= start + j < r0 + kr else None
                          for j in range(n_rows)])
        table.append(per_i)
    return table


def _na_build_bias(rpb_ref, hp, bias_ref, base_ref, rows):
    n_dr = 2 * NA_WIN_ROWS - 1
    n_dc = 2 * NA_WIN_COLS - 1
    ck = lax.broadcasted_iota(jnp.int32, (GRID_W, LANES), 0)
    lane = lax.broadcasted_iota(jnp.int32, (GRID_W, LANES), 1)
    left = lane < GRID_W
    cq = jnp.where(left, lane, lane - GRID_W)
    c0 = jnp.clip(cq - NA_WIN_COLS // 2, 0, GRID_W - NA_WIN_COLS)
    valid_c = (ck >= c0) & (ck < c0 + NA_WIN_COLS)
    dc = jnp.clip(ck - cq + NA_WIN_COLS - 1, 0, n_dc - 1)

    offsets = _na_row_offsets(rows)
    users = {}
    for var in range(3):
        for j in range(len(offsets[var][0])):
            for ip in range(NA_Q_ROWS // 2):
                key = (offsets[var][2 * ip][j], offsets[var][2 * ip + 1][j])
                users.setdefault(key, []).append((var, j, ip))

    used = sorted({dr for pair in users for dr in pair if dr is not None})
    masked = jnp.full((GRID_W, LANES), MASK_VALUE, jnp.float32)
    for h in range(2):
        head = 2 * hp + h
        for dr in used:
            def body(d, acc, dr=dr, head=head):
                return jnp.where(dc == d, rpb_ref[(head * n_dr + dr) * n_dc + d], acc)
            acc = lax.fori_loop(0, n_dc, body, jnp.zeros((GRID_W, LANES), jnp.float32), unroll=True)
            base_ref[dr] = jnp.where(valid_c, acc * LOG2E, MASK_VALUE)
        for (dr_l, dr_r), dests in users.items():
            t_l = masked if dr_l is None else base_ref[dr_l]
            t_r = masked if dr_r is None else base_ref[dr_r]
            tile = t_l if dr_l == dr_r else jnp.where(left, t_l, t_r)
            for var, j, ip in dests:
                bias_ref[var, h, j * GRID_W:(j + 1) * GRID_W, ip * LANES:(ip + 1) * LANES] = tile


def _na_program(q_ref, k_ref, vt_ref, g_ref, kc_ref, vct_ref, o_ref, bias_ref, vth_ref, vcth_ref, rows):
    geo = _na_geometry(rows)
    tq = NA_Q_ROWS * GRID_W
    dh = NA_HEAD_DIM

    lane = lax.broadcasted_iota(jnp.int32, (1, LANES), 1)
    head0_lanes = lane < dh
    n_ones = vth_ref.shape[1] - dh
    for h in range(2):
        vth_ref[h, :dh, :] = vt_ref[0, h * dh:(h + 1) * dh, :]
        vth_ref[h, dh:, :] = jnp.ones((n_ones, vt_ref.shape[2]), vt_ref.dtype)
        vcth_ref[h, :dh, :] = vct_ref[0, h * dh:(h + 1) * dh, :]
        vcth_ref[h, dh:, :] = jnp.ones((n_ones, vct_ref.shape[2]), vct_ref.dtype)

    kc = kc_ref[0]
    contract_last = (((1,), (1,)), ((), ()))
    tasks = [(g, h) for g in range(len(geo)) for h in range(2)]

    def scores(t):
        g, h = tasks[t]
        ws, n_rows, var = geo[g]
        nk = n_rows * GRID_W
        qg = q_ref[0, g * tq:(g + 1) * tq, :]
        kw = k_ref[0, ws * GRID_W:ws * GRID_W + nk, :]
        sel = head0_lanes if h == 0 else jnp.logical_not(head0_lanes)
        qh = jnp.where(sel, qg, jnp.zeros_like(qg))
        s_loc = lax.dot_general(kw, qh, contract_last,
                                preferred_element_type=jnp.float32) + bias_ref[var, h, :nk, :]
        s_ctx = lax.dot_general(kc, qh, contract_last, preferred_element_type=jnp.float32)
        return s_loc, s_ctx

    halves = []

    def attend(t, s):
        g, h = tasks[t]
        s_loc, s_ctx = s
        ws, n_rows, _ = geo[g]
        m = jnp.maximum(jnp.max(s_loc, axis=0, keepdims=True), jnp.max(s_ctx, axis=0, keepdims=True))
        p_loc = jnp.exp2(s_loc - m).astype(jnp.bfloat16)
        p_ctx = jnp.exp2(s_ctx - m).astype(jnp.bfloat16)
        pad_rows = (-n_rows) % (MXU_TILE // GRID_W)
        if pad_rows:
            p_loc = jnp.concatenate([p_loc, jnp.zeros((pad_rows * GRID_W, tq), p_loc.dtype)], axis=0)
        nk = (n_rows + pad_rows) * GRID_W
        ot = jnp.dot(vth_ref[h, :, ws * GRID_W:ws * GRID_W + nk], p_loc, preferred_element_type=jnp.float32)
        ot = ot + jnp.dot(vcth_ref[h], p_ctx, preferred_element_type=jnp.float32)
        halves.append(ot[:dh] / ot[dh:dh + 1])
        if h == 1:
            o2 = jnp.concatenate(halves, axis=0).T
            halves.clear()
            gate = g_ref[0, g * tq:(g + 1) * tq, :].astype(jnp.float32)
            o_ref[0, g * tq:(g + 1) * tq, :] = (o2 * gate).astype(o_ref.dtype)

    return len(tasks), scores, attend


def _ret_program(lg_f, lg_b, q_ref, k_ref, v_ref, g_ref, kc_ref, vc_ref, ng_ref, o_ref, t_ref, s_ref):
    c_len = RET_CHUNK
    l = q_ref.shape[1]
    lc = kc_ref.shape[1]
    n_chunks = l // c_len
    contract_last = (((1,), (1,)), ((), ()))

    ii = lax.broadcasted_iota(jnp.int32, (c_len, c_len), 0).astype(jnp.float32)
    jj = lax.broadcasted_iota(jnp.int32, (c_len, c_len), 1).astype(jnp.float32)
    dist = ii - jj
    decay = (jnp.where(dist >= 0, jnp.exp(lg_f * jnp.maximum(dist, 0.0)), 0.0)
             + jnp.where(dist <= 0, jnp.exp(lg_b * jnp.maximum(-dist, 0.0)), 0.0))
    ic = lax.broadcasted_iota(jnp.int32, (c_len, 1), 0).astype(jnp.float32)
    kdec_f = jnp.exp(lg_f * (c_len - 1 - ic))
    kdec_b = jnp.exp(lg_b * ic)
    qdec_f = jnp.exp(lg_f * (ic + 1.0))
    qdec_b = jnp.exp(lg_b * (c_len - ic))
    one = jnp.ones((1, 1), jnp.float32)
    cdec_f = jnp.exp(one * (lg_f * c_len))
    cdec_b = jnp.exp(one * (lg_b * c_len))

    jcr = lax.broadcasted_iota(jnp.int32, (1, lc), 1).astype(jnp.float32)
    kct = kc_ref[0].astype(jnp.float32).T
    vcx = vc_ref[0]
    s_f = jnp.dot((kct * jnp.exp(lg_f * (lc - 1 - jcr))).astype(jnp.bfloat16), vcx,
                  preferred_element_type=jnp.float32)
    s_b = jnp.dot((kct * jnp.exp(lg_b * jcr)).astype(jnp.bfloat16), vcx,
                  preferred_element_type=jnp.float32)

    for c in range(n_chunks):
        rows_c = slice(c * c_len, (c + 1) * c_len)
        kt = k_ref[0, rows_c, :].astype(jnp.float32).T.astype(jnp.bfloat16)
        vf = v_ref[0, rows_c, :].astype(jnp.float32)
        vw = jnp.concatenate([(vf * kdec_f).astype(jnp.bfloat16), (vf * kdec_b).astype(jnp.bfloat16)], axis=1)
        t_ref[c] = jnp.dot(kt, vw, preferred_element_type=jnp.float32)

    for c in range(n_chunks):
        s_ref[c, :, :RET_DIM] = s_f.astype(jnp.bfloat16)
        s_f = cdec_f * s_f + t_ref[c, :, :RET_DIM]
    for c in reversed(range(n_chunks)):
        s_ref[c, :, RET_DIM:] = s_b.astype(jnp.bfloat16)
        s_b = cdec_b * s_b + t_ref[c, :, RET_DIM:]

    ng = ng_ref[...]

    def scores(c):
        rows_c = slice(c * c_len, (c + 1) * c_len)
        qc = q_ref[0, rows_c, :]
        s = lax.dot_general(qc, k_ref[0, rows_c, :], contract_last, preferred_element_type=jnp.float32) * decay
        cross = jnp.dot(qc, s_ref[c], preferred_element_type=jnp.float32)
        return s.astype(jnp.bfloat16), cross[:, :RET_DIM] * qdec_f + cross[:, RET_DIM:] * qdec_b

    def finish(c, sc):
        s, cross = sc
        rows_c = slice(c * c_len, (c + 1) * c_len)
        o = jnp.dot(s, v_ref[0, rows_c, :], preferred_element_type=jnp.float32) + cross
        o = o * lax.rsqrt(jnp.mean(o * o, axis=-1, keepdims=True) + EPS) * ng
        gate = g_ref[0, rows_c, :].astype(jnp.float32)
        o_ref[0, rows_c, :] = (o * gate).astype(o_ref.dtype)

    return n_chunks, scores, finish


def _mixer_kernel(rpb_ref, lg_ref,
                  aq_ref, ak_ref, avt_ref, ag_ref, akc_ref, avct_ref,
                  rq_ref, rk_ref, rv_ref, rg_ref, rkc_ref, rvc_ref, ng_ref,
                  yna_ref, yret_ref,
                  bias_ref, base_ref, vth_ref, vcth_ref, t_ref, s_ref, *, rows):
    i = pl.program_id(0)

    @pl.when(pl.program_id(1) == 0)
    def _():
        _na_build_bias(rpb_ref, i, bias_ref, base_ref, rows)

    na_tasks, ret_tasks = [], []
    for j in range(aq_ref.shape[0]):
        slab = lambda ref, j=j: ref.at[j]
        one = lambda ref, j=j: ref.at[pl.ds(j, 1)]
        n_ret, ret_scores, ret_finish = _ret_program(
            lg_ref[0, i], lg_ref[1, i], slab(rq_ref), slab(rk_ref), slab(rv_ref), slab(rg_ref),
            slab(rkc_ref), slab(rvc_ref), ng_ref, slab(yret_ref), t_ref.at[j], s_ref.at[j])
        n_na, na_scores, na_attend = _na_program(
            slab(aq_ref), slab(ak_ref), one(avt_ref), slab(ag_ref), slab(akc_ref), one(avct_ref),
            slab(yna_ref), bias_ref, vth_ref.at[j], vcth_ref.at[j], rows)
        na_tasks += [(na_scores, na_attend, t) for t in range(n_na)]
        ret_tasks += [(ret_scores, ret_finish, c) for c in range(n_ret)]

    issue = lambda task: task[0](task[2])
    na_q = [issue(task) for task in na_tasks[:NA_AHEAD]]
    ret_q = [issue(task) for task in ret_tasks[:RET_AHEAD]]
    for t in range(max(len(na_tasks), len(ret_tasks))):
        if t + NA_AHEAD < len(na_tasks):
            na_q.append(issue(na_tasks[t + NA_AHEAD]))
        if t + RET_AHEAD < len(ret_tasks):
            ret_q.append(issue(ret_tasks[t + RET_AHEAD]))
        if t < len(na_tasks):
            na_tasks[t][1](na_tasks[t][2], na_q.pop(0))
        if t < len(ret_tasks):
            ret_tasks[t][1](ret_tasks[t][2], ret_q.pop(0))


def _mixers(rpb, lg, proj, vt, proj_ctx, vct, ret_norm_g, rows, cols, ctx_cols):
    b, _, l, _ = proj.shape
    lc = proj_ctx.shape[2]
    nb = MIX_BATCH
    assert NA_HEADS // 2 == RET_HEADS and b % nb == 0
    blk = lambda name: pl.BlockSpec((nb, 1, l, LANES), lambda i, bi, *_, off=cols[name]: (bi, off + i, 0, 0))
    cblk = lambda name: pl.BlockSpec((nb, 1, lc, LANES), lambda i, bi, *_, off=ctx_cols[name]: (bi, off + i, 0, 0))
    n_chunks = l // RET_CHUNK
    grid_spec = pltpu.PrefetchScalarGridSpec(
        num_scalar_prefetch=2,
        grid=(RET_HEADS, b // nb),
        in_specs=[blk("na_q"), blk("na_k"),
                  pl.BlockSpec((nb, LANES, l), lambda i, bi, *_: (bi, i, 0)),
                  blk("na_g"), cblk("na_k"),
                  pl.BlockSpec((nb, LANES, lc), lambda i, bi, *_: (bi, i, 0)),
                  blk("r_q"), blk("r_k"), blk("r_v"), blk("r_g"), cblk("r_k"), cblk("r_v"),
                  pl.BlockSpec((1, LANES), lambda i, bi, *_: (0, i))],
        out_specs=(pl.BlockSpec((nb, 1, l, LANES), lambda i, bi, *_: (bi, i, 0, 0)),
                   pl.BlockSpec((nb, 1, l, LANES), lambda i, bi, *_: (bi, i, 0, 0))),
        scratch_shapes=[pltpu.VMEM((3, 2, NA_K_ROWS * GRID_W, NA_Q_ROWS * GRID_W), jnp.float32),
                        pltpu.VMEM((2 * NA_WIN_ROWS - 1, GRID_W, LANES), jnp.float32),
                        pltpu.VMEM((nb, 2, NA_HEAD_DIM + BF16_SUBLANES, l), jnp.bfloat16),
                        pltpu.VMEM((nb, 2, NA_HEAD_DIM + BF16_SUBLANES, lc), jnp.bfloat16),
                        pltpu.VMEM((nb, n_chunks, RET_DIM, 2 * RET_DIM), jnp.float32),
                        pltpu.VMEM((nb, n_chunks, RET_DIM, 2 * RET_DIM), jnp.bfloat16)],
    )
    return pl.pallas_call(
        functools.partial(_mixer_kernel, rows=rows),
        out_shape=(jax.ShapeDtypeStruct((b, NA_WIDTH // LANES, l, LANES), jnp.bfloat16),
                   jax.ShapeDtypeStruct((b, RET_WIDTH // LANES, l, LANES), jnp.bfloat16)),
        grid_spec=grid_spec,
        compiler_params=pltpu.CompilerParams(
            dimension_semantics=("arbitrary", "arbitrary"), vmem_limit_bytes=VMEM_LIMIT),
        name="mixers",
    )(rpb.astype(jnp.float32).reshape(-1), lg,
      proj, proj, vt, proj, proj_ctx, vct,
      proj, proj, proj, proj, proj_ctx, proj_ctx, ret_norm_g.reshape(1, RET_WIDTH))


def _out_kernel(x_ref, gate_ref, yna_ref, yret_ref, w_ref, fg_ref, o_ref, *, sub):
    tm = x_ref.shape[1]

    def mix(j):
        rows = slice(j * sub, (j + 1) * sub)
        y = jnp.concatenate([ref[0, k, rows, :] for ref in (yna_ref, yret_ref) for k in range(ref.shape[1])],
                            axis=1)
        return jnp.dot(y, w_ref[...], preferred_element_type=jnp.float32)

    def finish(j, y):
        rows = slice(j * sub, (j + 1) * sub)
        z = x_ref[0, rows, :] + gate_ref[0] * y
        ms = jnp.mean(z * z, axis=-1, keepdims=True)
        o_ref[0, rows, :] = z * lax.rsqrt(ms + EPS) * fg_ref[...]

    pending = mix(0)
    for j in range(tm // sub):
        current = pending
        if (j + 1) * sub < tm:
            pending = mix(j + 1)
        finish(j, current)


def _out_proj(x, gate, y_na, y_ret, w_out, final_g, tm, sub):
    b, l, d = x.shape
    assert l % tm == 0 and tm % sub == 0
    return pl.pallas_call(
        functools.partial(_out_kernel, sub=sub),
        out_shape=jax.ShapeDtypeStruct((b, l, d), jnp.float32),
        grid=(b, l // tm),
        in_specs=[pl.BlockSpec((1, tm, d), lambda bi, i: (bi, i, 0)),
                  pl.BlockSpec((1, 1, d), lambda bi, i: (bi, 0, 0)),
                  pl.BlockSpec((1, NA_WIDTH // LANES, tm, LANES), lambda bi, i: (bi, 0, i, 0)),
                  pl.BlockSpec((1, RET_WIDTH // LANES, tm, LANES), lambda bi, i: (bi, 0, i, 0)),
                  pl.BlockSpec((NA_WIDTH + RET_WIDTH, d), lambda bi, i: (0, 0)),
                  pl.BlockSpec((1, d), lambda bi, i: (0, 0))],
        out_specs=pl.BlockSpec((1, tm, d), lambda bi, i: (bi, i, 0)),
        compiler_params=pltpu.CompilerParams(
            dimension_semantics=("arbitrary", "arbitrary"), vmem_limit_bytes=VMEM_LIMIT),
        name="out_proj",
    )(x, gate, y_na, y_ret, w_out, final_g.reshape(1, d))


def _rotary_tables(l):
    half = RET_DIM // 2
    nf = half // 2
    t = np.arange(l)
    row = (t // GRID_W).astype(np.float64)
    col = (t % GRID_W).astype(np.float64)
    inv = ROPE_BASE ** (-np.arange(nf, dtype=np.float64) / nf)
    ang = np.concatenate([row[:, None] * inv, col[:, None] * inv], axis=-1)
    cos, sin = np.cos(ang), np.sin(ang)
    cosf = np.concatenate([cos, cos], axis=-1).astype(np.float32)
    sinf = np.concatenate([-sin, sin], axis=-1).astype(np.float32)
    return jnp.asarray(cosf), jnp.asarray(sinf)


def kernel(x, c, ctx, c_ctx, norm_g, w_ada, b_ada, w_in, na_rpb, ret_decay_fwd, ret_decay_bwd,
           ret_norm_g, w_out, final_norm_g):
    depth = norm_g.shape[0]
    assert depth == 1, "context stream update between layers is not implemented"
    b, l, d = x.shape
    rows = l // GRID_W
    i = 0

    cc = jnp.concatenate([c, c_ctx[None, :]], axis=0)
    pad = (-cc.shape[0]) % 8
    cc = jnp.pad(cc, ((0, pad), (0, 0)))
    mod = _adaln_mod(cc, w_ada[i], b_ada[i])
    shift, scale, gate = (mod[:b, None, :d], mod[:b, None, d:2 * d], mod[:b, None, 2 * d:])
    shift_c, scale_c = mod[b:b + 1, None, :d], mod[b:b + 1, None, d:2 * d]

    w, w_vt = _cast_w_in(w_in[i], v_group=2)
    cosf, sinf = _rotary_tables(l)
    blocks = GROUP_COLS // LANES

    lat_groups = ((0, "na_q"), (1, "plain"), (3, "silu"), (4, "rot"), (5, "rot_kscale"), (6, "plain"), (7, "silu"))
    lat_names = ("na_q", "na_k", "na_g", "r_q", "r_k", "r_v", "r_g")
    proj, vt = _in_proj(x, scale, shift, norm_g[i], w, w_vt, cosf, sinf, lat_groups, tm=1024, sub=256)

    lc = ctx.shape[1]
    ctx_groups = ((1, "plain"), (5, "kscale"), (6, "plain"))
    ctx_names = ("na_k", "r_k", "r_v")
    proj_ctx, vct = _in_proj(ctx, scale_c, shift_c, norm_g[i], w, w_vt, cosf[:lc], sinf[:lc], ctx_groups,
                             tm=lc, sub=lc)

    lg = jnp.stack([-jnp.exp(ret_decay_fwd[i].astype(jnp.float32)),
                    -jnp.exp(ret_decay_bwd[i].astype(jnp.float32))])
    y_na, y_ret = _mixers(na_rpb[i], lg, proj, vt, proj_ctx, vct, ret_norm_g[i], rows,
                          cols={name: k * blocks for k, name in enumerate(lat_names)},
                          ctx_cols={name: k * blocks for k, name in enumerate(ctx_names)})

    return _out_proj(x, gate, y_na, y_ret, w_out[i].astype(jnp.bfloat16), final_norm_g, tm=2048, sub=256)
```

```python
import functools
import math

import jax
import jax.numpy as jnp
import numpy as np
from jax import lax
from jax.experimental import pallas as pl
from jax.experimental.pallas import tpu as pltpu

D_MODEL = 1024
GRID_W = 64
NA_HEAD_DIM = 64
NA_WIDTH = 512
NA_HEADS = 8
NA_WIN_ROWS = 8
NA_WIN_COLS = 16
RET_HEADS = 4
RET_WIDTH = 512
RET_DIM = 128
RET_CHUNK = 128
ROPE_BASE = 10000.0
EPS = 1e-6

LANES = 128
MXU_TILE = 256
BF16_SUBLANES = 16
GROUP_COLS = 512
NA_Q_ROWS = 4
NA_K_ROWS = 12
NA_AHEAD = 1
RET_AHEAD = 1
MIX_BATCH = 2
MASK_VALUE = -1e30
LOG2E = math.log2(math.e)

VMEM_LIMIT = 56 * 1024 * 1024


def _silu(v):
    return v * (1.0 / (1.0 + jnp.exp(-v)))


def _mod_kernel(c_ref, w_ref, b_ref, o_ref):
    a = _silu(c_ref[...])
    o_ref[...] = jnp.dot(a, w_ref[...], preferred_element_type=jnp.float32,
                         precision=lax.Precision.HIGHEST) + b_ref[...]


def _adaln_mod(cc, w_ada, b_ada):
    m, d = cc.shape
    n = w_ada.shape[1]
    tn = 512
    return pl.pallas_call(
        _mod_kernel,
        out_shape=jax.ShapeDtypeStruct((m, n), jnp.float32),
        grid=(n // tn,),
        in_specs=[pl.BlockSpec((m, d), lambda j: (0, 0)),
                  pl.BlockSpec((d, tn), lambda j: (0, j)),
                  pl.BlockSpec((1, tn), lambda j: (0, j))],
        out_specs=pl.BlockSpec((m, tn), lambda j: (0, j)),
        compiler_params=pltpu.CompilerParams(
            dimension_semantics=("arbitrary",), vmem_limit_bytes=VMEM_LIMIT),
        name="adaln_mod",
    )(cc, w_ada, b_ada.reshape(1, n))


def _cast_kernel(w_ref, o_ref, ot_ref, *, v_group):
    w = w_ref[...]
    o_ref[...] = w.astype(o_ref.dtype)
    ot_ref[...] = w[:, v_group * GROUP_COLS:(v_group + 1) * GROUP_COLS].T.astype(ot_ref.dtype)


def _cast_w_in(w, v_group):
    d, n = w.shape
    rb = 256
    return pl.pallas_call(
        functools.partial(_cast_kernel, v_group=v_group),
        out_shape=(jax.ShapeDtypeStruct((d, n), jnp.bfloat16),
                   jax.ShapeDtypeStruct((GROUP_COLS, d), jnp.bfloat16)),
        grid=(d // rb,),
        in_specs=[pl.BlockSpec((rb, n), lambda r: (r, 0))],
        out_specs=(pl.BlockSpec((rb, n), lambda r: (r, 0)),
                   pl.BlockSpec((GROUP_COLS, rb), lambda r: (0, r))),
        compiler_params=pltpu.CompilerParams(
            dimension_semantics=("arbitrary",), vmem_limit_bytes=VMEM_LIMIT),
        name="cast_w_in",
    )(w)


def _rotary(acc, cosf, sinf):
    outs = []
    for h in range(GROUP_COLS // LANES):
        xh = acc[:, h * LANES:(h + 1) * LANES]
        outs.append(xh * cosf + pltpu.roll(xh, LANES // 2, 1) * sinf)
    return jnp.concatenate(outs, axis=-1)


def _proj_kernel(x_ref, scale_ref, shift_ref, g_ref, w_ref, wvt_ref, cos_ref, sin_ref, o_ref, vt_ref,
                 *, groups, sub):
    tm = x_ref.shape[1]

    def normed(j):
        x = x_ref[0, j * sub:(j + 1) * sub, :]
        ms = jnp.mean(x * x, axis=-1, keepdims=True)
        h = x * lax.rsqrt(ms + EPS) * g_ref[...]
        h = h * (1.0 + scale_ref[0]) + shift_ref[0]
        return h.astype(jnp.bfloat16)

    def project(j, hb):
        rows = slice(j * sub, (j + 1) * sub)
        vt = lax.dot_general(wvt_ref[...], hb, (((1,), (1,)), ((), ())), preferred_element_type=jnp.float32)
        vt_ref[0, :, rows] = vt.astype(vt_ref.dtype)
        for gi, (src, kind) in enumerate(groups):
            acc = jnp.dot(hb, w_ref[:, src * GROUP_COLS:(src + 1) * GROUP_COLS],
                          preferred_element_type=jnp.float32)
            if kind == "na_q":
                acc = acc * (NA_HEAD_DIM ** -0.5 * LOG2E)
            elif kind == "silu":
                acc = _silu(acc)
            elif kind == "rot":
                acc = _rotary(acc, cos_ref[rows, :], sin_ref[rows, :])
            elif kind == "rot_kscale":
                acc = _rotary(acc, cos_ref[rows, :], sin_ref[rows, :]) * (RET_DIM ** -0.5)
            elif kind == "kscale":
                acc = acc * (RET_DIM ** -0.5)
            else:
                assert kind == "plain"
            per_group = GROUP_COLS // LANES
            for k in range(per_group):
                o_ref[0, gi * per_group + k, rows, :] = acc[:, k * LANES:(k + 1) * LANES].astype(o_ref.dtype)

    pending = normed(0)
    for j in range(tm // sub):
        current = pending
        if (j + 1) * sub < tm:
            pending = normed(j + 1)
        project(j, current)


def _in_proj(x, scale, shift, norm_g, w, w_vt, cosf, sinf, groups, tm, sub):
    b, l, d = x.shape
    n_in = w.shape[1]
    n = GROUP_COLS * len(groups)
    nv = w_vt.shape[0]
    assert l % tm == 0 and tm % sub == 0
    per_batch = scale.shape[0] > 1
    mod_map = (lambda bi, i: (bi, 0, 0)) if per_batch else (lambda bi, i: (0, 0, 0))
    return pl.pallas_call(
        functools.partial(_proj_kernel, groups=tuple(groups), sub=sub),
        out_shape=(jax.ShapeDtypeStruct((b, n // LANES, l, LANES), jnp.bfloat16),
                   jax.ShapeDtypeStruct((b, nv, l), jnp.bfloat16)),
        grid=(b, l // tm),
        in_specs=[pl.BlockSpec((1, tm, d), lambda bi, i: (bi, i, 0)),
                  pl.BlockSpec((1, 1, d), mod_map),
                  pl.BlockSpec((1, 1, d), mod_map),
                  pl.BlockSpec((1, d), lambda bi, i: (0, 0)),
                  pl.BlockSpec((d, n_in), lambda bi, i: (0, 0)),
                  pl.BlockSpec((nv, d), lambda bi, i: (0, 0)),
                  pl.BlockSpec((tm, LANES), lambda bi, i: (i, 0)),
                  pl.BlockSpec((tm, LANES), lambda bi, i: (i, 0))],
        out_specs=(pl.BlockSpec((1, n // LANES, tm, LANES), lambda bi, i: (bi, 0, i, 0)),
                   pl.BlockSpec((1, nv, tm), lambda bi, i: (bi, 0, i))),
        compiler_params=pltpu.CompilerParams(
            dimension_semantics=("arbitrary", "arbitrary"), vmem_limit_bytes=VMEM_LIMIT),
        name="in_proj",
    )(x, scale, shift, norm_g.reshape(1, d), w, w_vt, cosf, sinf)


def _na_window_start(rq, rows):
    kr = min(NA_WIN_ROWS, rows)
    return int(np.clip(rq - kr // 2, 0, rows - kr))


def _na_geometry(rows):
    kr = min(NA_WIN_ROWS, rows)
    n_groups = rows // NA_Q_ROWS
    assert n_groups >= 3
    geo = []
    for g in range(n_groups):
        lo = _na_window_start(NA_Q_ROWS * g, rows)
        hi = _na_window_start(NA_Q_ROWS * g + NA_Q_ROWS - 1, rows) + kr
        geo.append((lo, hi - lo, 0 if g == 0 else (2 if g == n_groups - 1 else 1)))
    assert len({n for _, n, v in geo if v == 1}) == 1 and max(n for _, n, _ in geo) <= NA_K_ROWS
    return geo


def _na_row_offsets(rows):
    kr = min(NA_WIN_ROWS, rows)
    geo = _na_geometry(rows)
    table = []
    for rep in (0, 1, len(geo) - 1):
        start, n_rows, _ = geo[rep]
        per_i = []
        for i in range(NA_Q_ROWS):
            rq = NA_Q_ROWS * rep + i
            r0 = _na_window_start(rq, rows)
            per_i.append([start + j - rq + NA_WIN_ROWS - 1 if r0 <= start + j < r0 + kr else None
                          for j in range(n_rows)])
        table.append(per_i)
    return table


def _na_live_rows(rows):
    live = []
    for per_i in _na_row_offsets(rows):
        pairs = []
        for ip in range(NA_Q_ROWS // 2):
            seen = [j for j in range(len(per_i[0]))
                    if per_i[2 * ip][j] is not None or per_i[2 * ip + 1][j] is not None]
            assert seen == list(range(seen[0], seen[-1] + 1))
            pairs.append((seen[0], seen[-1] + 1))
        live.append(pairs)
    return live


def _na_build_bias(rpb_ref, hp, bias_ref, base_ref, rows):
    n_dr = 2 * NA_WIN_ROWS - 1
    n_dc = 2 * NA_WIN_COLS - 1
    ck = lax.broadcasted_iota(jnp.int32, (GRID_W, LANES), 0)
    lane = lax.broadcasted_iota(jnp.int32, (GRID_W, LANES), 1)
    left = lane < GRID_W
    cq = jnp.where(left, lane, lane - GRID_W)
    c0 = jnp.clip(cq - NA_WIN_COLS // 2, 0, GRID_W - NA_WIN_COLS)
    valid_c = (ck >= c0) & (ck < c0 + NA_WIN_COLS)
    dc = jnp.clip(ck - cq + NA_WIN_COLS - 1, 0, n_dc - 1)

    offsets = _na_row_offsets(rows)
    users = {}
    for var in range(3):
        for j in range(len(offsets[var][0])):
            for ip in range(NA_Q_ROWS // 2):
                key = (offsets[var][2 * ip][j], offsets[var][2 * ip + 1][j])
                users.setdefault(key, []).append((var, j, ip))

    used = sorted({dr for pair in users for dr in pair if dr is not None})
    masked = jnp.full((GRID_W, LANES), MASK_VALUE, jnp.float32)
    for h in range(2):
        head = 2 * hp + h
        for dr in used:
            def body(d, acc, dr=dr, head=head):
                return jnp.where(dc == d, rpb_ref[(head * n_dr + dr) * n_dc + d], acc)
            acc = lax.fori_loop(0, n_dc, body, jnp.zeros((GRID_W, LANES), jnp.float32), unroll=True)
            base_ref[dr] = jnp.where(valid_c, acc * LOG2E, MASK_VALUE)
        for (dr_l, dr_r), dests in users.items():
            t_l = masked if dr_l is None else base_ref[dr_l]
            t_r = masked if dr_r is None else base_ref[dr_r]
            tile = t_l if dr_l == dr_r else jnp.where(left, t_l, t_r)
            for var, j, ip in dests:
                bias_ref[var, h, j * GRID_W:(j + 1) * GRID_W, ip * LANES:(ip + 1) * LANES] = tile


def _na_program(q_ref, k_ref, vt_ref, g_ref, kc_ref, vct_ref, o_ref, bias_ref, vth_ref, vcth_ref, rows):
    geo = _na_geometry(rows)
    live = _na_live_rows(rows)
    tq = NA_Q_ROWS * GRID_W
    dh = NA_HEAD_DIM

    lane = lax.broadcasted_iota(jnp.int32, (1, LANES), 1)
    head0_lanes = lane < dh
    n_ones = vth_ref.shape[1] - dh
    for h in range(2):
        vth_ref[h, :dh, :] = vt_ref[0, h * dh:(h + 1) * dh, :]
        vth_ref[h, dh:, :] = jnp.ones((n_ones, vt_ref.shape[2]), vt_ref.dtype)
        vcth_ref[h, :dh, :] = vct_ref[0, h * dh:(h + 1) * dh, :]
        vcth_ref[h, dh:, :] = jnp.ones((n_ones, vct_ref.shape[2]), vct_ref.dtype)

    kc = kc_ref[0]
    contract_last = (((1,), (1,)), ((), ()))
    tasks = [(g, h) for g in range(len(geo)) for h in range(2)]

    def scores(t):
        g, h = tasks[t]
        ws, n_rows, var = geo[g]
        nk = n_rows * GRID_W
        qg = q_ref[0, g * tq:(g + 1) * tq, :]
        kw = k_ref[0, ws * GRID_W:ws * GRID_W + nk, :]
        sel = head0_lanes if h == 0 else jnp.logical_not(head0_lanes)
        qh = jnp.where(sel, qg, jnp.zeros_like(qg))
        s_raw = lax.dot_general(kw, qh, contract_last, preferred_element_type=jnp.float32)
        s_ctx = lax.dot_general(kc, qh, contract_last, preferred_element_type=jnp.float32)
        s_loc = []
        for ip, (lo, hi) in enumerate(live[var]):
            r, c = slice(lo * GRID_W, hi * GRID_W), slice(ip * LANES, (ip + 1) * LANES)
            s_loc.append(s_raw[r, c] + bias_ref[var, h, r, c])
        return s_loc, s_ctx

    halves = []

    def attend(t, s):
        g, h = tasks[t]
        s_loc, s_ctx = s
        ws, n_rows, var = geo[g]
        n_pad = n_rows + (-n_rows) % (MXU_TILE // GRID_W)
        p_loc, p_ctx = [], []
        for ip, (lo, hi) in enumerate(live[var]):
            sc = s_ctx[:, ip * LANES:(ip + 1) * LANES]
            m = jnp.maximum(jnp.max(s_loc[ip], axis=0, keepdims=True), jnp.max(sc, axis=0, keepdims=True))
            zeros = lambda n: [jnp.zeros((n * GRID_W, LANES), jnp.bfloat16)] if n else []
            p_loc.append(jnp.concatenate(
                zeros(lo) + [jnp.exp2(s_loc[ip] - m).astype(jnp.bfloat16)] + zeros(n_pad - hi), axis=0))
            p_ctx.append(jnp.exp2(sc - m).astype(jnp.bfloat16))
        p_loc = jnp.concatenate(p_loc, axis=1)
        p_ctx = jnp.concatenate(p_ctx, axis=1)
        nk = n_pad * GRID_W
        ot = jnp.dot(vth_ref[h, :, ws * GRID_W:ws * GRID_W + nk], p_loc, preferred_element_type=jnp.float32)
        ot = ot + jnp.dot(vcth_ref[h], p_ctx, preferred_element_type=jnp.float32)
        halves.append(ot[:dh] / ot[dh:dh + 1])
        if h == 1:
            o2 = jnp.concatenate(halves, axis=0).T
            halves.clear()
            gate = g_ref[0, g * tq:(g + 1) * tq, :].astype(jnp.float32)
            o_ref[0, g * tq:(g + 1) * tq, :] = (o2 * gate).astype(o_ref.dtype)

    return len(tasks), scores, attend


def _ret_program(lg_f, lg_b, q_ref, k_ref, v_ref, g_ref, kc_ref, vc_ref, ng_ref, o_ref, t_ref, s_ref):
    c_len = RET_CHUNK
    l = q_ref.shape[1]
    lc = kc_ref.shape[1]
    n_chunks = l // c_len
    contract_last = (((1,), (1,)), ((), ()))

    ii = lax.broadcasted_iota(jnp.int32, (c_len, c_len), 0).astype(jnp.float32)
    jj = lax.broadcasted_iota(jnp.int32, (c_len, c_len), 1).astype(jnp.float32)
    dist = ii - jj
    decay = (jnp.where(dist >= 0, jnp.exp(lg_f * jnp.maximum(dist, 0.0)), 0.0)
             + jnp.where(dist <= 0, jnp.exp(lg_b * jnp.maximum(-dist, 0.0)), 0.0))
    ic = lax.broadcasted_iota(jnp.int32, (c_len, 1), 0).astype(jnp.float32)
    kdec_f = jnp.exp(lg_f * (c_len - 1 - ic))
    kdec_b = jnp.exp(lg_b * ic)
    qdec_f = jnp.exp(lg_f * (ic + 1.0))
    qdec_b = jnp.exp(lg_b * (c_len - ic))
    one = jnp.ones((1, 1), jnp.float32)
    cdec_f = jnp.exp(one * (lg_f * c_len))
    cdec_b = jnp.exp(one * (lg_b * c_len))

    jcr = lax.broadcasted_iota(jnp.int32, (1, lc), 1).astype(jnp.float32)
    kct = kc_ref[0].astype(jnp.float32).T
    vcx = vc_ref[0]
    s_f = jnp.dot((kct * jnp.exp(lg_f * (lc - 1 - jcr))).astype(jnp.bfloat16), vcx,
                  preferred_element_type=jnp.float32)
    s_b = jnp.dot((kct * jnp.exp(lg_b * jcr)).astype(jnp.bfloat16), vcx,
                  preferred_element_type=jnp.float32)

    for c in range(n_chunks):
        rows_c = slice(c * c_len, (c + 1) * c_len)
        kt = k_ref[0, rows_c, :].astype(jnp.float32).T.astype(jnp.bfloat16)
        vf = v_ref[0, rows_c, :].astype(jnp.float32)
        vw = jnp.concatenate([(vf * kdec_f).astype(jnp.bfloat16), (vf * kdec_b).astype(jnp.bfloat16)], axis=1)
        t_ref[c] = jnp.dot(kt, vw, preferred_element_type=jnp.float32)

    for c in range(n_chunks):
        s_ref[c, :, :RET_DIM] = s_f.astype(jnp.bfloat16)
        s_f = cdec_f * s_f + t_ref[c, :, :RET_DIM]
    for c in reversed(range(n_chunks)):
        s_ref[c, :, RET_DIM:] = s_b.astype(jnp.bfloat16)
        s_b = cdec_b * s_b + t_ref[c, :, RET_DIM:]

    ng = ng_ref[...]

    def scores(c):
        rows_c = slice(c * c_len, (c + 1) * c_len)
        qc = q_ref[0, rows_c, :]
        s = lax.dot_general(qc, k_ref[0, rows_c, :], contract_last, preferred_element_type=jnp.float32) * decay
        cross = jnp.dot(qc, s_ref[c], preferred_element_type=jnp.float32)
        return s.astype(jnp.bfloat16), cross[:, :RET_DIM] * qdec_f + cross[:, RET_DIM:] * qdec_b

    def finish(c, sc):
        s, cross = sc
        rows_c = slice(c * c_len, (c + 1) * c_len)
        o = jnp.dot(s, v_ref[0, rows_c, :], preferred_element_type=jnp.float32) + cross
        o = o * lax.rsqrt(jnp.mean(o * o, axis=-1, keepdims=True) + EPS) * ng
        gate = g_ref[0, rows_c, :].astype(jnp.float32)
        o_ref[0, rows_c, :] = (o * gate).astype(o_ref.dtype)

    return n_chunks, scores, finish


def _mixer_kernel(rpb_ref, lg_ref,
                  aq_ref, ak_ref, avt_ref, ag_ref, akc_ref, avct_ref,
                  rq_ref, rk_ref, rv_ref, rg_ref, rkc_ref, rvc_ref, ng_ref,
                  yna_ref, yret_ref,
                  bias_ref, base_ref, vth_ref, vcth_ref, t_ref, s_ref, *, rows):
    i = pl.program_id(0)

    @pl.when(pl.program_id(1) == 0)
    def _():
        _na_build_bias(rpb_ref, i, bias_ref, base_ref, rows)

    na_tasks, ret_tasks = [], []
    for j in range(aq_ref.shape[0]):
        slab = lambda ref, j=j: ref.at[j]
        one = lambda ref, j=j: ref.at[pl.ds(j, 1)]
        n_ret, ret_scores, ret_finish = _ret_program(
            lg_ref[0, i], lg_ref[1, i], slab(rq_ref), slab(rk_ref), slab(rv_ref), slab(rg_ref),
            slab(rkc_ref), slab(rvc_ref), ng_ref, slab(yret_ref), t_ref.at[j], s_ref.at[j])
        n_na, na_scores, na_attend = _na_program(
            slab(aq_ref), slab(ak_ref), one(avt_ref), slab(ag_ref), slab(akc_ref), one(avct_ref),
            slab(yna_ref), bias_ref, vth_ref.at[j], vcth_ref.at[j], rows)
        na_tasks += [(na_scores, na_attend, t) for t in range(n_na)]
        ret_tasks += [(ret_scores, ret_finish, c) for c in range(n_ret)]

    issue = lambda task: task[0](task[2])
    na_q = [issue(task) for task in na_tasks[:NA_AHEAD]]
    ret_q = [issue(task) for task in ret_tasks[:RET_AHEAD]]
    for t in range(max(len(na_tasks), len(ret_tasks))):
        if t + NA_AHEAD < len(na_tasks):
            na_q.append(issue(na_tasks[t + NA_AHEAD]))
        if t + RET_AHEAD < len(ret_tasks):
            ret_q.append(issue(ret_tasks[t + RET_AHEAD]))
        if t < len(na_tasks):
            na_tasks[t][1](na_tasks[t][2], na_q.pop(0))
        if t < len(ret_tasks):
            ret_tasks[t][1](ret_tasks[t][2], ret_q.pop(0))


def _mixers(rpb, lg, proj, vt, proj_ctx, vct, ret_norm_g, rows, cols, ctx_cols):
    b, _, l, _ = proj.shape
    lc = proj_ctx.shape[2]
    nb = MIX_BATCH
    assert NA_HEADS // 2 == RET_HEADS and b % nb == 0
    blk = lambda name: pl.BlockSpec((nb, 1, l, LANES), lambda i, bi, *_, off=cols[name]: (bi, off + i, 0, 0))
    cblk = lambda name: pl.BlockSpec((nb, 1, lc, LANES), lambda i, bi, *_, off=ctx_cols[name]: (bi, off + i, 0, 0))
    n_chunks = l // RET_CHUNK
    grid_spec = pltpu.PrefetchScalarGridSpec(
        num_scalar_prefetch=2,
        grid=(RET_HEADS, b // nb),
        in_specs=[blk("na_q"), blk("na_k"),
                  pl.BlockSpec((nb, LANES, l), lambda i, bi, *_: (bi, i, 0)),
                  blk("na_g"), cblk("na_k"),
                  pl.BlockSpec((nb, LANES, lc), lambda i, bi, *_: (bi, i, 0)),
                  blk("r_q"), blk("r_k"), blk("r_v"), blk("r_g"), cblk("r_k"), cblk("r_v"),
                  pl.BlockSpec((1, LANES), lambda i, bi, *_: (0, i))],
        out_specs=(pl.BlockSpec((nb, 1, l, LANES), lambda i, bi, *_: (bi, i, 0, 0)),
                   pl.BlockSpec((nb, 1, l, LANES), lambda i, bi, *_: (bi, i, 0, 0))),
        scratch_shapes=[pltpu.VMEM((3, 2, NA_K_ROWS * GRID_W, NA_Q_ROWS * GRID_W), jnp.float32),
                        pltpu.VMEM((2 * NA_WIN_ROWS - 1, GRID_W, LANES), jnp.float32),
                        pltpu.VMEM((nb, 2, NA_HEAD_DIM + BF16_SUBLANES, l), jnp.bfloat16),
                        pltpu.VMEM((nb, 2, NA_HEAD_DIM + BF16_SUBLANES, lc), jnp.bfloat16),
                        pltpu.VMEM((nb, n_chunks, RET_DIM, 2 * RET_DIM), jnp.float32),
                        pltpu.VMEM((nb, n_chunks, RET_DIM, 2 * RET_DIM), jnp.bfloat16)],
    )
    return pl.pallas_call(
        functools.partial(_mixer_kernel, rows=rows),
        out_shape=(jax.ShapeDtypeStruct((b, NA_WIDTH // LANES, l, LANES), jnp.bfloat16),
                   jax.ShapeDtypeStruct((b, RET_WIDTH // LANES, l, LANES), jnp.bfloat16)),
        grid_spec=grid_spec,
        compiler_params=pltpu.CompilerParams(
            dimension_semantics=("arbitrary", "arbitrary"), vmem_limit_bytes=VMEM_LIMIT),
        name="mixers",
    )(rpb.astype(jnp.float32).reshape(-1), lg,
      proj, proj, vt, proj, proj_ctx, vct,
      proj, proj, proj, proj, proj_ctx, proj_ctx, ret_norm_g.reshape(1, RET_WIDTH))


def _out_kernel(x_ref, gate_ref, yna_ref, yret_ref, w_ref, fg_ref, o_ref, *, sub):
    tm = x_ref.shape[1]

    def mix(j):
        rows = slice(j * sub, (j + 1) * sub)
        y = jnp.concatenate([ref[0, k, rows, :] for ref in (yna_ref, yret_ref) for k in range(ref.shape[1])],
                            axis=1)
        return jnp.dot(y, w_ref[...], preferred_element_type=jnp.float32)

    def finish(j, y):
        rows = slice(j * sub, (j + 1) * sub)
        z = x_ref[0, rows, :] + gate_ref[0] * y
        ms = jnp.mean(z * z, axis=-1, keepdims=True)
        o_ref[0, rows, :] = z * lax.rsqrt(ms + EPS) * fg_ref[...]

    pending = mix(0)
    for j in range(tm // sub):
        current = pending
        if (j + 1) * sub < tm:
            pending = mix(j + 1)
        finish(j, current)


def _out_proj(x, gate, y_na, y_ret, w_out, final_g, tm, sub):
    b, l, d = x.shape
    assert l % tm == 0 and tm % sub == 0
    return pl.pallas_call(
        functools.partial(_out_kernel, sub=sub),
        out_shape=jax.ShapeDtypeStruct((b, l, d), jnp.float32),
        grid=(b, l // tm),
        in_specs=[pl.BlockSpec((1, tm, d), lambda bi, i: (bi, i, 0)),
                  pl.BlockSpec((1, 1, d), lambda bi, i: (bi, 0, 0)),
                  pl.BlockSpec((1, NA_WIDTH // LANES, tm, LANES), lambda bi, i: (bi, 0, i, 0)),
                  pl.BlockSpec((1, RET_WIDTH // LANES, tm, LANES), lambda bi, i: (bi, 0, i, 0)),
                  pl.BlockSpec((NA_WIDTH + RET_WIDTH, d), lambda bi, i: (0, 0)),
                  pl.BlockSpec((1, d), lambda bi, i: (0, 0))],
        out_specs=pl.BlockSpec((1, tm, d), lambda bi, i: (bi, i, 0)),
        compiler_params=pltpu.CompilerParams(
            dimension_semantics=("arbitrary", "arbitrary"), vmem_limit_bytes=VMEM_LIMIT),
        name="out_proj",
    )(x, gate, y_na, y_ret, w_out, final_g.reshape(1, d))


def _rotary_tables(l):
    half = RET_DIM // 2
    nf = half // 2
    t = np.arange(l)
    row = (t // GRID_W).astype(np.float64)
    col = (t % GRID_W).astype(np.float64)
    inv = ROPE_BASE ** (-np.arange(nf, dtype=np.float64) / nf)
    ang = np.concatenate([row[:, None] * inv, col[:, None] * inv], axis=-1)
    cos, sin = np.cos(ang), np.sin(ang)
    cosf = np.concatenate([cos, cos], axis=-1).astype(np.float32)
    sinf = np.concatenate([-sin, sin], axis=-1).astype(np.float32)
    return jnp.asarray(cosf), jnp.asarray(sinf)


def kernel(x, c, ctx, c_ctx, norm_g, w_ada, b_ada, w_in, na_rpb, ret_decay_fwd, ret_decay_bwd,
           ret_norm_g, w_out, final_norm_g):
    depth = norm_g.shape[0]
    assert depth == 1, "context stream update between layers is not implemented"
    b, l, d = x.shape
    rows = l // GRID_W
    i = 0

    cc = jnp.concatenate([c, c_ctx[None, :]], axis=0)
    pad = (-cc.shape[0]) % 8
    cc = jnp.pad(cc, ((0, pad), (0, 0)))
    mod = _adaln_mod(cc, w_ada[i], b_ada[i])
    shift, scale, gate = (mod[:b, None, :d], mod[:b, None, d:2 * d], mod[:b, None, 2 * d:])
    shift_c, scale_c = mod[b:b + 1, None, :d], mod[b:b + 1, None, d:2 * d]

    w, w_vt = _cast_w_in(w_in[i], v_group=2)
    cosf, sinf = _rotary_tables(l)
    blocks = GROUP_COLS // LANES

    lat_groups = ((0, "na_q"), (1, "plain"), (3, "silu"), (4, "rot"), (5, "rot_kscale"), (6, "plain"), (7, "silu"))
    lat_names = ("na_q", "na_k", "na_g", "r_q", "r_k", "r_v", "r_g")
    proj, vt = _in_proj(x, scale, shift, norm_g[i], w, w_vt, cosf, sinf, lat_groups, tm=1024, sub=256)

    lc = ctx.shape[1]
    ctx_groups = ((1, "plain"), (5, "kscale"), (6, "plain"))
    ctx_names = ("na_k", "r_k", "r_v")
    proj_ctx, vct = _in_proj(ctx, scale_c, shift_c, norm_g[i], w, w_vt, cosf[:lc], sinf[:lc], ctx_groups,
                             tm=lc, sub=lc)

    lg = jnp.stack([-jnp.exp(ret_decay_fwd[i].astype(jnp.float32)),
                    -jnp.exp(ret_decay_bwd[i].astype(jnp.float32))])
    y_na, y_ret = _mixers(na_rpb[i], lg, proj, vt, proj_ctx, vct, ret_norm_g[i], rows,
                          cols={name: k * blocks for k, name in enumerate(lat_names)},
                          ctx_cols={name: k * blocks for k, name in enumerate(ctx_names)})

    return _out_proj(x, gate, y_na, y_ret, w_out[i].astype(jnp.bfloat16), final_norm_g, tm=2048, sub=256)
```

```python
import functools
import math

import jax
import jax.numpy as jnp
import numpy as np
from jax import lax
from jax.experimental import pallas as pl
from jax.experimental.pallas import tpu as pltpu

D_MODEL = 1024
GRID_W = 64
NA_HEAD_DIM = 64
NA_WIDTH = 512
NA_HEADS = 8
NA_WIN_ROWS = 8
NA_WIN_COLS = 16
RET_HEADS = 4
RET_WIDTH = 512
RET_DIM = 128
RET_CHUNK = 256
ROPE_BASE = 10000.0
EPS = 1e-6

LANES = 128
MXU_TILE = 256
BF16_SUBLANES = 16
GROUP_COLS = 512
NA_Q_ROWS = 4
NA_K_ROWS = 12
NA_AHEAD = 1
RET_AHEAD = 1
MIX_BATCH = 2
MASK_VALUE = -1e30
LOG2E = math.log2(math.e)

VMEM_LIMIT = 56 * 1024 * 1024


def _silu(v):
    return v * (1.0 / (1.0 + jnp.exp(-v)))


def _split_bf16(v):
    hi = v.astype(jnp.bfloat16)
    return hi, (v - hi.astype(jnp.float32)).astype(jnp.bfloat16)


def _mod_kernel(c_ref, w_ref, b_ref, o_ref):
    m = c_ref.shape[0]
    a_hi, a_lo = _split_bf16(_silu(c_ref[...]))
    w_hi, w_lo = _split_bf16(w_ref[...])
    both = jnp.dot(jnp.concatenate([a_hi, a_lo], axis=0), w_hi, preferred_element_type=jnp.float32)
    o_ref[...] = (both[:m] + both[m:]) + jnp.dot(a_hi, w_lo, preferred_element_type=jnp.float32) + b_ref[...]


def _adaln_mod(cc, w_ada, b_ada):
    m, d = cc.shape
    n = w_ada.shape[1]
    tn = 512
    return pl.pallas_call(
        _mod_kernel,
        out_shape=jax.ShapeDtypeStruct((m, n), jnp.float32),
        grid=(n // tn,),
        in_specs=[pl.BlockSpec((m, d), lambda j: (0, 0)),
                  pl.BlockSpec((d, tn), lambda j: (0, j)),
                  pl.BlockSpec((1, tn), lambda j: (0, j))],
        out_specs=pl.BlockSpec((m, tn), lambda j: (0, j)),
        compiler_params=pltpu.CompilerParams(
            dimension_semantics=("arbitrary",), vmem_limit_bytes=VMEM_LIMIT),
        name="adaln_mod",
    )(cc, w_ada, b_ada.reshape(1, n))


def _cast_kernel(w_ref, o_ref, ot_ref, *, v_group):
    w = w_ref[...]
    o_ref[...] = w.astype(o_ref.dtype)
    ot_ref[...] = w[:, v_group * GROUP_COLS:(v_group + 1) * GROUP_COLS].T.astype(ot_ref.dtype)


def _cast_w_in(w, v_group):
    d, n = w.shape
    rb = 256
    return pl.pallas_call(
        functools.partial(_cast_kernel, v_group=v_group),
        out_shape=(jax.ShapeDtypeStruct((d, n), jnp.bfloat16),
                   jax.ShapeDtypeStruct((GROUP_COLS, d), jnp.bfloat16)),
        grid=(d // rb,),
        in_specs=[pl.BlockSpec((rb, n), lambda r: (r, 0))],
        out_specs=(pl.BlockSpec((rb, n), lambda r: (r, 0)),
                   pl.BlockSpec((GROUP_COLS, rb), lambda r: (0, r))),
        compiler_params=pltpu.CompilerParams(
            dimension_semantics=("arbitrary",), vmem_limit_bytes=VMEM_LIMIT),
        name="cast_w_in",
    )(w)


def _rotary(acc, cosf, sinf):
    outs = []
    for h in range(GROUP_COLS // LANES):
        xh = acc[:, h * LANES:(h + 1) * LANES]
        outs.append(xh * cosf + pltpu.roll(xh, LANES // 2, 1) * sinf)
    return jnp.concatenate(outs, axis=-1)


def _proj_kernel(x_ref, scale_ref, shift_ref, g_ref, w_ref, wvt_ref, cos_ref, sin_ref, o_ref, vt_ref,
                 *, groups, sub):
    tm = x_ref.shape[1]

    def normed(j):
        x = x_ref[0, j * sub:(j + 1) * sub, :]
        ms = jnp.mean(x * x, axis=-1, keepdims=True)
        h = x * lax.rsqrt(ms + EPS) * g_ref[...]
        h = h * (1.0 + scale_ref[0]) + shift_ref[0]
        return h.astype(jnp.bfloat16)

    def project(j, hb):
        rows = slice(j * sub, (j + 1) * sub)
        vt = lax.dot_general(wvt_ref[...], hb, (((1,), (1,)), ((), ())), preferred_element_type=jnp.float32)
        vt_ref[0, :, rows] = vt.astype(vt_ref.dtype)
        for gi, (src, kind) in enumerate(groups):
            acc = jnp.dot(hb, w_ref[:, src * GROUP_COLS:(src + 1) * GROUP_COLS],
                          preferred_element_type=jnp.float32)
            if kind == "na_q":
                acc = acc * (NA_HEAD_DIM ** -0.5 * LOG2E)
            elif kind == "silu":
                acc = _silu(acc)
            elif kind == "rot":
                acc = _rotary(acc, cos_ref[rows, :], sin_ref[rows, :])
            elif kind == "rot_kscale":
                acc = _rotary(acc, cos_ref[rows, :], sin_ref[rows, :]) * (RET_DIM ** -0.5)
            elif kind == "kscale":
                acc = acc * (RET_DIM ** -0.5)
            else:
                assert kind == "plain"
            per_group = GROUP_COLS // LANES
            for k in range(per_group):
                o_ref[0, gi * per_group + k, rows, :] = acc[:, k * LANES:(k + 1) * LANES].astype(o_ref.dtype)

    pending = normed(0)
    for j in range(tm // sub):
        current = pending
        if (j + 1) * sub < tm:
            pending = normed(j + 1)
        project(j, current)


def _in_proj(x, scale, shift, norm_g, w, w_vt, cosf, sinf, groups, tm, sub):
    b, l, d = x.shape
    n_in = w.shape[1]
    n = GROUP_COLS * len(groups)
    nv = w_vt.shape[0]
    assert l % tm == 0 and tm % sub == 0
    per_batch = scale.shape[0] > 1
    mod_map = (lambda bi, i: (bi, 0, 0)) if per_batch else (lambda bi, i: (0, 0, 0))
    return pl.pallas_call(
        functools.partial(_proj_kernel, groups=tuple(groups), sub=sub),
        out_shape=(jax.ShapeDtypeStruct((b, n // LANES, l, LANES), jnp.bfloat16),
                   jax.ShapeDtypeStruct((b, nv, l), jnp.bfloat16)),
        grid=(b, l // tm),
        in_specs=[pl.BlockSpec((1, tm, d), lambda bi, i: (bi, i, 0)),
                  pl.BlockSpec((1, 1, d), mod_map),
                  pl.BlockSpec((1, 1, d), mod_map),
                  pl.BlockSpec((1, d), lambda bi, i: (0, 0)),
                  pl.BlockSpec((d, n_in), lambda bi, i: (0, 0)),
                  pl.BlockSpec((nv, d), lambda bi, i: (0, 0)),
                  pl.BlockSpec((tm, LANES), lambda bi, i: (i, 0)),
                  pl.BlockSpec((tm, LANES), lambda bi, i: (i, 0))],
        out_specs=(pl.BlockSpec((1, n // LANES, tm, LANES), lambda bi, i: (bi, 0, i, 0)),
                   pl.BlockSpec((1, nv, tm), lambda bi, i: (bi, 0, i))),
        compiler_params=pltpu.CompilerParams(
            dimension_semantics=("arbitrary", "arbitrary"), vmem_limit_bytes=VMEM_LIMIT),
        name="in_proj",
    )(x, scale, shift, norm_g.reshape(1, d), w, w_vt, cosf, sinf)


def _na_window_start(rq, rows):
    kr = min(NA_WIN_ROWS, rows)
    return int(np.clip(rq - kr // 2, 0, rows - kr))


def _na_geometry(rows):
    kr = min(NA_WIN_ROWS, rows)
    n_groups = rows // NA_Q_ROWS
    assert n_groups >= 3
    geo = []
    for g in range(n_groups):
        lo = _na_window_start(NA_Q_ROWS * g, rows)
        hi = _na_window_start(NA_Q_ROWS * g + NA_Q_ROWS - 1, rows) + kr
        geo.append((lo, hi - lo, 0 if g == 0 else (2 if g == n_groups - 1 else 1)))
    assert len({n for _, n, v in geo if v == 1}) == 1 and max(n for _, n, _ in geo) <= NA_K_ROWS
    return geo


def _na_row_offsets(rows):
    kr = min(NA_WIN_ROWS, rows)
    geo = _na_geometry(rows)
    table = []
    for rep in (0, 1, len(geo) - 1):
        start, n_rows, _ = geo[rep]
        per_i = []
        for i in range(NA_Q_ROWS):
            rq = NA_Q_ROWS * rep + i
            r0 = _na_window_start(rq, rows)
            per_i.append([start + j - rq + NA_WIN_ROWS - 1 if r0 <= start + j < r0 + kr else None
                          for j in range(n_rows)])
        table.append(per_i)
    return table


def _na_live_rows(rows):
    live = []
    for per_i in _na_row_offsets(rows):
        pairs = []
        for ip in range(NA_Q_ROWS // 2):
            seen = [j for j in range(len(per_i[0]))
                    if per_i[2 * ip][j] is not None or per_i[2 * ip + 1][j] is not None]
            assert seen == list(range(seen[0], seen[-1] + 1))
            pairs.append((seen[0], seen[-1] + 1))
        live.append(pairs)
    return live


def _na_build_bias(rpb_ref, hp, bias_ref, base_ref, rows):
    n_dr = 2 * NA_WIN_ROWS - 1
    n_dc = 2 * NA_WIN_COLS - 1
    ck = lax.broadcasted_iota(jnp.int32, (GRID_W, LANES), 0)
    lane = lax.broadcasted_iota(jnp.int32, (GRID_W, LANES), 1)
    left = lane < GRID_W
    cq = jnp.where(left, lane, lane - GRID_W)
    c0 = jnp.clip(cq - NA_WIN_COLS // 2, 0, GRID_W - NA_WIN_COLS)
    valid_c = (ck >= c0) & (ck < c0 + NA_WIN_COLS)
    dc = jnp.clip(ck - cq + NA_WIN_COLS - 1, 0, n_dc - 1)

    offsets = _na_row_offsets(rows)
    users = {}
    for var in range(3):
        for j in range(len(offsets[var][0])):
            for ip in range(NA_Q_ROWS // 2):
                key = (offsets[var][2 * ip][j], offsets[var][2 * ip + 1][j])
                users.setdefault(key, []).append((var, j, ip))

    used = sorted({dr for pair in users for dr in pair if dr is not None})
    masked = jnp.full((GRID_W, LANES), MASK_VALUE, jnp.float32)
    for h in range(2):
        head = 2 * hp + h
        for dr in used:
            def body(d, acc, dr=dr, head=head):
                return jnp.where(dc == d, rpb_ref[(head * n_dr + dr) * n_dc + d], acc)
            acc = lax.fori_loop(0, n_dc, body, jnp.zeros((GRID_W, LANES), jnp.float32), unroll=True)
            base_ref[dr] = jnp.where(valid_c, acc * LOG2E, MASK_VALUE)
        for (dr_l, dr_r), dests in users.items():
            t_l = masked if dr_l is None else base_ref[dr_l]
            t_r = masked if dr_r is None else base_ref[dr_r]
            tile = t_l if dr_l == dr_r else jnp.where(left, t_l, t_r)
            for var, j, ip in dests:
                bias_ref[var, h, j * GRID_W:(j + 1) * GRID_W, ip * LANES:(ip + 1) * LANES] = tile


def _na_program(q_ref, k_ref, vt_ref, g_ref, kc_ref, vct_ref, o_ref, bias_ref, vth_ref, vcth_ref, rows):
    geo = _na_geometry(rows)
    live = _na_live_rows(rows)
    tq = NA_Q_ROWS * GRID_W
    dh = NA_HEAD_DIM

    lane = lax.broadcasted_iota(jnp.int32, (1, LANES), 1)
    head0_lanes = lane < dh
    n_ones = vth_ref.shape[1] - dh
    for h in range(2):
        vth_ref[h, :dh, :] = vt_ref[0, h * dh:(h + 1) * dh, :]
        vth_ref[h, dh:, :] = jnp.ones((n_ones, vt_ref.shape[2]), vt_ref.dtype)
        vcth_ref[h, :dh, :] = vct_ref[0, h * dh:(h + 1) * dh, :]
        vcth_ref[h, dh:, :] = jnp.ones((n_ones, vct_ref.shape[2]), vct_ref.dtype)

    kc = kc_ref[0]
    contract_last = (((1,), (1,)), ((), ()))
    tasks = [(g, h) for g in range(len(geo)) for h in range(2)]

    def scores(t):
        g, h = tasks[t]
        ws, n_rows, var = geo[g]
        nk = n_rows * GRID_W
        qg = q_ref[0, g * tq:(g + 1) * tq, :]
        kw = k_ref[0, ws * GRID_W:ws * GRID_W + nk, :]
        sel = head0_lanes if h == 0 else jnp.logical_not(head0_lanes)
        qh = jnp.where(sel, qg, jnp.zeros_like(qg))
        s_raw = lax.dot_general(kw, qh, contract_last, preferred_element_type=jnp.float32)
        s_ctx = lax.dot_general(kc, qh, contract_last, preferred_element_type=jnp.float32)
        s_loc = []
        for ip, (lo, hi) in enumerate(live[var]):
            r, c = slice(lo * GRID_W, hi * GRID_W), slice(ip * LANES, (ip + 1) * LANES)
            s_loc.append(s_raw[r, c] + bias_ref[var, h, r, c])
        return s_loc, s_ctx

    halves = []

    def attend(t, s):
        g, h = tasks[t]
        s_loc, s_ctx = s
        ws, n_rows, var = geo[g]
        n_pad = n_rows + (-n_rows) % (MXU_TILE // GRID_W)
        p_loc, p_ctx = [], []
        for ip, (lo, hi) in enumerate(live[var]):
            sc = s_ctx[:, ip * LANES:(ip + 1) * LANES]
            m = jnp.maximum(jnp.max(s_loc[ip], axis=0, keepdims=True), jnp.max(sc, axis=0, keepdims=True))
            zeros = lambda n: [jnp.zeros((n * GRID_W, LANES), jnp.bfloat16)] if n else []
            p_loc.append(jnp.concatenate(
                zeros(lo) + [jnp.exp2(s_loc[ip] - m).astype(jnp.bfloat16)] + zeros(n_pad - hi), axis=0))
            p_ctx.append(jnp.exp2(sc - m).astype(jnp.bfloat16))
        p_loc = jnp.concatenate(p_loc, axis=1)
        p_ctx = jnp.concatenate(p_ctx, axis=1)
        nk = n_pad * GRID_W
        ot = jnp.dot(vth_ref[h, :, ws * GRID_W:ws * GRID_W + nk], p_loc, preferred_element_type=jnp.float32)
        ot = ot + jnp.dot(vcth_ref[h], p_ctx, preferred_element_type=jnp.float32)
        halves.append(ot[:dh] / ot[dh:dh + 1])
        if h == 1:
            o2 = jnp.concatenate(halves, axis=0).T
            halves.clear()
            gate = g_ref[0, g * tq:(g + 1) * tq, :].astype(jnp.float32)
            o_ref[0, g * tq:(g + 1) * tq, :] = (o2 * gate).astype(o_ref.dtype)

    return len(tasks), scores, attend


def _ret_program(lg_f, lg_b, q_ref, k_ref, v_ref, g_ref, kc_ref, vc_ref, ng_ref, o_ref, t_ref, s_ref):
    c_len = RET_CHUNK
    l = q_ref.shape[1]
    lc = kc_ref.shape[1]
    n_chunks = l // c_len
    contract_last = (((1,), (1,)), ((), ()))

    ii = lax.broadcasted_iota(jnp.int32, (c_len, c_len), 0).astype(jnp.float32)
    jj = lax.broadcasted_iota(jnp.int32, (c_len, c_len), 1).astype(jnp.float32)
    dist = ii - jj
    decay = (jnp.where(dist >= 0, jnp.exp(lg_f * jnp.maximum(dist, 0.0)), 0.0)
             + jnp.where(dist <= 0, jnp.exp(lg_b * jnp.maximum(-dist, 0.0)), 0.0))
    ic = lax.broadcasted_iota(jnp.int32, (c_len, 1), 0).astype(jnp.float32)
    kdec_f = jnp.exp(lg_f * (c_len - 1 - ic))
    kdec_b = jnp.exp(lg_b * ic)
    qdec_f = jnp.exp(lg_f * (ic + 1.0))
    qdec_b = jnp.exp(lg_b * (c_len - ic))
    one = jnp.ones((1, 1), jnp.float32)
    cdec_f = jnp.exp(one * (lg_f * c_len))
    cdec_b = jnp.exp(one * (lg_b * c_len))

    jcr = lax.broadcasted_iota(jnp.int32, (1, lc), 1).astype(jnp.float32)
    kct = kc_ref[0].astype(jnp.float32).T
    vcx = vc_ref[0]
    s_f = jnp.dot((kct * jnp.exp(lg_f * (lc - 1 - jcr))).astype(jnp.bfloat16), vcx,
                  preferred_element_type=jnp.float32)
    s_b = jnp.dot((kct * jnp.exp(lg_b * jcr)).astype(jnp.bfloat16), vcx,
                  preferred_element_type=jnp.float32)

    for c in range(n_chunks):
        rows_c = slice(c * c_len, (c + 1) * c_len)
        kt = k_ref[0, rows_c, :].astype(jnp.float32).T.astype(jnp.bfloat16)
        vf = v_ref[0, rows_c, :].astype(jnp.float32)
        vw = jnp.concatenate([(vf * kdec_f).astype(jnp.bfloat16), (vf * kdec_b).astype(jnp.bfloat16)], axis=1)
        t_ref[c] = jnp.dot(kt, vw, preferred_element_type=jnp.float32)

    for c in range(n_chunks):
        s_ref[c, :, :RET_DIM] = s_f.astype(jnp.bfloat16)
        s_f = cdec_f * s_f + t_ref[c, :, :RET_DIM]
    for c in reversed(range(n_chunks)):
        s_ref[c, :, RET_DIM:] = s_b.astype(jnp.bfloat16)
        s_b = cdec_b * s_b + t_ref[c, :, RET_DIM:]

    ng = ng_ref[...]

    def scores(c):
        rows_c = slice(c * c_len, (c + 1) * c_len)
        qc = q_ref[0, rows_c, :]
        s = lax.dot_general(qc, k_ref[0, rows_c, :], contract_last, preferred_element_type=jnp.float32) * decay
        cross = jnp.dot(qc, s_ref[c], preferred_element_type=jnp.float32)
        return s.astype(jnp.bfloat16), cross[:, :RET_DIM] * qdec_f + cross[:, RET_DIM:] * qdec_b

    def finish(c, sc):
        s, cross = sc
        rows_c = slice(c * c_len, (c + 1) * c_len)
        o = jnp.dot(s, v_ref[0, rows_c, :], preferred_element_type=jnp.float32) + cross
        o = o * lax.rsqrt(jnp.mean(o * o, axis=-1, keepdims=True) + EPS) * ng
        gate = g_ref[0, rows_c, :].astype(jnp.float32)
        o_ref[0, rows_c, :] = (o * gate).astype(o_ref.dtype)

    return n_chunks, scores, finish


def _mixer_kernel(rpb_ref, lg_ref,
                  aq_ref, ak_ref, avt_ref, ag_ref, akc_ref, avct_ref,
                  rq_ref, rk_ref, rv_ref, rg_ref, rkc_ref, rvc_ref, ng_ref,
                  yna_ref, yret_ref,
                  bias_ref, base_ref, vth_ref, vcth_ref, t_ref, s_ref, *, rows):
    i = pl.program_id(0)

    @pl.when(pl.program_id(1) == 0)
    def _():
        _na_build_bias(rpb_ref, i, bias_ref, base_ref, rows)

    nb = aq_ref.shape[0]
    slab = lambda ref, j: ref.at[j]
    one = lambda ref, j: ref.at[pl.ds(j, 1)]

    na_tasks, ret_tasks = [], []
    for j in range(nb):
        n_ret, ret_scores, ret_finish = _ret_program(
            lg_ref[0, i], lg_ref[1, i], slab(rq_ref, j), slab(rk_ref, j), slab(rv_ref, j), slab(rg_ref, j),
            slab(rkc_ref, j), slab(rvc_ref, j), ng_ref, slab(yret_ref, j), t_ref.at[j], s_ref.at[j])
        n_na, na_scores, na_attend = _na_program(
            slab(aq_ref, j), slab(ak_ref, j), one(avt_ref, j), slab(ag_ref, j), slab(akc_ref, j),
            one(avct_ref, j), slab(yna_ref, j), bias_ref, vth_ref.at[j], vcth_ref.at[j], rows)
        na_tasks += [(na_scores, na_attend, t) for t in range(n_na)]
        ret_tasks += [(ret_scores, ret_finish, c) for c in range(n_ret)]

    issue = lambda task: task[0](task[2])
    na_q = [issue(task) for task in na_tasks[:NA_AHEAD]]
    ret_q = [issue(task) for task in ret_tasks[:RET_AHEAD]]
    for t in range(max(len(na_tasks), len(ret_tasks))):
        if t + NA_AHEAD < len(na_tasks):
            na_q.append(issue(na_tasks[t + NA_AHEAD]))
        if t + RET_AHEAD < len(ret_tasks):
            ret_q.append(issue(ret_tasks[t + RET_AHEAD]))
        if t < len(na_tasks):
            na_tasks[t][1](na_tasks[t][2], na_q.pop(0))
        if t < len(ret_tasks):
            ret_tasks[t][1](ret_tasks[t][2], ret_q.pop(0))


def _mixers(rpb, lg, proj, vt, proj_ctx, vct, ret_norm_g, rows, cols, ctx_cols):
    b, _, l, _ = proj.shape
    lc = proj_ctx.shape[2]
    nb = MIX_BATCH
    assert NA_HEADS // 2 == RET_HEADS and b % nb == 0
    blk = lambda name: pl.BlockSpec((nb, 1, l, LANES), lambda i, bi, *_, off=cols[name]: (bi, off + i, 0, 0))
    cblk = lambda name: pl.BlockSpec((nb, 1, lc, LANES), lambda i, bi, *_, off=ctx_cols[name]: (bi, off + i, 0, 0))
    n_chunks = l // RET_CHUNK
    grid_spec = pltpu.PrefetchScalarGridSpec(
        num_scalar_prefetch=2,
        grid=(RET_HEADS, b // nb),
        in_specs=[blk("na_q"), blk("na_k"),
                  pl.BlockSpec((nb, LANES, l), lambda i, bi, *_: (bi, i, 0)),
                  blk("na_g"), cblk("na_k"),
                  pl.BlockSpec((nb, LANES, lc), lambda i, bi, *_: (bi, i, 0)),
                  blk("r_q"), blk("r_k"), blk("r_v"), blk("r_g"), cblk("r_k"), cblk("r_v"),
                  pl.BlockSpec((1, LANES), lambda i, bi, *_: (0, i))],
        out_specs=(pl.BlockSpec((nb, 1, l, LANES), lambda i, bi, *_: (bi, i, 0, 0)),
                   pl.BlockSpec((nb, 1, l, LANES), lambda i, bi, *_: (bi, i, 0, 0))),
        scratch_shapes=[pltpu.VMEM((3, 2, NA_K_ROWS * GRID_W, NA_Q_ROWS * GRID_W), jnp.float32),
                        pltpu.VMEM((2 * NA_WIN_ROWS - 1, GRID_W, LANES), jnp.float32),
                        pltpu.VMEM((nb, 2, NA_HEAD_DIM + BF16_SUBLANES, l), jnp.bfloat16),
                        pltpu.VMEM((nb, 2, NA_HEAD_DIM + BF16_SUBLANES, lc), jnp.bfloat16),
                        pltpu.VMEM((nb, n_chunks, RET_DIM, 2 * RET_DIM), jnp.float32),
                        pltpu.VMEM((nb, n_chunks, RET_DIM, 2 * RET_DIM), jnp.bfloat16)],
    )
    return pl.pallas_call(
        functools.partial(_mixer_kernel, rows=rows),
        out_shape=(jax.ShapeDtypeStruct((b, NA_WIDTH // LANES, l, LANES), jnp.bfloat16),
                   jax.ShapeDtypeStruct((b, RET_WIDTH // LANES, l, LANES), jnp.bfloat16)),
        grid_spec=grid_spec,
        compiler_params=pltpu.CompilerParams(
            dimension_semantics=("arbitrary", "arbitrary"), vmem_limit_bytes=VMEM_LIMIT),
        name="mixers",
    )(rpb.astype(jnp.float32).reshape(-1), lg,
      proj, proj, vt, proj, proj_ctx, vct,
      proj, proj, proj, proj, proj_ctx, proj_ctx, ret_norm_g.reshape(1, RET_WIDTH))


def _out_kernel(x_ref, gate_ref, yna_ref, yret_ref, w_ref, fg_ref, o_ref, *, sub):
    tm = x_ref.shape[1]

    def mix(j):
        rows = slice(j * sub, (j + 1) * sub)
        y = jnp.concatenate([ref[0, k, rows, :] for ref in (yna_ref, yret_ref) for k in range(ref.shape[1])],
                            axis=1)
        return jnp.dot(y, w_ref[...], preferred_element_type=jnp.float32)

    def finish(j, y):
        rows = slice(j * sub, (j + 1) * sub)
        z = x_ref[0, rows, :] + gate_ref[0] * y
        ms = jnp.mean(z * z, axis=-1, keepdims=True)
        o_ref[0, rows, :] = z * lax.rsqrt(ms + EPS) * fg_ref[...]

    pending = mix(0)
    for j in range(tm // sub):
        current = pending
        if (j + 1) * sub < tm:
            pending = mix(j + 1)
        finish(j, current)


def _out_proj(x, gate, y_na, y_ret, w_out, final_g, tm, sub):
    b, l, d = x.shape
    assert l % tm == 0 and tm % sub == 0
    return pl.pallas_call(
        functools.partial(_out_kernel, sub=sub),
        out_shape=jax.ShapeDtypeStruct((b, l, d), jnp.float32),
        grid=(b, l // tm),
        in_specs=[pl.BlockSpec((1, tm, d), lambda bi, i: (bi, i, 0)),
                  pl.BlockSpec((1, 1, d), lambda bi, i: (bi, 0, 0)),
                  pl.BlockSpec((1, NA_WIDTH // LANES, tm, LANES), lambda bi, i: (bi, 0, i, 0)),
                  pl.BlockSpec((1, RET_WIDTH // LANES, tm, LANES), lambda bi, i: (bi, 0, i, 0)),
                  pl.BlockSpec((NA_WIDTH + RET_WIDTH, d), lambda bi, i: (0, 0)),
                  pl.BlockSpec((1, d), lambda bi, i: (0, 0))],
        out_specs=pl.BlockSpec((1, tm, d), lambda bi, i: (bi, i, 0)),
        compiler_params=pltpu.CompilerParams(
            dimension_semantics=("arbitrary", "arbitrary"), vmem_limit_bytes=VMEM_LIMIT),
        name="out_proj",
    )(x, gate, y_na, y_ret, w_out, final_g.reshape(1, d))


def _rotary_tables(l):
    half = RET_DIM // 2
    nf = half // 2
    t = np.arange(l)
    row = (t // GRID_W).astype(np.float64)
    col = (t % GRID_W).astype(np.float64)
    inv = ROPE_BASE ** (-np.arange(nf, dtype=np.float64) / nf)
    ang = np.concatenate([row[:, None] * inv, col[:, None] * inv], axis=-1)
    cos, sin = np.cos(ang), np.sin(ang)
    cosf = np.concatenate([cos, cos], axis=-1).astype(np.float32)
    sinf = np.concatenate([-sin, sin], axis=-1).astype(np.float32)
    return jnp.asarray(cosf), jnp.asarray(sinf)


def kernel(x, c, ctx, c_ctx, norm_g, w_ada, b_ada, w_in, na_rpb, ret_decay_fwd, ret_decay_bwd,
           ret_norm_g, w_out, final_norm_g):
    depth = norm_g.shape[0]
    assert depth == 1, "context stream update between layers is not implemented"
    b, l, d = x.shape
    rows = l // GRID_W
    i = 0

    cc = jnp.concatenate([c, c_ctx[None, :]], axis=0)
    pad = (-cc.shape[0]) % BF16_SUBLANES
    cc = jnp.pad(cc, ((0, pad), (0, 0)))
    mod = _adaln_mod(cc, w_ada[i], b_ada[i])
    shift, scale, gate = (mod[:b, None, :d], mod[:b, None, d:2 * d], mod[:b, None, 2 * d:])
    shift_c, scale_c = mod[b:b + 1, None, :d], mod[b:b + 1, None, d:2 * d]

    w, w_vt = _cast_w_in(w_in[i], v_group=2)
    cosf, sinf = _rotary_tables(l)
    blocks = GROUP_COLS // LANES

    lat_groups = ((0, "na_q"), (1, "plain"), (3, "silu"), (4, "rot"), (5, "rot_kscale"), (6, "plain"), (7, "silu"))
    lat_names = ("na_q", "na_k", "na_g", "r_q", "r_k", "r_v", "r_g")
    proj, vt = _in_proj(x, scale, shift, norm_g[i], w, w_vt, cosf, sinf, lat_groups, tm=1024, sub=256)

    lc = ctx.shape[1]
    ctx_groups = ((1, "plain"), (5, "kscale"), (6, "plain"))
    ctx_names = ("na_k", "r_k", "r_v")
    proj_ctx, vct = _in_proj(ctx, scale_c, shift_c, norm_g[i], w, w_vt, cosf[:lc], sinf[:lc], ctx_groups,
                             tm=lc, sub=lc)

    lg = jnp.stack([-jnp.exp(ret_decay_fwd[i].astype(jnp.float32)),
                    -jnp.exp(ret_decay_bwd[i].astype(jnp.float32))])
    y_na, y_ret = _mixers(na_rpb[i], lg, proj, vt, proj_ctx, vct, ret_norm_g[i], rows,
                          cols={name: k * blocks for k, name in enumerate(lat_names)},
                          ctx_cols={name: k * blocks for k, name in enumerate(ctx_names)})

    return _out_proj(x, gate, y_na, y_ret, w_out[i].astype(jnp.bfloat16), final_norm_g, tm=2048, sub=256)
```

```python
import functools
import math

import jax
import jax.numpy as jnp
import numpy as np
from jax import lax
from jax.experimental import pallas as pl
from jax.experimental.pallas import tpu as pltpu

D_MODEL = 1024
GRID_W = 64
NA_HEAD_DIM = 64
NA_WIDTH = 512
NA_HEADS = 8
NA_WIN_ROWS = 8
NA_WIN_COLS = 16
RET_HEADS = 4
RET_WIDTH = 512
RET_DIM = 128
RET_CHUNK = 256
ROPE_BASE = 10000.0
EPS = 1e-6

LANES = 128
MXU_TILE = 256
BF16_SUBLANES = 16
GROUP_COLS = 512
NA_Q_ROWS = 4
NA_K_ROWS = 12
NA_AHEAD = 1
RET_AHEAD = 1
MIX_BATCH = 2
MASK_VALUE = -1e30
LOG2E = math.log2(math.e)

VMEM_LIMIT = 56 * 1024 * 1024


def _silu(v):
    return v * (1.0 / (1.0 + jnp.exp(-v)))


def _split_bf16(v):
    hi = v.astype(jnp.bfloat16)
    return hi, (v - hi.astype(jnp.float32)).astype(jnp.bfloat16)


def _prep_kernel(c_ref, wa_ref, b_ref, w_ref, mod_ref, o_ref, ot_ref, *, v_group):
    m = c_ref.shape[0]
    a_hi, a_lo = _split_bf16(_silu(c_ref[...]))
    w_hi, w_lo = _split_bf16(wa_ref[...])
    both = jnp.dot(jnp.concatenate([a_hi, a_lo], axis=0), w_hi, preferred_element_type=jnp.float32)
    mod_ref[...] = (both[:m] + both[m:]) + jnp.dot(a_hi, w_lo, preferred_element_type=jnp.float32) + b_ref[...]

    w = w_ref[...]
    o_ref[...] = w.astype(o_ref.dtype)
    ot_ref[...] = w[:, v_group * GROUP_COLS:(v_group + 1) * GROUP_COLS].T.astype(ot_ref.dtype)


def _prep(cc, w_ada, b_ada, w_in, v_group):
    m, d = cc.shape
    n_mod = w_ada.shape[1]
    n = w_in.shape[1]
    steps = 8
    tn, rb = n_mod // steps, d // steps
    assert tn % LANES == 0 and rb % LANES == 0
    return pl.pallas_call(
        functools.partial(_prep_kernel, v_group=v_group),
        out_shape=(jax.ShapeDtypeStruct((m, n_mod), jnp.float32),
                   jax.ShapeDtypeStruct((d, n), jnp.bfloat16),
                   jax.ShapeDtypeStruct((GROUP_COLS, d), jnp.bfloat16)),
        grid=(steps,),
        in_specs=[pl.BlockSpec((m, d), lambda j: (0, 0)),
                  pl.BlockSpec((d, tn), lambda j: (0, j)),
                  pl.BlockSpec((1, tn), lambda j: (0, j)),
                  pl.BlockSpec((rb, n), lambda j: (j, 0))],
        out_specs=(pl.BlockSpec((m, tn), lambda j: (0, j)),
                   pl.BlockSpec((rb, n), lambda j: (j, 0)),
                   pl.BlockSpec((GROUP_COLS, rb), lambda j: (0, j))),
        compiler_params=pltpu.CompilerParams(
            dimension_semantics=("arbitrary",), vmem_limit_bytes=VMEM_LIMIT),
        name="prep",
    )(cc, w_ada, b_ada.reshape(1, n_mod), w_in)


def _rotary(acc, cosf, sinf):
    outs = []
    for h in range(GROUP_COLS // LANES):
        xh = acc[:, h * LANES:(h + 1) * LANES]
        outs.append(xh * cosf + pltpu.roll(xh, LANES // 2, 1) * sinf)
    return jnp.concatenate(outs, axis=-1)


def _proj_kernel(x_ref, scale_ref, shift_ref, g_ref, w_ref, wvt_ref, cos_ref, sin_ref, o_ref, vt_ref,
                 *, groups, sub):
    tm = x_ref.shape[1]

    def normed(j):
        x = x_ref[0, j * sub:(j + 1) * sub, :]
        ms = jnp.mean(x * x, axis=-1, keepdims=True)
        h = x * lax.rsqrt(ms + EPS) * g_ref[...]
        h = h * (1.0 + scale_ref[0]) + shift_ref[0]
        return h.astype(jnp.bfloat16)

    def project(j, hb):
        rows = slice(j * sub, (j + 1) * sub)
        vt = lax.dot_general(wvt_ref[...], hb, (((1,), (1,)), ((), ())), preferred_element_type=jnp.float32)
        vt_ref[0, :, rows] = vt.astype(vt_ref.dtype)
        for gi, (src, kind) in enumerate(groups):
            acc = jnp.dot(hb, w_ref[:, src * GROUP_COLS:(src + 1) * GROUP_COLS],
                          preferred_element_type=jnp.float32)
            if kind == "na_q":
                acc = acc * (NA_HEAD_DIM ** -0.5 * LOG2E)
            elif kind == "silu":
                acc = _silu(acc)
            elif kind == "rot":
                acc = _rotary(acc, cos_ref[rows, :], sin_ref[rows, :])
            elif kind == "rot_kscale":
                acc = _rotary(acc, cos_ref[rows, :], sin_ref[rows, :]) * (RET_DIM ** -0.5)
            elif kind == "kscale":
                acc = acc * (RET_DIM ** -0.5)
            else:
                assert kind == "plain"
            per_group = GROUP_COLS // LANES
            for k in range(per_group):
                o_ref[0, gi * per_group + k, rows, :] = acc[:, k * LANES:(k + 1) * LANES].astype(o_ref.dtype)

    pending = normed(0)
    for j in range(tm // sub):
        current = pending
        if (j + 1) * sub < tm:
            pending = normed(j + 1)
        project(j, current)


def _in_proj(x, scale, shift, norm_g, w, w_vt, cosf, sinf, groups, tm, sub):
    b, l, d = x.shape
    n_in = w.shape[1]
    n = GROUP_COLS * len(groups)
    nv = w_vt.shape[0]
    assert l % tm == 0 and tm % sub == 0
    per_batch = scale.shape[0] > 1
    mod_map = (lambda bi, i: (bi, 0, 0)) if per_batch else (lambda bi, i: (0, 0, 0))
    rotated = any(kind.startswith("rot") for _, kind in groups)
    pos_map = (lambda bi, i: (i, 0)) if rotated else (lambda bi, i: (0, 0))
    return pl.pallas_call(
        functools.partial(_proj_kernel, groups=tuple(groups), sub=sub),
        out_shape=(jax.ShapeDtypeStruct((b, n // LANES, l, LANES), jnp.bfloat16),
                   jax.ShapeDtypeStruct((b, nv, l), jnp.bfloat16)),
        grid=(b, l // tm),
        in_specs=[pl.BlockSpec((1, tm, d), lambda bi, i: (bi, i, 0)),
                  pl.BlockSpec((1, 1, d), mod_map),
                  pl.BlockSpec((1, 1, d), mod_map),
                  pl.BlockSpec((1, d), lambda bi, i: (0, 0)),
                  pl.BlockSpec((d, n_in), lambda bi, i: (0, 0)),
                  pl.BlockSpec((nv, d), lambda bi, i: (0, 0)),
                  pl.BlockSpec((tm, LANES), pos_map),
                  pl.BlockSpec((tm, LANES), pos_map)],
        out_specs=(pl.BlockSpec((1, n // LANES, tm, LANES), lambda bi, i: (bi, 0, i, 0)),
                   pl.BlockSpec((1, nv, tm), lambda bi, i: (bi, 0, i))),
        compiler_params=pltpu.CompilerParams(
            dimension_semantics=("arbitrary", "arbitrary"), vmem_limit_bytes=VMEM_LIMIT),
        name="in_proj",
    )(x, scale, shift, norm_g.reshape(1, d), w, w_vt, cosf, sinf)


def _na_window_start(rq, rows):
    kr = min(NA_WIN_ROWS, rows)
    return int(np.clip(rq - kr // 2, 0, rows - kr))


def _na_geometry(rows):
    kr = min(NA_WIN_ROWS, rows)
    n_groups = rows // NA_Q_ROWS
    assert n_groups >= 3
    geo = []
    for g in range(n_groups):
        lo = _na_window_start(NA_Q_ROWS * g, rows)
        hi = _na_window_start(NA_Q_ROWS * g + NA_Q_ROWS - 1, rows) + kr
        geo.append((lo, hi - lo, 0 if g == 0 else (2 if g == n_groups - 1 else 1)))
    assert len({n for _, n, v in geo if v == 1}) == 1 and max(n for _, n, _ in geo) <= NA_K_ROWS
    return geo


def _na_row_offsets(rows):
    kr = min(NA_WIN_ROWS, rows)
    geo = _na_geometry(rows)
    table = []
    for rep in (0, 1, len(geo) - 1):
        start, n_rows, _ = geo[rep]
        per_i = []
        for i in range(NA_Q_ROWS):
            rq = NA_Q_ROWS * rep + i
            r0 = _na_window_start(rq, rows)
            per_i.append([start + j - rq + NA_WIN_ROWS - 1 if r0 <= start + j < r0 + kr else None
                          for j in range(n_rows)])
        table.append(per_i)
    return table


def _na_live_rows(rows):
    live = []
    for per_i in _na_row_offsets(rows):
        pairs = []
        for ip in range(NA_Q_ROWS // 2):
            seen = [j for j in range(len(per_i[0]))
                    if per_i[2 * ip][j] is not None or per_i[2 * ip + 1][j] is not None]
            assert seen == list(range(seen[0], seen[-1] + 1))
            pairs.append((seen[0], seen[-1] + 1))
        live.append(pairs)
    return live


def _na_build_bias(rpb_ref, hp, bias_ref, base_ref, rows):
    n_dr = 2 * NA_WIN_ROWS - 1
    n_dc = 2 * NA_WIN_COLS - 1
    ck = lax.broadcasted_iota(jnp.int32, (GRID_W, LANES), 0)
    lane = lax.broadcasted_iota(jnp.int32, (GRID_W, LANES), 1)
    left = lane < GRID_W
    cq = jnp.where(left, lane, lane - GRID_W)
    c0 = jnp.clip(cq - NA_WIN_COLS // 2, 0, GRID_W - NA_WIN_COLS)
    valid_c = (ck >= c0) & (ck < c0 + NA_WIN_COLS)
    dc = jnp.clip(ck - cq + NA_WIN_COLS - 1, 0, n_dc - 1)

    offsets = _na_row_offsets(rows)
    users = {}
    for var in range(3):
        for j in range(len(offsets[var][0])):
            for ip in range(NA_Q_ROWS // 2):
                key = (offsets[var][2 * ip][j], offsets[var][2 * ip + 1][j])
                users.setdefault(key, []).append((var, j, ip))

    used = sorted({dr for pair in users for dr in pair if dr is not None})
    masked = jnp.full((GRID_W, LANES), MASK_VALUE, jnp.float32)
    for h in range(2):
        head = 2 * hp + h
        for dr in used:
            def body(d, acc, dr=dr, head=head):
                return jnp.where(dc == d, rpb_ref[(head * n_dr + dr) * n_dc + d], acc)
            acc = lax.fori_loop(0, n_dc, body, jnp.zeros((GRID_W, LANES), jnp.float32), unroll=True)
            base_ref[dr] = jnp.where(valid_c, acc * LOG2E, MASK_VALUE)
        for (dr_l, dr_r), dests in users.items():
            t_l = masked if dr_l is None else base_ref[dr_l]
            t_r = masked if dr_r is None else base_ref[dr_r]
            tile = t_l if dr_l == dr_r else jnp.where(left, t_l, t_r)
            for var, j, ip in dests:
                bias_ref[var, h, j * GRID_W:(j + 1) * GRID_W, ip * LANES:(ip + 1) * LANES] = tile


def _na_program(q_ref, k_ref, vt_ref, g_ref, kc_ref, vct_ref, o_ref, bias_ref, vth_ref, vcth_ref, rows):
    geo = _na_geometry(rows)
    live = _na_live_rows(rows)
    tq = NA_Q_ROWS * GRID_W
    dh = NA_HEAD_DIM

    lane = lax.broadcasted_iota(jnp.int32, (1, LANES), 1)
    head0_lanes = lane < dh
    n_ones = vth_ref.shape[1] - dh
    for h in range(2):
        vth_ref[h, :dh, :] = vt_ref[0, h * dh:(h + 1) * dh, :]
        vth_ref[h, dh:, :] = jnp.ones((n_ones, vt_ref.shape[2]), vt_ref.dtype)
        vcth_ref[h, :dh, :] = vct_ref[0, h * dh:(h + 1) * dh, :]
        vcth_ref[h, dh:, :] = jnp.ones((n_ones, vct_ref.shape[2]), vct_ref.dtype)

    kc = kc_ref[0]
    contract_last = (((1,), (1,)), ((), ()))
    tasks = [(g, h) for g in range(len(geo)) for h in range(2)]

    def scores(t):
        g, h = tasks[t]
        ws, n_rows, var = geo[g]
        nk = n_rows * GRID_W
        qg = q_ref[0, g * tq:(g + 1) * tq, :]
        kw = k_ref[0, ws * GRID_W:ws * GRID_W + nk, :]
        sel = head0_lanes if h == 0 else jnp.logical_not(head0_lanes)
        qh = jnp.where(sel, qg, jnp.zeros_like(qg))
        s_raw = lax.dot_general(kw, qh, contract_last, preferred_element_type=jnp.float32)
        s_ctx = lax.dot_general(kc, qh, contract_last, preferred_element_type=jnp.float32)
        s_loc = []
        for ip, (lo, hi) in enumerate(live[var]):
            r, c = slice(lo * GRID_W, hi * GRID_W), slice(ip * LANES, (ip + 1) * LANES)
            s_loc.append(s_raw[r, c] + bias_ref[var, h, r, c])
        return s_loc, s_ctx

    halves = []

    def attend(t, s):
        g, h = tasks[t]
        s_loc, s_ctx = s
        ws, n_rows, var = geo[g]
        n_pad = n_rows + (-n_rows) % (MXU_TILE // GRID_W)
        p_loc, p_ctx = [], []
        for ip, (lo, hi) in enumerate(live[var]):
            sc = s_ctx[:, ip * LANES:(ip + 1) * LANES]
            m = jnp.maximum(jnp.max(s_loc[ip], axis=0, keepdims=True), jnp.max(sc, axis=0, keepdims=True))
            zeros = lambda n: [jnp.zeros((n * GRID_W, LANES), jnp.bfloat16)] if n else []
            p_loc.append(jnp.concatenate(
                zeros(lo) + [jnp.exp2(s_loc[ip] - m).astype(jnp.bfloat16)] + zeros(n_pad - hi), axis=0))
            p_ctx.append(jnp.exp2(sc - m).astype(jnp.bfloat16))
        p_loc = jnp.concatenate(p_loc, axis=1)
        p_ctx = jnp.concatenate(p_ctx, axis=1)
        nk = n_pad * GRID_W
        ot = jnp.dot(vth_ref[h, :, ws * GRID_W:ws * GRID_W + nk], p_loc, preferred_element_type=jnp.float32)
        ot = ot + jnp.dot(vcth_ref[h], p_ctx, preferred_element_type=jnp.float32)
        halves.append(ot[:dh] / ot[dh:dh + 1])
        if h == 1:
            o2 = jnp.concatenate(halves, axis=0).T
            halves.clear()
            gate = g_ref[0, g * tq:(g + 1) * tq, :].astype(jnp.float32)
            o_ref[0, g * tq:(g + 1) * tq, :] = (o2 * gate).astype(o_ref.dtype)

    return len(tasks), scores, attend


def _ret_program(lg_f, lg_b, q_ref, k_ref, v_ref, g_ref, kc_ref, vc_ref, ng_ref, o_ref, t_ref, s_ref):
    c_len = RET_CHUNK
    l = q_ref.shape[1]
    lc = kc_ref.shape[1]
    n_chunks = l // c_len
    contract_last = (((1,), (1,)), ((), ()))

    ii = lax.broadcasted_iota(jnp.int32, (c_len, c_len), 0).astype(jnp.float32)
    jj = lax.broadcasted_iota(jnp.int32, (c_len, c_len), 1).astype(jnp.float32)
    dist = ii - jj
    decay = (jnp.where(dist >= 0, jnp.exp(lg_f * jnp.maximum(dist, 0.0)), 0.0)
             + jnp.where(dist <= 0, jnp.exp(lg_b * jnp.maximum(-dist, 0.0)), 0.0))
    ic = lax.broadcasted_iota(jnp.int32, (c_len, 1), 0).astype(jnp.float32)
    kdec_f = jnp.exp(lg_f * (c_len - 1 - ic))
    kdec_b = jnp.exp(lg_b * ic)
    qdec_f = jnp.exp(lg_f * (ic + 1.0))
    qdec_b = jnp.exp(lg_b * (c_len - ic))
    one = jnp.ones((1, 1), jnp.float32)
    cdec_f = jnp.exp(one * (lg_f * c_len))
    cdec_b = jnp.exp(one * (lg_b * c_len))

    jcr = lax.broadcasted_iota(jnp.int32, (1, lc), 1).astype(jnp.float32)
    kct = kc_ref[0].astype(jnp.float32).T
    vcx = vc_ref[0]
    s_f = jnp.dot((kct * jnp.exp(lg_f * (lc - 1 - jcr))).astype(jnp.bfloat16), vcx,
                  preferred_element_type=jnp.float32)
    s_b = jnp.dot((kct * jnp.exp(lg_b * jcr)).astype(jnp.bfloat16), vcx,
                  preferred_element_type=jnp.float32)

    for c in range(n_chunks):
        rows_c = slice(c * c_len, (c + 1) * c_len)
        kt = k_ref[0, rows_c, :].astype(jnp.float32).T.astype(jnp.bfloat16)
        vf = v_ref[0, rows_c, :].astype(jnp.float32)
        vw = jnp.concatenate([(vf * kdec_f).astype(jnp.bfloat16), (vf * kdec_b).astype(jnp.bfloat16)], axis=1)
        t_ref[c] = jnp.dot(kt, vw, preferred_element_type=jnp.float32)

    for c in range(n_chunks):
        s_ref[c, :, :RET_DIM] = s_f.astype(jnp.bfloat16)
        s_f = cdec_f * s_f + t_ref[c, :, :RET_DIM]
    for c in reversed(range(n_chunks)):
        s_ref[c, :, RET_DIM:] = s_b.astype(jnp.bfloat16)
        s_b = cdec_b * s_b + t_ref[c, :, RET_DIM:]

    ng = ng_ref[...]

    def scores(c):
        rows_c = slice(c * c_len, (c + 1) * c_len)
        qc = q_ref[0, rows_c, :]
        s = lax.dot_general(qc, k_ref[0, rows_c, :], contract_last, preferred_element_type=jnp.float32) * decay
        cross = jnp.dot(qc, s_ref[c], preferred_element_type=jnp.float32)
        return s.astype(jnp.bfloat16), cross[:, :RET_DIM] * qdec_f + cross[:, RET_DIM:] * qdec_b

    def finish(c, sc):
        s, cross = sc
        rows_c = slice(c * c_len, (c + 1) * c_len)
        o = jnp.dot(s, v_ref[0, rows_c, :], preferred_element_type=jnp.float32) + cross
        o = o * lax.rsqrt(jnp.mean(o * o, axis=-1, keepdims=True) + EPS) * ng
        gate = g_ref[0, rows_c, :].astype(jnp.float32)
        o_ref[0, rows_c, :] = (o * gate).astype(o_ref.dtype)

    return n_chunks, scores, finish


def _mixer_kernel(rpb_ref, lg_ref,
                  aq_ref, ak_ref, avt_ref, ag_ref, akc_ref, avct_ref,
                  rq_ref, rk_ref, rv_ref, rg_ref, rkc_ref, rvc_ref, ng_ref,
                  yna_ref, yret_ref,
                  bias_ref, base_ref, vth_ref, vcth_ref, t_ref, s_ref, *, rows):
    i = pl.program_id(0)

    @pl.when(pl.program_id(1) == 0)
    def _():
        _na_build_bias(rpb_ref, i, bias_ref, base_ref, rows)

    nb = aq_ref.shape[0]
    lc = akc_ref.shape[2] // nb
    slab = lambda ref, j: ref.at[j]
    one = lambda ref, j: ref.at[pl.ds(j, 1)]
    cslab = lambda ref, j: ref.at[0].at[:, pl.ds(j * lc, lc), :]
    cone = lambda ref, j: ref.at[:, :, pl.ds(j * lc, lc)]

    na_tasks, ret_tasks = [], []
    for j in range(nb):
        n_ret, ret_scores, ret_finish = _ret_program(
            lg_ref[0, i], lg_ref[1, i], slab(rq_ref, j), slab(rk_ref, j), slab(rv_ref, j), slab(rg_ref, j),
            cslab(rkc_ref, j), cslab(rvc_ref, j), ng_ref, slab(yret_ref, j), t_ref.at[j], s_ref.at[j])
        n_na, na_scores, na_attend = _na_program(
            slab(aq_ref, j), slab(ak_ref, j), one(avt_ref, j), slab(ag_ref, j), cslab(akc_ref, j),
            cone(avct_ref, j), slab(yna_ref, j), bias_ref, vth_ref.at[j], vcth_ref.at[j], rows)
        na_tasks += [(na_scores, na_attend, t) for t in range(n_na)]
        ret_tasks += [(ret_scores, ret_finish, c) for c in range(n_ret)]

    issue = lambda task: task[0](task[2])
    na_q = [issue(task) for task in na_tasks[:NA_AHEAD]]
    ret_q = [issue(task) for task in ret_tasks[:RET_AHEAD]]
    for t in range(max(len(na_tasks), len(ret_tasks))):
        if t + NA_AHEAD < len(na_tasks):
            na_q.append(issue(na_tasks[t + NA_AHEAD]))
        if t + RET_AHEAD < len(ret_tasks):
            ret_q.append(issue(ret_tasks[t + RET_AHEAD]))
        if t < len(na_tasks):
            na_tasks[t][1](na_tasks[t][2], na_q.pop(0))
        if t < len(ret_tasks):
            ret_tasks[t][1](ret_tasks[t][2], ret_q.pop(0))


def _mixers(rpb, lg, proj, vt, proj_ctx, vct, ret_norm_g, rows, cols, ctx_cols):
    b, _, l, _ = proj.shape
    lc = proj_ctx.shape[2] // b
    nb = MIX_BATCH
    assert NA_HEADS // 2 == RET_HEADS and b % nb == 0
    blk = lambda name: pl.BlockSpec((nb, 1, l, LANES), lambda i, bi, *_, off=cols[name]: (bi, off + i, 0, 0))
    cblk = lambda name: pl.BlockSpec((1, 1, nb * lc, LANES),
                                     lambda i, bi, *_, off=ctx_cols[name]: (0, off + i, bi, 0))
    n_chunks = l // RET_CHUNK
    grid_spec = pltpu.PrefetchScalarGridSpec(
        num_scalar_prefetch=2,
        grid=(RET_HEADS, b // nb),
        in_specs=[blk("na_q"), blk("na_k"),
                  pl.BlockSpec((nb, LANES, l), lambda i, bi, *_: (bi, i, 0)),
                  blk("na_g"), cblk("na_k"),
                  pl.BlockSpec((1, LANES, nb * lc), lambda i, bi, *_: (0, i, bi)),
                  blk("r_q"), blk("r_k"), blk("r_v"), blk("r_g"), cblk("r_k"), cblk("r_v"),
                  pl.BlockSpec((1, LANES), lambda i, bi, *_: (0, i))],
        out_specs=(pl.BlockSpec((nb, 1, l, LANES), lambda i, bi, *_: (bi, i, 0, 0)),
                   pl.BlockSpec((nb, 1, l, LANES), lambda i, bi, *_: (bi, i, 0, 0))),
        scratch_shapes=[pltpu.VMEM((3, 2, NA_K_ROWS * GRID_W, NA_Q_ROWS * GRID_W), jnp.float32),
                        pltpu.VMEM((2 * NA_WIN_ROWS - 1, GRID_W, LANES), jnp.float32),
                        pltpu.VMEM((nb, 2, NA_HEAD_DIM + BF16_SUBLANES, l), jnp.bfloat16),
                        pltpu.VMEM((nb, 2, NA_HEAD_DIM + BF16_SUBLANES, lc), jnp.bfloat16),
                        pltpu.VMEM((nb, n_chunks, RET_DIM, 2 * RET_DIM), jnp.float32),
                        pltpu.VMEM((nb, n_chunks, RET_DIM, 2 * RET_DIM), jnp.bfloat16)],
    )
    return pl.pallas_call(
        functools.partial(_mixer_kernel, rows=rows),
        out_shape=(jax.ShapeDtypeStruct((b, NA_WIDTH // LANES, l, LANES), jnp.bfloat16),
                   jax.ShapeDtypeStruct((b, RET_WIDTH // LANES, l, LANES), jnp.bfloat16)),
        grid_spec=grid_spec,
        compiler_params=pltpu.CompilerParams(
            dimension_semantics=("arbitrary", "arbitrary"), vmem_limit_bytes=VMEM_LIMIT),
        name="mixers",
    )(rpb.astype(jnp.float32).reshape(-1), lg,
      proj, proj, vt, proj, proj_ctx, vct,
      proj, proj, proj, proj, proj_ctx, proj_ctx, ret_norm_g.reshape(1, RET_WIDTH))


def _out_kernel(x_ref, gate_ref, yna_ref, yret_ref, w_ref, fg_ref, o_ref, *, sub):
    tm = x_ref.shape[1]

    def mix(j):
        rows = slice(j * sub, (j + 1) * sub)
        y = jnp.concatenate([ref[0, k, rows, :] for ref in (yna_ref, yret_ref) for k in range(ref.shape[1])],
                            axis=1)
        return jnp.dot(y, w_ref[...], preferred_element_type=jnp.float32)

    def finish(j, y):
        rows = slice(j * sub, (j + 1) * sub)
        z = x_ref[0, rows, :] + gate_ref[0] * y
        ms = jnp.mean(z * z, axis=-1, keepdims=True)
        o_ref[0, rows, :] = z * lax.rsqrt(ms + EPS) * fg_ref[...]

    pending = mix(0)
    for j in range(tm // sub):
        current = pending
        if (j + 1) * sub < tm:
            pending = mix(j + 1)
        finish(j, current)


def _out_proj(x, gate, y_na, y_ret, w_out, final_g, tm, sub):
    b, l, d = x.shape
    assert l % tm == 0 and tm % sub == 0
    return pl.pallas_call(
        functools.partial(_out_kernel, sub=sub),
        out_shape=jax.ShapeDtypeStruct((b, l, d), jnp.float32),
        grid=(b, l // tm),
        in_specs=[pl.BlockSpec((1, tm, d), lambda bi, i: (bi, i, 0)),
                  pl.BlockSpec((1, 1, d), lambda bi, i: (bi, 0, 0)),
                  pl.BlockSpec((1, NA_WIDTH // LANES, tm, LANES), lambda bi, i: (bi, 0, i, 0)),
                  pl.BlockSpec((1, RET_WIDTH // LANES, tm, LANES), lambda bi, i: (bi, 0, i, 0)),
                  pl.BlockSpec((NA_WIDTH + RET_WIDTH, d), lambda bi, i: (0, 0)),
                  pl.BlockSpec((1, d), lambda bi, i: (0, 0))],
        out_specs=pl.BlockSpec((1, tm, d), lambda bi, i: (bi, i, 0)),
        compiler_params=pltpu.CompilerParams(
            dimension_semantics=("arbitrary", "arbitrary"), vmem_limit_bytes=VMEM_LIMIT),
        name="out_proj",
    )(x, gate, y_na, y_ret, w_out, final_g.reshape(1, d))


def _rotary_tables(l):
    half = RET_DIM // 2
    nf = half // 2
    t = np.arange(l)
    row = (t // GRID_W).astype(np.float64)
    col = (t % GRID_W).astype(np.float64)
    inv = ROPE_BASE ** (-np.arange(nf, dtype=np.float64) / nf)
    ang = np.concatenate([row[:, None] * inv, col[:, None] * inv], axis=-1)
    cos, sin = np.cos(ang), np.sin(ang)
    cosf = np.concatenate([cos, cos], axis=-1).astype(np.float32)
    sinf = np.concatenate([-sin, sin], axis=-1).astype(np.float32)
    return jnp.asarray(cosf), jnp.asarray(sinf)


def kernel(x, c, ctx, c_ctx, norm_g, w_ada, b_ada, w_in, na_rpb, ret_decay_fwd, ret_decay_bwd,
           ret_norm_g, w_out, final_norm_g):
    depth = norm_g.shape[0]
    assert depth == 1, "context stream update between layers is not implemented"
    b, l, d = x.shape
    rows = l // GRID_W
    i = 0

    cc = jnp.concatenate([c, c_ctx[None, :]], axis=0)
    pad = (-cc.shape[0]) % BF16_SUBLANES
    cc = jnp.pad(cc, ((0, pad), (0, 0)))
    mod, w, w_vt = _prep(cc, w_ada[i], b_ada[i], w_in[i], v_group=2)
    shift, scale, gate = (mod[:b, None, :d], mod[:b, None, d:2 * d], mod[:b, None, 2 * d:])
    shift_c, scale_c = mod[b:b + 1, None, :d], mod[b:b + 1, None, d:2 * d]
    cosf, sinf = _rotary_tables(l)
    blocks = GROUP_COLS // LANES

    lat_groups = ((0, "na_q"), (1, "plain"), (3, "silu"), (4, "rot"), (5, "rot_kscale"), (6, "plain"), (7, "silu"))
    lat_names = ("na_q", "na_k", "na_g", "r_q", "r_k", "r_v", "r_g")
    proj, vt = _in_proj(x, scale, shift, norm_g[i], w, w_vt, cosf, sinf, lat_groups, tm=1024, sub=256)

    lc = ctx.shape[1]
    ctx_groups = ((1, "plain"), (5, "kscale"), (6, "plain"))
    ctx_names = ("na_k", "r_k", "r_v")
    proj_ctx, vct = _in_proj(ctx.reshape(1, b * lc, d), scale_c, shift_c, norm_g[i], w, w_vt, cosf, sinf,
                             ctx_groups, tm=1024, sub=256)

    lg = jnp.stack([-jnp.exp(ret_decay_fwd[i].astype(jnp.float32)),
                    -jnp.exp(ret_decay_bwd[i].astype(jnp.float32))])
    y_na, y_ret = _mixers(na_rpb[i], lg, proj, vt, proj_ctx, vct, ret_norm_g[i], rows,
                          cols={name: k * blocks for k, name in enumerate(lat_names)},
                          ctx_cols={name: k * blocks for k, name in enumerate(ctx_names)})

    return _out_proj(x, gate, y_na, y_ret, w_out[i].astype(jnp.bfloat16), final_norm_g, tm=2048, sub=256)
```

```python
import functools
import math

import jax
import jax.numpy as jnp
import numpy as np
from jax import lax
from jax.experimental import pallas as pl
from jax.experimental.pallas import tpu as pltpu

D_MODEL = 1024
GRID_W = 64
NA_HEAD_DIM = 64
NA_WIDTH = 512
NA_HEADS = 8
NA_WIN_ROWS = 8
NA_WIN_COLS = 16
RET_HEADS = 4
RET_WIDTH = 512
RET_DIM = 128
RET_CHUNK = 256
ROPE_BASE = 10000.0
EPS = 1e-6

LANES = 128
MXU_TILE = 256
BF16_SUBLANES = 16
GROUP_COLS = 512
NA_Q_ROWS = 4
NA_V_ROWS = NA_HEAD_DIM + BF16_SUBLANES
NA_K_ROWS = 12
NA_AHEAD = 1
RET_AHEAD = 1
MIX_BATCH = 2
MASK_VALUE = -1e30
LOG2E = math.log2(math.e)

VMEM_LIMIT = 56 * 1024 * 1024


def _silu(v):
    return v * (1.0 / (1.0 + jnp.exp(-v)))


def _split_bf16(v):
    hi = v.astype(jnp.bfloat16)
    return hi, (v - hi.astype(jnp.float32)).astype(jnp.bfloat16)


def _prep_kernel(c_ref, wa_ref, b_ref, w_ref, mod_ref, o_ref, ot_ref, *, v_group):
    m = c_ref.shape[0]
    a_hi, a_lo = _split_bf16(_silu(c_ref[...]))
    w_hi, w_lo = _split_bf16(wa_ref[...])
    both = jnp.dot(jnp.concatenate([a_hi, a_lo], axis=0), w_hi, preferred_element_type=jnp.float32)
    mod_ref[...] = (both[:m] + both[m:]) + jnp.dot(a_hi, w_lo, preferred_element_type=jnp.float32) + b_ref[...]

    w = w_ref[...]
    o_ref[...] = w.astype(o_ref.dtype)
    ot_ref[...] = w[:, v_group * GROUP_COLS:(v_group + 1) * GROUP_COLS].T.astype(ot_ref.dtype)


def _prep(cc, w_ada, b_ada, w_in, v_group):
    m, d = cc.shape
    n_mod = w_ada.shape[1]
    n = w_in.shape[1]
    steps = 8
    tn, rb = n_mod // steps, d // steps
    assert tn % LANES == 0 and rb % LANES == 0
    return pl.pallas_call(
        functools.partial(_prep_kernel, v_group=v_group),
        out_shape=(jax.ShapeDtypeStruct((m, n_mod), jnp.float32),
                   jax.ShapeDtypeStruct((d, n), jnp.bfloat16),
                   jax.ShapeDtypeStruct((GROUP_COLS, d), jnp.bfloat16)),
        grid=(steps,),
        in_specs=[pl.BlockSpec((m, d), lambda j: (0, 0)),
                  pl.BlockSpec((d, tn), lambda j: (0, j)),
                  pl.BlockSpec((1, tn), lambda j: (0, j)),
                  pl.BlockSpec((rb, n), lambda j: (j, 0))],
        out_specs=(pl.BlockSpec((m, tn), lambda j: (0, j)),
                   pl.BlockSpec((rb, n), lambda j: (j, 0)),
                   pl.BlockSpec((GROUP_COLS, rb), lambda j: (0, j))),
        compiler_params=pltpu.CompilerParams(
            dimension_semantics=("arbitrary",), vmem_limit_bytes=VMEM_LIMIT),
        name="prep",
    )(cc, w_ada, b_ada.reshape(1, n_mod), w_in)


def _rotary(acc, cosf, sinf):
    outs = []
    for h in range(GROUP_COLS // LANES):
        xh = acc[:, h * LANES:(h + 1) * LANES]
        outs.append(xh * cosf + pltpu.roll(xh, LANES // 2, 1) * sinf)
    return jnp.concatenate(outs, axis=-1)


def _proj_kernel(x_ref, scale_ref, shift_ref, g_ref, w_ref, wvt_ref, cos_ref, sin_ref, o_ref, vt_ref,
                 *, groups, sub):
    tm = x_ref.shape[1]

    def normed(j):
        x = x_ref[0, j * sub:(j + 1) * sub, :]
        ms = jnp.mean(x * x, axis=-1, keepdims=True)
        h = x * lax.rsqrt(ms + EPS) * g_ref[...]
        h = h * (1.0 + scale_ref[0]) + shift_ref[0]
        return h.astype(jnp.bfloat16)

    def project(j, hb):
        rows = slice(j * sub, (j + 1) * sub)
        vt = lax.dot_general(wvt_ref[...], hb, (((1,), (1,)), ((), ())), preferred_element_type=jnp.float32)
        dh = NA_HEAD_DIM
        for hd in range(vt_ref.shape[1]):
            vt_ref[0, hd, :dh, rows] = vt[hd * dh:(hd + 1) * dh].astype(vt_ref.dtype)
            vt_ref[0, hd, dh:, rows] = jnp.ones((vt_ref.shape[2] - dh, sub), vt_ref.dtype)
        for gi, (src, kind) in enumerate(groups):
            acc = jnp.dot(hb, w_ref[:, src * GROUP_COLS:(src + 1) * GROUP_COLS],
                          preferred_element_type=jnp.float32)
            if kind == "na_q":
                acc = acc * (NA_HEAD_DIM ** -0.5 * LOG2E)
            elif kind == "silu":
                acc = _silu(acc)
            elif kind == "rot":
                acc = _rotary(acc, cos_ref[rows, :], sin_ref[rows, :])
            elif kind == "rot_kscale":
                acc = _rotary(acc, cos_ref[rows, :], sin_ref[rows, :]) * (RET_DIM ** -0.5)
            elif kind == "kscale":
                acc = acc * (RET_DIM ** -0.5)
            else:
                assert kind == "plain"
            per_group = GROUP_COLS // LANES
            for k in range(per_group):
                o_ref[0, gi * per_group + k, rows, :] = acc[:, k * LANES:(k + 1) * LANES].astype(o_ref.dtype)

    pending = normed(0)
    for j in range(tm // sub):
        current = pending
        if (j + 1) * sub < tm:
            pending = normed(j + 1)
        project(j, current)


def _in_proj(x, scale, shift, norm_g, w, w_vt, cosf, sinf, groups, tm, sub):
    b, l, d = x.shape
    n_in = w.shape[1]
    n = GROUP_COLS * len(groups)
    nv = w_vt.shape[0]
    assert l % tm == 0 and tm % sub == 0
    per_batch = scale.shape[0] > 1
    mod_map = (lambda bi, i: (bi, 0, 0)) if per_batch else (lambda bi, i: (0, 0, 0))
    rotated = any(kind.startswith("rot") for _, kind in groups)
    pos_map = (lambda bi, i: (i, 0)) if rotated else (lambda bi, i: (0, 0))
    return pl.pallas_call(
        functools.partial(_proj_kernel, groups=tuple(groups), sub=sub),
        out_shape=(jax.ShapeDtypeStruct((b, n // LANES, l, LANES), jnp.bfloat16),
                   jax.ShapeDtypeStruct((b, nv // NA_HEAD_DIM, NA_V_ROWS, l), jnp.bfloat16)),
        grid=(b, l // tm),
        in_specs=[pl.BlockSpec((1, tm, d), lambda bi, i: (bi, i, 0)),
                  pl.BlockSpec((1, 1, d), mod_map),
                  pl.BlockSpec((1, 1, d), mod_map),
                  pl.BlockSpec((1, d), lambda bi, i: (0, 0)),
                  pl.BlockSpec((d, n_in), lambda bi, i: (0, 0)),
                  pl.BlockSpec((nv, d), lambda bi, i: (0, 0)),
                  pl.BlockSpec((tm, LANES), pos_map),
                  pl.BlockSpec((tm, LANES), pos_map)],
        out_specs=(pl.BlockSpec((1, n // LANES, tm, LANES), lambda bi, i: (bi, 0, i, 0)),
                   pl.BlockSpec((1, nv // NA_HEAD_DIM, NA_V_ROWS, tm), lambda bi, i: (bi, 0, 0, i))),
        compiler_params=pltpu.CompilerParams(
            dimension_semantics=("arbitrary", "arbitrary"), vmem_limit_bytes=VMEM_LIMIT),
        name="in_proj",
    )(x, scale, shift, norm_g.reshape(1, d), w, w_vt, cosf, sinf)


def _na_window_start(rq, rows):
    kr = min(NA_WIN_ROWS, rows)
    return int(np.clip(rq - kr // 2, 0, rows - kr))


def _na_geometry(rows):
    kr = min(NA_WIN_ROWS, rows)
    n_groups = rows // NA_Q_ROWS
    assert n_groups >= 3
    geo = []
    for g in range(n_groups):
        lo = _na_window_start(NA_Q_ROWS * g, rows)
        hi = _na_window_start(NA_Q_ROWS * g + NA_Q_ROWS - 1, rows) + kr
        geo.append((lo, hi - lo, 0 if g == 0 else (2 if g == n_groups - 1 else 1)))
    assert len({n for _, n, v in geo if v == 1}) == 1 and max(n for _, n, _ in geo) <= NA_K_ROWS
    return geo


def _na_row_offsets(rows):
    kr = min(NA_WIN_ROWS, rows)
    geo = _na_geometry(rows)
    table = []
    for rep in (0, 1, len(geo) - 1):
        start, n_rows, _ = geo[rep]
        per_i = []
        for i in range(NA_Q_ROWS):
            rq = NA_Q_ROWS * rep + i
            r0 = _na_window_start(rq, rows)
            per_i.append([start + j - rq + NA_WIN_ROWS - 1 if r0 <= start + j < r0 + kr else None
                          for j in range(n_rows)])
        table.append(per_i)
    return table


def _na_live_rows(rows):
    live = []
    for per_i in _na_row_offsets(rows):
        pairs = []
        for ip in range(NA_Q_ROWS // 2):
            seen = [j for j in range(len(per_i[0]))
                    if per_i[2 * ip][j] is not None or per_i[2 * ip + 1][j] is not None]
            assert seen == list(range(seen[0], seen[-1] + 1))
            pairs.append((seen[0], seen[-1] + 1))
        live.append(pairs)
    return live


def _na_build_bias(rpb_ref, hp, bias_ref, base_ref, rows):
    n_dr = 2 * NA_WIN_ROWS - 1
    n_dc = 2 * NA_WIN_COLS - 1
    ck = lax.broadcasted_iota(jnp.int32, (GRID_W, LANES), 0)
    lane = lax.broadcasted_iota(jnp.int32, (GRID_W, LANES), 1)
    left = lane < GRID_W
    cq = jnp.where(left, lane, lane - GRID_W)
    c0 = jnp.clip(cq - NA_WIN_COLS // 2, 0, GRID_W - NA_WIN_COLS)
    valid_c = (ck >= c0) & (ck < c0 + NA_WIN_COLS)
    dc = jnp.clip(ck - cq + NA_WIN_COLS - 1, 0, n_dc - 1)

    offsets = _na_row_offsets(rows)
    users = {}
    for var in range(3):
        for j in range(len(offsets[var][0])):
            for ip in range(NA_Q_ROWS // 2):
                key = (offsets[var][2 * ip][j], offsets[var][2 * ip + 1][j])
                users.setdefault(key, []).append((var, j, ip))

    used = sorted({dr for pair in users for dr in pair if dr is not None})
    masked = jnp.full((GRID_W, LANES), MASK_VALUE, jnp.float32)
    for h in range(2):
        head = 2 * hp + h
        for dr in used:
            def body(d, acc, dr=dr, head=head):
                return jnp.where(dc == d, rpb_ref[(head * n_dr + dr) * n_dc + d], acc)
            acc = lax.fori_loop(0, n_dc, body, jnp.zeros((GRID_W, LANES), jnp.float32), unroll=True)
            base_ref[dr] = jnp.where(valid_c, acc * LOG2E, MASK_VALUE)
        for (dr_l, dr_r), dests in users.items():
            t_l = masked if dr_l is None else base_ref[dr_l]
            t_r = masked if dr_r is None else base_ref[dr_r]
            tile = t_l if dr_l == dr_r else jnp.where(left, t_l, t_r)
            for var, j, ip in dests:
                bias_ref[var, h, j * GRID_W:(j + 1) * GRID_W, ip * LANES:(ip + 1) * LANES] = tile


def _na_program(q_ref, k_ref, vth_ref, g_ref, kc_ref, vcth_ref, o_ref, bias_ref, rows):
    geo = _na_geometry(rows)
    live = _na_live_rows(rows)
    tq = NA_Q_ROWS * GRID_W
    dh = NA_HEAD_DIM

    lane = lax.broadcasted_iota(jnp.int32, (1, LANES), 1)
    head0_lanes = lane < dh
    kc = kc_ref[0]
    contract_last = (((1,), (1,)), ((), ()))
    tasks = [(g, h) for g in range(len(geo)) for h in range(2)]

    def scores(t):
        g, h = tasks[t]
        ws, n_rows, var = geo[g]
        nk = n_rows * GRID_W
        qg = q_ref[0, g * tq:(g + 1) * tq, :]
        kw = k_ref[0, ws * GRID_W:ws * GRID_W + nk, :]
        sel = head0_lanes if h == 0 else jnp.logical_not(head0_lanes)
        qh = jnp.where(sel, qg, jnp.zeros_like(qg))
        s_raw = lax.dot_general(kw, qh, contract_last, preferred_element_type=jnp.float32)
        s_ctx = lax.dot_general(kc, qh, contract_last, preferred_element_type=jnp.float32)
        s_loc = []
        for ip, (lo, hi) in enumerate(live[var]):
            r, c = slice(lo * GRID_W, hi * GRID_W), slice(ip * LANES, (ip + 1) * LANES)
            s_loc.append(s_raw[r, c] + bias_ref[var, h, r, c])
        return s_loc, s_ctx

    halves = []

    def attend(t, s):
        g, h = tasks[t]
        s_loc, s_ctx = s
        ws, n_rows, var = geo[g]
        n_pad = n_rows + (-n_rows) % (MXU_TILE // GRID_W)
        p_loc, p_ctx = [], []
        for ip, (lo, hi) in enumerate(live[var]):
            sc = s_ctx[:, ip * LANES:(ip + 1) * LANES]
            m = jnp.maximum(jnp.max(s_loc[ip], axis=0, keepdims=True), jnp.max(sc, axis=0, keepdims=True))
            zeros = lambda n: [jnp.zeros((n * GRID_W, LANES), jnp.bfloat16)] if n else []
            p_loc.append(jnp.concatenate(
                zeros(lo) + [jnp.exp2(s_loc[ip] - m).astype(jnp.bfloat16)] + zeros(n_pad - hi), axis=0))
            p_ctx.append(jnp.exp2(sc - m).astype(jnp.bfloat16))
        p_loc = jnp.concatenate(p_loc, axis=1)
        p_ctx = jnp.concatenate(p_ctx, axis=1)
        nk = n_pad * GRID_W
        ot = jnp.dot(vth_ref[h, :, ws * GRID_W:ws * GRID_W + nk], p_loc, preferred_element_type=jnp.float32)
        ot = ot + jnp.dot(vcth_ref[h], p_ctx, preferred_element_type=jnp.float32)
        halves.append(ot[:dh] * (1.0 / ot[dh:dh + 1]))
        if h == 1:
            o2 = jnp.concatenate(halves, axis=0).T
            halves.clear()
            gate = g_ref[0, g * tq:(g + 1) * tq, :].astype(jnp.float32)
            o_ref[0, g * tq:(g + 1) * tq, :] = (o2 * gate).astype(o_ref.dtype)

    return len(tasks), scores, attend


def _ret_program(lg_f, lg_b, q_ref, k_ref, v_ref, g_ref, kc_ref, vc_ref, ng_ref, o_ref, t_ref, s_ref):
    c_len = RET_CHUNK
    l = q_ref.shape[1]
    lc = kc_ref.shape[1]
    n_chunks = l // c_len
    contract_last = (((1,), (1,)), ((), ()))

    ii = lax.broadcasted_iota(jnp.int32, (c_len, c_len), 0).astype(jnp.float32)
    jj = lax.broadcasted_iota(jnp.int32, (c_len, c_len), 1).astype(jnp.float32)
    dist = ii - jj
    decay = (jnp.where(dist >= 0, jnp.exp(lg_f * jnp.maximum(dist, 0.0)), 0.0)
             + jnp.where(dist <= 0, jnp.exp(lg_b * jnp.maximum(-dist, 0.0)), 0.0))
    ic = lax.broadcasted_iota(jnp.int32, (c_len, 1), 0).astype(jnp.float32)
    kdec_f = jnp.exp(lg_f * (c_len - 1 - ic))
    kdec_b = jnp.exp(lg_b * ic)
    qdec_f = jnp.exp(lg_f * (ic + 1.0))
    qdec_b = jnp.exp(lg_b * (c_len - ic))
    one = jnp.ones((1, 1), jnp.float32)
    cdec_f = jnp.exp(one * (lg_f * c_len))
    cdec_b = jnp.exp(one * (lg_b * c_len))

    jcr = lax.broadcasted_iota(jnp.int32, (1, lc), 1).astype(jnp.float32)
    kct = kc_ref[0].astype(jnp.float32).T
    vcx = vc_ref[0]
    s_f = jnp.dot((kct * jnp.exp(lg_f * (lc - 1 - jcr))).astype(jnp.bfloat16), vcx,
                  preferred_element_type=jnp.float32)
    s_b = jnp.dot((kct * jnp.exp(lg_b * jcr)).astype(jnp.bfloat16), vcx,
                  preferred_element_type=jnp.float32)

    for c in range(n_chunks):
        rows_c = slice(c * c_len, (c + 1) * c_len)
        kt = k_ref[0, rows_c, :].astype(jnp.float32).T.astype(jnp.bfloat16)
        vf = v_ref[0, rows_c, :].astype(jnp.float32)
        vw = jnp.concatenate([(vf * kdec_f).astype(jnp.bfloat16), (vf * kdec_b).astype(jnp.bfloat16)], axis=1)
        t_ref[c] = jnp.dot(kt, vw, preferred_element_type=jnp.float32)

    for c in range(n_chunks):
        s_ref[c, :, :RET_DIM] = s_f.astype(jnp.bfloat16)
        s_f = cdec_f * s_f + t_ref[c, :, :RET_DIM]
    for c in reversed(range(n_chunks)):
        s_ref[c, :, RET_DIM:] = s_b.astype(jnp.bfloat16)
        s_b = cdec_b * s_b + t_ref[c, :, RET_DIM:]

    ng = ng_ref[...]

    def scores(c):
        rows_c = slice(c * c_len, (c + 1) * c_len)
        qc = q_ref[0, rows_c, :]
        s = lax.dot_general(qc, k_ref[0, rows_c, :], contract_last, preferred_element_type=jnp.float32) * decay
        cross = jnp.dot(qc, s_ref[c], preferred_element_type=jnp.float32)
        return s.astype(jnp.bfloat16), cross[:, :RET_DIM] * qdec_f + cross[:, RET_DIM:] * qdec_b

    def finish(c, sc):
        s, cross = sc
        rows_c = slice(c * c_len, (c + 1) * c_len)
        o = jnp.dot(s, v_ref[0, rows_c, :], preferred_element_type=jnp.float32) + cross
        o = o * lax.rsqrt(jnp.mean(o * o, axis=-1, keepdims=True) + EPS) * ng
        gate = g_ref[0, rows_c, :].astype(jnp.float32)
        o_ref[0, rows_c, :] = (o * gate).astype(o_ref.dtype)

    return n_chunks, scores, finish


def _mixer_kernel(rpb_ref, lg_ref,
                  aq_ref, ak_ref, avt_ref, ag_ref, akc_ref, avct_ref,
                  rq_ref, rk_ref, rv_ref, rg_ref, rkc_ref, rvc_ref, ng_ref,
                  yna_ref, yret_ref,
                  bias_ref, base_ref, t_ref, s_ref, *, rows):
    i = pl.program_id(0)

    @pl.when(pl.program_id(1) == 0)
    def _():
        _na_build_bias(rpb_ref, i, bias_ref, base_ref, rows)

    nb = aq_ref.shape[0]
    lc = akc_ref.shape[2] // nb
    slab = lambda ref, j: ref.at[j]
    cslab = lambda ref, j: ref.at[0].at[:, pl.ds(j * lc, lc), :]
    cvals = lambda ref, j: ref.at[0].at[:, :, pl.ds(j * lc, lc)]

    na_tasks, ret_tasks = [], []
    for j in range(nb):
        n_ret, ret_scores, ret_finish = _ret_program(
            lg_ref[0, i], lg_ref[1, i], slab(rq_ref, j), slab(rk_ref, j), slab(rv_ref, j), slab(rg_ref, j),
            cslab(rkc_ref, j), cslab(rvc_ref, j), ng_ref, slab(yret_ref, j), t_ref.at[j], s_ref.at[j])
        n_na, na_scores, na_attend = _na_program(
            slab(aq_ref, j), slab(ak_ref, j), avt_ref.at[j], slab(ag_ref, j), cslab(akc_ref, j),
            cvals(avct_ref, j), slab(yna_ref, j), bias_ref, rows)
        na_tasks += [(na_scores, na_attend, t) for t in range(n_na)]
        ret_tasks += [(ret_scores, ret_finish, c) for c in range(n_ret)]

    issue = lambda task: task[0](task[2])
    na_q = [issue(task) for task in na_tasks[:NA_AHEAD]]
    ret_q = [issue(task) for task in ret_tasks[:RET_AHEAD]]
    for t in range(max(len(na_tasks), len(ret_tasks))):
        if t + NA_AHEAD < len(na_tasks):
            na_q.append(issue(na_tasks[t + NA_AHEAD]))
        if t + RET_AHEAD < len(ret_tasks):
            ret_q.append(issue(ret_tasks[t + RET_AHEAD]))
        if t < len(na_tasks):
            na_tasks[t][1](na_tasks[t][2], na_q.pop(0))
        if t < len(ret_tasks):
            ret_tasks[t][1](ret_tasks[t][2], ret_q.pop(0))


def _mixers(rpb, lg, proj, vt, proj_ctx, vct, ret_norm_g, rows, cols, ctx_cols):
    b, _, l, _ = proj.shape
    lc = proj_ctx.shape[2] // b
    nb = MIX_BATCH
    assert NA_HEADS // 2 == RET_HEADS and b % nb == 0
    blk = lambda name: pl.BlockSpec((nb, 1, l, LANES), lambda i, bi, *_, off=cols[name]: (bi, off + i, 0, 0))
    cblk = lambda name: pl.BlockSpec((1, 1, nb * lc, LANES),
                                     lambda i, bi, *_, off=ctx_cols[name]: (0, off + i, bi, 0))
    n_chunks = l // RET_CHUNK
    grid_spec = pltpu.PrefetchScalarGridSpec(
        num_scalar_prefetch=2,
        grid=(RET_HEADS, b // nb),
        in_specs=[blk("na_q"), blk("na_k"),
                  pl.BlockSpec((nb, 2, NA_V_ROWS, l), lambda i, bi, *_: (bi, i, 0, 0)),
                  blk("na_g"), cblk("na_k"),
                  pl.BlockSpec((1, 2, NA_V_ROWS, nb * lc), lambda i, bi, *_: (0, i, 0, bi)),
                  blk("r_q"), blk("r_k"), blk("r_v"), blk("r_g"), cblk("r_k"), cblk("r_v"),
                  pl.BlockSpec((1, LANES), lambda i, bi, *_: (0, i))],
        out_specs=(pl.BlockSpec((nb, 1, l, LANES), lambda i, bi, *_: (bi, i, 0, 0)),
                   pl.BlockSpec((nb, 1, l, LANES), lambda i, bi, *_: (bi, i, 0, 0))),
        scratch_shapes=[pltpu.VMEM((3, 2, NA_K_ROWS * GRID_W, NA_Q_ROWS * GRID_W), jnp.float32),
                        pltpu.VMEM((2 * NA_WIN_ROWS - 1, GRID_W, LANES), jnp.float32),
                        pltpu.VMEM((nb, n_chunks, RET_DIM, 2 * RET_DIM), jnp.float32),
                        pltpu.VMEM((nb, n_chunks, RET_DIM, 2 * RET_DIM), jnp.bfloat16)],
    )
    return pl.pallas_call(
        functools.partial(_mixer_kernel, rows=rows),
        out_shape=(jax.ShapeDtypeStruct((b, NA_WIDTH // LANES, l, LANES), jnp.bfloat16),
                   jax.ShapeDtypeStruct((b, RET_WIDTH // LANES, l, LANES), jnp.bfloat16)),
        grid_spec=grid_spec,
        compiler_params=pltpu.CompilerParams(
            dimension_semantics=("arbitrary", "arbitrary"), vmem_limit_bytes=VMEM_LIMIT),
        name="mixers",
    )(rpb.astype(jnp.float32).reshape(-1), lg,
      proj, proj, vt, proj, proj_ctx, vct,
      proj, proj, proj, proj, proj_ctx, proj_ctx, ret_norm_g.reshape(1, RET_WIDTH))


def _out_kernel(x_ref, gate_ref, yna_ref, yret_ref, w_ref, fg_ref, o_ref, *, sub):
    tm = x_ref.shape[1]

    def mix(j):
        rows = slice(j * sub, (j + 1) * sub)
        y = jnp.concatenate([ref[0, k, rows, :] for ref in (yna_ref, yret_ref) for k in range(ref.shape[1])],
                            axis=1)
        return jnp.dot(y, w_ref[...], preferred_element_type=jnp.float32)

    def finish(j, y):
        rows = slice(j * sub, (j + 1) * sub)
        z = x_ref[0, rows, :] + gate_ref[0] * y
        ms = jnp.mean(z * z, axis=-1, keepdims=True)
        o_ref[0, rows, :] = z * lax.rsqrt(ms + EPS) * fg_ref[...]

    pending = mix(0)
    for j in range(tm // sub):
        current = pending
        if (j + 1) * sub < tm:
            pending = mix(j + 1)
        finish(j, current)


def _out_proj(x, gate, y_na, y_ret, w_out, final_g, tm, sub):
    b, l, d = x.shape
    assert l % tm == 0 and tm % sub == 0
    return pl.pallas_call(
        functools.partial(_out_kernel, sub=sub),
        out_shape=jax.ShapeDtypeStruct((b, l, d), jnp.float32),
        grid=(b, l // tm),
        in_specs=[pl.BlockSpec((1, tm, d), lambda bi, i: (bi, i, 0)),
                  pl.BlockSpec((1, 1, d), lambda bi, i: (bi, 0, 0)),
                  pl.BlockSpec((1, NA_WIDTH // LANES, tm, LANES), lambda bi, i: (bi, 0, i, 0)),
                  pl.BlockSpec((1, RET_WIDTH // LANES, tm, LANES), lambda bi, i: (bi, 0, i, 0)),
                  pl.BlockSpec((NA_WIDTH + RET_WIDTH, d), lambda bi, i: (0, 0)),
                  pl.BlockSpec((1, d), lambda bi, i: (0, 0))],
        out_specs=pl.BlockSpec((1, tm, d), lambda bi, i: (bi, i, 0)),
        compiler_params=pltpu.CompilerParams(
            dimension_semantics=("arbitrary", "arbitrary"), vmem_limit_bytes=VMEM_LIMIT),
        name="out_proj",
    )(x, gate, y_na, y_ret, w_out, final_g.reshape(1, d))


def _rotary_tables(l):
    half = RET_DIM // 2
    nf = half // 2
    t = np.arange(l)
    row = (t // GRID_W).astype(np.float64)
    col = (t % GRID_W).astype(np.float64)
    inv = ROPE_BASE ** (-np.arange(nf, dtype=np.float64) / nf)
    ang = np.concatenate([row[:, None] * inv, col[:, None] * inv], axis=-1)
    cos, sin = np.cos(ang), np.sin(ang)
    cosf = np.concatenate([cos, cos], axis=-1).astype(np.float32)
    sinf = np.concatenate([-sin, sin], axis=-1).astype(np.float32)
    return jnp.asarray(cosf), jnp.asarray(sinf)


def kernel(x, c, ctx, c_ctx, norm_g, w_ada, b_ada, w_in, na_rpb, ret_decay_fwd, ret_decay_bwd,
           ret_norm_g, w_out, final_norm_g):
    depth = norm_g.shape[0]
    assert depth == 1, "context stream update between layers is not implemented"
    b, l, d = x.shape
    rows = l // GRID_W
    i = 0

    cc = jnp.concatenate([c, c_ctx[None, :]], axis=0)
    pad = (-cc.shape[0]) % BF16_SUBLANES
    cc = jnp.pad(cc, ((0, pad), (0, 0)))
    mod, w, w_vt = _prep(cc, w_ada[i], b_ada[i], w_in[i], v_group=2)
    shift, scale, gate = (mod[:b, None, :d], mod[:b, None, d:2 * d], mod[:b, None, 2 * d:])
    shift_c, scale_c = mod[b:b + 1, None, :d], mod[b:b + 1, None, d:2 * d]
    cosf, sinf = _rotary_tables(l)
    blocks = GROUP_COLS // LANES

    lat_groups = ((0, "na_q"), (1, "plain"), (3, "silu"), (4, "rot"), (5, "rot_kscale"), (6, "plain"), (7, "silu"))
    lat_names = ("na_q", "na_k", "na_g", "r_q", "r_k", "r_v", "r_g")
    proj, vt = _in_proj(x, scale, shift, norm_g[i], w, w_vt, cosf, sinf, lat_groups, tm=1024, sub=256)

    lc = ctx.shape[1]
    ctx_groups = ((1, "plain"), (5, "kscale"), (6, "plain"))
    ctx_names = ("na_k", "r_k", "r_v")
    proj_ctx, vct = _in_proj(ctx.reshape(1, b * lc, d), scale_c, shift_c, norm_g[i], w, w_vt, cosf, sinf,
                             ctx_groups, tm=1024, sub=256)

    lg = jnp.stack([-jnp.exp(ret_decay_fwd[i].astype(jnp.float32)),
                    -jnp.exp(ret_decay_bwd[i].astype(jnp.float32))])
    y_na, y_ret = _mixers(na_rpb[i], lg, proj, vt, proj_ctx, vct, ret_norm_g[i], rows,
                          cols={name: k * blocks for k, name in enumerate(lat_names)},
                          ctx_cols={name: k * blocks for k, name in enumerate(ctx_names)})

    return _out_proj(x, gate, y_na, y_ret, w_out[i].astype(jnp.bfloat16), final_norm_g, tm=2048, sub=256)
```

```python
import functools
import math

import jax
import jax.numpy as jnp
import numpy as np
from jax import lax
from jax.experimental import pallas as pl
from jax.experimental.pallas import tpu as pltpu

D_MODEL = 1024
GRID_W = 64
NA_HEAD_DIM = 64
NA_WIDTH = 512
NA_HEADS = 8
NA_WIN_ROWS = 8
NA_WIN_COLS = 16
RET_HEADS = 4
RET_WIDTH = 512
RET_DIM = 128
RET_CHUNK = 256
ROPE_BASE = 10000.0
EPS = 1e-6

LANES = 128
MXU_TILE = 256
BF16_SUBLANES = 16
GROUP_COLS = 512
NA_Q_ROWS = 4
NA_K_ROWS = 12
NA_AHEAD = 1
RET_AHEAD = 1
MIX_BATCH = 2
MASK_VALUE = -1e30
LOG2E = math.log2(math.e)

VMEM_LIMIT = 56 * 1024 * 1024


def _silu(v):
    return v * (1.0 / (1.0 + jnp.exp(-v)))


def _split_bf16(v):
    hi = v.astype(jnp.bfloat16)
    return hi, (v - hi.astype(jnp.float32)).astype(jnp.bfloat16)


def _prep_kernel(c_ref, wa_ref, b_ref, w_ref, mod_ref, o_ref, ot_ref, *, v_group):
    m = c_ref.shape[0]
    a_hi, a_lo = _split_bf16(_silu(c_ref[...]))
    w_hi, w_lo = _split_bf16(wa_ref[...])
    both = jnp.dot(jnp.concatenate([a_hi, a_lo], axis=0), w_hi, preferred_element_type=jnp.float32)
    mod_ref[...] = (both[:m] + both[m:]) + jnp.dot(a_hi, w_lo, preferred_element_type=jnp.float32) + b_ref[...]

    w = w_ref[...]
    o_ref[...] = w.astype(o_ref.dtype)
    ot_ref[...] = w[:, v_group * GROUP_COLS:(v_group + 1) * GROUP_COLS].T.astype(ot_ref.dtype)


def _prep(cc, w_ada, b_ada, w_in, v_group):
    m, d = cc.shape
    n_mod = w_ada.shape[1]
    n = w_in.shape[1]
    steps = 8
    tn, rb = n_mod // steps, d // steps
    assert tn % LANES == 0 and rb % LANES == 0
    return pl.pallas_call(
        functools.partial(_prep_kernel, v_group=v_group),
        out_shape=(jax.ShapeDtypeStruct((m, n_mod), jnp.float32),
                   jax.ShapeDtypeStruct((d, n), jnp.bfloat16),
                   jax.ShapeDtypeStruct((GROUP_COLS, d), jnp.bfloat16)),
        grid=(steps,),
        in_specs=[pl.BlockSpec((m, d), lambda j: (0, 0)),
                  pl.BlockSpec((d, tn), lambda j: (0, j)),
                  pl.BlockSpec((1, tn), lambda j: (0, j)),
                  pl.BlockSpec((rb, n), lambda j: (j, 0))],
        out_specs=(pl.BlockSpec((m, tn), lambda j: (0, j)),
                   pl.BlockSpec((rb, n), lambda j: (j, 0)),
                   pl.BlockSpec((GROUP_COLS, rb), lambda j: (0, j))),
        compiler_params=pltpu.CompilerParams(
            dimension_semantics=("arbitrary",), vmem_limit_bytes=VMEM_LIMIT),
        name="prep",
    )(cc, w_ada, b_ada.reshape(1, n_mod), w_in)


def _rotary(acc, cosf, sinf):
    outs = []
    for h in range(GROUP_COLS // LANES):
        xh = acc[:, h * LANES:(h + 1) * LANES]
        outs.append(xh * cosf + pltpu.roll(xh, LANES // 2, 1) * sinf)
    return jnp.concatenate(outs, axis=-1)


def _proj_kernel(x_ref, scale_ref, shift_ref, g_ref, w_ref, wvt_ref, cos_ref, sin_ref, o_ref, vt_ref,
                 *, groups, sub):
    tm = x_ref.shape[1]

    def normed(j):
        x = x_ref[0, j * sub:(j + 1) * sub, :]
        ms = jnp.mean(x * x, axis=-1, keepdims=True)
        h = x * lax.rsqrt(ms + EPS) * g_ref[...]
        h = h * (1.0 + scale_ref[0]) + shift_ref[0]
        return h.astype(jnp.bfloat16)

    def project(j, hb):
        rows = slice(j * sub, (j + 1) * sub)
        vt = lax.dot_general(wvt_ref[...], hb, (((1,), (1,)), ((), ())), preferred_element_type=jnp.float32)
        vt_ref[0, :, rows] = vt.astype(vt_ref.dtype)
        for gi, (src, kind) in enumerate(groups):
            acc = jnp.dot(hb, w_ref[:, src * GROUP_COLS:(src + 1) * GROUP_COLS],
                          preferred_element_type=jnp.float32)
            if kind == "na_q":
                acc = acc * (NA_HEAD_DIM ** -0.5 * LOG2E)
            elif kind == "silu":
                acc = _silu(acc)
            elif kind == "rot":
                acc = _rotary(acc, cos_ref[rows, :], sin_ref[rows, :])
            elif kind == "rot_kscale":
                acc = _rotary(acc, cos_ref[rows, :], sin_ref[rows, :]) * (RET_DIM ** -0.5)
            elif kind == "kscale":
                acc = acc * (RET_DIM ** -0.5)
            else:
                assert kind == "plain"
            per_group = GROUP_COLS // LANES
            for k in range(per_group):
                o_ref[0, gi * per_group + k, rows, :] = acc[:, k * LANES:(k + 1) * LANES].astype(o_ref.dtype)

    pending = normed(0)
    for j in range(tm // sub):
        current = pending
        if (j + 1) * sub < tm:
            pending = normed(j + 1)
        project(j, current)


def _in_proj(x, scale, shift, norm_g, w, w_vt, cosf, sinf, groups, tm, sub):
    b, l, d = x.shape
    n_in = w.shape[1]
    n = GROUP_COLS * len(groups)
    nv = w_vt.shape[0]
    assert l % tm == 0 and tm % sub == 0
    per_batch = scale.shape[0] > 1
    mod_map = (lambda bi, i: (bi, 0, 0)) if per_batch else (lambda bi, i: (0, 0, 0))
    rotated = any(kind.startswith("rot") for _, kind in groups)
    pos_map = (lambda bi, i: (i, 0)) if rotated else (lambda bi, i: (0, 0))
    return pl.pallas_call(
        functools.partial(_proj_kernel, groups=tuple(groups), sub=sub),
        out_shape=(jax.ShapeDtypeStruct((b, n // LANES, l, LANES), jnp.bfloat16),
                   jax.ShapeDtypeStruct((b, nv, l), jnp.bfloat16)),
        grid=(b, l // tm),
        in_specs=[pl.BlockSpec((1, tm, d), lambda bi, i: (bi, i, 0)),
                  pl.BlockSpec((1, 1, d), mod_map),
                  pl.BlockSpec((1, 1, d), mod_map),
                  pl.BlockSpec((1, d), lambda bi, i: (0, 0)),
                  pl.BlockSpec((d, n_in), lambda bi, i: (0, 0)),
                  pl.BlockSpec((nv, d), lambda bi, i: (0, 0)),
                  pl.BlockSpec((tm, LANES), pos_map),
                  pl.BlockSpec((tm, LANES), pos_map)],
        out_specs=(pl.BlockSpec((1, n // LANES, tm, LANES), lambda bi, i: (bi, 0, i, 0)),
                   pl.BlockSpec((1, nv, tm), lambda bi, i: (bi, 0, i))),
        compiler_params=pltpu.CompilerParams(
            dimension_semantics=("arbitrary", "arbitrary"), vmem_limit_bytes=VMEM_LIMIT),
        name="in_proj",
    )(x, scale, shift, norm_g.reshape(1, d), w, w_vt, cosf, sinf)


def _na_window_start(rq, rows):
    kr = min(NA_WIN_ROWS, rows)
    return int(np.clip(rq - kr // 2, 0, rows - kr))


def _na_geometry(rows):
    kr = min(NA_WIN_ROWS, rows)
    n_groups = rows // NA_Q_ROWS
    assert n_groups >= 3
    geo = []
    for g in range(n_groups):
        lo = _na_window_start(NA_Q_ROWS * g, rows)
        hi = _na_window_start(NA_Q_ROWS * g + NA_Q_ROWS - 1, rows) + kr
        geo.append((lo, hi - lo, 0 if g == 0 else (2 if g == n_groups - 1 else 1)))
    assert len({n for _, n, v in geo if v == 1}) == 1 and max(n for _, n, _ in geo) <= NA_K_ROWS
    return geo


def _na_row_offsets(rows):
    kr = min(NA_WIN_ROWS, rows)
    geo = _na_geometry(rows)
    table = []
    for rep in (0, 1, len(geo) - 1):
        start, n_rows, _ = geo[rep]
        per_i = []
        for i in range(NA_Q_ROWS):
            rq = NA_Q_ROWS * rep + i
            r0 = _na_window_start(rq, rows)
            per_i.append([start + j - rq + NA_WIN_ROWS - 1 if r0 <= start + j < r0 + kr else None
                          for j in range(n_rows)])
        table.append(per_i)
    return table


def _na_live_rows(rows):
    live = []
    for per_i in _na_row_offsets(rows):
        pairs = []
        for ip in range(NA_Q_ROWS // 2):
            seen = [j for j in range(len(per_i[0]))
                    if per_i[2 * ip][j] is not None or per_i[2 * ip + 1][j] is not None]
            assert seen == list(range(seen[0], seen[-1] + 1))
            pairs.append((seen[0], seen[-1] + 1))
        live.append(pairs)
    return live


def _na_build_bias(rpb_ref, hp, bias_ref, base_ref, rows):
    n_dr = 2 * NA_WIN_ROWS - 1
    n_dc = 2 * NA_WIN_COLS - 1
    ck = lax.broadcasted_iota(jnp.int32, (GRID_W, LANES), 0)
    lane = lax.broadcasted_iota(jnp.int32, (GRID_W, LANES), 1)
    left = lane < GRID_W
    cq = jnp.where(left, lane, lane - GRID_W)
    c0 = jnp.clip(cq - NA_WIN_COLS // 2, 0, GRID_W - NA_WIN_COLS)
    valid_c = (ck >= c0) & (ck < c0 + NA_WIN_COLS)
    dc = jnp.clip(ck - cq + NA_WIN_COLS - 1, 0, n_dc - 1)

    offsets = _na_row_offsets(rows)
    users = {}
    for var in range(3):
        for j in range(len(offsets[var][0])):
            for ip in range(NA_Q_ROWS // 2):
                key = (offsets[var][2 * ip][j], offsets[var][2 * ip + 1][j])
                users.setdefault(key, []).append((var, j, ip))

    used = sorted({dr for pair in users for dr in pair if dr is not None})
    masked = jnp.full((GRID_W, LANES), MASK_VALUE, jnp.float32)
    for h in range(2):
        head = 2 * hp + h
        for dr in used:
            def body(d, acc, dr=dr, head=head):
                return jnp.where(dc == d, rpb_ref[(head * n_dr + dr) * n_dc + d], acc)
            acc = lax.fori_loop(0, n_dc, body, jnp.zeros((GRID_W, LANES), jnp.float32), unroll=True)
            base_ref[dr] = jnp.where(valid_c, acc * LOG2E, MASK_VALUE)
        for (dr_l, dr_r), dests in users.items():
            t_l = masked if dr_l is None else base_ref[dr_l]
            t_r = masked if dr_r is None else base_ref[dr_r]
            tile = t_l if dr_l == dr_r else jnp.where(left, t_l, t_r)
            for var, j, ip in dests:
                bias_ref[var, h, j * GRID_W:(j + 1) * GRID_W, ip * LANES:(ip + 1) * LANES] = tile


def _na_program(q_ref, k_ref, vt_ref, g_ref, kc_ref, vct_ref, o_ref, bias_ref, vth_ref, vcth_ref, rows):
    geo = _na_geometry(rows)
    live = _na_live_rows(rows)
    tq = NA_Q_ROWS * GRID_W
    dh = NA_HEAD_DIM

    lane = lax.broadcasted_iota(jnp.int32, (1, LANES), 1)
    head0_lanes = lane < dh
    n_ones = vth_ref.shape[1] - dh
    for h in range(2):
        vth_ref[h, :dh, :] = vt_ref[0, h * dh:(h + 1) * dh, :]
        vth_ref[h, dh:, :] = jnp.ones((n_ones, vt_ref.shape[2]), vt_ref.dtype)
        vcth_ref[h, :dh, :] = vct_ref[0, h * dh:(h + 1) * dh, :]
        vcth_ref[h, dh:, :] = jnp.ones((n_ones, vct_ref.shape[2]), vct_ref.dtype)

    kc = kc_ref[0]
    contract_last = (((1,), (1,)), ((), ()))
    tasks = [(g, h) for g in range(len(geo)) for h in range(2)]

    def scores(t):
        g, h = tasks[t]
        ws, n_rows, var = geo[g]
        nk = n_rows * GRID_W
        qg = q_ref[0, g * tq:(g + 1) * tq, :]
        kw = k_ref[0, ws * GRID_W:ws * GRID_W + nk, :]
        sel = head0_lanes if h == 0 else jnp.logical_not(head0_lanes)
        qh = jnp.where(sel, qg, jnp.zeros_like(qg))
        s_raw = lax.dot_general(kw, qh, contract_last, preferred_element_type=jnp.float32)
        s_ctx = lax.dot_general(kc, qh, contract_last, preferred_element_type=jnp.float32)
        s_loc = []
        for ip, (lo, hi) in enumerate(live[var]):
            r, c = slice(lo * GRID_W, hi * GRID_W), slice(ip * LANES, (ip + 1) * LANES)
            s_loc.append(s_raw[r, c] + bias_ref[var, h, r, c])
        return s_loc, s_ctx

    halves = []

    def attend(t, s):
        g, h = tasks[t]
        s_loc, s_ctx = s
        ws, n_rows, var = geo[g]
        n_pad = n_rows + (-n_rows) % (MXU_TILE // GRID_W)
        p_loc, p_ctx = [], []
        for ip, (lo, hi) in enumerate(live[var]):
            sc = s_ctx[:, ip * LANES:(ip + 1) * LANES]
            m = jnp.maximum(jnp.max(s_loc[ip], axis=0, keepdims=True), jnp.max(sc, axis=0, keepdims=True))
            zeros = lambda n: [jnp.zeros((n * GRID_W, LANES), jnp.bfloat16)] if n else []
            p_loc.append(jnp.concatenate(
                zeros(lo) + [jnp.exp2((s_loc[ip] - m).astype(jnp.bfloat16))] + zeros(n_pad - hi), axis=0))
            p_ctx.append(jnp.exp2((sc - m).astype(jnp.bfloat16)))
        p_loc = jnp.concatenate(p_loc, axis=1)
        p_ctx = jnp.concatenate(p_ctx, axis=1)
        nk = n_pad * GRID_W
        ot = jnp.dot(vth_ref[h, :, ws * GRID_W:ws * GRID_W + nk], p_loc, preferred_element_type=jnp.float32)
        ot = ot + jnp.dot(vcth_ref[h], p_ctx, preferred_element_type=jnp.float32)
        halves.append(ot[:dh] / ot[dh:dh + 1])
        if h == 1:
            o2 = jnp.concatenate(halves, axis=0).T
            halves.clear()
            gate = g_ref[0, g * tq:(g + 1) * tq, :].astype(jnp.float32)
            o_ref[0, g * tq:(g + 1) * tq, :] = (o2 * gate).astype(o_ref.dtype)

    return len(tasks), scores, attend


def _ret_program(lg_f, lg_b, q_ref, k_ref, v_ref, g_ref, kc_ref, vc_ref, ng_ref, o_ref, t_ref, s_ref):
    c_len = RET_CHUNK
    l = q_ref.shape[1]
    lc = kc_ref.shape[1]
    n_chunks = l // c_len
    contract_last = (((1,), (1,)), ((), ()))

    ii = lax.broadcasted_iota(jnp.int32, (c_len, c_len), 0).astype(jnp.float32)
    jj = lax.broadcasted_iota(jnp.int32, (c_len, c_len), 1).astype(jnp.float32)
    dist = ii - jj
    decay = (jnp.where(dist >= 0, jnp.exp(lg_f * jnp.maximum(dist, 0.0)), 0.0)
             + jnp.where(dist <= 0, jnp.exp(lg_b * jnp.maximum(-dist, 0.0)), 0.0))
    ic = lax.broadcasted_iota(jnp.int32, (c_len, 1), 0).astype(jnp.float32)
    kdec_f = jnp.exp(lg_f * (c_len - 1 - ic))
    kdec_b = jnp.exp(lg_b * ic)
    qdec_f = jnp.exp(lg_f * (ic + 1.0))
    qdec_b = jnp.exp(lg_b * (c_len - ic))
    one = jnp.ones((1, 1), jnp.float32)
    cdec_f = jnp.exp(one * (lg_f * c_len))
    cdec_b = jnp.exp(one * (lg_b * c_len))

    jcr = lax.broadcasted_iota(jnp.int32, (1, lc), 1).astype(jnp.float32)
    kct = kc_ref[0].astype(jnp.float32).T
    vcx = vc_ref[0]
    s_f = jnp.dot((kct * jnp.exp(lg_f * (lc - 1 - jcr))).astype(jnp.bfloat16), vcx,
                  preferred_element_type=jnp.float32)
    s_b = jnp.dot((kct * jnp.exp(lg_b * jcr)).astype(jnp.bfloat16), vcx,
                  preferred_element_type=jnp.float32)

    for c in range(n_chunks):
        rows_c = slice(c * c_len, (c + 1) * c_len)
        kt = k_ref[0, rows_c, :].astype(jnp.float32).T.astype(jnp.bfloat16)
        vf = v_ref[0, rows_c, :].astype(jnp.float32)
        vw = jnp.concatenate([(vf * kdec_f).astype(jnp.bfloat16), (vf * kdec_b).astype(jnp.bfloat16)], axis=1)
        t_ref[c] = jnp.dot(kt, vw, preferred_element_type=jnp.float32)

    for c in range(n_chunks):
        s_ref[c, :, :RET_DIM] = s_f.astype(jnp.bfloat16)
        s_f = cdec_f * s_f + t_ref[c, :, :RET_DIM]
    for c in reversed(range(n_chunks)):
        s_ref[c, :, RET_DIM:] = s_b.astype(jnp.bfloat16)
        s_b = cdec_b * s_b + t_ref[c, :, RET_DIM:]

    ng = ng_ref[...]

    def scores(c):
        rows_c = slice(c * c_len, (c + 1) * c_len)
        qc = q_ref[0, rows_c, :]
        s = lax.dot_general(qc, k_ref[0, rows_c, :], contract_last, preferred_element_type=jnp.float32) * decay
        cross = jnp.dot(qc, s_ref[c], preferred_element_type=jnp.float32)
        return s.astype(jnp.bfloat16), cross[:, :RET_DIM] * qdec_f + cross[:, RET_DIM:] * qdec_b

    def finish(c, sc):
        s, cross = sc
        rows_c = slice(c * c_len, (c + 1) * c_len)
        o = jnp.dot(s, v_ref[0, rows_c, :], preferred_element_type=jnp.float32) + cross
        o = o * lax.rsqrt(jnp.mean(o * o, axis=-1, keepdims=True) + EPS) * ng
        gate = g_ref[0, rows_c, :].astype(jnp.float32)
        o_ref[0, rows_c, :] = (o * gate).astype(o_ref.dtype)

    return n_chunks, scores, finish


def _mixer_kernel(rpb_ref, lg_ref,
                  aq_ref, ak_ref, avt_ref, ag_ref, akc_ref, avct_ref,
                  rq_ref, rk_ref, rv_ref, rg_ref, rkc_ref, rvc_ref, ng_ref,
                  yna_ref, yret_ref,
                  bias_ref, base_ref, vth_ref, vcth_ref, t_ref, s_ref, *, rows):
    i = pl.program_id(0)

    @pl.when(pl.program_id(1) == 0)
    def _():
        _na_build_bias(rpb_ref, i, bias_ref, base_ref, rows)

    nb = aq_ref.shape[0]
    lc = akc_ref.shape[2] // nb
    slab = lambda ref, j: ref.at[j]
    one = lambda ref, j: ref.at[pl.ds(j, 1)]
    cslab = lambda ref, j: ref.at[0].at[:, pl.ds(j * lc, lc), :]
    cone = lambda ref, j: ref.at[:, :, pl.ds(j * lc, lc)]

    na_tasks, ret_tasks = [], []
    for j in range(nb):
        n_ret, ret_scores, ret_finish = _ret_program(
            lg_ref[0, i], lg_ref[1, i], slab(rq_ref, j), slab(rk_ref, j), slab(rv_ref, j), slab(rg_ref, j),
            cslab(rkc_ref, j), cslab(rvc_ref, j), ng_ref, slab(yret_ref, j), t_ref.at[j], s_ref.at[j])
        n_na, na_scores, na_attend = _na_program(
            slab(aq_ref, j), slab(ak_ref, j), one(avt_ref, j), slab(ag_ref, j), cslab(akc_ref, j),
            cone(avct_ref, j), slab(yna_ref, j), bias_ref, vth_ref.at[j], vcth_ref.at[j], rows)
        na_tasks += [(na_scores, na_attend, t) for t in range(n_na)]
        ret_tasks += [(ret_scores, ret_finish, c) for c in range(n_ret)]

    issue = lambda task: task[0](task[2])
    na_q = [issue(task) for task in na_tasks[:NA_AHEAD]]
    ret_q = [issue(task) for task in ret_tasks[:RET_AHEAD]]
    for t in range(max(len(na_tasks), len(ret_tasks))):
        if t + NA_AHEAD < len(na_tasks):
            na_q.append(issue(na_tasks[t + NA_AHEAD]))
        if t + RET_AHEAD < len(ret_tasks):
            ret_q.append(issue(ret_tasks[t + RET_AHEAD]))
        if t < len(na_tasks):
            na_tasks[t][1](na_tasks[t][2], na_q.pop(0))
        if t < len(ret_tasks):
            ret_tasks[t][1](ret_tasks[t][2], ret_q.pop(0))


def _mixers(rpb, lg, proj, vt, proj_ctx, vct, ret_norm_g, rows, cols, ctx_cols):
    b, _, l, _ = proj.shape
    lc = proj_ctx.shape[2] // b
    nb = MIX_BATCH
    assert NA_HEADS // 2 == RET_HEADS and b % nb == 0
    blk = lambda name: pl.BlockSpec((nb, 1, l, LANES), lambda i, bi, *_, off=cols[name]: (bi, off + i, 0, 0))
    cblk = lambda name: pl.BlockSpec((1, 1, nb * lc, LANES),
                                     lambda i, bi, *_, off=ctx_cols[name]: (0, off + i, bi, 0))
    n_chunks = l // RET_CHUNK
    grid_spec = pltpu.PrefetchScalarGridSpec(
        num_scalar_prefetch=2,
        grid=(RET_HEADS, b // nb),
        in_specs=[blk("na_q"), blk("na_k"),
                  pl.BlockSpec((nb, LANES, l), lambda i, bi, *_: (bi, i, 0)),
                  blk("na_g"), cblk("na_k"),
                  pl.BlockSpec((1, LANES, nb * lc), lambda i, bi, *_: (0, i, bi)),
                  blk("r_q"), blk("r_k"), blk("r_v"), blk("r_g"), cblk("r_k"), cblk("r_v"),
                  pl.BlockSpec((1, LANES), lambda i, bi, *_: (0, i))],
        out_specs=(pl.BlockSpec((nb, 1, l, LANES), lambda i, bi, *_: (bi, i, 0, 0)),
                   pl.BlockSpec((nb, 1, l, LANES), lambda i, bi, *_: (bi, i, 0, 0))),
        scratch_shapes=[pltpu.VMEM((3, 2, NA_K_ROWS * GRID_W, NA_Q_ROWS * GRID_W), jnp.float32),
                        pltpu.VMEM((2 * NA_WIN_ROWS - 1, GRID_W, LANES), jnp.float32),
                        pltpu.VMEM((nb, 2, NA_HEAD_DIM + BF16_SUBLANES, l), jnp.bfloat16),
                        pltpu.VMEM((nb, 2, NA_HEAD_DIM + BF16_SUBLANES, lc), jnp.bfloat16),
                        pltpu.VMEM((nb, n_chunks, RET_DIM, 2 * RET_DIM), jnp.float32),
                        pltpu.VMEM((nb, n_chunks, RET_DIM, 2 * RET_DIM), jnp.bfloat16)],
    )
    return pl.pallas_call(
        functools.partial(_mixer_kernel, rows=rows),
        out_shape=(jax.ShapeDtypeStruct((b, NA_WIDTH // LANES, l, LANES), jnp.bfloat16),
                   jax.ShapeDtypeStruct((b, RET_WIDTH // LANES, l, LANES), jnp.bfloat16)),
        grid_spec=grid_spec,
        compiler_params=pltpu.CompilerParams(
            dimension_semantics=("arbitrary", "arbitrary"), vmem_limit_bytes=VMEM_LIMIT),
        name="mixers",
    )(rpb.astype(jnp.float32).reshape(-1), lg,
      proj, proj, vt, proj, proj_ctx, vct,
      proj, proj, proj, proj, proj_ctx, proj_ctx, ret_norm_g.reshape(1, RET_WIDTH))


def _out_kernel(x_ref, gate_ref, yna_ref, yret_ref, w_ref, fg_ref, o_ref, *, sub):
    tm = x_ref.shape[1]

    def mix(j):
        rows = slice(j * sub, (j + 1) * sub)
        y = jnp.concatenate([ref[0, k, rows, :] for ref in (yna_ref, yret_ref) for k in range(ref.shape[1])],
                            axis=1)
        return jnp.dot(y, w_ref[...], preferred_element_type=jnp.float32)

    def finish(j, y):
        rows = slice(j * sub, (j + 1) * sub)
        z = x_ref[0, rows, :] + gate_ref[0] * y
        ms = jnp.mean(z * z, axis=-1, keepdims=True)
        o_ref[0, rows, :] = z * lax.rsqrt(ms + EPS) * fg_ref[...]

    pending = mix(0)
    for j in range(tm // sub):
        current = pending
        if (j + 1) * sub < tm:
            pending = mix(j + 1)
        finish(j, current)


def _out_proj(x, gate, y_na, y_ret, w_out, final_g, tm, sub):
    b, l, d = x.shape
    assert l % tm == 0 and tm % sub == 0
    return pl.pallas_call(
        functools.partial(_out_kernel, sub=sub),
        out_shape=jax.ShapeDtypeStruct((b, l, d), jnp.float32),
        grid=(b, l // tm),
        in_specs=[pl.BlockSpec((1, tm, d), lambda bi, i: (bi, i, 0)),
                  pl.BlockSpec((1, 1, d), lambda bi, i: (bi, 0, 0)),
                  pl.BlockSpec((1, NA_WIDTH // LANES, tm, LANES), lambda bi, i: (bi, 0, i, 0)),
                  pl.BlockSpec((1, RET_WIDTH // LANES, tm, LANES), lambda bi, i: (bi, 0, i, 0)),
                  pl.BlockSpec((NA_WIDTH + RET_WIDTH, d), lambda bi, i: (0, 0)),
                  pl.BlockSpec((1, d), lambda bi, i: (0, 0))],
        out_specs=pl.BlockSpec((1, tm, d), lambda bi, i: (bi, i, 0)),
        compiler_params=pltpu.CompilerParams(
            dimension_semantics=("arbitrary", "arbitrary"), vmem_limit_bytes=VMEM_LIMIT),
        name="out_proj",
    )(x, gate, y_na, y_ret, w_out, final_g.reshape(1, d))


def _rotary_tables(l):
    half = RET_DIM // 2
    nf = half // 2
    t = np.arange(l)
    row = (t // GRID_W).astype(np.float64)
    col = (t % GRID_W).astype(np.float64)
    inv = ROPE_BASE ** (-np.arange(nf, dtype=np.float64) / nf)
    ang = np.concatenate([row[:, None] * inv, col[:, None] * inv], axis=-1)
    cos, sin = np.cos(ang), np.sin(ang)
    cosf = np.concatenate([cos, cos], axis=-1).astype(np.float32)
    sinf = np.concatenate([-sin, sin], axis=-1).astype(np.float32)
    return jnp.asarray(cosf), jnp.asarray(sinf)


def kernel(x, c, ctx, c_ctx, norm_g, w_ada, b_ada, w_in, na_rpb, ret_decay_fwd, ret_decay_bwd,
           ret_norm_g, w_out, final_norm_g):
    depth = norm_g.shape[0]
    assert depth == 1, "context stream update between layers is not implemented"
    b, l, d = x.shape
    rows = l // GRID_W
    i = 0

    cc = jnp.concatenate([c, c_ctx[None, :]], axis=0)
    pad = (-cc.shape[0]) % BF16_SUBLANES
    cc = jnp.pad(cc, ((0, pad), (0, 0)))
    mod, w, w_vt = _prep(cc, w_ada[i], b_ada[i], w_in[i], v_group=2)
    shift, scale, gate = (mod[:b, None, :d], mod[:b, None, d:2 * d], mod[:b, None, 2 * d:])
    shift_c, scale_c = mod[b:b + 1, None, :d], mod[b:b + 1, None, d:2 * d]
    cosf, sinf = _rotary_tables(l)
    blocks = GROUP_COLS // LANES

    lat_groups = ((0, "na_q"), (1, "plain"), (3, "silu"), (4, "rot"), (5, "rot_kscale"), (6, "plain"), (7, "silu"))
    lat_names = ("na_q", "na_k", "na_g", "r_q", "r_k", "r_v", "r_g")
    proj, vt = _in_proj(x, scale, shift, norm_g[i], w, w_vt, cosf, sinf, lat_groups, tm=1024, sub=256)

    lc = ctx.shape[1]
    ctx_groups = ((1, "plain"), (5, "kscale"), (6, "plain"))
    ctx_names = ("na_k", "r_k", "r_v")
    proj_ctx, vct = _in_proj(ctx.reshape(1, b * lc, d), scale_c, shift_c, norm_g[i], w, w_vt, cosf, sinf,
                             ctx_groups, tm=1024, sub=256)

    lg = jnp.stack([-jnp.exp(ret_decay_fwd[i].astype(jnp.float32)),
                    -jnp.exp(ret_decay_bwd[i].astype(jnp.float32))])
    y_na, y_ret = _mixers(na_rpb[i], lg, proj, vt, proj_ctx, vct, ret_norm_g[i], rows,
                          cols={name: k * blocks for k, name in enumerate(lat_names)},
                          ctx_cols={name: k * blocks for k, name in enumerate(ctx_names)})

    return _out_proj(x, gate, y_na, y_ret, w_out[i].astype(jnp.bfloat16), final_norm_g, tm=2048, sub=256)
```

```python
import functools
import math

import jax
import jax.numpy as jnp
import numpy as np
from jax import lax
from jax.experimental import pallas as pl
from jax.experimental.pallas import tpu as pltpu

D_MODEL = 1024
GRID_W = 64
NA_HEAD_DIM = 64
NA_WIDTH = 512
NA_HEADS = 8
NA_WIN_ROWS = 8
NA_WIN_COLS = 16
RET_HEADS = 4
RET_WIDTH = 512
RET_DIM = 128
RET_CHUNK = 256
ROPE_BASE = 10000.0
EPS = 1e-6

LANES = 128
MXU_TILE = 256
BF16_SUBLANES = 16
GROUP_COLS = 512
NA_Q_ROWS = 4
NA_K_ROWS = 12
NA_AHEAD = 1
RET_AHEAD = 1
MIX_BATCH = 2
MASK_VALUE = -1e30
LOG2E = math.log2(math.e)

VMEM_LIMIT = 56 * 1024 * 1024


def _silu(v):
    return v * (1.0 / (1.0 + jnp.exp(-v)))


def _split_bf16(v):
    hi = v.astype(jnp.bfloat16)
    return hi, (v - hi.astype(jnp.float32)).astype(jnp.bfloat16)


def _prep_kernel(c_ref, wa_ref, b_ref, w_ref, mod_ref, o_ref, ot_ref, *, v_group):
    m = c_ref.shape[0]
    a_hi, a_lo = _split_bf16(_silu(c_ref[...]))
    w_hi, w_lo = _split_bf16(wa_ref[...])
    both = jnp.dot(jnp.concatenate([a_hi, a_lo], axis=0), w_hi, preferred_element_type=jnp.float32)
    mod_ref[...] = (both[:m] + both[m:]) + jnp.dot(a_hi, w_lo, preferred_element_type=jnp.float32) + b_ref[...]

    w = w_ref[...]
    o_ref[...] = w.astype(o_ref.dtype)
    ot_ref[...] = w[:, v_group * GROUP_COLS:(v_group + 1) * GROUP_COLS].T.astype(ot_ref.dtype)


def _prep(cc, w_ada, b_ada, w_in, v_group):
    m, d = cc.shape
    n_mod = w_ada.shape[1]
    n = w_in.shape[1]
    steps = 8
    tn, rb = n_mod // steps, d // steps
    assert tn % LANES == 0 and rb % LANES == 0
    return pl.pallas_call(
        functools.partial(_prep_kernel, v_group=v_group),
        out_shape=(jax.ShapeDtypeStruct((m, n_mod), jnp.float32),
                   jax.ShapeDtypeStruct((d, n), jnp.bfloat16),
                   jax.ShapeDtypeStruct((GROUP_COLS, d), jnp.bfloat16)),
        grid=(steps,),
        in_specs=[pl.BlockSpec((m, d), lambda j: (0, 0)),
                  pl.BlockSpec((d, tn), lambda j: (0, j)),
                  pl.BlockSpec((1, tn), lambda j: (0, j)),
                  pl.BlockSpec((rb, n), lambda j: (j, 0))],
        out_specs=(pl.BlockSpec((m, tn), lambda j: (0, j)),
                   pl.BlockSpec((rb, n), lambda j: (j, 0)),
                   pl.BlockSpec((GROUP_COLS, rb), lambda j: (0, j))),
        compiler_params=pltpu.CompilerParams(
            dimension_semantics=("arbitrary",), vmem_limit_bytes=VMEM_LIMIT),
        name="prep",
    )(cc, w_ada, b_ada.reshape(1, n_mod), w_in)


def _rotary(acc, cosf, sinf):
    outs = []
    for h in range(GROUP_COLS // LANES):
        xh = acc[:, h * LANES:(h + 1) * LANES]
        outs.append(xh * cosf + pltpu.roll(xh, LANES // 2, 1) * sinf)
    return jnp.concatenate(outs, axis=-1)


def _proj_kernel(x_ref, scale_ref, shift_ref, g_ref, w_ref, wvt_ref, cos_ref, sin_ref, o_ref, vt_ref,
                 *, groups, sub):
    tm = x_ref.shape[1]

    def normed(j):
        x = x_ref[0, j * sub:(j + 1) * sub, :]
        ms = jnp.mean(x * x, axis=-1, keepdims=True)
        h = x * lax.rsqrt(ms + EPS) * g_ref[...]
        h = h * (1.0 + scale_ref[0]) + shift_ref[0]
        return h.astype(jnp.bfloat16)

    def project(j, hb):
        rows = slice(j * sub, (j + 1) * sub)
        vt = lax.dot_general(wvt_ref[...], hb, (((1,), (1,)), ((), ())), preferred_element_type=jnp.float32)
        vt_ref[0, :, rows] = vt.astype(vt_ref.dtype)
        for gi, (src, kind) in enumerate(groups):
            acc = jnp.dot(hb, w_ref[:, src * GROUP_COLS:(src + 1) * GROUP_COLS],
                          preferred_element_type=jnp.float32)
            if kind == "na_q":
                acc = acc * (NA_HEAD_DIM ** -0.5 * LOG2E)
            elif kind == "silu":
                acc = _silu(acc)
            elif kind == "rot":
                acc = _rotary(acc, cos_ref[rows, :], sin_ref[rows, :])
            elif kind == "rot_kscale":
                acc = _rotary(acc, cos_ref[rows, :], sin_ref[rows, :]) * (RET_DIM ** -0.5)
            elif kind == "kscale":
                acc = acc * (RET_DIM ** -0.5)
            else:
                assert kind == "plain"
            per_group = GROUP_COLS // LANES
            for k in range(per_group):
                o_ref[0, gi * per_group + k, rows, :] = acc[:, k * LANES:(k + 1) * LANES].astype(o_ref.dtype)

    pending = normed(0)
    for j in range(tm // sub):
        current = pending
        if (j + 1) * sub < tm:
            pending = normed(j + 1)
        project(j, current)


def _in_proj(x, scale, shift, norm_g, w, w_vt, cosf, sinf, groups, tm, sub):
    b, l, d = x.shape
    n_in = w.shape[1]
    n = GROUP_COLS * len(groups)
    nv = w_vt.shape[0]
    assert l % tm == 0 and tm % sub == 0
    per_batch = scale.shape[0] > 1
    mod_map = (lambda bi, i: (bi, 0, 0)) if per_batch else (lambda bi, i: (0, 0, 0))
    rotated = any(kind.startswith("rot") for _, kind in groups)
    pos_map = (lambda bi, i: (i, 0)) if rotated else (lambda bi, i: (0, 0))
    return pl.pallas_call(
        functools.partial(_proj_kernel, groups=tuple(groups), sub=sub),
        out_shape=(jax.ShapeDtypeStruct((b, n // LANES, l, LANES), jnp.bfloat16),
                   jax.ShapeDtypeStruct((b, nv, l), jnp.bfloat16)),
        grid=(b, l // tm),
        in_specs=[pl.BlockSpec((1, tm, d), lambda bi, i: (bi, i, 0)),
                  pl.BlockSpec((1, 1, d), mod_map),
                  pl.BlockSpec((1, 1, d), mod_map),
                  pl.BlockSpec((1, d), lambda bi, i: (0, 0)),
                  pl.BlockSpec((d, n_in), lambda bi, i: (0, 0)),
                  pl.BlockSpec((nv, d), lambda bi, i: (0, 0)),
                  pl.BlockSpec((tm, LANES), pos_map),
                  pl.BlockSpec((tm, LANES), pos_map)],
        out_specs=(pl.BlockSpec((1, n // LANES, tm, LANES), lambda bi, i: (bi, 0, i, 0)),
                   pl.BlockSpec((1, nv, tm), lambda bi, i: (bi, 0, i))),
        compiler_params=pltpu.CompilerParams(
            dimension_semantics=("arbitrary", "arbitrary"), vmem_limit_bytes=VMEM_LIMIT),
        name="in_proj",
    )(x, scale, shift, norm_g.reshape(1, d), w, w_vt, cosf, sinf)


def _na_window_start(rq, rows):
    kr = min(NA_WIN_ROWS, rows)
    return int(np.clip(rq - kr // 2, 0, rows - kr))


def _na_geometry(rows):
    kr = min(NA_WIN_ROWS, rows)
    n_groups = rows // NA_Q_ROWS
    assert n_groups >= 3
    geo = []
    for g in range(n_groups):
        lo = _na_window_start(NA_Q_ROWS * g, rows)
        hi = _na_window_start(NA_Q_ROWS * g + NA_Q_ROWS - 1, rows) + kr
        geo.append((lo, hi - lo, 0 if g == 0 else (2 if g == n_groups - 1 else 1)))
    assert len({n for _, n, v in geo if v == 1}) == 1 and max(n for _, n, _ in geo) <= NA_K_ROWS
    return geo


def _na_row_offsets(rows):
    kr = min(NA_WIN_ROWS, rows)
    geo = _na_geometry(rows)
    table = []
    for rep in (0, 1, len(geo) - 1):
        start, n_rows, _ = geo[rep]
        per_i = []
        for i in range(NA_Q_ROWS):
            rq = NA_Q_ROWS * rep + i
            r0 = _na_window_start(rq, rows)
            per_i.append([start + j - rq + NA_WIN_ROWS - 1 if r0 <= start + j < r0 + kr else None
                          for j in range(n_rows)])
        table.append(per_i)
    return table


def _na_live_rows(rows):
    live = []
    for per_i in _na_row_offsets(rows):
        pairs = []
        for ip in range(NA_Q_ROWS // 2):
            seen = [j for j in range(len(per_i[0]))
                    if per_i[2 * ip][j] is not None or per_i[2 * ip + 1][j] is not None]
            assert seen == list(range(seen[0], seen[-1] + 1))
            pairs.append((seen[0], seen[-1] + 1))
        live.append(pairs)
    return live


def _na_build_bias(rpb_ref, hp, bias_ref, base_ref, rows):
    n_dr = 2 * NA_WIN_ROWS - 1
    n_dc = 2 * NA_WIN_COLS - 1
    ck = lax.broadcasted_iota(jnp.int32, (GRID_W, LANES), 0)
    lane = lax.broadcasted_iota(jnp.int32, (GRID_W, LANES), 1)
    left = lane < GRID_W
    cq = jnp.where(left, lane, lane - GRID_W)
    c0 = jnp.clip(cq - NA_WIN_COLS // 2, 0, GRID_W - NA_WIN_COLS)
    valid_c = (ck >= c0) & (ck < c0 + NA_WIN_COLS)
    dc = jnp.clip(ck - cq + NA_WIN_COLS - 1, 0, n_dc - 1)

    offsets = _na_row_offsets(rows)
    users = {}
    for var in range(3):
        for j in range(len(offsets[var][0])):
            for ip in range(NA_Q_ROWS // 2):
                key = (offsets[var][2 * ip][j], offsets[var][2 * ip + 1][j])
                users.setdefault(key, []).append((var, j, ip))

    used = sorted({dr for pair in users for dr in pair if dr is not None})
    masked = jnp.full((GRID_W, LANES), MASK_VALUE, jnp.float32)
    for h in range(2):
        head = 2 * hp + h
        for dr in used:
            def body(d, acc, dr=dr, head=head):
                return jnp.where(dc == d, rpb_ref[(head * n_dr + dr) * n_dc + d], acc)
            acc = lax.fori_loop(0, n_dc, body, jnp.zeros((GRID_W, LANES), jnp.float32), unroll=True)
            base_ref[dr] = jnp.where(valid_c, acc * LOG2E, MASK_VALUE)
        for (dr_l, dr_r), dests in users.items():
            t_l = masked if dr_l is None else base_ref[dr_l]
            t_r = masked if dr_r is None else base_ref[dr_r]
            tile = t_l if dr_l == dr_r else jnp.where(left, t_l, t_r)
            for var, j, ip in dests:
                bias_ref[var, h, j * GRID_W:(j + 1) * GRID_W, ip * LANES:(ip + 1) * LANES] = tile


def _na_program(q_ref, k_ref, vt_ref, g_ref, kc_ref, vct_ref, o_ref, bias_ref, vth_ref, vcth_ref, rows):
    geo = _na_geometry(rows)
    live = _na_live_rows(rows)
    tq = NA_Q_ROWS * GRID_W
    dh = NA_HEAD_DIM

    lane = lax.broadcasted_iota(jnp.int32, (1, LANES), 1)
    head0_lanes = lane < dh
    n_ones = vth_ref.shape[1] - dh
    for h in range(2):
        vth_ref[h, :dh, :] = vt_ref[0, h * dh:(h + 1) * dh, :]
        vth_ref[h, dh:, :] = jnp.ones((n_ones, vt_ref.shape[2]), vt_ref.dtype)
        vcth_ref[h, :dh, :] = vct_ref[0, h * dh:(h + 1) * dh, :]
        vcth_ref[h, dh:, :] = jnp.ones((n_ones, vct_ref.shape[2]), vct_ref.dtype)

    kc = kc_ref[0]
    contract_last = (((1,), (1,)), ((), ()))
    tasks = [(g, h) for g in range(len(geo)) for h in range(2)]

    def scores(t):
        g, h = tasks[t]
        ws, n_rows, var = geo[g]
        nk = n_rows * GRID_W
        qg = q_ref[0, g * tq:(g + 1) * tq, :]
        kw = k_ref[0, ws * GRID_W:ws * GRID_W + nk, :]
        sel = head0_lanes if h == 0 else jnp.logical_not(head0_lanes)
        qh = jnp.where(sel, qg, jnp.zeros_like(qg))
        s_ctx = lax.dot_general(kc, qh, contract_last, preferred_element_type=jnp.float32)
        (lo_a, hi_a), (lo_b, hi_b) = live[var]
        biased = lambda s, lo, hi, ip: s + bias_ref[var, h, lo * GRID_W:hi * GRID_W, ip * LANES:(ip + 1) * LANES]
        if lo_a < lo_b and hi_a < hi_b and lo_b - lo_a == hi_b - hi_a and LANES == tq // 2:
            edge = (lo_b - lo_a) * GRID_W
            k_edge = jnp.concatenate([kw[lo_a * GRID_W:lo_b * GRID_W], kw[hi_a * GRID_W:hi_b * GRID_W]], axis=1)
            zq = jnp.zeros((LANES, LANES), qh.dtype)
            q_edge = jnp.concatenate([jnp.concatenate([qh[:LANES], zq], axis=1),
                                      jnp.concatenate([zq, qh[LANES:]], axis=1)], axis=0)
            s_edge = lax.dot_general(k_edge, q_edge, contract_last, preferred_element_type=jnp.float32)
            s_mid = lax.dot_general(kw[lo_b * GRID_W:hi_a * GRID_W], qh, contract_last,
                                    preferred_element_type=jnp.float32)
            assert s_edge.shape == (edge, tq)
            s_loc = [[biased(s_edge[:, :LANES], lo_a, lo_b, 0), biased(s_mid[:, :LANES], lo_b, hi_a, 0)],
                     [biased(s_mid[:, LANES:], lo_b, hi_a, 1), biased(s_edge[:, LANES:], hi_a, hi_b, 1)]]
        else:
            s_raw = lax.dot_general(kw, qh, contract_last, preferred_element_type=jnp.float32)
            s_loc = [[biased(s_raw[lo * GRID_W:hi * GRID_W, ip * LANES:(ip + 1) * LANES], lo, hi, ip)]
                     for ip, (lo, hi) in enumerate(live[var])]
        return s_loc, s_ctx

    halves = []

    def attend(t, s):
        g, h = tasks[t]
        s_loc, s_ctx = s
        ws, n_rows, var = geo[g]
        n_pad = n_rows + (-n_rows) % (MXU_TILE // GRID_W)
        p_loc, p_ctx = [], []
        for ip, (lo, hi) in enumerate(live[var]):
            sc = s_ctx[:, ip * LANES:(ip + 1) * LANES]
            m = jnp.max(sc, axis=0, keepdims=True)
            for piece in s_loc[ip]:
                m = jnp.maximum(m, jnp.max(piece, axis=0, keepdims=True))
            zeros = lambda n: [jnp.zeros((n * GRID_W, LANES), jnp.bfloat16)] if n else []
            p_loc.append(jnp.concatenate(
                zeros(lo) + [jnp.exp2((piece - m).astype(jnp.bfloat16)) for piece in s_loc[ip]]
                + zeros(n_pad - hi), axis=0))
            p_ctx.append(jnp.exp2((sc - m).astype(jnp.bfloat16)))
        p_loc = jnp.concatenate(p_loc, axis=1)
        p_ctx = jnp.concatenate(p_ctx, axis=1)
        nk = n_pad * GRID_W
        ot = jnp.dot(vth_ref[h, :, ws * GRID_W:ws * GRID_W + nk], p_loc, preferred_element_type=jnp.float32)
        ot = ot + jnp.dot(vcth_ref[h], p_ctx, preferred_element_type=jnp.float32)
        halves.append(ot[:dh] / ot[dh:dh + 1])
        if h == 1:
            o2 = jnp.concatenate(halves, axis=0).T
            halves.clear()
            gate = g_ref[0, g * tq:(g + 1) * tq, :].astype(jnp.float32)
            o_ref[0, g * tq:(g + 1) * tq, :] = (o2 * gate).astype(o_ref.dtype)

    return len(tasks), scores, attend


def _ret_program(lg_f, lg_b, q_ref, k_ref, v_ref, g_ref, kc_ref, vc_ref, ng_ref, o_ref, t_ref, s_ref):
    c_len = RET_CHUNK
    l = q_ref.shape[1]
    lc = kc_ref.shape[1]
    n_chunks = l // c_len
    contract_last = (((1,), (1,)), ((), ()))

    ii = lax.broadcasted_iota(jnp.int32, (c_len, c_len), 0).astype(jnp.float32)
    jj = lax.broadcasted_iota(jnp.int32, (c_len, c_len), 1).astype(jnp.float32)
    dist = ii - jj
    decay = (jnp.where(dist >= 0, jnp.exp(lg_f * jnp.maximum(dist, 0.0)), 0.0)
             + jnp.where(dist <= 0, jnp.exp(lg_b * jnp.maximum(-dist, 0.0)), 0.0))
    ic = lax.broadcasted_iota(jnp.int32, (c_len, 1), 0).astype(jnp.float32)
    kdec_f = jnp.exp(lg_f * (c_len - 1 - ic))
    kdec_b = jnp.exp(lg_b * ic)
    qdec_f = jnp.exp(lg_f * (ic + 1.0))
    qdec_b = jnp.exp(lg_b * (c_len - ic))
    one = jnp.ones((1, 1), jnp.float32)
    cdec_f = jnp.exp(one * (lg_f * c_len))
    cdec_b = jnp.exp(one * (lg_b * c_len))

    jcr = lax.broadcasted_iota(jnp.int32, (1, lc), 1).astype(jnp.float32)
    kct = kc_ref[0].astype(jnp.float32).T
    vcx = vc_ref[0]
    s_f = jnp.dot((kct * jnp.exp(lg_f * (lc - 1 - jcr))).astype(jnp.bfloat16), vcx,
                  preferred_element_type=jnp.float32)
    s_b = jnp.dot((kct * jnp.exp(lg_b * jcr)).astype(jnp.bfloat16), vcx,
                  preferred_element_type=jnp.float32)

    for c in range(n_chunks):
        rows_c = slice(c * c_len, (c + 1) * c_len)
        kt = k_ref[0, rows_c, :].astype(jnp.float32).T.astype(jnp.bfloat16)
        vf = v_ref[0, rows_c, :].astype(jnp.float32)
        vw = jnp.concatenate([(vf * kdec_f).astype(jnp.bfloat16), (vf * kdec_b).astype(jnp.bfloat16)], axis=1)
        t_ref[c] = jnp.dot(kt, vw, preferred_element_type=jnp.float32)

    for c in range(n_chunks):
        s_ref[c, :, :RET_DIM] = s_f.astype(jnp.bfloat16)
        s_f = cdec_f * s_f + t_ref[c, :, :RET_DIM]
    for c in reversed(range(n_chunks)):
        s_ref[c, :, RET_DIM:] = s_b.astype(jnp.bfloat16)
        s_b = cdec_b * s_b + t_ref[c, :, RET_DIM:]

    ng = ng_ref[...]

    def scores(c):
        rows_c = slice(c * c_len, (c + 1) * c_len)
        qc = q_ref[0, rows_c, :]
        s = lax.dot_general(qc, k_ref[0, rows_c, :], contract_last, preferred_element_type=jnp.float32) * decay
        cross = jnp.dot(qc, s_ref[c], preferred_element_type=jnp.float32)
        return s.astype(jnp.bfloat16), cross[:, :RET_DIM] * qdec_f + cross[:, RET_DIM:] * qdec_b

    def finish(c, sc):
        s, cross = sc
        rows_c = slice(c * c_len, (c + 1) * c_len)
        o = jnp.dot(s, v_ref[0, rows_c, :], preferred_element_type=jnp.float32) + cross
        o = o * lax.rsqrt(jnp.mean(o * o, axis=-1, keepdims=True) + EPS) * ng
        gate = g_ref[0, rows_c, :].astype(jnp.float32)
        o_ref[0, rows_c, :] = (o * gate).astype(o_ref.dtype)

    return n_chunks, scores, finish


def _mixer_kernel(rpb_ref, lg_ref,
                  aq_ref, ak_ref, avt_ref, ag_ref, akc_ref, avct_ref,
                  rq_ref, rk_ref, rv_ref, rg_ref, rkc_ref, rvc_ref, ng_ref,
                  yna_ref, yret_ref,
                  bias_ref, base_ref, vth_ref, vcth_ref, t_ref, s_ref, *, rows):
    i = pl.program_id(0)

    @pl.when(pl.program_id(1) == 0)
    def _():
        _na_build_bias(rpb_ref, i, bias_ref, base_ref, rows)

    nb = aq_ref.shape[0]
    lc = akc_ref.shape[2] // nb
    slab = lambda ref, j: ref.at[j]
    one = lambda ref, j: ref.at[pl.ds(j, 1)]
    cslab = lambda ref, j: ref.at[0].at[:, pl.ds(j * lc, lc), :]
    cone = lambda ref, j: ref.at[:, :, pl.ds(j * lc, lc)]

    na_tasks, ret_tasks = [], []
    for j in range(nb):
        n_ret, ret_scores, ret_finish = _ret_program(
            lg_ref[0, i], lg_ref[1, i], slab(rq_ref, j), slab(rk_ref, j), slab(rv_ref, j), slab(rg_ref, j),
            cslab(rkc_ref, j), cslab(rvc_ref, j), ng_ref, slab(yret_ref, j), t_ref.at[j], s_ref.at[j])
        n_na, na_scores, na_attend = _na_program(
            slab(aq_ref, j), slab(ak_ref, j), one(avt_ref, j), slab(ag_ref, j), cslab(akc_ref, j),
            cone(avct_ref, j), slab(yna_ref, j), bias_ref, vth_ref.at[j], vcth_ref.at[j], rows)
        na_tasks += [(na_scores, na_attend, t) for t in range(n_na)]
        ret_tasks += [(ret_scores, ret_finish, c) for c in range(n_ret)]

    issue = lambda task: task[0](task[2])
    na_q = [issue(task) for task in na_tasks[:NA_AHEAD]]
    ret_q = [issue(task) for task in ret_tasks[:RET_AHEAD]]
    for t in range(max(len(na_tasks), len(ret_tasks))):
        if t + NA_AHEAD < len(na_tasks):
            na_q.append(issue(na_tasks[t + NA_AHEAD]))
        if t + RET_AHEAD < len(ret_tasks):
            ret_q.append(issue(ret_tasks[t + RET_AHEAD]))
        if t < len(na_tasks):
            na_tasks[t][1](na_tasks[t][2], na_q.pop(0))
        if t < len(ret_tasks):
            ret_tasks[t][1](ret_tasks[t][2], ret_q.pop(0))


def _mixers(rpb, lg, proj, vt, proj_ctx, vct, ret_norm_g, rows, cols, ctx_cols):
    b, _, l, _ = proj.shape
    lc = proj_ctx.shape[2] // b
    nb = MIX_BATCH
    assert NA_HEADS // 2 == RET_HEADS and b % nb == 0
    blk = lambda name: pl.BlockSpec((nb, 1, l, LANES), lambda i, bi, *_, off=cols[name]: (bi, off + i, 0, 0))
    cblk = lambda name: pl.BlockSpec((1, 1, nb * lc, LANES),
                                     lambda i, bi, *_, off=ctx_cols[name]: (0, off + i, bi, 0))
    n_chunks = l // RET_CHUNK
    grid_spec = pltpu.PrefetchScalarGridSpec(
        num_scalar_prefetch=2,
        grid=(RET_HEADS, b // nb),
        in_specs=[blk("na_q"), blk("na_k"),
                  pl.BlockSpec((nb, LANES, l), lambda i, bi, *_: (bi, i, 0)),
                  blk("na_g"), cblk("na_k"),
                  pl.BlockSpec((1, LANES, nb * lc), lambda i, bi, *_: (0, i, bi)),
                  blk("r_q"), blk("r_k"), blk("r_v"), blk("r_g"), cblk("r_k"), cblk("r_v"),
                  pl.BlockSpec((1, LANES), lambda i, bi, *_: (0, i))],
        out_specs=(pl.BlockSpec((nb, 1, l, LANES), lambda i, bi, *_: (bi, i, 0, 0)),
                   pl.BlockSpec((nb, 1, l, LANES), lambda i, bi, *_: (bi, i, 0, 0))),
        scratch_shapes=[pltpu.VMEM((3, 2, NA_K_ROWS * GRID_W, NA_Q_ROWS * GRID_W), jnp.float32),
                        pltpu.VMEM((2 * NA_WIN_ROWS - 1, GRID_W, LANES), jnp.float32),
                        pltpu.VMEM((nb, 2, NA_HEAD_DIM + BF16_SUBLANES, l), jnp.bfloat16),
                        pltpu.VMEM((nb, 2, NA_HEAD_DIM + BF16_SUBLANES, lc), jnp.bfloat16),
                        pltpu.VMEM((nb, n_chunks, RET_DIM, 2 * RET_DIM), jnp.float32),
                        pltpu.VMEM((nb, n_chunks, RET_DIM, 2 * RET_DIM), jnp.bfloat16)],
    )
    return pl.pallas_call(
        functools.partial(_mixer_kernel, rows=rows),
        out_shape=(jax.ShapeDtypeStruct((b, NA_WIDTH // LANES, l, LANES), jnp.bfloat16),
                   jax.ShapeDtypeStruct((b, RET_WIDTH // LANES, l, LANES), jnp.bfloat16)),
        grid_spec=grid_spec,
        compiler_params=pltpu.CompilerParams(
            dimension_semantics=("arbitrary", "arbitrary"), vmem_limit_bytes=VMEM_LIMIT),
        name="mixers",
    )(rpb.astype(jnp.float32).reshape(-1), lg,
      proj, proj, vt, proj, proj_ctx, vct,
      proj, proj, proj, proj, proj_ctx, proj_ctx, ret_norm_g.reshape(1, RET_WIDTH))


def _out_kernel(x_ref, gate_ref, yna_ref, yret_ref, w_ref, fg_ref, o_ref, *, sub):
    tm = x_ref.shape[1]

    def mix(j):
        rows = slice(j * sub, (j + 1) * sub)
        y = jnp.concatenate([ref[0, k, rows, :] for ref in (yna_ref, yret_ref) for k in range(ref.shape[1])],
                            axis=1)
        return jnp.dot(y, w_ref[...], preferred_element_type=jnp.float32)

    def finish(j, y):
        rows = slice(j * sub, (j + 1) * sub)
        z = x_ref[0, rows, :] + gate_ref[0] * y
        ms = jnp.mean(z * z, axis=-1, keepdims=True)
        o_ref[0, rows, :] = z * lax.rsqrt(ms + EPS) * fg_ref[...]

    pending = mix(0)
    for j in range(tm // sub):
        current = pending
        if (j + 1) * sub < tm:
            pending = mix(j + 1)
        finish(j, current)


def _out_proj(x, gate, y_na, y_ret, w_out, final_g, tm, sub):
    b, l, d = x.shape
    assert l % tm == 0 and tm % sub == 0
    return pl.pallas_call(
        functools.partial(_out_kernel, sub=sub),
        out_shape=jax.ShapeDtypeStruct((b, l, d), jnp.float32),
        grid=(b, l // tm),
        in_specs=[pl.BlockSpec((1, tm, d), lambda bi, i: (bi, i, 0)),
                  pl.BlockSpec((1, 1, d), lambda bi, i: (bi, 0, 0)),
                  pl.BlockSpec((1, NA_WIDTH // LANES, tm, LANES), lambda bi, i: (bi, 0, i, 0)),
                  pl.BlockSpec((1, RET_WIDTH // LANES, tm, LANES), lambda bi, i: (bi, 0, i, 0)),
                  pl.BlockSpec((NA_WIDTH + RET_WIDTH, d), lambda bi, i: (0, 0)),
                  pl.BlockSpec((1, d), lambda bi, i: (0, 0))],
        out_specs=pl.BlockSpec((1, tm, d), lambda bi, i: (bi, i, 0)),
        compiler_params=pltpu.CompilerParams(
            dimension_semantics=("arbitrary", "arbitrary"), vmem_limit_bytes=VMEM_LIMIT),
        name="out_proj",
    )(x, gate, y_na, y_ret, w_out, final_g.reshape(1, d))


def _rotary_tables(l):
    half = RET_DIM // 2
    nf = half // 2
    t = np.arange(l)
    row = (t // GRID_W).astype(np.float64)
    col = (t % GRID_W).astype(np.float64)
    inv = ROPE_BASE ** (-np.arange(nf, dtype=np.float64) / nf)
    ang = np.concatenate([row[:, None] * inv, col[:, None] * inv], axis=-1)
    cos, sin = np.cos(ang), np.sin(ang)
    cosf = np.concatenate([cos, cos], axis=-1).astype(np.float32)
    sinf = np.concatenate([-sin, sin], axis=-1).astype(np.float32)
    return jnp.asarray(cosf), jnp.asarray(sinf)


def kernel(x, c, ctx, c_ctx, norm_g, w_ada, b_ada, w_in, na_rpb, ret_decay_fwd, ret_decay_bwd,
           ret_norm_g, w_out, final_norm_g):
    depth = norm_g.shape[0]
    assert depth == 1, "context stream update between layers is not implemented"
    b, l, d = x.shape
    rows = l // GRID_W
    i = 0

    cc = jnp.concatenate([c, c_ctx[None, :]], axis=0)
    pad = (-cc.shape[0]) % BF16_SUBLANES
    cc = jnp.pad(cc, ((0, pad), (0, 0)))
    mod, w, w_vt = _prep(cc, w_ada[i], b_ada[i], w_in[i], v_group=2)
    shift, scale, gate = (mod[:b, None, :d], mod[:b, None, d:2 * d], mod[:b, None, 2 * d:])
    shift_c, scale_c = mod[b:b + 1, None, :d], mod[b:b + 1, None, d:2 * d]
    cosf, sinf = _rotary_tables(l)
    blocks = GROUP_COLS // LANES

    lat_groups = ((0, "na_q"), (1, "plain"), (3, "silu"), (4, "rot"), (5, "rot_kscale"), (6, "plain"), (7, "silu"))
    lat_names = ("na_q", "na_k", "na_g", "r_q", "r_k", "r_v", "r_g")
    proj, vt = _in_proj(x, scale, shift, norm_g[i], w, w_vt, cosf, sinf, lat_groups, tm=1024, sub=256)

    lc = ctx.shape[1]
    ctx_groups = ((1, "plain"), (5, "kscale"), (6, "plain"))
    ctx_names = ("na_k", "r_k", "r_v")
    proj_ctx, vct = _in_proj(ctx.reshape(1, b * lc, d), scale_c, shift_c, norm_g[i], w, w_vt, cosf, sinf,
                             ctx_groups, tm=1024, sub=256)

    lg = jnp.stack([-jnp.exp(ret_decay_fwd[i].astype(jnp.float32)),
                    -jnp.exp(ret_decay_bwd[i].astype(jnp.float32))])
    y_na, y_ret = _mixers(na_rpb[i], lg, proj, vt, proj_ctx, vct, ret_norm_g[i], rows,
                          cols={name: k * blocks for k, name in enumerate(lat_names)},
                          ctx_cols={name: k * blocks for k, name in enumerate(ctx_names)})

    return _out_proj(x, gate, y_na, y_ret, w_out[i].astype(jnp.bfloat16), final_norm_g, tm=2048, sub=256)
```

```python
import functools
import math

import jax
import jax.numpy as jnp
import numpy as np
from jax import lax
from jax.experimental import pallas as pl
from jax.experimental.pallas import tpu as pltpu

D_MODEL = 1024
GRID_W = 64
NA_HEAD_DIM = 64
NA_WIDTH = 512
NA_HEADS = 8
NA_WIN_ROWS = 8
NA_WIN_COLS = 16
RET_HEADS = 4
RET_WIDTH = 512
RET_DIM = 128
RET_CHUNK = 256
ROPE_BASE = 10000.0
EPS = 1e-6

LANES = 128
MXU_TILE = 256
BF16_SUBLANES = 16
GROUP_COLS = 512
NA_Q_ROWS = 4
NA_K_ROWS = 12
NA_AHEAD = 1
RET_AHEAD = 1
MIX_BATCH = 2
MASK_VALUE = -1e30
LOG2E = math.log2(math.e)

VMEM_LIMIT = 56 * 1024 * 1024


def _silu(v):
    return v * (1.0 / (1.0 + jnp.exp(-v)))


def _split_bf16(v):
    hi = v.astype(jnp.bfloat16)
    return hi, (v - hi.astype(jnp.float32)).astype(jnp.bfloat16)


def _prep_kernel(c_ref, wa_ref, b_ref, w_ref, mod_ref, o_ref, ot_ref, *, v_group):
    m = c_ref.shape[0]
    a_hi, a_lo = _split_bf16(_silu(c_ref[...]))
    w_hi, w_lo = _split_bf16(wa_ref[...])
    both = jnp.dot(jnp.concatenate([a_hi, a_lo], axis=0), w_hi, preferred_element_type=jnp.float32)
    mod_ref[...] = (both[:m] + both[m:]) + jnp.dot(a_hi, w_lo, preferred_element_type=jnp.float32) + b_ref[...]

    w = w_ref[...]
    o_ref[...] = w.astype(o_ref.dtype)
    ot_ref[...] = w[:, v_group * GROUP_COLS:(v_group + 1) * GROUP_COLS].T.astype(ot_ref.dtype)


def _prep(cc, w_ada, b_ada, w_in, v_group):
    m, d = cc.shape
    n_mod = w_ada.shape[1]
    n = w_in.shape[1]
    steps = 8
    tn, rb = n_mod // steps, d // steps
    assert tn % LANES == 0 and rb % LANES == 0
    return pl.pallas_call(
        functools.partial(_prep_kernel, v_group=v_group),
        out_shape=(jax.ShapeDtypeStruct((m, n_mod), jnp.float32),
                   jax.ShapeDtypeStruct((d, n), jnp.bfloat16),
                   jax.ShapeDtypeStruct((GROUP_COLS, d), jnp.bfloat16)),
        grid=(steps,),
        in_specs=[pl.BlockSpec((m, d), lambda j: (0, 0)),
                  pl.BlockSpec((d, tn), lambda j: (0, j)),
                  pl.BlockSpec((1, tn), lambda j: (0, j)),
                  pl.BlockSpec((rb, n), lambda j: (j, 0))],
        out_specs=(pl.BlockSpec((m, tn), lambda j: (0, j)),
                   pl.BlockSpec((rb, n), lambda j: (j, 0)),
                   pl.BlockSpec((GROUP_COLS, rb), lambda j: (0, j))),
        compiler_params=pltpu.CompilerParams(
            dimension_semantics=("arbitrary",), vmem_limit_bytes=VMEM_LIMIT),
        name="prep",
    )(cc, w_ada, b_ada.reshape(1, n_mod), w_in)


def _rotary(acc, cosf, sinf):
    outs = []
    for h in range(GROUP_COLS // LANES):
        xh = acc[:, h * LANES:(h + 1) * LANES]
        outs.append(xh * cosf + pltpu.roll(xh, LANES // 2, 1) * sinf)
    return jnp.concatenate(outs, axis=-1)


def _proj_kernel(x_ref, scale_ref, shift_ref, g_ref, w_ref, wvt_ref, cos_ref, sin_ref, o_ref, vt_ref,
                 *, groups, sub):
    tm = x_ref.shape[1]

    def normed(j):
        x = x_ref[0, j * sub:(j + 1) * sub, :]
        ms = jnp.mean(x * x, axis=-1, keepdims=True)
        h = x * lax.rsqrt(ms + EPS) * g_ref[...]
        h = h * (1.0 + scale_ref[0]) + shift_ref[0]
        return h.astype(jnp.bfloat16)

    def project(j, hb):
        rows = slice(j * sub, (j + 1) * sub)
        vt = lax.dot_general(wvt_ref[...], hb, (((1,), (1,)), ((), ())), preferred_element_type=jnp.float32)
        vt_ref[0, :, rows] = vt.astype(vt_ref.dtype)
        for gi, (src, kind) in enumerate(groups):
            acc = jnp.dot(hb, w_ref[:, src * GROUP_COLS:(src + 1) * GROUP_COLS],
                          preferred_element_type=jnp.float32)
            if kind == "na_q":
                acc = acc * (NA_HEAD_DIM ** -0.5 * LOG2E)
            elif kind == "silu":
                acc = _silu(acc)
            elif kind == "rot":
                acc = _rotary(acc, cos_ref[rows, :], sin_ref[rows, :])
            elif kind == "rot_kscale":
                acc = _rotary(acc, cos_ref[rows, :], sin_ref[rows, :]) * (RET_DIM ** -0.5)
            elif kind == "kscale":
                acc = acc * (RET_DIM ** -0.5)
            else:
                assert kind == "plain"
            per_group = GROUP_COLS // LANES
            for k in range(per_group):
                o_ref[0, gi * per_group + k, rows, :] = acc[:, k * LANES:(k + 1) * LANES].astype(o_ref.dtype)

    pending = normed(0)
    for j in range(tm // sub):
        current = pending
        if (j + 1) * sub < tm:
            pending = normed(j + 1)
        project(j, current)


def _in_proj(x, scale, shift, norm_g, w, w_vt, cosf, sinf, groups, tm, sub):
    b, l, d = x.shape
    n_in = w.shape[1]
    n = GROUP_COLS * len(groups)
    nv = w_vt.shape[0]
    assert l % tm == 0 and tm % sub == 0
    per_batch = scale.shape[0] > 1
    mod_map = (lambda bi, i: (bi, 0, 0)) if per_batch else (lambda bi, i: (0, 0, 0))
    rotated = any(kind.startswith("rot") for _, kind in groups)
    pos_map = (lambda bi, i: (i, 0)) if rotated else (lambda bi, i: (0, 0))
    return pl.pallas_call(
        functools.partial(_proj_kernel, groups=tuple(groups), sub=sub),
        out_shape=(jax.ShapeDtypeStruct((b, n // LANES, l, LANES), jnp.bfloat16),
                   jax.ShapeDtypeStruct((b, nv, l), jnp.bfloat16)),
        grid=(b, l // tm),
        in_specs=[pl.BlockSpec((1, tm, d), lambda bi, i: (bi, i, 0)),
                  pl.BlockSpec((1, 1, d), mod_map),
                  pl.BlockSpec((1, 1, d), mod_map),
                  pl.BlockSpec((1, d), lambda bi, i: (0, 0)),
                  pl.BlockSpec((d, n_in), lambda bi, i: (0, 0)),
                  pl.BlockSpec((nv, d), lambda bi, i: (0, 0)),
                  pl.BlockSpec((tm, LANES), pos_map),
                  pl.BlockSpec((tm, LANES), pos_map)],
        out_specs=(pl.BlockSpec((1, n // LANES, tm, LANES), lambda bi, i: (bi, 0, i, 0)),
                   pl.BlockSpec((1, nv, tm), lambda bi, i: (bi, 0, i))),
        compiler_params=pltpu.CompilerParams(
            dimension_semantics=("arbitrary", "arbitrary"), vmem_limit_bytes=VMEM_LIMIT),
        name="in_proj",
    )(x, scale, shift, norm_g.reshape(1, d), w, w_vt, cosf, sinf)


def _na_window_start(rq, rows):
    kr = min(NA_WIN_ROWS, rows)
    return int(np.clip(rq - kr // 2, 0, rows - kr))


def _na_geometry(rows):
    kr = min(NA_WIN_ROWS, rows)
    n_groups = rows // NA_Q_ROWS
    assert n_groups >= 3
    geo = []
    for g in range(n_groups):
        lo = _na_window_start(NA_Q_ROWS * g, rows)
        hi = _na_window_start(NA_Q_ROWS * g + NA_Q_ROWS - 1, rows) + kr
        geo.append((lo, hi - lo, 0 if g == 0 else (2 if g == n_groups - 1 else 1)))
    assert len({n for _, n, v in geo if v == 1}) == 1 and max(n for _, n, _ in geo) <= NA_K_ROWS
    return geo


def _na_row_offsets(rows):
    kr = min(NA_WIN_ROWS, rows)
    geo = _na_geometry(rows)
    table = []
    for rep in (0, 1, len(geo) - 1):
        start, n_rows, _ = geo[rep]
        per_i = []
        for i in range(NA_Q_ROWS):
            rq = NA_Q_ROWS * rep + i
            r0 = _na_window_start(rq, rows)
            per_i.append([start + j - rq + NA_WIN_ROWS - 1 if r0 <= start + j < r0 + kr else None
                          for j in range(n_rows)])
        table.append(per_i)
    return table


def _na_live_rows(rows):
    live = []
    for per_i in _na_row_offsets(rows):
        pairs = []
        for ip in range(NA_Q_ROWS // 2):
            seen = [j for j in range(len(per_i[0]))
                    if per_i[2 * ip][j] is not None or per_i[2 * ip + 1][j] is not None]
            assert seen == list(range(seen[0], seen[-1] + 1))
            pairs.append((seen[0], seen[-1] + 1))
        live.append(pairs)
    return live


def _na_build_bias(rpb_ref, hp, bias_ref, base_ref, rows):
    n_dr = 2 * NA_WIN_ROWS - 1
    n_dc = 2 * NA_WIN_COLS - 1
    ck = lax.broadcasted_iota(jnp.int32, (GRID_W, LANES), 0)
    lane = lax.broadcasted_iota(jnp.int32, (GRID_W, LANES), 1)
    left = lane < GRID_W
    cq = jnp.where(left, lane, lane - GRID_W)
    c0 = jnp.clip(cq - NA_WIN_COLS // 2, 0, GRID_W - NA_WIN_COLS)
    valid_c = (ck >= c0) & (ck < c0 + NA_WIN_COLS)
    dc = jnp.clip(ck - cq + NA_WIN_COLS - 1, 0, n_dc - 1)

    offsets = _na_row_offsets(rows)
    users = {}
    for var in range(3):
        for j in range(len(offsets[var][0])):
            for ip in range(NA_Q_ROWS // 2):
                key = (offsets[var][2 * ip][j], offsets[var][2 * ip + 1][j])
                users.setdefault(key, []).append((var, j, ip))

    used = sorted({dr for pair in users for dr in pair if dr is not None})
    masked = jnp.full((GRID_W, LANES), MASK_VALUE, jnp.float32)
    for h in range(2):
        head = 2 * hp + h
        for dr in used:
            def body(d, acc, dr=dr, head=head):
                return jnp.where(dc == d, rpb_ref[(head * n_dr + dr) * n_dc + d], acc)
            acc = lax.fori_loop(0, n_dc, body, jnp.zeros((GRID_W, LANES), jnp.float32), unroll=True)
            base_ref[dr] = jnp.where(valid_c, acc * LOG2E, MASK_VALUE)
        for (dr_l, dr_r), dests in users.items():
            t_l = masked if dr_l is None else base_ref[dr_l]
            t_r = masked if dr_r is None else base_ref[dr_r]
            tile = t_l if dr_l == dr_r else jnp.where(left, t_l, t_r)
            for var, j, ip in dests:
                bias_ref[var, h, j * GRID_W:(j + 1) * GRID_W, ip * LANES:(ip + 1) * LANES] = tile


def _na_program(q_ref, k_ref, vt_ref, g_ref, kc_ref, vct_ref, o_ref, bias_ref, vth_ref, vcth_ref, rows):
    geo = _na_geometry(rows)
    live = _na_live_rows(rows)
    tq = NA_Q_ROWS * GRID_W
    dh = NA_HEAD_DIM

    lane = lax.broadcasted_iota(jnp.int32, (1, LANES), 1)
    head0_lanes = lane < dh
    n_ones = vth_ref.shape[1] - dh
    for h in range(2):
        vth_ref[h, :dh, :] = vt_ref[0, h * dh:(h + 1) * dh, :]
        vth_ref[h, dh:, :] = jnp.ones((n_ones, vt_ref.shape[2]), vt_ref.dtype)
        vcth_ref[h, :dh, :] = vct_ref[0, h * dh:(h + 1) * dh, :]
        vcth_ref[h, dh:, :] = jnp.ones((n_ones, vct_ref.shape[2]), vct_ref.dtype)

    kc = kc_ref[0]
    contract_last = (((1,), (1,)), ((), ()))
    tasks = [(g, h) for g in range(len(geo)) for h in range(2)]

    def scores(t):
        g, h = tasks[t]
        ws, n_rows, var = geo[g]
        nk = n_rows * GRID_W
        qg = q_ref[0, g * tq:(g + 1) * tq, :]
        kw = k_ref[0, ws * GRID_W:ws * GRID_W + nk, :]
        sel = head0_lanes if h == 0 else jnp.logical_not(head0_lanes)
        qh = jnp.where(sel, qg, jnp.zeros_like(qg))
        s_raw = lax.dot_general(kw, qh, contract_last, preferred_element_type=jnp.float32)
        s_ctx = lax.dot_general(kc, qh, contract_last, preferred_element_type=jnp.float32)
        s_loc = []
        for ip, (lo, hi) in enumerate(live[var]):
            r, c = slice(lo * GRID_W, hi * GRID_W), slice(ip * LANES, (ip + 1) * LANES)
            s_loc.append(s_raw[r, c] + bias_ref[var, h, r, c])
        return s_loc, s_ctx

    halves = []

    def attend(t, s):
        g, h = tasks[t]
        s_loc, s_ctx = s
        ws, n_rows, var = geo[g]
        n_pad = n_rows + (-n_rows) % (MXU_TILE // GRID_W)
        p_loc, p_ctx = [], []
        for ip, (lo, hi) in enumerate(live[var]):
            sc = s_ctx[:, ip * LANES:(ip + 1) * LANES]
            m = jnp.maximum(jnp.max(s_loc[ip], axis=0, keepdims=True), jnp.max(sc, axis=0, keepdims=True))
            zeros = lambda n: [jnp.zeros((n * GRID_W, LANES), jnp.bfloat16)] if n else []
            p_loc.append(jnp.concatenate(
                zeros(lo) + [jnp.exp2((s_loc[ip] - m).astype(jnp.bfloat16))] + zeros(n_pad - hi), axis=0))
            p_ctx.append(jnp.exp2((sc - m).astype(jnp.bfloat16)))
        p_loc = jnp.concatenate(p_loc, axis=1)
        p_ctx = jnp.concatenate(p_ctx, axis=1)
        nk = n_pad * GRID_W
        ot = jnp.dot(vth_ref[h, :, ws * GRID_W:ws * GRID_W + nk], p_loc, preferred_element_type=jnp.float32)
        ot = ot + jnp.dot(vcth_ref[h], p_ctx, preferred_element_type=jnp.float32)
        halves.append(ot[:dh] / ot[dh:dh + 1])
        if h == 1:
            o2 = jnp.concatenate(halves, axis=0).T
            halves.clear()
            gate = g_ref[0, g * tq:(g + 1) * tq, :].astype(jnp.float32)
            o_ref[0, g * tq:(g + 1) * tq, :] = (o2 * gate).astype(o_ref.dtype)

    return len(tasks), scores, attend


def _ret_program(lg_f, lg_b, q_ref, k_ref, v_ref, g_ref, kc_ref, vc_ref, ng_ref, o_ref, t_ref, s_ref):
    c_len = RET_CHUNK
    l = q_ref.shape[1]
    lc = kc_ref.shape[1]
    n_chunks = l // c_len
    contract_last = (((1,), (1,)), ((), ()))

    ii = lax.broadcasted_iota(jnp.int32, (c_len, c_len), 0).astype(jnp.float32)
    jj = lax.broadcasted_iota(jnp.int32, (c_len, c_len), 1).astype(jnp.float32)
    dist = ii - jj
    decay = (jnp.where(dist >= 0, jnp.exp(lg_f * jnp.maximum(dist, 0.0)), 0.0)
             + jnp.where(dist <= 0, jnp.exp(lg_b * jnp.maximum(-dist, 0.0)), 0.0))
    ic = lax.broadcasted_iota(jnp.int32, (c_len, 1), 0).astype(jnp.float32)
    kdec_f = jnp.exp(lg_f * (c_len - 1 - ic))
    kdec_b = jnp.exp(lg_b * ic)
    qdec_f = jnp.exp(lg_f * (ic + 1.0))
    qdec_b = jnp.exp(lg_b * (c_len - ic))
    one = jnp.ones((1, 1), jnp.float32)
    cdec_f = jnp.exp(one * (lg_f * c_len))
    cdec_b = jnp.exp(one * (lg_b * c_len))

    jcr = lax.broadcasted_iota(jnp.int32, (1, lc), 1).astype(jnp.float32)
    kct = kc_ref[0].astype(jnp.float32).T
    vcx = vc_ref[0]
    s_f = jnp.dot((kct * jnp.exp(lg_f * (lc - 1 - jcr))).astype(jnp.bfloat16), vcx,
                  preferred_element_type=jnp.float32)
    s_b = jnp.dot((kct * jnp.exp(lg_b * jcr)).astype(jnp.bfloat16), vcx,
                  preferred_element_type=jnp.float32)

    for c in range(n_chunks):
        rows_c = slice(c * c_len, (c + 1) * c_len)
        kt = k_ref[0, rows_c, :].astype(jnp.float32).T.astype(jnp.bfloat16)
        vf = v_ref[0, rows_c, :].astype(jnp.float32)
        vw = jnp.concatenate([(vf * kdec_f).astype(jnp.bfloat16), (vf * kdec_b).astype(jnp.bfloat16)], axis=1)
        t_ref[c] = jnp.dot(kt, vw, preferred_element_type=jnp.float32)

    for c in range(n_chunks):
        s_ref[c, :, :RET_DIM] = s_f.astype(jnp.bfloat16)
        s_f = cdec_f * s_f + t_ref[c, :, :RET_DIM]
    for c in reversed(range(n_chunks)):
        s_ref[c, :, RET_DIM:] = s_b.astype(jnp.bfloat16)
        s_b = cdec_b * s_b + t_ref[c, :, RET_DIM:]

    ng = ng_ref[...]

    def scores(c):
        rows_c = slice(c * c_len, (c + 1) * c_len)
        qc = q_ref[0, rows_c, :]
        s = lax.dot_general(qc, k_ref[0, rows_c, :], contract_last, preferred_element_type=jnp.float32) * decay
        cross = jnp.dot(qc, s_ref[c], preferred_element_type=jnp.float32)
        return s.astype(jnp.bfloat16), cross[:, :RET_DIM] * qdec_f + cross[:, RET_DIM:] * qdec_b

    def finish(c, sc):
        s, cross = sc
        rows_c = slice(c * c_len, (c + 1) * c_len)
        o = jnp.dot(s, v_ref[0, rows_c, :], preferred_element_type=jnp.float32) + cross
        o = o * lax.rsqrt(jnp.mean(o * o, axis=-1, keepdims=True) + EPS) * ng
        gate = g_ref[0, rows_c, :].astype(jnp.float32)
        o_ref[0, rows_c, :] = (o * gate).astype(o_ref.dtype)

    return n_chunks, scores, finish


def _mixer_kernel(rpb_ref, lg_ref,
                  aq_ref, ak_ref, avt_ref, ag_ref, akc_ref, avct_ref,
                  rq_ref, rk_ref, rv_ref, rg_ref, rkc_ref, rvc_ref, ng_ref,
                  yna_ref, yret_ref,
                  bias_ref, base_ref, vth_ref, vcth_ref, t_ref, s_ref, *, rows):
    i = pl.program_id(0)

    @pl.when(pl.program_id(1) == 0)
    def _():
        _na_build_bias(rpb_ref, i, bias_ref, base_ref, rows)

    nb = aq_ref.shape[0]
    lc = akc_ref.shape[2] // nb
    slab = lambda ref, j: ref.at[j]
    one = lambda ref, j: ref.at[pl.ds(j, 1)]
    cslab = lambda ref, j: ref.at[0].at[:, pl.ds(j * lc, lc), :]
    cone = lambda ref, j: ref.at[:, :, pl.ds(j * lc, lc)]

    na_tasks, ret_tasks = [], []
    for j in range(nb):
        n_ret, ret_scores, ret_finish = _ret_program(
            lg_ref[0, i], lg_ref[1, i], slab(rq_ref, j), slab(rk_ref, j), slab(rv_ref, j), slab(rg_ref, j),
            cslab(rkc_ref, j), cslab(rvc_ref, j), ng_ref, slab(yret_ref, j), t_ref.at[j], s_ref.at[j])
        n_na, na_scores, na_attend = _na_program(
            slab(aq_ref, j), slab(ak_ref, j), one(avt_ref, j), slab(ag_ref, j), cslab(akc_ref, j),
            cone(avct_ref, j), slab(yna_ref, j), bias_ref, vth_ref.at[j], vcth_ref.at[j], rows)
        na_tasks += [(na_scores, na_attend, t) for t in range(n_na)]
        ret_tasks += [(ret_scores, ret_finish, c) for c in range(n_ret)]

    issue = lambda task: task[0](task[2])
    na_q = [issue(task) for task in na_tasks[:NA_AHEAD]]
    ret_q = [issue(task) for task in ret_tasks[:RET_AHEAD]]
    stride = max(1, len(na_tasks) // len(ret_tasks))
    for t in range(max(len(na_tasks), stride * len(ret_tasks))):
        r = t // stride if t % stride == 0 else None
        if t + NA_AHEAD < len(na_tasks):
            na_q.append(issue(na_tasks[t + NA_AHEAD]))
        if r is not None and r + RET_AHEAD < len(ret_tasks):
            ret_q.append(issue(ret_tasks[r + RET_AHEAD]))
        if t < len(na_tasks):
            na_tasks[t][1](na_tasks[t][2], na_q.pop(0))
        if r is not None and r < len(ret_tasks):
            ret_tasks[r][1](ret_tasks[r][2], ret_q.pop(0))


def _mixers(rpb, lg, proj, vt, proj_ctx, vct, ret_norm_g, rows, cols, ctx_cols):
    b, _, l, _ = proj.shape
    lc = proj_ctx.shape[2] // b
    nb = MIX_BATCH
    assert NA_HEADS // 2 == RET_HEADS and b % nb == 0
    blk = lambda name: pl.BlockSpec((nb, 1, l, LANES), lambda i, bi, *_, off=cols[name]: (bi, off + i, 0, 0))
    cblk = lambda name: pl.BlockSpec((1, 1, nb * lc, LANES),
                                     lambda i, bi, *_, off=ctx_cols[name]: (0, off + i, bi, 0))
    n_chunks = l // RET_CHUNK
    grid_spec = pltpu.PrefetchScalarGridSpec(
        num_scalar_prefetch=2,
        grid=(RET_HEADS, b // nb),
        in_specs=[blk("na_q"), blk("na_k"),
                  pl.BlockSpec((nb, LANES, l), lambda i, bi, *_: (bi, i, 0)),
                  blk("na_g"), cblk("na_k"),
                  pl.BlockSpec((1, LANES, nb * lc), lambda i, bi, *_: (0, i, bi)),
                  blk("r_q"), blk("r_k"), blk("r_v"), blk("r_g"), cblk("r_k"), cblk("r_v"),
                  pl.BlockSpec((1, LANES), lambda i, bi, *_: (0, i))],
        out_specs=(pl.BlockSpec((nb, 1, l, LANES), lambda i, bi, *_: (bi, i, 0, 0)),
                   pl.BlockSpec((nb, 1, l, LANES), lambda i, bi, *_: (bi, i, 0, 0))),
        scratch_shapes=[pltpu.VMEM((3, 2, NA_K_ROWS * GRID_W, NA_Q_ROWS * GRID_W), jnp.float32),
                        pltpu.VMEM((2 * NA_WIN_ROWS - 1, GRID_W, LANES), jnp.float32),
                        pltpu.VMEM((nb, 2, NA_HEAD_DIM + BF16_SUBLANES, l), jnp.bfloat16),
                        pltpu.VMEM((nb, 2, NA_HEAD_DIM + BF16_SUBLANES, lc), jnp.bfloat16),
                        pltpu.VMEM((nb, n_chunks, RET_DIM, 2 * RET_DIM), jnp.float32),
                        pltpu.VMEM((nb, n_chunks, RET_DIM, 2 * RET_DIM), jnp.bfloat16)],
    )
    return pl.pallas_call(
        functools.partial(_mixer_kernel, rows=rows),
        out_shape=(jax.ShapeDtypeStruct((b, NA_WIDTH // LANES, l, LANES), jnp.bfloat16),
                   jax.ShapeDtypeStruct((b, RET_WIDTH // LANES, l, LANES), jnp.bfloat16)),
        grid_spec=grid_spec,
        compiler_params=pltpu.CompilerParams(
            dimension_semantics=("arbitrary", "arbitrary"), vmem_limit_bytes=VMEM_LIMIT),
        name="mixers",
    )(rpb.astype(jnp.float32).reshape(-1), lg,
      proj, proj, vt, proj, proj_ctx, vct,
      proj, proj, proj, proj, proj_ctx, proj_ctx, ret_norm_g.reshape(1, RET_WIDTH))


def _out_kernel(x_ref, gate_ref, yna_ref, yret_ref, w_ref, fg_ref, o_ref, *, sub):
    tm = x_ref.shape[1]

    def mix(j):
        rows = slice(j * sub, (j + 1) * sub)
        y = jnp.concatenate([ref[0, k, rows, :] for ref in (yna_ref, yret_ref) for k in range(ref.shape[1])],
                            axis=1)
        return jnp.dot(y, w_ref[...], preferred_element_type=jnp.float32)

    def finish(j, y):
        rows = slice(j * sub, (j + 1) * sub)
        z = x_ref[0, rows, :] + gate_ref[0] * y
        ms = jnp.mean(z * z, axis=-1, keepdims=True)
        o_ref[0, rows, :] = z * lax.rsqrt(ms + EPS) * fg_ref[...]

    pending = mix(0)
    for j in range(tm // sub):
        current = pending
        if (j + 1) * sub < tm:
            pending = mix(j + 1)
        finish(j, current)


def _out_proj(x, gate, y_na, y_ret, w_out, final_g, tm, sub):
    b, l, d = x.shape
    assert l % tm == 0 and tm % sub == 0
    return pl.pallas_call(
        functools.partial(_out_kernel, sub=sub),
        out_shape=jax.ShapeDtypeStruct((b, l, d), jnp.float32),
        grid=(b, l // tm),
        in_specs=[pl.BlockSpec((1, tm, d), lambda bi, i: (bi, i, 0)),
                  pl.BlockSpec((1, 1, d), lambda bi, i: (bi, 0, 0)),
                  pl.BlockSpec((1, NA_WIDTH // LANES, tm, LANES), lambda bi, i: (bi, 0, i, 0)),
                  pl.BlockSpec((1, RET_WIDTH // LANES, tm, LANES), lambda bi, i: (bi, 0, i, 0)),
                  pl.BlockSpec((NA_WIDTH + RET_WIDTH, d), lambda bi, i: (0, 0)),
                  pl.BlockSpec((1, d), lambda bi, i: (0, 0))],
        out_specs=pl.BlockSpec((1, tm, d), lambda bi, i: (bi, i, 0)),
        compiler_params=pltpu.CompilerParams(
            dimension_semantics=("arbitrary", "arbitrary"), vmem_limit_bytes=VMEM_LIMIT),
        name="out_proj",
    )(x, gate, y_na, y_ret, w_out, final_g.reshape(1, d))


def _rotary_tables(l):
    half = RET_DIM // 2
    nf = half // 2
    t = np.arange(l)
    row = (t // GRID_W).astype(np.float64)
    col = (t % GRID_W).astype(np.float64)
    inv = ROPE_BASE ** (-np.arange(nf, dtype=np.float64) / nf)
    ang = np.concatenate([row[:, None] * inv, col[:, None] * inv], axis=-1)
    cos, sin = np.cos(ang), np.sin(ang)
    cosf = np.concatenate([cos, cos], axis=-1).astype(np.float32)
    sinf = np.concatenate([-sin, sin], axis=-1).astype(np.float32)
    return jnp.asarray(cosf), jnp.asarray(sinf)


def kernel(x, c, ctx, c_ctx, norm_g, w_ada, b_ada, w_in, na_rpb, ret_decay_fwd, ret_decay_bwd,
           ret_norm_g, w_out, final_norm_g):
    depth = norm_g.shape[0]
    assert depth == 1, "context stream update between layers is not implemented"
    b, l, d = x.shape
    rows = l // GRID_W
    i = 0

    cc = jnp.concatenate([c, c_ctx[None, :]], axis=0)
    pad = (-cc.shape[0]) % BF16_SUBLANES
    cc = jnp.pad(cc, ((0, pad), (0, 0)))
    mod, w, w_vt = _prep(cc, w_ada[i], b_ada[i], w_in[i], v_group=2)
    shift, scale, gate = (mod[:b, None, :d], mod[:b, None, d:2 * d], mod[:b, None, 2 * d:])
    shift_c, scale_c = mod[b:b + 1, None, :d], mod[b:b + 1, None, d:2 * d]
    cosf, sinf = _rotary_tables(l)
    blocks = GROUP_COLS // LANES

    lat_groups = ((0, "na_q"), (1, "plain"), (3, "silu"), (4, "rot"), (5, "rot_kscale"), (6, "plain"), (7, "silu"))
    lat_names = ("na_q", "na_k", "na_g", "r_q", "r_k", "r_v", "r_g")
    proj, vt = _in_proj(x, scale, shift, norm_g[i], w, w_vt, cosf, sinf, lat_groups, tm=1024, sub=256)

    lc = ctx.shape[1]
    ctx_groups = ((1, "plain"), (5, "kscale"), (6, "plain"))
    ctx_names = ("na_k", "r_k", "r_v")
    proj_ctx, vct = _in_proj(ctx.reshape(1, b * lc, d), scale_c, shift_c, norm_g[i], w, w_vt, cosf, sinf,
                             ctx_groups, tm=1024, sub=256)

    lg = jnp.stack([-jnp.exp(ret_decay_fwd[i].astype(jnp.float32)),
                    -jnp.exp(ret_decay_bwd[i].astype(jnp.float32))])
    y_na, y_ret = _mixers(na_rpb[i], lg, proj, vt, proj_ctx, vct, ret_norm_g[i], rows,
                          cols={name: k * blocks for k, name in enumerate(lat_names)},
                          ctx_cols={name: k * blocks for k, name in enumerate(ctx_names)})

    return _out_proj(x, gate, y_na, y_ret, w_out[i].astype(jnp.bfloat16), final_norm_g, tm=2048, sub=256)
```

```python
import functools
import math

import jax
import jax.numpy as jnp
import numpy as np
from jax import lax
from jax.experimental import pallas as pl
from jax.experimental.pallas import tpu as pltpu

D_MODEL = 1024
GRID_W = 64
NA_HEAD_DIM = 64
NA_WIDTH = 512
NA_HEADS = 8
NA_WIN_ROWS = 8
NA_WIN_COLS = 16
RET_HEADS = 4
RET_WIDTH = 512
RET_DIM = 128
RET_CHUNK = 256
ROPE_BASE = 10000.0
EPS = 1e-6

LANES = 128
MXU_TILE = 256
BF16_SUBLANES = 16
GROUP_COLS = 512
NA_Q_ROWS = 4
NA_K_ROWS = 12
NA_AHEAD = 2
RET_AHEAD = 1
MIX_BATCH = 2
MASK_VALUE = -1e30
LOG2E = math.log2(math.e)

VMEM_LIMIT = 56 * 1024 * 1024


def _silu(v):
    return v * (1.0 / (1.0 + jnp.exp(-v)))


def _split_bf16(v):
    hi = v.astype(jnp.bfloat16)
    return hi, (v - hi.astype(jnp.float32)).astype(jnp.bfloat16)


def _prep_kernel(c_ref, wa_ref, b_ref, w_ref, mod_ref, o_ref, ot_ref, *, v_group):
    m = c_ref.shape[0]
    a_hi, a_lo = _split_bf16(_silu(c_ref[...]))
    w_hi, w_lo = _split_bf16(wa_ref[...])
    both = jnp.dot(jnp.concatenate([a_hi, a_lo], axis=0), w_hi, preferred_element_type=jnp.float32)
    mod_ref[...] = (both[:m] + both[m:]) + jnp.dot(a_hi, w_lo, preferred_element_type=jnp.float32) + b_ref[...]

    w = w_ref[...]
    o_ref[...] = w.astype(o_ref.dtype)
    ot_ref[...] = w[:, v_group * GROUP_COLS:(v_group + 1) * GROUP_COLS].T.astype(ot_ref.dtype)


def _prep(cc, w_ada, b_ada, w_in, v_group):
    m, d = cc.shape
    n_mod = w_ada.shape[1]
    n = w_in.shape[1]
    steps = 8
    tn, rb = n_mod // steps, d // steps
    assert tn % LANES == 0 and rb % LANES == 0
    return pl.pallas_call(
        functools.partial(_prep_kernel, v_group=v_group),
        out_shape=(jax.ShapeDtypeStruct((m, n_mod), jnp.float32),
                   jax.ShapeDtypeStruct((d, n), jnp.bfloat16),
                   jax.ShapeDtypeStruct((GROUP_COLS, d), jnp.bfloat16)),
        grid=(steps,),
        in_specs=[pl.BlockSpec((m, d), lambda j: (0, 0)),
                  pl.BlockSpec((d, tn), lambda j: (0, j)),
                  pl.BlockSpec((1, tn), lambda j: (0, j)),
                  pl.BlockSpec((rb, n), lambda j: (j, 0))],
        out_specs=(pl.BlockSpec((m, tn), lambda j: (0, j)),
                   pl.BlockSpec((rb, n), lambda j: (j, 0)),
                   pl.BlockSpec((GROUP_COLS, rb), lambda j: (0, j))),
        compiler_params=pltpu.CompilerParams(
            dimension_semantics=("arbitrary",), vmem_limit_bytes=VMEM_LIMIT),
        name="prep",
    )(cc, w_ada, b_ada.reshape(1, n_mod), w_in)


def _rotary(acc, cosf, sinf):
    outs = []
    for h in range(GROUP_COLS // LANES):
        xh = acc[:, h * LANES:(h + 1) * LANES]
        outs.append(xh * cosf + pltpu.roll(xh, LANES // 2, 1) * sinf)
    return jnp.concatenate(outs, axis=-1)


def _proj_kernel(x_ref, scale_ref, shift_ref, g_ref, w_ref, wvt_ref, cos_ref, sin_ref, o_ref, vt_ref,
                 *, groups, sub):
    tm = x_ref.shape[1]

    def normed(j):
        x = x_ref[0, j * sub:(j + 1) * sub, :]
        ms = jnp.mean(x * x, axis=-1, keepdims=True)
        h = x * lax.rsqrt(ms + EPS) * g_ref[...]
        h = h * (1.0 + scale_ref[0]) + shift_ref[0]
        return h.astype(jnp.bfloat16)

    def project(j, hb):
        rows = slice(j * sub, (j + 1) * sub)
        vt = lax.dot_general(wvt_ref[...], hb, (((1,), (1,)), ((), ())), preferred_element_type=jnp.float32)
        vt_ref[0, :, rows] = vt.astype(vt_ref.dtype)
        for gi, (src, kind) in enumerate(groups):
            acc = jnp.dot(hb, w_ref[:, src * GROUP_COLS:(src + 1) * GROUP_COLS],
                          preferred_element_type=jnp.float32)
            if kind == "na_q":
                acc = acc * (NA_HEAD_DIM ** -0.5 * LOG2E)
            elif kind == "silu":
                acc = _silu(acc)
            elif kind == "rot":
                acc = _rotary(acc, cos_ref[rows, :], sin_ref[rows, :])
            elif kind == "rot_kscale":
                acc = _rotary(acc, cos_ref[rows, :], sin_ref[rows, :]) * (RET_DIM ** -0.5)
            elif kind == "kscale":
                acc = acc * (RET_DIM ** -0.5)
            else:
                assert kind == "plain"
            per_group = GROUP_COLS // LANES
            for k in range(per_group):
                o_ref[0, gi * per_group + k, rows, :] = acc[:, k * LANES:(k + 1) * LANES].astype(o_ref.dtype)

    pending = normed(0)
    for j in range(tm // sub):
        current = pending
        if (j + 1) * sub < tm:
            pending = normed(j + 1)
        project(j, current)


def _in_proj(x, scale, shift, norm_g, w, w_vt, cosf, sinf, groups, tm, sub):
    b, l, d = x.shape
    n_in = w.shape[1]
    n = GROUP_COLS * len(groups)
    nv = w_vt.shape[0]
    assert l % tm == 0 and tm % sub == 0
    per_batch = scale.shape[0] > 1
    mod_map = (lambda bi, i: (bi, 0, 0)) if per_batch else (lambda bi, i: (0, 0, 0))
    rotated = any(kind.startswith("rot") for _, kind in groups)
    pos_map = (lambda bi, i: (i, 0)) if rotated else (lambda bi, i: (0, 0))
    return pl.pallas_call(
        functools.partial(_proj_kernel, groups=tuple(groups), sub=sub),
        out_shape=(jax.ShapeDtypeStruct((b, n // LANES, l, LANES), jnp.bfloat16),
                   jax.ShapeDtypeStruct((b, nv, l), jnp.bfloat16)),
        grid=(b, l // tm),
        in_specs=[pl.BlockSpec((1, tm, d), lambda bi, i: (bi, i, 0)),
                  pl.BlockSpec((1, 1, d), mod_map),
                  pl.BlockSpec((1, 1, d), mod_map),
                  pl.BlockSpec((1, d), lambda bi, i: (0, 0)),
                  pl.BlockSpec((d, n_in), lambda bi, i: (0, 0)),
                  pl.BlockSpec((nv, d), lambda bi, i: (0, 0)),
                  pl.BlockSpec((tm, LANES), pos_map),
                  pl.BlockSpec((tm, LANES), pos_map)],
        out_specs=(pl.BlockSpec((1, n // LANES, tm, LANES), lambda bi, i: (bi, 0, i, 0)),
                   pl.BlockSpec((1, nv, tm), lambda bi, i: (bi, 0, i))),
        compiler_params=pltpu.CompilerParams(
            dimension_semantics=("arbitrary", "arbitrary"), vmem_limit_bytes=VMEM_LIMIT),
        name="in_proj",
    )(x, scale, shift, norm_g.reshape(1, d), w, w_vt, cosf, sinf)


def _na_window_start(rq, rows):
    kr = min(NA_WIN_ROWS, rows)
    return int(np.clip(rq - kr // 2, 0, rows - kr))


def _na_geometry(rows):
    kr = min(NA_WIN_ROWS, rows)
    n_groups = rows // NA_Q_ROWS
    assert n_groups >= 3
    geo = []
    for g in range(n_groups):
        lo = _na_window_start(NA_Q_ROWS * g, rows)
        hi = _na_window_start(NA_Q_ROWS * g + NA_Q_ROWS - 1, rows) + kr
        geo.append((lo, hi - lo, 0 if g == 0 else (2 if g == n_groups - 1 else 1)))
    assert len({n for _, n, v in geo if v == 1}) == 1 and max(n for _, n, _ in geo) <= NA_K_ROWS
    return geo


def _na_row_offsets(rows):
    kr = min(NA_WIN_ROWS, rows)
    geo = _na_geometry(rows)
    table = []
    for rep in (0, 1, len(geo) - 1):
        start, n_rows, _ = geo[rep]
        per_i = []
        for i in range(NA_Q_ROWS):
            rq = NA_Q_ROWS * rep + i
            r0 = _na_window_start(rq, rows)
            per_i.append([start + j - rq + NA_WIN_ROWS - 1 if r0 <= start + j < r0 + kr else None
                          for j in range(n_rows)])
        table.append(per_i)
    return table


def _na_live_rows(rows):
    live = []
    for per_i in _na_row_offsets(rows):
        pairs = []
        for ip in range(NA_Q_ROWS // 2):
            seen = [j for j in range(len(per_i[0]))
                    if per_i[2 * ip][j] is not None or per_i[2 * ip + 1][j] is not None]
            assert seen == list(range(seen[0], seen[-1] + 1))
            pairs.append((seen[0], seen[-1] + 1))
        live.append(pairs)
    return live


def _na_build_bias(rpb_ref, hp, bias_ref, base_ref, rows):
    n_dr = 2 * NA_WIN_ROWS - 1
    n_dc = 2 * NA_WIN_COLS - 1
    ck = lax.broadcasted_iota(jnp.int32, (GRID_W, LANES), 0)
    lane = lax.broadcasted_iota(jnp.int32, (GRID_W, LANES), 1)
    left = lane < GRID_W
    cq = jnp.where(left, lane, lane - GRID_W)
    c0 = jnp.clip(cq - NA_WIN_COLS // 2, 0, GRID_W - NA_WIN_COLS)
    valid_c = (ck >= c0) & (ck < c0 + NA_WIN_COLS)
    dc = jnp.clip(ck - cq + NA_WIN_COLS - 1, 0, n_dc - 1)

    offsets = _na_row_offsets(rows)
    users = {}
    for var in range(3):
        for j in range(len(offsets[var][0])):
            for ip in range(NA_Q_ROWS // 2):
                key = (offsets[var][2 * ip][j], offsets[var][2 * ip + 1][j])
                users.setdefault(key, []).append((var, j, ip))

    used = sorted({dr for pair in users for dr in pair if dr is not None})
    masked = jnp.full((GRID_W, LANES), MASK_VALUE, jnp.float32)
    for h in range(2):
        head = 2 * hp + h
        for dr in used:
            def body(d, acc, dr=dr, head=head):
                return jnp.where(dc == d, rpb_ref[(head * n_dr + dr) * n_dc + d], acc)
            acc = lax.fori_loop(0, n_dc, body, jnp.zeros((GRID_W, LANES), jnp.float32), unroll=True)
            base_ref[dr] = jnp.where(valid_c, acc * LOG2E, MASK_VALUE)
        for (dr_l, dr_r), dests in users.items():
            t_l = masked if dr_l is None else base_ref[dr_l]
            t_r = masked if dr_r is None else base_ref[dr_r]
            tile = t_l if dr_l == dr_r else jnp.where(left, t_l, t_r)
            for var, j, ip in dests:
                bias_ref[var, h, j * GRID_W:(j + 1) * GRID_W, ip * LANES:(ip + 1) * LANES] = tile


def _na_program(q_ref, k_ref, vt_ref, g_ref, kc_ref, vct_ref, o_ref, bias_ref, vth_ref, vcth_ref, rows):
    geo = _na_geometry(rows)
    live = _na_live_rows(rows)
    tq = NA_Q_ROWS * GRID_W
    dh = NA_HEAD_DIM

    lane = lax.broadcasted_iota(jnp.int32, (1, LANES), 1)
    head0_lanes = lane < dh
    n_ones = vth_ref.shape[1] - dh
    for h in range(2):
        vth_ref[h, :dh, :] = vt_ref[0, h * dh:(h + 1) * dh, :]
        vth_ref[h, dh:, :] = jnp.ones((n_ones, vt_ref.shape[2]), vt_ref.dtype)
        vcth_ref[h, :dh, :] = vct_ref[0, h * dh:(h + 1) * dh, :]
        vcth_ref[h, dh:, :] = jnp.ones((n_ones, vct_ref.shape[2]), vct_ref.dtype)

    kc = kc_ref[0]
    contract_last = (((1,), (1,)), ((), ()))
    tasks = [(g, h) for g in range(len(geo)) for h in range(2)]

    def scores(t):
        g, h = tasks[t]
        ws, n_rows, var = geo[g]
        nk = n_rows * GRID_W
        qg = q_ref[0, g * tq:(g + 1) * tq, :]
        kw = k_ref[0, ws * GRID_W:ws * GRID_W + nk, :]
        sel = head0_lanes if h == 0 else jnp.logical_not(head0_lanes)
        qh = jnp.where(sel, qg, jnp.zeros_like(qg))
        s_raw = lax.dot_general(kw, qh, contract_last, preferred_element_type=jnp.float32)
        s_ctx = lax.dot_general(kc, qh, contract_last, preferred_element_type=jnp.float32)
        s_loc = []
        for ip, (lo, hi) in enumerate(live[var]):
            r, c = slice(lo * GRID_W, hi * GRID_W), slice(ip * LANES, (ip + 1) * LANES)
            s_loc.append(s_raw[r, c] + bias_ref[var, h, r, c])
        return s_loc, s_ctx

    halves = []

    def attend(t, s):
        g, h = tasks[t]
        s_loc, s_ctx = s
        ws, n_rows, var = geo[g]
        n_pad = n_rows + (-n_rows) % (MXU_TILE // GRID_W)
        p_loc, p_ctx = [], []
        for ip, (lo, hi) in enumerate(live[var]):
            sc = s_ctx[:, ip * LANES:(ip + 1) * LANES]
            m = jnp.maximum(jnp.max(s_loc[ip], axis=0, keepdims=True), jnp.max(sc, axis=0, keepdims=True))
            zeros = lambda n: [jnp.zeros((n * GRID_W, LANES), jnp.bfloat16)] if n else []
            p_loc.append(jnp.concatenate(
                zeros(lo) + [jnp.exp2((s_loc[ip] - m).astype(jnp.bfloat16))] + zeros(n_pad - hi), axis=0))
            p_ctx.append(jnp.exp2((sc - m).astype(jnp.bfloat16)))
        p_loc = jnp.concatenate(p_loc, axis=1)
        p_ctx = jnp.concatenate(p_ctx, axis=1)
        nk = n_pad * GRID_W
        ot = jnp.dot(vth_ref[h, :, ws * GRID_W:ws * GRID_W + nk], p_loc, preferred_element_type=jnp.float32)
        ot = ot + jnp.dot(vcth_ref[h], p_ctx, preferred_element_type=jnp.float32)
        halves.append(ot[:dh] / ot[dh:dh + 1])
        if h == 1:
            o2 = jnp.concatenate(halves, axis=0).T
            halves.clear()
            gate = g_ref[0, g * tq:(g + 1) * tq, :].astype(jnp.float32)
            o_ref[0, g * tq:(g + 1) * tq, :] = (o2 * gate).astype(o_ref.dtype)

    return len(tasks), scores, attend


def _ret_program(lg_f, lg_b, q_ref, k_ref, v_ref, g_ref, kc_ref, vc_ref, ng_ref, o_ref, t_ref, s_ref):
    c_len = RET_CHUNK
    l = q_ref.shape[1]
    lc = kc_ref.shape[1]
    n_chunks = l // c_len
    contract_last = (((1,), (1,)), ((), ()))

    ii = lax.broadcasted_iota(jnp.int32, (c_len, c_len), 0).astype(jnp.float32)
    jj = lax.broadcasted_iota(jnp.int32, (c_len, c_len), 1).astype(jnp.float32)
    dist = ii - jj
    decay = (jnp.where(dist >= 0, jnp.exp(lg_f * jnp.maximum(dist, 0.0)), 0.0)
             + jnp.where(dist <= 0, jnp.exp(lg_b * jnp.maximum(-dist, 0.0)), 0.0))
    ic = lax.broadcasted_iota(jnp.int32, (c_len, 1), 0).astype(jnp.float32)
    kdec_f = jnp.exp(lg_f * (c_len - 1 - ic))
    kdec_b = jnp.exp(lg_b * ic)
    qdec_f = jnp.exp(lg_f * (ic + 1.0))
    qdec_b = jnp.exp(lg_b * (c_len - ic))
    one = jnp.ones((1, 1), jnp.float32)
    cdec_f = jnp.exp(one * (lg_f * c_len))
    cdec_b = jnp.exp(one * (lg_b * c_len))

    jcr = lax.broadcasted_iota(jnp.int32, (1, lc), 1).astype(jnp.float32)
    kct = kc_ref[0].astype(jnp.float32).T
    vcx = vc_ref[0]
    s_f = jnp.dot((kct * jnp.exp(lg_f * (lc - 1 - jcr))).astype(jnp.bfloat16), vcx,
                  preferred_element_type=jnp.float32)
    s_b = jnp.dot((kct * jnp.exp(lg_b * jcr)).astype(jnp.bfloat16), vcx,
                  preferred_element_type=jnp.float32)

    for c in range(n_chunks):
        rows_c = slice(c * c_len, (c + 1) * c_len)
        kt = k_ref[0, rows_c, :].astype(jnp.float32).T.astype(jnp.bfloat16)
        vf = v_ref[0, rows_c, :].astype(jnp.float32)
        vw = jnp.concatenate([(vf * kdec_f).astype(jnp.bfloat16), (vf * kdec_b).astype(jnp.bfloat16)], axis=1)
        t_ref[c] = jnp.dot(kt, vw, preferred_element_type=jnp.float32)

    for c in range(n_chunks):
        s_ref[c, :, :RET_DIM] = s_f.astype(jnp.bfloat16)
        s_f = cdec_f * s_f + t_ref[c, :, :RET_DIM]
    for c in reversed(range(n_chunks)):
        s_ref[c, :, RET_DIM:] = s_b.astype(jnp.bfloat16)
        s_b = cdec_b * s_b + t_ref[c, :, RET_DIM:]

    ng = ng_ref[...]

    def scores(c):
        rows_c = slice(c * c_len, (c + 1) * c_len)
        qc = q_ref[0, rows_c, :]
        s = lax.dot_general(qc, k_ref[0, rows_c, :], contract_last, preferred_element_type=jnp.float32) * decay
        cross = jnp.dot(qc, s_ref[c], preferred_element_type=jnp.float32)
        return s.astype(jnp.bfloat16), cross[:, :RET_DIM] * qdec_f + cross[:, RET_DIM:] * qdec_b

    def finish(c, sc):
        s, cross = sc
        rows_c = slice(c * c_len, (c + 1) * c_len)
        o = jnp.dot(s, v_ref[0, rows_c, :], preferred_element_type=jnp.float32) + cross
        o = o * lax.rsqrt(jnp.mean(o * o, axis=-1, keepdims=True) + EPS) * ng
        gate = g_ref[0, rows_c, :].astype(jnp.float32)
        o_ref[0, rows_c, :] = (o * gate).astype(o_ref.dtype)

    return n_chunks, scores, finish


def _mixer_kernel(rpb_ref, lg_ref,
                  aq_ref, ak_ref, avt_ref, ag_ref, akc_ref, avct_ref,
                  rq_ref, rk_ref, rv_ref, rg_ref, rkc_ref, rvc_ref, ng_ref,
                  yna_ref, yret_ref,
                  bias_ref, base_ref, vth_ref, vcth_ref, t_ref, s_ref, *, rows):
    i = pl.program_id(0)

    @pl.when(pl.program_id(1) == 0)
    def _():
        _na_build_bias(rpb_ref, i, bias_ref, base_ref, rows)

    nb = aq_ref.shape[0]
    lc = akc_ref.shape[2] // nb
    slab = lambda ref, j: ref.at[j]
    one = lambda ref, j: ref.at[pl.ds(j, 1)]
    cslab = lambda ref, j: ref.at[0].at[:, pl.ds(j * lc, lc), :]
    cone = lambda ref, j: ref.at[:, :, pl.ds(j * lc, lc)]

    na_tasks, ret_tasks = [], []
    for j in range(nb):
        n_ret, ret_scores, ret_finish = _ret_program(
            lg_ref[0, i], lg_ref[1, i], slab(rq_ref, j), slab(rk_ref, j), slab(rv_ref, j), slab(rg_ref, j),
            cslab(rkc_ref, j), cslab(rvc_ref, j), ng_ref, slab(yret_ref, j), t_ref.at[j], s_ref.at[j])
        n_na, na_scores, na_attend = _na_program(
            slab(aq_ref, j), slab(ak_ref, j), one(avt_ref, j), slab(ag_ref, j), cslab(akc_ref, j),
            cone(avct_ref, j), slab(yna_ref, j), bias_ref, vth_ref.at[j], vcth_ref.at[j], rows)
        na_tasks += [(na_scores, na_attend, t) for t in range(n_na)]
        ret_tasks += [(ret_scores, ret_finish, c) for c in range(n_ret)]

    issue = lambda task: task[0](task[2])
    na_q = [issue(task) for task in na_tasks[:NA_AHEAD]]
    ret_q = [issue(task) for task in ret_tasks[:RET_AHEAD]]
    stride = max(1, len(na_tasks) // len(ret_tasks))
    for t in range(max(len(na_tasks), stride * len(ret_tasks))):
        r = t // stride if t % stride == 0 else None
        if t + NA_AHEAD < len(na_tasks):
            na_q.append(issue(na_tasks[t + NA_AHEAD]))
        if r is not None and r + RET_AHEAD < len(ret_tasks):
            ret_q.append(issue(ret_tasks[r + RET_AHEAD]))
        if t < len(na_tasks):
            na_tasks[t][1](na_tasks[t][2], na_q.pop(0))
        if r is not None and r < len(ret_tasks):
            ret_tasks[r][1](ret_tasks[r][2], ret_q.pop(0))


def _mixers(rpb, lg, proj, vt, proj_ctx, vct, ret_norm_g, rows, cols, ctx_cols):
    b, _, l, _ = proj.shape
    lc = proj_ctx.shape[2] // b
    nb = MIX_BATCH
    assert NA_HEADS // 2 == RET_HEADS and b % nb == 0
    blk = lambda name: pl.BlockSpec((nb, 1, l, LANES), lambda i, bi, *_, off=cols[name]: (bi, off + i, 0, 0))
    cblk = lambda name: pl.BlockSpec((1, 1, nb * lc, LANES),
                                     lambda i, bi, *_, off=ctx_cols[name]: (0, off + i, bi, 0))
    n_chunks = l // RET_CHUNK
    grid_spec = pltpu.PrefetchScalarGridSpec(
        num_scalar_prefetch=2,
        grid=(RET_HEADS, b // nb),
        in_specs=[blk("na_q"), blk("na_k"),
                  pl.BlockSpec((nb, LANES, l), lambda i, bi, *_: (bi, i, 0)),
                  blk("na_g"), cblk("na_k"),
                  pl.BlockSpec((1, LANES, nb * lc), lambda i, bi, *_: (0, i, bi)),
                  blk("r_q"), blk("r_k"), blk("r_v"), blk("r_g"), cblk("r_k"), cblk("r_v"),
                  pl.BlockSpec((1, LANES), lambda i, bi, *_: (0, i))],
        out_specs=(pl.BlockSpec((nb, 1, l, LANES), lambda i, bi, *_: (bi, i, 0, 0)),
                   pl.BlockSpec((nb, 1, l, LANES), lambda i, bi, *_: (bi, i, 0, 0))),
        scratch_shapes=[pltpu.VMEM((3, 2, NA_K_ROWS * GRID_W, NA_Q_ROWS * GRID_W), jnp.float32),
                        pltpu.VMEM((2 * NA_WIN_ROWS - 1, GRID_W, LANES), jnp.float32),
                        pltpu.VMEM((nb, 2, NA_HEAD_DIM + BF16_SUBLANES, l), jnp.bfloat16),
                        pltpu.VMEM((nb, 2, NA_HEAD_DIM + BF16_SUBLANES, lc), jnp.bfloat16),
                        pltpu.VMEM((nb, n_chunks, RET_DIM, 2 * RET_DIM), jnp.float32),
                        pltpu.VMEM((nb, n_chunks, RET_DIM, 2 * RET_DIM), jnp.bfloat16)],
    )
    return pl.pallas_call(
        functools.partial(_mixer_kernel, rows=rows),
        out_shape=(jax.ShapeDtypeStruct((b, NA_WIDTH // LANES, l, LANES), jnp.bfloat16),
                   jax.ShapeDtypeStruct((b, RET_WIDTH // LANES, l, LANES), jnp.bfloat16)),
        grid_spec=grid_spec,
        compiler_params=pltpu.CompilerParams(
            dimension_semantics=("arbitrary", "arbitrary"), vmem_limit_bytes=VMEM_LIMIT),
        name="mixers",
    )(rpb.astype(jnp.float32).reshape(-1), lg,
      proj, proj, vt, proj, proj_ctx, vct,
      proj, proj, proj, proj, proj_ctx, proj_ctx, ret_norm_g.reshape(1, RET_WIDTH))


def _out_kernel(x_ref, gate_ref, yna_ref, yret_ref, w_ref, fg_ref, o_ref, *, sub):
    tm = x_ref.shape[1]

    def mix(j):
        rows = slice(j * sub, (j + 1) * sub)
        y = jnp.concatenate([ref[0, k, rows, :] for ref in (yna_ref, yret_ref) for k in range(ref.shape[1])],
                            axis=1)
        return jnp.dot(y, w_ref[...], preferred_element_type=jnp.float32)

    def finish(j, y):
        rows = slice(j * sub, (j + 1) * sub)
        z = x_ref[0, rows, :] + gate_ref[0] * y
        ms = jnp.mean(z * z, axis=-1, keepdims=True)
        o_ref[0, rows, :] = z * lax.rsqrt(ms + EPS) * fg_ref[...]

    pending = mix(0)
    for j in range(tm // sub):
        current = pending
        if (j + 1) * sub < tm:
            pending = mix(j + 1)
        finish(j, current)


def _out_proj(x, gate, y_na, y_ret, w_out, final_g, tm, sub):
    b, l, d = x.shape
    assert l % tm == 0 and tm % sub == 0
    return pl.pallas_call(
        functools.partial(_out_kernel, sub=sub),
        out_shape=jax.ShapeDtypeStruct((b, l, d), jnp.float32),
        grid=(b, l // tm),
        in_specs=[pl.BlockSpec((1, tm, d), lambda bi, i: (bi, i, 0)),
                  pl.BlockSpec((1, 1, d), lambda bi, i: (bi, 0, 0)),
                  pl.BlockSpec((1, NA_WIDTH // LANES, tm, LANES), lambda bi, i: (bi, 0, i, 0)),
                  pl.BlockSpec((1, RET_WIDTH // LANES, tm, LANES), lambda bi, i: (bi, 0, i, 0)),
                  pl.BlockSpec((NA_WIDTH + RET_WIDTH, d), lambda bi, i: (0, 0)),
                  pl.BlockSpec((1, d), lambda bi, i: (0, 0))],
        out_specs=pl.BlockSpec((1, tm, d), lambda bi, i: (bi, i, 0)),
        compiler_params=pltpu.CompilerParams(
            dimension_semantics=("arbitrary", "arbitrary"), vmem_limit_bytes=VMEM_LIMIT),
        name="out_proj",
    )(x, gate, y_na, y_ret, w_out, final_g.reshape(1, d))


def _rotary_tables(l):
    half = RET_DIM // 2
    nf = half // 2
    t = np.arange(l)
    row = (t // GRID_W).astype(np.float64)
    col = (t % GRID_W).astype(np.float64)
    inv = ROPE_BASE ** (-np.arange(nf, dtype=np.float64) / nf)
    ang = np.concatenate([row[:, None] * inv, col[:, None] * inv], axis=-1)
    cos, sin = np.cos(ang), np.sin(ang)
    cosf = np.concatenate([cos, cos], axis=-1).astype(np.float32)
    sinf = np.concatenate([-sin, sin], axis=-1).astype(np.float32)
    return jnp.asarray(cosf), jnp.asarray(sinf)


def kernel(x, c, ctx, c_ctx, norm_g, w_ada, b_ada, w_in, na_rpb, ret_decay_fwd, ret_decay_bwd,
           ret_norm_g, w_out, final_norm_g):
    depth = norm_g.shape[0]
    assert depth == 1, "context stream update between layers is not implemented"
    b, l, d = x.shape
    rows = l // GRID_W
    i = 0

    cc = jnp.concatenate([c, c_ctx[None, :]], axis=0)
    pad = (-cc.shape[0]) % BF16_SUBLANES
    cc = jnp.pad(cc, ((0, pad), (0, 0)))
    mod, w, w_vt = _prep(cc, w_ada[i], b_ada[i], w_in[i], v_group=2)
    shift, scale, gate = (mod[:b, None, :d], mod[:b, None, d:2 * d], mod[:b, None, 2 * d:])
    shift_c, scale_c = mod[b:b + 1, None, :d], mod[b:b + 1, None, d:2 * d]
    cosf, sinf = _rotary_tables(l)
    blocks = GROUP_COLS // LANES

    lat_groups = ((0, "na_q"), (1, "plain"), (3, "silu"), (4, "rot"), (5, "rot_kscale"), (6, "plain"), (7, "silu"))
    lat_names = ("na_q", "na_k", "na_g", "r_q", "r_k", "r_v", "r_g")
    proj, vt = _in_proj(x, scale, shift, norm_g[i], w, w_vt, cosf, sinf, lat_groups, tm=1024, sub=256)

    lc = ctx.shape[1]
    ctx_groups = ((1, "plain"), (5, "kscale"), (6, "plain"))
    ctx_names = ("na_k", "r_k", "r_v")
    proj_ctx, vct = _in_proj(ctx.reshape(1, b * lc, d), scale_c, shift_c, norm_g[i], w, w_vt, cosf, sinf,
                             ctx_groups, tm=1024, sub=256)

    lg = jnp.stack([-jnp.exp(ret_decay_fwd[i].astype(jnp.float32)),
                    -jnp.exp(ret_decay_bwd[i].astype(jnp.float32))])
    y_na, y_ret = _mixers(na_rpb[i], lg, proj, vt, proj_ctx, vct, ret_norm_g[i], rows,
                          cols={name: k * blocks for k, name in enumerate(lat_names)},
                          ctx_cols={name: k * blocks for k, name in enumerate(ctx_names)})

    return _out_proj(x, gate, y_na, y_ret, w_out[i].astype(jnp.bfloat16), final_norm_g, tm=2048, sub=256)
```

```python
import functools
import math

import jax
import jax.numpy as jnp
import numpy as np
from jax import lax
from jax.experimental import pallas as pl
from jax.experimental.pallas import tpu as pltpu

D_MODEL = 1024
GRID_W = 64
NA_HEAD_DIM = 64
NA_WIDTH = 512
NA_HEADS = 8
NA_WIN_ROWS = 8
NA_WIN_COLS = 16
RET_HEADS = 4
RET_WIDTH = 512
RET_DIM = 128
RET_CHUNK = 256
ROPE_BASE = 10000.0
EPS = 1e-6

LANES = 128
MXU_TILE = 256
BF16_SUBLANES = 16
GROUP_COLS = 512
NA_Q_ROWS = 4
NA_K_ROWS = 12
NA_AHEAD = 2
RET_AHEAD = 1
MIX_BATCH = 2
MASK_VALUE = -1e30
LOG2E = math.log2(math.e)

VMEM_LIMIT = 56 * 1024 * 1024


def _silu(v):
    return v * (1.0 / (1.0 + jnp.exp(-v)))


def _split_bf16(v):
    hi = v.astype(jnp.bfloat16)
    return hi, (v - hi.astype(jnp.float32)).astype(jnp.bfloat16)


def _prep_kernel(c_ref, wa_ref, b_ref, w_ref, mod_ref, o_ref, ot_ref, *, v_group):
    m = c_ref.shape[0]
    a_hi, a_lo = _split_bf16(_silu(c_ref[...]))
    w_hi, w_lo = _split_bf16(wa_ref[...])
    both = jnp.dot(jnp.concatenate([a_hi, a_lo], axis=0), w_hi, preferred_element_type=jnp.float32)
    mod_ref[...] = (both[:m] + both[m:]) + jnp.dot(a_hi, w_lo, preferred_element_type=jnp.float32) + b_ref[...]

    w = w_ref[...]
    o_ref[...] = w.astype(o_ref.dtype)
    ot_ref[...] = w[:, v_group * GROUP_COLS:(v_group + 1) * GROUP_COLS].T.astype(ot_ref.dtype)


def _prep(cc, w_ada, b_ada, w_in, v_group):
    m, d = cc.shape
    n_mod = w_ada.shape[1]
    n = w_in.shape[1]
    steps = 8
    tn, rb = n_mod // steps, d // steps
    assert tn % LANES == 0 and rb % LANES == 0
    return pl.pallas_call(
        functools.partial(_prep_kernel, v_group=v_group),
        out_shape=(jax.ShapeDtypeStruct((m, n_mod), jnp.float32),
                   jax.ShapeDtypeStruct((d, n), jnp.bfloat16),
                   jax.ShapeDtypeStruct((GROUP_COLS, d), jnp.bfloat16)),
        grid=(steps,),
        in_specs=[pl.BlockSpec((m, d), lambda j: (0, 0)),
                  pl.BlockSpec((d, tn), lambda j: (0, j)),
                  pl.BlockSpec((1, tn), lambda j: (0, j)),
                  pl.BlockSpec((rb, n), lambda j: (j, 0))],
        out_specs=(pl.BlockSpec((m, tn), lambda j: (0, j)),
                   pl.BlockSpec((rb, n), lambda j: (j, 0)),
                   pl.BlockSpec((GROUP_COLS, rb), lambda j: (0, j))),
        compiler_params=pltpu.CompilerParams(
            dimension_semantics=("arbitrary",), vmem_limit_bytes=VMEM_LIMIT),
        name="prep",
    )(cc, w_ada, b_ada.reshape(1, n_mod), w_in)


def _rotary(acc, cosf, sinf):
    outs = []
    for h in range(GROUP_COLS // LANES):
        xh = acc[:, h * LANES:(h + 1) * LANES]
        outs.append(xh * cosf + pltpu.roll(xh, LANES // 2, 1) * sinf)
    return jnp.concatenate(outs, axis=-1)


def _proj_kernel(x_ref, scale_ref, shift_ref, g_ref, w_ref, wvt_ref, cos_ref, sin_ref, o_ref, vt_ref,
                 *, groups, sub):
    tm = x_ref.shape[1]

    def normed(j):
        x = x_ref[0, j * sub:(j + 1) * sub, :]
        ms = jnp.mean(x * x, axis=-1, keepdims=True)
        h = x * lax.rsqrt(ms + EPS) * g_ref[...]
        h = h * (1.0 + scale_ref[0]) + shift_ref[0]
        return h.astype(jnp.bfloat16)

    def project(j, hb):
        rows = slice(j * sub, (j + 1) * sub)
        vt = lax.dot_general(wvt_ref[...], hb, (((1,), (1,)), ((), ())), preferred_element_type=jnp.float32)
        vt_ref[0, :, rows] = vt.astype(vt_ref.dtype)
        for gi, (src, kind) in enumerate(groups):
            acc = jnp.dot(hb, w_ref[:, src * GROUP_COLS:(src + 1) * GROUP_COLS],
                          preferred_element_type=jnp.float32)
            if kind == "na_q":
                acc = acc * (NA_HEAD_DIM ** -0.5 * LOG2E)
            elif kind == "silu":
                acc = _silu(acc)
            elif kind == "rot":
                acc = _rotary(acc, cos_ref[rows, :], sin_ref[rows, :])
            elif kind == "rot_kscale":
                acc = _rotary(acc, cos_ref[rows, :], sin_ref[rows, :]) * (RET_DIM ** -0.5)
            elif kind == "kscale":
                acc = acc * (RET_DIM ** -0.5)
            else:
                assert kind == "plain"
            per_group = GROUP_COLS // LANES
            for k in range(per_group):
                o_ref[0, gi * per_group + k, rows, :] = acc[:, k * LANES:(k + 1) * LANES].astype(o_ref.dtype)

    pending = normed(0)
    for j in range(tm // sub):
        current = pending
        if (j + 1) * sub < tm:
            pending = normed(j + 1)
        project(j, current)


def _in_proj(x, scale, shift, norm_g, w, w_vt, cosf, sinf, groups, tm, sub):
    b, l, d = x.shape
    n_in = w.shape[1]
    n = GROUP_COLS * len(groups)
    nv = w_vt.shape[0]
    assert l % tm == 0 and tm % sub == 0
    per_batch = scale.shape[0] > 1
    mod_map = (lambda bi, i: (bi, 0, 0)) if per_batch else (lambda bi, i: (0, 0, 0))
    rotated = any(kind.startswith("rot") for _, kind in groups)
    pos_map = (lambda bi, i: (i, 0)) if rotated else (lambda bi, i: (0, 0))
    return pl.pallas_call(
        functools.partial(_proj_kernel, groups=tuple(groups), sub=sub),
        out_shape=(jax.ShapeDtypeStruct((b, n // LANES, l, LANES), jnp.bfloat16),
                   jax.ShapeDtypeStruct((b, nv, l), jnp.bfloat16)),
        grid=(b, l // tm),
        in_specs=[pl.BlockSpec((1, tm, d), lambda bi, i: (bi, i, 0)),
                  pl.BlockSpec((1, 1, d), mod_map),
                  pl.BlockSpec((1, 1, d), mod_map),
                  pl.BlockSpec((1, d), lambda bi, i: (0, 0)),
                  pl.BlockSpec((d, n_in), lambda bi, i: (0, 0)),
                  pl.BlockSpec((nv, d), lambda bi, i: (0, 0)),
                  pl.BlockSpec((tm, LANES), pos_map),
                  pl.BlockSpec((tm, LANES), pos_map)],
        out_specs=(pl.BlockSpec((1, n // LANES, tm, LANES), lambda bi, i: (bi, 0, i, 0)),
                   pl.BlockSpec((1, nv, tm), lambda bi, i: (bi, 0, i))),
        compiler_params=pltpu.CompilerParams(
            dimension_semantics=("arbitrary", "arbitrary"), vmem_limit_bytes=VMEM_LIMIT),
        name="in_proj",
    )(x, scale, shift, norm_g.reshape(1, d), w, w_vt, cosf, sinf)


def _na_window_start(rq, rows):
    kr = min(NA_WIN_ROWS, rows)
    return int(np.clip(rq - kr // 2, 0, rows - kr))


def _na_geometry(rows):
    kr = min(NA_WIN_ROWS, rows)
    n_groups = rows // NA_Q_ROWS
    assert n_groups >= 3
    geo = []
    for g in range(n_groups):
        lo = _na_window_start(NA_Q_ROWS * g, rows)
        hi = _na_window_start(NA_Q_ROWS * g + NA_Q_ROWS - 1, rows) + kr
        geo.append((lo, hi - lo, 0 if g == 0 else (2 if g == n_groups - 1 else 1)))
    assert len({n for _, n, v in geo if v == 1}) == 1 and max(n for _, n, _ in geo) <= NA_K_ROWS
    return geo


def _na_row_offsets(rows):
    kr = min(NA_WIN_ROWS, rows)
    geo = _na_geometry(rows)
    table = []
    for rep in (0, 1, len(geo) - 1):
        start, n_rows, _ = geo[rep]
        per_i = []
        for i in range(NA_Q_ROWS):
            rq = NA_Q_ROWS * rep + i
            r0 = _na_window_start(rq, rows)
            per_i.append([start + j - rq + NA_WIN_ROWS - 1 if r0 <= start + j < r0 + kr else None
                          for j in range(n_rows)])
        table.append(per_i)
    return table


def _na_live_rows(rows):
    live = []
    for per_i in _na_row_offsets(rows):
        pairs = []
        for ip in range(NA_Q_ROWS // 2):
            seen = [j for j in range(len(per_i[0]))
                    if per_i[2 * ip][j] is not None or per_i[2 * ip + 1][j] is not None]
            assert seen == list(range(seen[0], seen[-1] + 1))
            pairs.append((seen[0], seen[-1] + 1))
        live.append(pairs)
    return live


def _na_build_bias(rpb_ref, hp, bias_ref, base_ref, rows):
    n_dr = 2 * NA_WIN_ROWS - 1
    n_dc = 2 * NA_WIN_COLS - 1
    ck = lax.broadcasted_iota(jnp.int32, (GRID_W, LANES), 0)
    lane = lax.broadcasted_iota(jnp.int32, (GRID_W, LANES), 1)
    left = lane < GRID_W
    cq = jnp.where(left, lane, lane - GRID_W)
    c0 = jnp.clip(cq - NA_WIN_COLS // 2, 0, GRID_W - NA_WIN_COLS)
    valid_c = (ck >= c0) & (ck < c0 + NA_WIN_COLS)
    dc = jnp.clip(ck - cq + NA_WIN_COLS - 1, 0, n_dc - 1)

    offsets = _na_row_offsets(rows)
    users = {}
    for var in range(3):
        for j in range(len(offsets[var][0])):
            for ip in range(NA_Q_ROWS // 2):
                key = (offsets[var][2 * ip][j], offsets[var][2 * ip + 1][j])
                users.setdefault(key, []).append((var, j, ip))

    used = sorted({dr for pair in users for dr in pair if dr is not None})
    masked = jnp.full((GRID_W, LANES), MASK_VALUE, jnp.float32)
    for h in range(2):
        head = 2 * hp + h
        for dr in used:
            def body(d, acc, dr=dr, head=head):
                return jnp.where(dc == d, rpb_ref[(head * n_dr + dr) * n_dc + d], acc)
            acc = lax.fori_loop(0, n_dc, body, jnp.zeros((GRID_W, LANES), jnp.float32), unroll=True)
            base_ref[dr] = jnp.where(valid_c, acc * LOG2E, MASK_VALUE)
        for (dr_l, dr_r), dests in users.items():
            t_l = masked if dr_l is None else base_ref[dr_l]
            t_r = masked if dr_r is None else base_ref[dr_r]
            tile = t_l if dr_l == dr_r else jnp.where(left, t_l, t_r)
            for var, j, ip in dests:
                bias_ref[var, h, j * GRID_W:(j + 1) * GRID_W, ip * LANES:(ip + 1) * LANES] = tile


def _na_program(q_ref, k_ref, vt_ref, g_ref, kc_ref, vct_ref, o_ref, bias_ref, vth_ref, vcth_ref, rows):
    geo = _na_geometry(rows)
    live = _na_live_rows(rows)
    tq = NA_Q_ROWS * GRID_W
    dh = NA_HEAD_DIM

    lane = lax.broadcasted_iota(jnp.int32, (1, LANES), 1)
    head0_lanes = lane < dh
    n_ones = vth_ref.shape[1] - dh
    for h in range(2):
        vth_ref[h, :dh, :] = vt_ref[0, h * dh:(h + 1) * dh, :]
        vth_ref[h, dh:, :] = jnp.ones((n_ones, vt_ref.shape[2]), vt_ref.dtype)
        vcth_ref[h, :dh, :] = vct_ref[0, h * dh:(h + 1) * dh, :]
        vcth_ref[h, dh:, :] = jnp.ones((n_ones, vct_ref.shape[2]), vct_ref.dtype)

    kc = kc_ref[0]
    contract_last = (((1,), (1,)), ((), ()))
    tasks = [(g, h) for g in range(len(geo)) for h in range(2)]

    def scores(t):
        g, h = tasks[t]
        ws, n_rows, var = geo[g]
        nk = n_rows * GRID_W
        qg = q_ref[0, g * tq:(g + 1) * tq, :]
        kw = k_ref[0, ws * GRID_W:ws * GRID_W + nk, :]
        sel = head0_lanes if h == 0 else jnp.logical_not(head0_lanes)
        qh = jnp.where(sel, qg, jnp.zeros_like(qg))
        s_raw = lax.dot_general(kw, qh, contract_last, preferred_element_type=jnp.float32)
        s_ctx = lax.dot_general(kc, qh, contract_last, preferred_element_type=jnp.float32)
        s_loc = []
        for ip, (lo, hi) in enumerate(live[var]):
            r, c = slice(lo * GRID_W, hi * GRID_W), slice(ip * LANES, (ip + 1) * LANES)
            s_loc.append(s_raw[r, c] + bias_ref[var, h, r, c])
        return s_loc, s_ctx

    halves = []

    def attend(t, s):
        g, h = tasks[t]
        s_loc, s_ctx = s
        ws, n_rows, var = geo[g]
        n_pad = n_rows + (-n_rows) % (MXU_TILE // GRID_W)
        p_loc, p_ctx = [], []
        for ip, (lo, hi) in enumerate(live[var]):
            sc = s_ctx[:, ip * LANES:(ip + 1) * LANES]
            m = jnp.maximum(jnp.max(s_loc[ip], axis=0, keepdims=True), jnp.max(sc, axis=0, keepdims=True))
            zeros = lambda n: [jnp.zeros((n * GRID_W, LANES), jnp.bfloat16)] if n else []
            p_loc.append(jnp.concatenate(
                zeros(lo) + [jnp.exp2((s_loc[ip] - m).astype(jnp.bfloat16))] + zeros(n_pad - hi), axis=0))
            p_ctx.append(jnp.exp2((sc - m).astype(jnp.bfloat16)))
        p_loc = jnp.concatenate(p_loc, axis=1)
        p_ctx = jnp.concatenate(p_ctx, axis=1)
        nk = n_pad * GRID_W
        ot = jnp.dot(vth_ref[h, :, ws * GRID_W:ws * GRID_W + nk], p_loc, preferred_element_type=jnp.float32)
        ot = ot + jnp.dot(vcth_ref[h], p_ctx, preferred_element_type=jnp.float32)
        halves.append(ot[:dh] / ot[dh:dh + 1])
        if h == 1:
            o2 = jnp.concatenate(halves, axis=0).T
            halves.clear()
            gate = g_ref[0, g * tq:(g + 1) * tq, :].astype(jnp.float32)
            o_ref[0, g * tq:(g + 1) * tq, :] = (o2 * gate).astype(o_ref.dtype)

    return len(tasks), scores, attend


def _ret_program(lg_f, lg_b, q_ref, k_ref, v_ref, g_ref, kc_ref, vc_ref, ng_ref, o_ref, t_ref, s_ref):
    c_len = RET_CHUNK
    l = q_ref.shape[1]
    lc = kc_ref.shape[1]
    n_chunks = l // c_len
    contract_last = (((1,), (1,)), ((), ()))

    ii = lax.broadcasted_iota(jnp.int32, (c_len, c_len), 0).astype(jnp.float32)
    jj = lax.broadcasted_iota(jnp.int32, (c_len, c_len), 1).astype(jnp.float32)
    dist = ii - jj
    decay = (jnp.where(dist >= 0, jnp.exp(lg_f * jnp.maximum(dist, 0.0)), 0.0)
             + jnp.where(dist <= 0, jnp.exp(lg_b * jnp.maximum(-dist, 0.0)), 0.0))
    ic = lax.broadcasted_iota(jnp.int32, (c_len, 1), 0).astype(jnp.float32)
    kdec_f = jnp.exp(lg_f * (c_len - 1 - ic))
    kdec_b = jnp.exp(lg_b * ic)
    qdec_f = jnp.exp(lg_f * (ic + 1.0))
    qdec_b = jnp.exp(lg_b * (c_len - ic))
    one = jnp.ones((1, 1), jnp.float32)
    cdec_f = jnp.exp(one * (lg_f * c_len))
    cdec_b = jnp.exp(one * (lg_b * c_len))

    jcr = lax.broadcasted_iota(jnp.int32, (1, lc), 1).astype(jnp.float32)
    kct = kc_ref[0].astype(jnp.float32).T
    vcx = vc_ref[0]
    s_f = jnp.dot((kct * jnp.exp(lg_f * (lc - 1 - jcr))).astype(jnp.bfloat16), vcx,
                  preferred_element_type=jnp.float32)
    s_b = jnp.dot((kct * jnp.exp(lg_b * jcr)).astype(jnp.bfloat16), vcx,
                  preferred_element_type=jnp.float32)

    for c in range(n_chunks):
        rows_c = slice(c * c_len, (c + 1) * c_len)
        kt = k_ref[0, rows_c, :].astype(jnp.float32).T.astype(jnp.bfloat16)
        vf = v_ref[0, rows_c, :].astype(jnp.float32)
        vw = jnp.concatenate([(vf * kdec_f).astype(jnp.bfloat16), (vf * kdec_b).astype(jnp.bfloat16)], axis=1)
        t_ref[c] = jnp.dot(kt, vw, preferred_element_type=jnp.float32)

    for c in range(n_chunks):
        s_ref[c, :, :RET_DIM] = s_f.astype(jnp.bfloat16)
        s_f = cdec_f * s_f + t_ref[c, :, :RET_DIM]
    for c in reversed(range(n_chunks)):
        s_ref[c, :, RET_DIM:] = s_b.astype(jnp.bfloat16)
        s_b = cdec_b * s_b + t_ref[c, :, RET_DIM:]

    ng = ng_ref[...]

    def scores(c):
        rows_c = slice(c * c_len, (c + 1) * c_len)
        qc = q_ref[0, rows_c, :]
        s = lax.dot_general(qc, k_ref[0, rows_c, :], contract_last, preferred_element_type=jnp.float32) * decay
        cross = jnp.dot(qc, s_ref[c], preferred_element_type=jnp.float32)
        return s.astype(jnp.bfloat16), cross[:, :RET_DIM] * qdec_f + cross[:, RET_DIM:] * qdec_b

    def finish(c, sc):
        s, cross = sc
        rows_c = slice(c * c_len, (c + 1) * c_len)
        o = jnp.dot(s, v_ref[0, rows_c, :], preferred_element_type=jnp.float32) + cross
        o = o * lax.rsqrt(jnp.mean(o * o, axis=-1, keepdims=True) + EPS) * ng
        gate = g_ref[0, rows_c, :].astype(jnp.float32)
        o_ref[0, rows_c, :] = (o * gate).astype(o_ref.dtype)

    return n_chunks, scores, finish


def _mixer_kernel(rpb_ref, lg_ref,
                  aq_ref, ak_ref, avt_ref, ag_ref, akc_ref, avct_ref,
                  rq_ref, rk_ref, rv_ref, rg_ref, rkc_ref, rvc_ref, ng_ref,
                  yna_ref, yret_ref,
                  bias_ref, base_ref, vth_ref, vcth_ref, t_ref, s_ref, *, rows):
    i = pl.program_id(0)

    @pl.when(pl.program_id(1) == 0)
    def _():
        _na_build_bias(rpb_ref, i, bias_ref, base_ref, rows)

    nb = aq_ref.shape[0]
    lc = akc_ref.shape[2] // nb
    slab = lambda ref, j: ref.at[j]
    one = lambda ref, j: ref.at[pl.ds(j, 1)]
    cslab = lambda ref, j: ref.at[0].at[:, pl.ds(j * lc, lc), :]
    cone = lambda ref, j: ref.at[:, :, pl.ds(j * lc, lc)]

    na_tasks, ret_tasks = [], []
    for j in range(nb):
        n_ret, ret_scores, ret_finish = _ret_program(
            lg_ref[0, i], lg_ref[1, i], slab(rq_ref, j), slab(rk_ref, j), slab(rv_ref, j), slab(rg_ref, j),
            cslab(rkc_ref, j), cslab(rvc_ref, j), ng_ref, slab(yret_ref, j), t_ref.at[j], s_ref.at[j])
        n_na, na_scores, na_attend = _na_program(
            slab(aq_ref, j), slab(ak_ref, j), one(avt_ref, j), slab(ag_ref, j), cslab(akc_ref, j),
            cone(avct_ref, j), slab(yna_ref, j), bias_ref, vth_ref.at[j], vcth_ref.at[j], rows)
        na_tasks += [(na_scores, na_attend, t) for t in range(n_na)]
        ret_tasks += [(ret_scores, ret_finish, c) for c in range(n_ret)]

    issue = lambda task: task[0](task[2])
    na_q = [issue(task) for task in na_tasks[:NA_AHEAD]]
    ret_q = [issue(task) for task in ret_tasks[:RET_AHEAD]]
    stride = max(1, len(na_tasks) // len(ret_tasks))
    for t in range(max(len(na_tasks), stride * len(ret_tasks))):
        r = t // stride if t % stride == 0 else None
        if t + NA_AHEAD < len(na_tasks):
            na_q.append(issue(na_tasks[t + NA_AHEAD]))
        if r is not None and r + RET_AHEAD < len(ret_tasks):
            ret_q.append(issue(ret_tasks[r + RET_AHEAD]))
        if t < len(na_tasks):
            na_tasks[t][1](na_tasks[t][2], na_q.pop(0))
        if r is not None and r < len(ret_tasks):
            ret_tasks[r][1](ret_tasks[r][2], ret_q.pop(0))


def _mixers(rpb, lg, proj, vt, proj_ctx, vct, ret_norm_g, rows, cols, ctx_cols):
    b, _, l, _ = proj.shape
    lc = proj_ctx.shape[2] // b
    nb = MIX_BATCH
    assert NA_HEADS // 2 == RET_HEADS and b % nb == 0
    blk = lambda name: pl.BlockSpec((nb, 1, l, LANES), lambda i, bi, *_, off=cols[name]: (bi, off + i, 0, 0))
    cblk = lambda name: pl.BlockSpec((1, 1, nb * lc, LANES),
                                     lambda i, bi, *_, off=ctx_cols[name]: (0, off + i, bi, 0))
    n_chunks = l // RET_CHUNK
    grid_spec = pltpu.PrefetchScalarGridSpec(
        num_scalar_prefetch=2,
        grid=(RET_HEADS, b // nb),
        in_specs=[blk("na_q"), blk("na_k"),
                  pl.BlockSpec((nb, LANES, l), lambda i, bi, *_: (bi, i, 0)),
                  blk("na_g"), cblk("na_k"),
                  pl.BlockSpec((1, LANES, nb * lc), lambda i, bi, *_: (0, i, bi)),
                  blk("r_q"), blk("r_k"), blk("r_v"), blk("r_g"), cblk("r_k"), cblk("r_v"),
                  pl.BlockSpec((1, LANES), lambda i, bi, *_: (0, i))],
        out_specs=(pl.BlockSpec((nb, 1, l, LANES), lambda i, bi, *_: (bi, i, 0, 0)),
                   pl.BlockSpec((nb, 1, l, LANES), lambda i, bi, *_: (bi, i, 0, 0))),
        scratch_shapes=[pltpu.VMEM((3, 2, NA_K_ROWS * GRID_W, NA_Q_ROWS * GRID_W), jnp.float32),
                        pltpu.VMEM((2 * NA_WIN_ROWS - 1, GRID_W, LANES), jnp.float32),
                        pltpu.VMEM((nb, 2, NA_HEAD_DIM + BF16_SUBLANES, l), jnp.bfloat16),
                        pltpu.VMEM((nb, 2, NA_HEAD_DIM + BF16_SUBLANES, lc), jnp.bfloat16),
                        pltpu.VMEM((nb, n_chunks, RET_DIM, 2 * RET_DIM), jnp.float32),
                        pltpu.VMEM((nb, n_chunks, RET_DIM, 2 * RET_DIM), jnp.bfloat16)],
    )
    return pl.pallas_call(
        functools.partial(_mixer_kernel, rows=rows),
        out_shape=(jax.ShapeDtypeStruct((b, NA_WIDTH // LANES, l, LANES), jnp.bfloat16),
                   jax.ShapeDtypeStruct((b, RET_WIDTH // LANES, l, LANES), jnp.bfloat16)),
        grid_spec=grid_spec,
        compiler_params=pltpu.CompilerParams(
            dimension_semantics=("arbitrary", "arbitrary"), vmem_limit_bytes=VMEM_LIMIT),
        name="mixers",
    )(rpb.astype(jnp.float32).reshape(-1), lg,
      proj, proj, vt, proj, proj_ctx, vct,
      proj, proj, proj, proj, proj_ctx, proj_ctx, ret_norm_g.reshape(1, RET_WIDTH))


def _out_kernel(x_ref, gate_ref, yna_ref, yret_ref, w_ref, fg_ref, o_ref, *, sub):
    tm = x_ref.shape[1]

    def mix(j):
        rows = slice(j * sub, (j + 1) * sub)
        y = jnp.concatenate([ref[0, k, rows, :] for ref in (yna_ref, yret_ref) for k in range(ref.shape[1])],
                            axis=1)
        return jnp.dot(y, w_ref[...], preferred_element_type=jnp.float32)

    def finish(j, y):
        rows = slice(j * sub, (j + 1) * sub)
        z = x_ref[0, rows, :] + gate_ref[0] * y
        ms = jnp.mean(z * z, axis=-1, keepdims=True)
        o_ref[0, rows, :] = z * lax.rsqrt(ms + EPS) * fg_ref[...]

    pending = mix(0)
    for j in range(tm // sub):
        current = pending
        if (j + 1) * sub < tm:
            pending = mix(j + 1)
        finish(j, current)


def _out_proj(x, gate, y_na, y_ret, w_out, final_g, tm, sub):
    b, l, d = x.shape
    assert l % tm == 0 and tm % sub == 0
    return pl.pallas_call(
        functools.partial(_out_kernel, sub=sub),
        out_shape=jax.ShapeDtypeStruct((b, l, d), jnp.float32),
        grid=(b, l // tm),
        in_specs=[pl.BlockSpec((1, tm, d), lambda bi, i: (bi, i, 0)),
                  pl.BlockSpec((1, 1, d), lambda bi, i: (bi, 0, 0)),
                  pl.BlockSpec((1, NA_WIDTH // LANES, tm, LANES), lambda bi, i: (bi, 0, i, 0)),
                  pl.BlockSpec((1, RET_WIDTH // LANES, tm, LANES), lambda bi, i: (bi, 0, i, 0)),
                  pl.BlockSpec((NA_WIDTH + RET_WIDTH, d), lambda bi, i: (0, 0)),
                  pl.BlockSpec((1, d), lambda bi, i: (0, 0))],
        out_specs=pl.BlockSpec((1, tm, d), lambda bi, i: (bi, i, 0)),
        compiler_params=pltpu.CompilerParams(
            dimension_semantics=("arbitrary", "arbitrary"), vmem_limit_bytes=VMEM_LIMIT),
        name="out_proj",
    )(x, gate, y_na, y_ret, w_out, final_g.reshape(1, d))


def _rotary_tables(l):
    half = RET_DIM // 2
    nf = half // 2
    t = np.arange(l)
    row = (t // GRID_W).astype(np.float64)
    col = (t % GRID_W).astype(np.float64)
    inv = ROPE_BASE ** (-np.arange(nf, dtype=np.float64) / nf)
    ang = np.concatenate([row[:, None] * inv, col[:, None] * inv], axis=-1)
    cos, sin = np.cos(ang), np.sin(ang)
    cosf = np.concatenate([cos, cos], axis=-1).astype(np.float32)
    sinf = np.concatenate([-sin, sin], axis=-1).astype(np.float32)
    return jnp.asarray(cosf), jnp.asarray(sinf)


def kernel(x, c, ctx, c_ctx, norm_g, w_ada, b_ada, w_in, na_rpb, ret_decay_fwd, ret_decay_bwd,
           ret_norm_g, w_out, final_norm_g):
    depth = norm_g.shape[0]
    assert depth == 1, "context stream update between layers is not implemented"
    b, l, d = x.shape
    rows = l // GRID_W
    i = 0

    cc = jnp.concatenate([c, c_ctx[None, :]], axis=0)
    pad = (-cc.shape[0]) % BF16_SUBLANES
    cc = jnp.pad(cc, ((0, pad), (0, 0)))
    mod, w, w_vt = _prep(cc, w_ada[i], b_ada[i], w_in[i], v_group=2)
    shift, scale, gate = (mod[:b, None, :d], mod[:b, None, d:2 * d], mod[:b, None, 2 * d:])
    shift_c, scale_c = mod[b:b + 1, None, :d], mod[b:b + 1, None, d:2 * d]
    cosf, sinf = _rotary_tables(l)
    blocks = GROUP_COLS // LANES

    lat_groups = ((0, "na_q"), (3, "silu"), (4, "rot"), (5, "rot_kscale"), (7, "silu"), (1, "plain"), (6, "plain"))
    lat_names = ("na_q", "na_g", "r_q", "r_k", "r_g", "na_k", "r_v")
    proj, vt = _in_proj(x, scale, shift, norm_g[i], w, w_vt, cosf, sinf, lat_groups, tm=1024, sub=256)

    lc = ctx.shape[1]
    ctx_groups = ((1, "plain"), (5, "kscale"), (6, "plain"))
    ctx_names = ("na_k", "r_k", "r_v")
    proj_ctx, vct = _in_proj(ctx.reshape(1, b * lc, d), scale_c, shift_c, norm_g[i], w, w_vt, cosf, sinf,
                             ctx_groups, tm=1024, sub=256)

    lg = jnp.stack([-jnp.exp(ret_decay_fwd[i].astype(jnp.float32)),
                    -jnp.exp(ret_decay_bwd[i].astype(jnp.float32))])
    y_na, y_ret = _mixers(na_rpb[i], lg, proj, vt, proj_ctx, vct, ret_norm_g[i], rows,
                          cols={name: k * blocks for k, name in enumerate(lat_names)},
                          ctx_cols={name: k * blocks for k, name in enumerate(ctx_names)})

    return _out_proj(x, gate, y_na, y_ret, w_out[i].astype(jnp.bfloat16), final_norm_g, tm=2048, sub=256)
```

```python
import functools
import math

import jax
import jax.numpy as jnp
import numpy as np
from jax import lax
from jax.experimental import pallas as pl
from jax.experimental.pallas import tpu as pltpu

D_MODEL = 1024
GRID_W = 64
NA_HEAD_DIM = 64
NA_WIDTH = 512
NA_HEADS = 8
NA_WIN_ROWS = 8
NA_WIN_COLS = 16
RET_HEADS = 4
RET_WIDTH = 512
RET_DIM = 128
RET_CHUNK = 256
ROPE_BASE = 10000.0
EPS = 1e-6

LANES = 128
MXU_TILE = 256
BF16_SUBLANES = 16
GROUP_COLS = 512
NA_Q_ROWS = 4
NA_K_ROWS = 12
NA_AHEAD = 2
RET_AHEAD = 1
MIX_BATCH = 2
MASK_VALUE = -1e30
LOG2E = math.log2(math.e)

VMEM_LIMIT = 56 * 1024 * 1024


def _silu(v):
    return v * (1.0 / (1.0 + jnp.exp(-v)))


def _split_bf16(v):
    hi = v.astype(jnp.bfloat16)
    return hi, (v - hi.astype(jnp.float32)).astype(jnp.bfloat16)


def _prep_kernel(c_ref, wa_ref, b_ref, w_ref, mod_ref, o_ref, ot_ref, *, v_group):
    m = c_ref.shape[0]
    a_hi, a_lo = _split_bf16(_silu(c_ref[...]))
    w_hi, w_lo = _split_bf16(wa_ref[...])
    both = jnp.dot(jnp.concatenate([a_hi, a_lo], axis=0), w_hi, preferred_element_type=jnp.float32)
    mod_ref[...] = (both[:m] + both[m:]) + jnp.dot(a_hi, w_lo, preferred_element_type=jnp.float32) + b_ref[...]

    w = w_ref[...]
    o_ref[...] = w.astype(o_ref.dtype)
    ot_ref[...] = w[:, v_group * GROUP_COLS:(v_group + 1) * GROUP_COLS].T.astype(ot_ref.dtype)


def _prep(cc, w_ada, b_ada, w_in, v_group):
    m, d = cc.shape
    n_mod = w_ada.shape[1]
    n = w_in.shape[1]
    steps = 8
    tn, rb = n_mod // steps, d // steps
    assert tn % LANES == 0 and rb % LANES == 0
    return pl.pallas_call(
        functools.partial(_prep_kernel, v_group=v_group),
        out_shape=(jax.ShapeDtypeStruct((m, n_mod), jnp.float32),
                   jax.ShapeDtypeStruct((d, n), jnp.bfloat16),
                   jax.ShapeDtypeStruct((GROUP_COLS, d), jnp.bfloat16)),
        grid=(steps,),
        in_specs=[pl.BlockSpec((m, d), lambda j: (0, 0)),
                  pl.BlockSpec((d, tn), lambda j: (0, j)),
                  pl.BlockSpec((1, tn), lambda j: (0, j)),
                  pl.BlockSpec((rb, n), lambda j: (j, 0))],
        out_specs=(pl.BlockSpec((m, tn), lambda j: (0, j)),
                   pl.BlockSpec((rb, n), lambda j: (j, 0)),
                   pl.BlockSpec((GROUP_COLS, rb), lambda j: (0, j))),
        compiler_params=pltpu.CompilerParams(
            dimension_semantics=("arbitrary",), vmem_limit_bytes=VMEM_LIMIT),
        name="prep",
    )(cc, w_ada, b_ada.reshape(1, n_mod), w_in)


def _rotary(acc, cosf, sinf):
    outs = []
    for h in range(GROUP_COLS // LANES):
        xh = acc[:, h * LANES:(h + 1) * LANES]
        outs.append(xh * cosf + pltpu.roll(xh, LANES // 2, 1) * sinf)
    return jnp.concatenate(outs, axis=-1)


def _proj_kernel(x_ref, scale_ref, shift_ref, g_ref, w_ref, wvt_ref, cos_ref, sin_ref, o_ref, vt_ref,
                 *, groups, sub):
    tm = x_ref.shape[1]

    def normed(j):
        x = x_ref[0, j * sub:(j + 1) * sub, :]
        ms = jnp.mean(x * x, axis=-1, keepdims=True)
        h = x * lax.rsqrt(ms + EPS) * g_ref[...]
        h = h * (1.0 + scale_ref[0]) + shift_ref[0]
        return h.astype(jnp.bfloat16)

    def project(j, hb):
        rows = slice(j * sub, (j + 1) * sub)
        vt = lax.dot_general(wvt_ref[...], hb, (((1,), (1,)), ((), ())), preferred_element_type=jnp.float32)
        vt_ref[0, :, rows] = vt.astype(vt_ref.dtype)
        for gi, (src, kind) in enumerate(groups):
            acc = jnp.dot(hb, w_ref[:, src * GROUP_COLS:(src + 1) * GROUP_COLS],
                          preferred_element_type=jnp.float32)
            if kind == "na_q":
                acc = acc * (NA_HEAD_DIM ** -0.5 * LOG2E)
            elif kind == "silu":
                acc = _silu(acc)
            elif kind == "rot":
                acc = _rotary(acc, cos_ref[rows, :], sin_ref[rows, :])
            elif kind == "rot_kscale":
                acc = _rotary(acc, cos_ref[rows, :], sin_ref[rows, :]) * (RET_DIM ** -0.5)
            elif kind == "kscale":
                acc = acc * (RET_DIM ** -0.5)
            else:
                assert kind == "plain"
            per_group = GROUP_COLS // LANES
            for k in range(per_group):
                o_ref[0, gi * per_group + k, rows, :] = acc[:, k * LANES:(k + 1) * LANES].astype(o_ref.dtype)

    pending = normed(0)
    for j in range(tm // sub):
        current = pending
        if (j + 1) * sub < tm:
            pending = normed(j + 1)
        project(j, current)


def _in_proj(x, scale, shift, norm_g, w, w_vt, cosf, sinf, groups, tm, sub):
    b, l, d = x.shape
    n_in = w.shape[1]
    n = GROUP_COLS * len(groups)
    nv = w_vt.shape[0]
    assert l % tm == 0 and tm % sub == 0
    per_batch = scale.shape[0] > 1
    mod_map = (lambda bi, i: (bi, 0, 0)) if per_batch else (lambda bi, i: (0, 0, 0))
    rotated = any(kind.startswith("rot") for _, kind in groups)
    pos_map = (lambda bi, i: (i, 0)) if rotated else (lambda bi, i: (0, 0))
    return pl.pallas_call(
        functools.partial(_proj_kernel, groups=tuple(groups), sub=sub),
        out_shape=(jax.ShapeDtypeStruct((b, n // LANES, l, LANES), jnp.bfloat16),
                   jax.ShapeDtypeStruct((b, nv, l), jnp.bfloat16)),
        grid=(b, l // tm),
        in_specs=[pl.BlockSpec((1, tm, d), lambda bi, i: (bi, i, 0)),
                  pl.BlockSpec((1, 1, d), mod_map),
                  pl.BlockSpec((1, 1, d), mod_map),
                  pl.BlockSpec((1, d), lambda bi, i: (0, 0)),
                  pl.BlockSpec((d, n_in), lambda bi, i: (0, 0)),
                  pl.BlockSpec((nv, d), lambda bi, i: (0, 0)),
                  pl.BlockSpec((tm, LANES), pos_map),
                  pl.BlockSpec((tm, LANES), pos_map)],
        out_specs=(pl.BlockSpec((1, n // LANES, tm, LANES), lambda bi, i: (bi, 0, i, 0)),
                   pl.BlockSpec((1, nv, tm), lambda bi, i: (bi, 0, i))),
        compiler_params=pltpu.CompilerParams(
            dimension_semantics=("arbitrary", "arbitrary"), vmem_limit_bytes=VMEM_LIMIT),
        name="in_proj",
    )(x, scale, shift, norm_g.reshape(1, d), w, w_vt, cosf, sinf)


def _na_window_start(rq, rows):
    kr = min(NA_WIN_ROWS, rows)
    return int(np.clip(rq - kr // 2, 0, rows - kr))


def _na_geometry(rows):
    kr = min(NA_WIN_ROWS, rows)
    n_groups = rows // NA_Q_ROWS
    assert n_groups >= 3
    geo = []
    for g in range(n_groups):
        lo = _na_window_start(NA_Q_ROWS * g, rows)
        hi = _na_window_start(NA_Q_ROWS * g + NA_Q_ROWS - 1, rows) + kr
        geo.append((lo, hi - lo, 0 if g == 0 else (2 if g == n_groups - 1 else 1)))
    assert len({n for _, n, v in geo if v == 1}) == 1 and max(n for _, n, _ in geo) <= NA_K_ROWS
    return geo


def _na_row_offsets(rows):
    kr = min(NA_WIN_ROWS, rows)
    geo = _na_geometry(rows)
    table = []
    for rep in (0, 1, len(geo) - 1):
        start, n_rows, _ = geo[rep]
        per_i = []
        for i in range(NA_Q_ROWS):
            rq = NA_Q_ROWS * rep + i
            r0 = _na_window_start(rq, rows)
            per_i.append([start + j - rq + NA_WIN_ROWS - 1 if r0 <= start + j < r0 + kr else None
                          for j in range(n_rows)])
        table.append(per_i)
    return table


def _na_live_rows(rows):
    live = []
    for per_i in _na_row_offsets(rows):
        pairs = []
        for ip in range(NA_Q_ROWS // 2):
            seen = [j for j in range(len(per_i[0]))
                    if per_i[2 * ip][j] is not None or per_i[2 * ip + 1][j] is not None]
            assert seen == list(range(seen[0], seen[-1] + 1))
            pairs.append((seen[0], seen[-1] + 1))
        live.append(pairs)
    return live


def _na_build_bias(rpb_ref, hp, bias_ref, base_ref, rows):
    n_dr = 2 * NA_WIN_ROWS - 1
    n_dc = 2 * NA_WIN_COLS - 1
    ck = lax.broadcasted_iota(jnp.int32, (GRID_W, LANES), 0)
    lane = lax.broadcasted_iota(jnp.int32, (GRID_W, LANES), 1)
    left = lane < GRID_W
    cq = jnp.where(left, lane, lane - GRID_W)
    c0 = jnp.clip(cq - NA_WIN_COLS // 2, 0, GRID_W - NA_WIN_COLS)
    valid_c = (ck >= c0) & (ck < c0 + NA_WIN_COLS)
    dc = jnp.clip(ck - cq + NA_WIN_COLS - 1, 0, n_dc - 1)

    offsets = _na_row_offsets(rows)
    users = {}
    for var in range(3):
        for j in range(len(offsets[var][0])):
            for ip in range(NA_Q_ROWS // 2):
                key = (offsets[var][2 * ip][j], offsets[var][2 * ip + 1][j])
                users.setdefault(key, []).append((var, j, ip))

    used = sorted({dr for pair in users for dr in pair if dr is not None})
    masked = jnp.full((GRID_W, LANES), MASK_VALUE, jnp.float32)
    for h in range(2):
        head = 2 * hp + h
        for dr in used:
            def body(d, acc, dr=dr, head=head):
                return jnp.where(dc == d, rpb_ref[(head * n_dr + dr) * n_dc + d], acc)
            acc = lax.fori_loop(0, n_dc, body, jnp.zeros((GRID_W, LANES), jnp.float32), unroll=True)
            base_ref[dr] = jnp.where(valid_c, acc * LOG2E, MASK_VALUE)
        for (dr_l, dr_r), dests in users.items():
            t_l = masked if dr_l is None else base_ref[dr_l]
            t_r = masked if dr_r is None else base_ref[dr_r]
            tile = t_l if dr_l == dr_r else jnp.where(left, t_l, t_r)
            for var, j, ip in dests:
                bias_ref[var, h, j * GRID_W:(j + 1) * GRID_W, ip * LANES:(ip + 1) * LANES] = tile


def _na_program(q_ref, k_ref, vt_ref, g_ref, kc_ref, vct_ref, o_ref, bias_ref, vth_ref, vcth_ref, rows):
    geo = _na_geometry(rows)
    live = _na_live_rows(rows)
    tq = NA_Q_ROWS * GRID_W
    dh = NA_HEAD_DIM

    lane = lax.broadcasted_iota(jnp.int32, (1, LANES), 1)
    head0_lanes = lane < dh
    n_ones = vth_ref.shape[1] - dh
    for h in range(2):
        vth_ref[h, :dh, :] = vt_ref[0, h * dh:(h + 1) * dh, :]
        vth_ref[h, dh:, :] = jnp.ones((n_ones, vt_ref.shape[2]), vt_ref.dtype)
        vcth_ref[h, :dh, :] = vct_ref[0, h * dh:(h + 1) * dh, :]
        vcth_ref[h, dh:, :] = jnp.ones((n_ones, vct_ref.shape[2]), vct_ref.dtype)

    kc = kc_ref[0]
    contract_last = (((1,), (1,)), ((), ()))
    tasks = [(g, h) for g in range(len(geo)) for h in range(2)]

    def scores(t):
        g, h = tasks[t]
        ws, n_rows, var = geo[g]
        nk = n_rows * GRID_W
        qg = q_ref[0, g * tq:(g + 1) * tq, :]
        kw = k_ref[0, ws * GRID_W:ws * GRID_W + nk, :]
        sel = head0_lanes if h == 0 else jnp.logical_not(head0_lanes)
        qh = jnp.where(sel, qg, jnp.zeros_like(qg))
        s_raw = lax.dot_general(kw, qh, contract_last, preferred_element_type=jnp.float32)
        s_ctx = lax.dot_general(kc, qh, contract_last, preferred_element_type=jnp.float32)
        s_loc = []
        for ip, (lo, hi) in enumerate(live[var]):
            r, c = slice(lo * GRID_W, hi * GRID_W), slice(ip * LANES, (ip + 1) * LANES)
            s_loc.append(s_raw[r, c] + bias_ref[var, h, r, c])
        return s_loc, s_ctx

    halves = []

    def attend(t, s):
        g, h = tasks[t]
        s_loc, s_ctx = s
        ws, n_rows, var = geo[g]
        n_pad = n_rows + (-n_rows) % (MXU_TILE // GRID_W)
        p_loc, p_ctx = [], []
        for ip, (lo, hi) in enumerate(live[var]):
            sc = s_ctx[:, ip * LANES:(ip + 1) * LANES]
            m = jnp.maximum(jnp.max(s_loc[ip], axis=0, keepdims=True), jnp.max(sc, axis=0, keepdims=True))
            zeros = lambda n: [jnp.zeros((n * GRID_W, LANES), jnp.bfloat16)] if n else []
            p_loc.append(jnp.concatenate(
                zeros(lo) + [jnp.exp2((s_loc[ip] - m).astype(jnp.bfloat16))] + zeros(n_pad - hi), axis=0))
            p_ctx.append(jnp.exp2((sc - m).astype(jnp.bfloat16)))
        p_loc = jnp.concatenate(p_loc, axis=1)
        p_ctx = jnp.concatenate(p_ctx, axis=1)
        nk = n_pad * GRID_W
        ot = jnp.dot(vth_ref[h, :, ws * GRID_W:ws * GRID_W + nk], p_loc, preferred_element_type=jnp.float32)
        ot = ot + jnp.dot(vcth_ref[h], p_ctx, preferred_element_type=jnp.float32)
        halves.append(ot[:dh] / ot[dh:dh + 1])
        if h == 1:
            o2 = jnp.concatenate(halves, axis=0).T
            halves.clear()
            gate = g_ref[0, g * tq:(g + 1) * tq, :].astype(jnp.float32)
            o_ref[0, g * tq:(g + 1) * tq, :] = (o2 * gate).astype(o_ref.dtype)

    return len(tasks), scores, attend


def _ret_program(lg_f, lg_b, q_ref, k_ref, v_ref, g_ref, kc_ref, vc_ref, ng_ref, o_ref, t_ref, s_ref):
    c_len = RET_CHUNK
    l = q_ref.shape[1]
    lc = kc_ref.shape[1]
    n_chunks = l // c_len
    contract_last = (((1,), (1,)), ((), ()))

    ii = lax.broadcasted_iota(jnp.int32, (c_len, c_len), 0).astype(jnp.float32)
    jj = lax.broadcasted_iota(jnp.int32, (c_len, c_len), 1).astype(jnp.float32)
    dist = ii - jj
    decay = (jnp.where(dist >= 0, jnp.exp(lg_f * jnp.maximum(dist, 0.0)), 0.0)
             + jnp.where(dist <= 0, jnp.exp(lg_b * jnp.maximum(-dist, 0.0)), 0.0))
    ic = lax.broadcasted_iota(jnp.int32, (c_len, 1), 0).astype(jnp.float32)
    kdec_f = jnp.exp(lg_f * (c_len - 1 - ic))
    kdec_b = jnp.exp(lg_b * ic)
    qdec_f = jnp.exp(lg_f * (ic + 1.0))
    qdec_b = jnp.exp(lg_b * (c_len - ic))
    one = jnp.ones((1, 1), jnp.float32)
    cdec_f = jnp.exp(one * (lg_f * c_len))
    cdec_b = jnp.exp(one * (lg_b * c_len))

    jcr = lax.broadcasted_iota(jnp.int32, (1, lc), 1).astype(jnp.float32)
    kct = kc_ref[0].astype(jnp.float32).T
    vcx = vc_ref[0]
    s_f = jnp.dot((kct * jnp.exp(lg_f * (lc - 1 - jcr))).astype(jnp.bfloat16), vcx,
                  preferred_element_type=jnp.float32)
    s_b = jnp.dot((kct * jnp.exp(lg_b * jcr)).astype(jnp.bfloat16), vcx,
                  preferred_element_type=jnp.float32)

    for c in range(n_chunks):
        rows_c = slice(c * c_len, (c + 1) * c_len)
        kt = k_ref[0, rows_c, :].astype(jnp.float32).T.astype(jnp.bfloat16)
        vf = v_ref[0, rows_c, :].astype(jnp.float32)
        vw = jnp.concatenate([(vf * kdec_f).astype(jnp.bfloat16), (vf * kdec_b).astype(jnp.bfloat16)], axis=1)
        t_ref[c] = jnp.dot(kt, vw, preferred_element_type=jnp.float32)

    for c in range(n_chunks):
        s_ref[c, :RET_DIM, :] = s_f.astype(jnp.bfloat16)
        s_f = cdec_f * s_f + t_ref[c, :, :RET_DIM]
    for c in reversed(range(n_chunks)):
        s_ref[c, RET_DIM:, :] = s_b.astype(jnp.bfloat16)
        s_b = cdec_b * s_b + t_ref[c, :, RET_DIM:]

    ng = ng_ref[...]

    def scores(c):
        rows_c = slice(c * c_len, (c + 1) * c_len)
        qc = q_ref[0, rows_c, :]
        s = lax.dot_general(qc, k_ref[0, rows_c, :], contract_last, preferred_element_type=jnp.float32) * decay
        qf = qc.astype(jnp.float32)
        qw = jnp.concatenate([(qf * qdec_f).astype(jnp.bfloat16), (qf * qdec_b).astype(jnp.bfloat16)], axis=1)
        cross = jnp.dot(qw, s_ref[c], preferred_element_type=jnp.float32)
        return s.astype(jnp.bfloat16), cross

    def finish(c, sc):
        s, cross = sc
        rows_c = slice(c * c_len, (c + 1) * c_len)
        o = jnp.dot(s, v_ref[0, rows_c, :], preferred_element_type=jnp.float32) + cross
        o = o * lax.rsqrt(jnp.mean(o * o, axis=-1, keepdims=True) + EPS) * ng
        gate = g_ref[0, rows_c, :].astype(jnp.float32)
        o_ref[0, rows_c, :] = (o * gate).astype(o_ref.dtype)

    return n_chunks, scores, finish


def _mixer_kernel(rpb_ref, lg_ref,
                  aq_ref, ak_ref, avt_ref, ag_ref, akc_ref, avct_ref,
                  rq_ref, rk_ref, rv_ref, rg_ref, rkc_ref, rvc_ref, ng_ref,
                  yna_ref, yret_ref,
                  bias_ref, base_ref, vth_ref, vcth_ref, t_ref, s_ref, *, rows):
    i = pl.program_id(0)

    @pl.when(pl.program_id(1) == 0)
    def _():
        _na_build_bias(rpb_ref, i, bias_ref, base_ref, rows)

    nb = aq_ref.shape[0]
    lc = akc_ref.shape[2] // nb
    slab = lambda ref, j: ref.at[j]
    one = lambda ref, j: ref.at[pl.ds(j, 1)]
    cslab = lambda ref, j: ref.at[0].at[:, pl.ds(j * lc, lc), :]
    cone = lambda ref, j: ref.at[:, :, pl.ds(j * lc, lc)]

    na_tasks, ret_tasks = [], []
    for j in range(nb):
        n_ret, ret_scores, ret_finish = _ret_program(
            lg_ref[0, i], lg_ref[1, i], slab(rq_ref, j), slab(rk_ref, j), slab(rv_ref, j), slab(rg_ref, j),
            cslab(rkc_ref, j), cslab(rvc_ref, j), ng_ref, slab(yret_ref, j), t_ref.at[j], s_ref.at[j])
        n_na, na_scores, na_attend = _na_program(
            slab(aq_ref, j), slab(ak_ref, j), one(avt_ref, j), slab(ag_ref, j), cslab(akc_ref, j),
            cone(avct_ref, j), slab(yna_ref, j), bias_ref, vth_ref.at[j], vcth_ref.at[j], rows)
        na_tasks += [(na_scores, na_attend, t) for t in range(n_na)]
        ret_tasks += [(ret_scores, ret_finish, c) for c in range(n_ret)]

    issue = lambda task: task[0](task[2])
    na_q = [issue(task) for task in na_tasks[:NA_AHEAD]]
    ret_q = [issue(task) for task in ret_tasks[:RET_AHEAD]]
    stride = max(1, len(na_tasks) // len(ret_tasks))
    for t in range(max(len(na_tasks), stride * len(ret_tasks))):
        r = t // stride if t % stride == 0 else None
        if t + NA_AHEAD < len(na_tasks):
            na_q.append(issue(na_tasks[t + NA_AHEAD]))
        if r is not None and r + RET_AHEAD < len(ret_tasks):
            ret_q.append(issue(ret_tasks[r + RET_AHEAD]))
        if t < len(na_tasks):
            na_tasks[t][1](na_tasks[t][2], na_q.pop(0))
        if r is not None and r < len(ret_tasks):
            ret_tasks[r][1](ret_tasks[r][2], ret_q.pop(0))


def _mixers(rpb, lg, proj, vt, proj_ctx, vct, ret_norm_g, rows, cols, ctx_cols):
    b, _, l, _ = proj.shape
    lc = proj_ctx.shape[2] // b
    nb = MIX_BATCH
    assert NA_HEADS // 2 == RET_HEADS and b % nb == 0
    blk = lambda name: pl.BlockSpec((nb, 1, l, LANES), lambda i, bi, *_, off=cols[name]: (bi, off + i, 0, 0))
    cblk = lambda name: pl.BlockSpec((1, 1, nb * lc, LANES),
                                     lambda i, bi, *_, off=ctx_cols[name]: (0, off + i, bi, 0))
    n_chunks = l // RET_CHUNK
    grid_spec = pltpu.PrefetchScalarGridSpec(
        num_scalar_prefetch=2,
        grid=(RET_HEADS, b // nb),
        in_specs=[blk("na_q"), blk("na_k"),
                  pl.BlockSpec((nb, LANES, l), lambda i, bi, *_: (bi, i, 0)),
                  blk("na_g"), cblk("na_k"),
                  pl.BlockSpec((1, LANES, nb * lc), lambda i, bi, *_: (0, i, bi)),
                  blk("r_q"), blk("r_k"), blk("r_v"), blk("r_g"), cblk("r_k"), cblk("r_v"),
                  pl.BlockSpec((1, LANES), lambda i, bi, *_: (0, i))],
        out_specs=(pl.BlockSpec((nb, 1, l, LANES), lambda i, bi, *_: (bi, i, 0, 0)),
                   pl.BlockSpec((nb, 1, l, LANES), lambda i, bi, *_: (bi, i, 0, 0))),
        scratch_shapes=[pltpu.VMEM((3, 2, NA_K_ROWS * GRID_W, NA_Q_ROWS * GRID_W), jnp.float32),
                        pltpu.VMEM((2 * NA_WIN_ROWS - 1, GRID_W, LANES), jnp.float32),
                        pltpu.VMEM((nb, 2, NA_HEAD_DIM + BF16_SUBLANES, l), jnp.bfloat16),
                        pltpu.VMEM((nb, 2, NA_HEAD_DIM + BF16_SUBLANES, lc), jnp.bfloat16),
                        pltpu.VMEM((nb, n_chunks, RET_DIM, 2 * RET_DIM), jnp.float32),
                        pltpu.VMEM((nb, n_chunks, 2 * RET_DIM, RET_DIM), jnp.bfloat16)],
    )
    return pl.pallas_call(
        functools.partial(_mixer_kernel, rows=rows),
        out_shape=(jax.ShapeDtypeStruct((b, NA_WIDTH // LANES, l, LANES), jnp.bfloat16),
                   jax.ShapeDtypeStruct((b, RET_WIDTH // LANES, l, LANES), jnp.bfloat16)),
        grid_spec=grid_spec,
        compiler_params=pltpu.CompilerParams(
            dimension_semantics=("arbitrary", "arbitrary"), vmem_limit_bytes=VMEM_LIMIT),
        name="mixers",
    )(rpb.astype(jnp.float32).reshape(-1), lg,
      proj, proj, vt, proj, proj_ctx, vct,
      proj, proj, proj, proj, proj_ctx, proj_ctx, ret_norm_g.reshape(1, RET_WIDTH))


def _out_kernel(x_ref, gate_ref, yna_ref, yret_ref, w_ref, fg_ref, o_ref, *, sub):
    tm = x_ref.shape[1]

    def mix(j):
        rows = slice(j * sub, (j + 1) * sub)
        y = jnp.concatenate([ref[0, k, rows, :] for ref in (yna_ref, yret_ref) for k in range(ref.shape[1])],
                            axis=1)
        return jnp.dot(y, w_ref[...], preferred_element_type=jnp.float32)

    def finish(j, y):
        rows = slice(j * sub, (j + 1) * sub)
        z = x_ref[0, rows, :] + gate_ref[0] * y
        ms = jnp.mean(z * z, axis=-1, keepdims=True)
        o_ref[0, rows, :] = z * lax.rsqrt(ms + EPS) * fg_ref[...]

    pending = mix(0)
    for j in range(tm // sub):
        current = pending
        if (j + 1) * sub < tm:
            pending = mix(j + 1)
        finish(j, current)


def _out_proj(x, gate, y_na, y_ret, w_out, final_g, tm, sub):
    b, l, d = x.shape
    assert l % tm == 0 and tm % sub == 0
    return pl.pallas_call(
        functools.partial(_out_kernel, sub=sub),
        out_shape=jax.ShapeDtypeStruct((b, l, d), jnp.float32),
        grid=(b, l // tm),
        in_specs=[pl.BlockSpec((1, tm, d), lambda bi, i: (bi, i, 0)),
                  pl.BlockSpec((1, 1, d), lambda bi, i: (bi, 0, 0)),
                  pl.BlockSpec((1, NA_WIDTH // LANES, tm, LANES), lambda bi, i: (bi, 0, i, 0)),
                  pl.BlockSpec((1, RET_WIDTH // LANES, tm, LANES), lambda bi, i: (bi, 0, i, 0)),
                  pl.BlockSpec((NA_WIDTH + RET_WIDTH, d), lambda bi, i: (0, 0)),
                  pl.BlockSpec((1, d), lambda bi, i: (0, 0))],
        out_specs=pl.BlockSpec((1, tm, d), lambda bi, i: (bi, i, 0)),
        compiler_params=pltpu.CompilerParams(
            dimension_semantics=("arbitrary", "arbitrary"), vmem_limit_bytes=VMEM_LIMIT),
        name="out_proj",
    )(x, gate, y_na, y_ret, w_out, final_g.reshape(1, d))


def _rotary_tables(l):
    half = RET_DIM // 2
    nf = half // 2
    t = np.arange(l)
    row = (t // GRID_W).astype(np.float64)
    col = (t % GRID_W).astype(np.float64)
    inv = ROPE_BASE ** (-np.arange(nf, dtype=np.float64) / nf)
    ang = np.concatenate([row[:, None] * inv, col[:, None] * inv], axis=-1)
    cos, sin = np.cos(ang), np.sin(ang)
    cosf = np.concatenate([cos, cos], axis=-1).astype(np.float32)
    sinf = np.concatenate([-sin, sin], axis=-1).astype(np.float32)
    return jnp.asarray(cosf), jnp.asarray(sinf)


def kernel(x, c, ctx, c_ctx, norm_g, w_ada, b_ada, w_in, na_rpb, ret_decay_fwd, ret_decay_bwd,
           ret_norm_g, w_out, final_norm_g):
    depth = norm_g.shape[0]
    assert depth == 1, "context stream update between layers is not implemented"
    b, l, d = x.shape
    rows = l // GRID_W
    i = 0

    cc = jnp.concatenate([c, c_ctx[None, :]], axis=0)
    pad = (-cc.shape[0]) % BF16_SUBLANES
    cc = jnp.pad(cc, ((0, pad), (0, 0)))
    mod, w, w_vt = _prep(cc, w_ada[i], b_ada[i], w_in[i], v_group=2)
    shift, scale, gate = (mod[:b, None, :d], mod[:b, None, d:2 * d], mod[:b, None, 2 * d:])
    shift_c, scale_c = mod[b:b + 1, None, :d], mod[b:b + 1, None, d:2 * d]
    cosf, sinf = _rotary_tables(l)
    blocks = GROUP_COLS // LANES

    lat_groups = ((0, "na_q"), (3, "silu"), (4, "rot"), (5, "rot_kscale"), (7, "silu"), (1, "plain"), (6, "plain"))
    lat_names = ("na_q", "na_g", "r_q", "r_k", "r_g", "na_k", "r_v")
    proj, vt = _in_proj(x, scale, shift, norm_g[i], w, w_vt, cosf, sinf, lat_groups, tm=1024, sub=256)

    lc = ctx.shape[1]
    ctx_groups = ((1, "plain"), (5, "kscale"), (6, "plain"))
    ctx_names = ("na_k", "r_k", "r_v")
    proj_ctx, vct = _in_proj(ctx.reshape(1, b * lc, d), scale_c, shift_c, norm_g[i], w, w_vt, cosf, sinf,
                             ctx_groups, tm=1024, sub=256)

    lg = jnp.stack([-jnp.exp(ret_decay_fwd[i].astype(jnp.float32)),
                    -jnp.exp(ret_decay_bwd[i].astype(jnp.float32))])
    y_na, y_ret = _mixers(na_rpb[i], lg, proj, vt, proj_ctx, vct, ret_norm_g[i], rows,
                          cols={name: k * blocks for k, name in enumerate(lat_names)},
                          ctx_cols={name: k * blocks for k, name in enumerate(ctx_names)})

    return _out_proj(x, gate, y_na, y_ret, w_out[i].astype(jnp.bfloat16), final_norm_g, tm=2048, sub=256)
```
